```python
import math
import jax
import jax.numpy as jnp
from jax import lax
import numpy as np

D_MODEL = 1024
BATCH = 32
SEQ = 256
DEPTH = 2
DEC_BATCH = 8
DEC_SEQ = 4096
PAST_LEN = 256

GRID_W = 64
EPS = 1e-6
N_MOD = 9
N_BRANCH = 4
D_FF = 2816
N_HEADS = 8
N_KV_HEADS = 2
HEAD_DIM = 64
GQA_GROUP = N_HEADS // N_KV_HEADS
D_ATTN = N_HEADS * HEAD_DIM
D_KV = N_KV_HEADS * HEAD_DIM
WINDOW = 128
ATTN_BLOCK = 128
ATTN_SCALE = HEAD_DIM ** -0.5
ROPE_BASE = 10000.0
D_LRU = D_MODEL // 4
LRU_HEADS = 4
LRU_BLOCK = D_LRU // LRU_HEADS
LRU_CONV = 4
LRU_C = 8.0
HG_HEADS = 4
HG_DK = 64
HG_DV = 64
D_HG_K = HG_HEADS * HG_DK
D_HG_V = HG_HEADS * HG_DV
HG_CHUNK = 64
HG_SCALE = HG_DK ** -0.5
D_HY = D_MODEL // 4
HY_ORDER = 2
HY_SHORT = 3
HY_BANDS = 8
HY_EMB = 2 * HY_BANDS + 1
HY_FFN = 64
HY_TARGET = 1e-2
HY_DECAY_SLOW = -math.log(HY_TARGET) / 1.5
HY_DECAY_FAST = -math.log(HY_TARGET) / 0.3
_IN_SIZES = (D_ATTN, D_KV, D_KV, D_LRU, D_LRU, D_HG_K, D_HG_K, D_HG_K, D_HG_V, D_HG_V, 3 * D_HY)
N_IN = D_ATTN + 2 * D_KV + 2 * D_LRU + 3 * D_HG_K + 2 * D_HG_V + 3 * D_HY

kernel_name = 'hybrid_diffusion_gated_parallel_step'


def rms_norm(x, w):
    x32 = x.astype(jnp.float32)
    y = x32 * lax.rsqrt(jnp.mean(x32 * x32, axis=-1, keepdims=True) + EPS)
    return (y * w.astype(jnp.float32)).astype(x.dtype)


def modulate(x, w, shift, scale):
    return rms_norm(x, w) * (1 + scale) + shift


def swiglu(x, w_in, w_out):
    gate, up = jnp.split(x @ w_in, 2, axis=-1)
    return (jax.nn.silu(gate) * up) @ w_out


def depthwise_conv(x, w, b, pad_left):
    width = w.shape[0]
    y = lax.conv_general_dilated(x, w[:, None, :], window_strides=(1,),
                                 padding=[(pad_left, width - 1 - pad_left)],
                                 dimension_numbers=('NWC', 'WIO', 'NWC'),
                                 feature_group_count=x.shape[-1])
    return y + b


def axial_rope_angles(seq):
    rows = seq // GRID_W
    row = jnp.repeat(jnp.arange(rows, dtype=jnp.float32), GRID_W)
    col = jnp.tile(jnp.arange(GRID_W, dtype=jnp.float32), rows)
    n = HEAD_DIM // 4
    inv = ROPE_BASE ** (-jnp.arange(n, dtype=jnp.float32) / n)
    return row[:, None] * inv, col[:, None] * inv


def _rotate(x, ang):
    c = jnp.cos(ang)[None, :, None, :].astype(x.dtype)
    s = jnp.sin(ang)[None, :, None, :].astype(x.dtype)
    x1, x2 = jnp.split(x, 2, axis=-1)
    return jnp.concatenate([x1 * c - x2 * s, x2 * c + x1 * s], axis=-1)


def apply_axial_rope(x, ang_r, ang_c):
    xr, xc = jnp.split(x, 2, axis=-1)
    return jnp.concatenate([_rotate(xr, ang_r), _rotate(xc, ang_c)], axis=-1)


def context_attention(q, k, v, sink):
    b, seq = q.shape[0], q.shape[1]
    nq = seq // ATTN_BLOCK
    qb = q.reshape(b, nq, ATTN_BLOCK, N_KV_HEADS, GQA_GROUP, HEAD_DIM).swapaxes(0, 1)
    sink_l = sink.astype(jnp.float32).reshape(N_KV_HEADS, GQA_GROUP, 1, 1)

    def one_block(qblk):
        s = jnp.einsum('bqkgd,bckd->bkgqc', qblk, k, preferred_element_type=jnp.float32) * ATTN_SCALE
        s_sink = jnp.broadcast_to(sink_l, s.shape[:-1] + (1,))
        p = jax.nn.softmax(jnp.concatenate([s, s_sink], axis=-1), axis=-1)[..., :-1]
        return jnp.einsum('bkgqc,bckd->bqkgd', p.astype(v.dtype), v)

    out = lax.map(one_block, qb)
    return out.swapaxes(0, 1).reshape(b, seq, D_ATTN)


def latent_attention(q, k, v, k_ctx, v_ctx, sink):
    b, seq = q.shape[0], q.shape[1]
    nb = seq // ATTN_BLOCK
    band = 3 * ATTN_BLOCK
    qb = q.reshape(b, nb, ATTN_BLOCK, N_KV_HEADS, GQA_GROUP, HEAD_DIM).swapaxes(0, 1)

    def to_band(t):
        tp = jnp.pad(t, ((0, 0), (ATTN_BLOCK, ATTN_BLOCK), (0, 0), (0, 0)))
        tp = tp.reshape(b, nb + 2, ATTN_BLOCK, N_KV_HEADS, HEAD_DIM)
        tb = jnp.concatenate([tp[:, :-2], tp[:, 1:-1], tp[:, 2:]], axis=2)
        return tb.swapaxes(0, 1)

    kb, vb = to_band(k), to_band(v)
    blk = jnp.arange(nb)[:, None, None]
    qpos = blk * ATTN_BLOCK + jnp.arange(ATTN_BLOCK)[None, :, None]
    kpos = (blk - 1) * ATTN_BLOCK + jnp.arange(band)[None, None, :]
    mask = (jnp.abs(qpos - kpos) <= WINDOW) & (kpos >= 0) & (kpos < seq)
    sink_l = sink.astype(jnp.float32).reshape(N_KV_HEADS, GQA_GROUP, 1, 1)

    def one_block(args):
        qblk, kblk, vblk, m = args
        s_loc = jnp.einsum('bqkgd,bskd->bkgqs', qblk, kblk, preferred_element_type=jnp.float32) * ATTN_SCALE
        s_loc = jnp.where(m, s_loc, -jnp.inf)
        s_ctx = jnp.einsum('bqkgd,bckd->bkgqc', qblk, k_ctx, preferred_element_type=jnp.float32) * ATTN_SCALE
        s_sink = jnp.broadcast_to(sink_l, s_loc.shape[:-1] + (1,))
        p = jax.nn.softmax(jnp.concatenate([s_loc, s_ctx, s_sink], axis=-1), axis=-1)
        p_loc = p[..., :band].astype(vblk.dtype)
        p_ctx = p[..., band:-1].astype(v_ctx.dtype)
        return (jnp.einsum('bkgqs,bskd->bqkgd', p_loc, vblk)
                + jnp.einsum('bkgqc,bckd->bqkgd', p_ctx, v_ctx))

    out = lax.map(one_block, (qb, kb, vb, mask))
    return out.swapaxes(0, 1).reshape(b, seq, D_ATTN)


def linear_scan(a, bx, h0):
    bx = bx.at[:, 0].add(a[:, 0] * h0)

    def combine(left, right):
        return left[0] * right[0], right[0] * left[1] + right[1]

    _, h = lax.associative_scan(combine, (a, bx), axis=1)
    return h


def rglru_direction(x, w_r, b_r, w_i, b_i, lam, h0):
    b, seq, _ = x.shape
    xb = x.reshape(b, seq, LRU_HEADS, LRU_BLOCK)
    r = jax.nn.sigmoid(jnp.einsum('blhi,hij->blhj', xb, w_r.astype(jnp.float32)).reshape(b, seq, D_LRU)
                       + b_r.astype(jnp.float32))
    i = jax.nn.sigmoid(jnp.einsum('blhi,hij->blhj', xb, w_i.astype(jnp.float32)).reshape(b, seq, D_LRU)
                       + b_i.astype(jnp.float32))
    log_a = -LRU_C * r * jax.nn.softplus(-lam.astype(jnp.float32))
    a = jnp.exp(log_a)
    return linear_scan(a, jnp.sqrt(-jnp.expm1(2.0 * log_a)) * (i * x), h0)


def hgrn2_chunk_scan(q, k, v, logf, s0):
    b, seq = q.shape[0], q.shape[1]
    nc = seq // HG_CHUNK

    def to_chunks(t):
        return t.reshape(b, nc, HG_CHUNK, HG_HEADS, t.shape[-1]).transpose(1, 0, 3, 2, 4)

    causal = jnp.tril(jnp.ones((HG_CHUNK, HG_CHUNK), dtype=bool))[:, :, None]

    def step(s, xs):
        qc, kc, vc, gc = xs
        cum = jnp.cumsum(gc, axis=2)
        o_inter = jnp.einsum('bhtk,bhkv->bhtv', qc * jnp.exp(cum), s)
        diff = cum[:, :, :, None, :] - cum[:, :, None, :, :]
        dec = jnp.exp(jnp.where(causal, diff, -jnp.inf))
        attn = jnp.einsum('bhtk,bhtsk,bhsk->bhts', qc, dec, kc)
        o_intra = jnp.einsum('bhts,bhsv->bhtv', attn, vc)
        last = cum[:, :, -1:, :]
        s_new = (jnp.exp(last[:, :, 0])[..., None] * s
                 + jnp.einsum('bhsk,bhsv->bhkv', kc * jnp.exp(last - cum), vc))
        return s_new, o_inter + o_intra

    s_fin, o = lax.scan(step, s0, (to_chunks(q), to_chunks(k), to_chunks(v), to_chunks(logf)))
    o = o.transpose(1, 0, 3, 2, 4).reshape(b, seq, HG_HEADS, v.shape[-1])
    return o, s_fin


def hyena_filters(seq, w1, b1, w2, b2, w3, freq):
    f32 = jnp.float32
    pos = jnp.arange(seq, dtype=f32)
    t = pos / max(seq - 1, 1)
    bands = jnp.linspace(1e-4, HY_BANDS - 1, HY_BANDS, dtype=f32)
    ang = (2.0 * math.pi / seq) * pos[:, None] * bands[None, :]
    feat = jnp.concatenate([t[:, None], jnp.cos(ang), jnp.sin(ang)], axis=-1)
    fr = freq.astype(f32)
    h = jnp.sin(fr * (feat @ w1.astype(f32) + b1.astype(f32)))
    h = jnp.sin(fr * (h @ w2.astype(f32) + b2.astype(f32)))
    h = (h @ w3.astype(f32)).reshape(seq, 2, HY_ORDER, D_HY)
    deltas = jnp.linspace(HY_DECAY_SLOW, HY_DECAY_FAST, D_HY, dtype=f32)
    h = h * jnp.exp(-t[:, None] * deltas[None, :])[:, None, None, :]
    return h * lax.rsqrt(jnp.sum(h * h, axis=(0, 1), keepdims=True))


def two_sided_long_conv(u, hf, hb):
    seq, ch = hf.shape
    filt = jnp.concatenate([hf, jnp.zeros((1, ch), hf.dtype), jnp.flip(hb[1:], axis=0)], axis=0)
    y = jnp.fft.irfft(jnp.fft.rfft(u, n=2 * seq, axis=1) * jnp.fft.rfft(filt, axis=0)[None],
                      n=2 * seq, axis=1)
    return y[:, :seq]


def hyena_mixer(y_in, lp):
    seq = y_in.shape[1]
    hz = depthwise_conv(y_in, lp['hy_conv_w'], lp['hy_conv_b'], HY_SHORT // 2).astype(jnp.float32)
    v, x1, x2 = jnp.split(hz, 3, axis=-1)
    filt = hyena_filters(seq, lp['hy_w1'], lp['hy_b1'], lp['hy_w2'], lp['hy_b2'], lp['hy_w3'], lp['hy_freq'])
    bias = lp['hy_bias'].astype(jnp.float32)
    z = v
    for n, gate in enumerate((x1, x2)):
        z = gate * (two_sided_long_conv(z, filt[:, 0, n], filt[:, 1, n]) + bias[n] * z)
    return z.astype(y_in.dtype)


def token_mixer(u, lp, lb, ctx):
    b, seq, _ = u.shape
    f32 = jnp.float32
    pts = np.cumsum(_IN_SIZES)[:-1].tolist()
    (a_q, a_k, a_v, r_x, r_g, h_q, h_ff, h_fb, h_i, h_g, y_in) = jnp.split(u @ lp['w_in'], pts, axis=-1)

    q = a_q.reshape(b, seq, N_KV_HEADS, GQA_GROUP, HEAD_DIM)
    k = a_k.reshape(b, seq, N_KV_HEADS, HEAD_DIM)
    v = a_v.reshape(b, seq, N_KV_HEADS, HEAD_DIM)
    if ctx is None:
        att = context_attention(q, k, v, lp['attn_sink'])
    else:
        ang_r, ang_c = axial_rope_angles(seq)
        q_rot = apply_axial_rope(q.reshape(b, seq, N_HEADS, HEAD_DIM), ang_r, ang_c).reshape(q.shape)
        att = latent_attention(q_rot, apply_axial_rope(k, ang_r, ang_c), v, ctx[0], ctx[1], lp['attn_sink'])

    xc = depthwise_conv(r_x, lp['lru_conv_w'], lp['lru_conv_b'], LRU_CONV // 2).astype(f32)
    lru_h0 = jnp.zeros((b, 2, D_LRU), f32) if ctx is None else ctx[2].astype(f32)
    lru_f = rglru_direction(xc, lp['lru_w_r'][0], lp['lru_b_r'][0], lp['lru_w_i'][0], lp['lru_b_i'][0],
                            lp['lru_lambda'][0], lru_h0[:, 0])
    lru_b_rev = rglru_direction(jnp.flip(xc, 1), lp['lru_w_r'][1], lp['lru_b_r'][1], lp['lru_w_i'][1],
                                lp['lru_b_i'][1], lp['lru_lambda'][1], lru_h0[:, 1])
    lru = ((lru_f + jnp.flip(lru_b_rev, 1)) * jax.nn.gelu(r_g.astype(f32))).astype(u.dtype)

    hq = (jax.nn.silu(h_q.astype(f32)) * HG_SCALE).reshape(b, seq, HG_HEADS, HG_DK)
    hv = h_i.astype(f32).reshape(b, seq, HG_HEADS, HG_DV)
    hg_s0 = jnp.zeros((b, 2, HG_HEADS, HG_DK, HG_DV), f32) if ctx is None else ctx[3].astype(f32)

    def forget(z, lower):
        f = (lower + (1.0 - lower) * jax.nn.sigmoid(z.astype(f32))).reshape(b, seq, HG_HEADS, HG_DK)
        return 1.0 - f, jnp.log(f)

    k_fw, g_fw = forget(h_ff, lb[0])
    k_bw, g_bw = forget(h_fb, lb[1])
    o_f, s_f = hgrn2_chunk_scan(hq, k_fw, hv, g_fw, hg_s0[:, 0])
    o_b, s_b = hgrn2_chunk_scan(jnp.flip(hq, 1), jnp.flip(k_bw, 1), jnp.flip(hv, 1), jnp.flip(g_bw, 1),
                                hg_s0[:, 1])
    o = o_f + jnp.flip(o_b, 1)
    o = (o * lax.rsqrt(jnp.mean(o * o, axis=-1, keepdims=True) + EPS)
         * lp['hg_norm'].astype(f32).reshape(HG_HEADS, HG_DV))
    hg = (o * jax.nn.silu(h_g.astype(f32)).reshape(b, seq, HG_HEADS, HG_DV)).reshape(b, seq, D_HG_V).astype(u.dtype)

    hy = hyena_mixer(y_in, lp)

    branches = (att @ lp['w_bo_attn'], lru @ lp['w_bo_lru'], hg @ lp['w_bo_hg'], hy @ lp['w_bo_hy'])
    mixed = jax.nn.sigmoid(u @ lp['w_gate'][0]) * branches[0]
    for n in range(1, N_BRANCH):
        mixed = mixed + jax.nn.sigmoid(u @ lp['w_gate'][n]) * branches[n]
    out = mixed @ lp['w_out']
    if ctx is None:
        state = (k, v, jnp.stack([lru_f[:, -1], lru_b_rev[:, -1]], axis=1), jnp.stack([s_f, s_b], axis=1))
        return out, state
    return out, None


def trunk_layer(x, cond, lp, lb, ctx):
    mods = (jax.nn.silu(cond) @ lp['w_ada'] + lp['b_ada'])[:, None, :]
    sh1, sc1, g1, sh2, sc2, g2, sh3, sc3, g3 = jnp.split(mods, N_MOD, axis=-1)
    x = x + 0.5 * g1 * swiglu(modulate(x, lp['ln_ffn1'], sh1, sc1), lp['w_ffn1_in'], lp['w_ffn1_out'])
    mix, state = token_mixer(modulate(x, lp['ln_mix'], sh2, sc2), lp, lb, ctx)
    x = x + g2 * mix
    x = x + 0.5 * g3 * swiglu(modulate(x, lp['ln_ffn2'], sh3, sc3), lp['w_ffn2_in'], lp['w_ffn2_out'])
    return x, state


def setup_inputs(seed: int = 0) -> dict:
    key = jax.random.key(seed)
    keys = jax.random.split(key, 80)
    counter = [0]

    def next_key():
        k = keys[counter[0]]
        counter[0] += 1
        return k

    def nrm(shape, scale=1.0):
        return jax.random.normal(next_key(), shape, jnp.float32) * scale

    def gain(shape):
        return 1.0 + nrm(shape, 0.02)

    D = D_MODEL
    a0 = jax.random.uniform(next_key(), (DEPTH, 2, D_LRU), jnp.float32, 0.9, 0.999)
    s0 = a0 ** (1.0 / LRU_C)
    lru_lambda = jnp.log(s0) - jnp.log1p(-s0)
    return {
        'x_prompt': nrm((BATCH, SEQ, D)),
        'x_sample': nrm((DEC_BATCH, DEC_SEQ, D)),
        'cache_k': nrm((DEC_BATCH, DEPTH, PAST_LEN, N_KV_HEADS, HEAD_DIM)),
        'cache_v': nrm((DEC_BATCH, DEPTH, PAST_LEN, N_KV_HEADS, HEAD_DIM)),
        'state_lru': nrm((DEC_BATCH, DEPTH, 2, D_LRU)),
        'state_hgrn': nrm((DEC_BATCH, DEPTH, 2, HG_HEADS, HG_DK, HG_DV), 0.5),
        'c': nrm((DEC_BATCH, D)),
        'c_ctx': nrm((D,)),
        'ln_ffn1': gain((DEPTH, D)),
        'ln_mix': gain((DEPTH, D)),
        'ln_ffn2': gain((DEPTH, D)),
        'w_ada': nrm((DEPTH, D, N_MOD * D), 0.5 * D ** -0.5),
        'b_ada': nrm((DEPTH, N_MOD * D), 0.02),
        'w_ffn1_in': nrm((DEPTH, D, 2 * D_FF), D ** -0.5),
        'w_ffn1_out': nrm((DEPTH, D_FF, D), D_FF ** -0.5),
        'w_ffn2_in': nrm((DEPTH, D, 2 * D_FF), D ** -0.5),
        'w_ffn2_out': nrm((DEPTH, D_FF, D), D_FF ** -0.5),
        'w_in': nrm((DEPTH, D, N_IN), D ** -0.5),
        'attn_sink': nrm((DEPTH, N_HEADS), 0.5),
        'lru_conv_w': nrm((DEPTH, LRU_CONV, D_LRU), LRU_CONV ** -0.5),
        'lru_conv_b': nrm((DEPTH, D_LRU), 0.02),
        'lru_w_r': nrm((DEPTH, 2, LRU_HEADS, LRU_BLOCK, LRU_BLOCK), LRU_BLOCK ** -0.5),
        'lru_b_r': nrm((DEPTH, 2, D_LRU), 0.02),
        'lru_w_i': nrm((DEPTH, 2, LRU_HEADS, LRU_BLOCK, LRU_BLOCK), LRU_BLOCK ** -0.5),
        'lru_b_i': nrm((DEPTH, 2, D_LRU), 0.02),
        'lru_lambda': lru_lambda,
        'hg_lb_logits': nrm((DEPTH, 2, D_HG_K), 0.5),
        'hg_norm': gain((DEPTH, D_HG_V)),
        'hy_conv_w': nrm((DEPTH, HY_SHORT, 3 * D_HY), HY_SHORT ** -0.5),
        'hy_conv_b': nrm((DEPTH, 3 * D_HY), 0.02),
        'hy_w1': nrm((DEPTH, HY_EMB, HY_FFN), HY_EMB ** -0.5),
        'hy_b1': nrm((DEPTH, HY_FFN), 0.02),
        'hy_w2': nrm((DEPTH, HY_FFN, HY_FFN), HY_FFN ** -0.5),
        'hy_b2': nrm((DEPTH, HY_FFN), 0.02),
        'hy_w3': nrm((DEPTH, HY_FFN, 2 * HY_ORDER * D_HY), HY_FFN ** -0.5),
        'hy_freq': 1.0 + nrm((DEPTH, HY_FFN), 0.1),
        'hy_bias': nrm((DEPTH, HY_ORDER, D_HY), 0.5),
        'w_bo_attn': nrm((DEPTH, D_ATTN, D), D_ATTN ** -0.5),
        'w_bo_lru': nrm((DEPTH, D_LRU, D), D_LRU ** -0.5),
        'w_bo_hg': nrm((DEPTH, D_HG_V, D), D_HG_V ** -0.5),
        'w_bo_hy': nrm((DEPTH, D_HY, D), D_HY ** -0.5),
        'w_gate': nrm((DEPTH, N_BRANCH, D, D), D ** -0.5),
        'w_out': nrm((DEPTH, D, D), D ** -0.5),
        'final_norm': gain((D,)),
    }


def reference(x_prompt, x_sample, cache_k, cache_v, state_lru, state_hgrn, c, c_ctx,
              ln_ffn1, ln_mix, ln_ffn2, w_ada, b_ada, w_ffn1_in, w_ffn1_out, w_ffn2_in, w_ffn2_out,
              w_in, attn_sink, lru_conv_w, lru_conv_b, lru_w_r, lru_b_r, lru_w_i, lru_b_i, lru_lambda,
              hg_lb_logits, hg_norm, hy_conv_w, hy_conv_b, hy_w1, hy_b1, hy_w2, hy_b2, hy_w3, hy_freq,
              hy_bias, w_bo_attn, w_bo_lru, w_bo_hg, w_bo_hy, w_gate, w_out, final_norm):
    lb_soft = jax.nn.softmax(hg_lb_logits.astype(jnp.float32), axis=0)
    lb_all = jnp.cumsum(lb_soft, axis=0) - lb_soft[0]

    def layer_params(l):
        return {
            'ln_ffn1': ln_ffn1[l], 'ln_mix': ln_mix[l], 'ln_ffn2': ln_ffn2[l],
            'w_ada': w_ada[l], 'b_ada': b_ada[l],
            'w_ffn1_in': w_ffn1_in[l], 'w_ffn1_out': w_ffn1_out[l],
            'w_ffn2_in': w_ffn2_in[l], 'w_ffn2_out': w_ffn2_out[l],
            'w_in': w_in[l], 'attn_sink': attn_sink[l],
            'lru_conv_w': lru_conv_w[l], 'lru_conv_b': lru_conv_b[l],
            'lru_w_r': lru_w_r[l], 'lru_b_r': lru_b_r[l], 'lru_w_i': lru_w_i[l], 'lru_b_i': lru_b_i[l],
            'lru_lambda': lru_lambda[l], 'hg_norm': hg_norm[l],
            'hy_conv_w': hy_conv_w[l], 'hy_conv_b': hy_conv_b[l],
            'hy_w1': hy_w1[l], 'hy_b1': hy_b1[l], 'hy_w2': hy_w2[l], 'hy_b2': hy_b2[l],
            'hy_w3': hy_w3[l], 'hy_freq': hy_freq[l], 'hy_bias': hy_bias[l],
            'w_bo_attn': w_bo_attn[l], 'w_bo_lru': w_bo_lru[l], 'w_bo_hg': w_bo_hg[l], 'w_bo_hy': w_bo_hy[l],
            'w_gate': w_gate[l], 'w_out': w_out[l],
        }

    h = x_prompt
    ks, vs, lrus, hgs = [], [], [], []
    for l in range(DEPTH):
        h, (k_l, v_l, lru_l, hg_l) = trunk_layer(h, c_ctx[None, :], layer_params(l), lb_all[l], None)
        ks.append(k_l)
        vs.append(v_l)
        lrus.append(lru_l)
        hgs.append(hg_l)

    z = x_sample
    for l in range(DEPTH):
        ctx = (cache_k[:, l], cache_v[:, l], state_lru[:, l], state_hgrn[:, l])
        z, _ = trunk_layer(z, c, layer_params(l), lb_all[l], ctx)

    y_prompt = rms_norm(h, final_norm)
    y_sample = rms_norm(z, final_norm)
    new_k = jnp.stack(ks, axis=1)
    new_v = jnp.stack(vs, axis=1)
    new_lru = jnp.stack(lrus, axis=1)
    new_hgrn = jnp.stack(hgs, axis=1)
    return (y_prompt, y_sample, new_k, new_v, new_lru, new_hgrn)
```

```python
import functools
import math

import numpy as np
import jax
import jax.numpy as jnp
from jax import lax
from jax.experimental import pallas as pl
from jax.experimental.pallas import tpu as pltpu

F32 = jnp.float32
BF16 = jnp.bfloat16

D_MODEL = 1024
DEPTH = 2
GRID_W = 64
EPS = 1e-6
N_MOD = 9
N_BRANCH = 4
D_FF = 2816
N_HEADS = 8
N_KV_HEADS = 2
HEAD_DIM = 64
GQA_GROUP = N_HEADS // N_KV_HEADS
D_ATTN = N_HEADS * HEAD_DIM
D_KV = N_KV_HEADS * HEAD_DIM
WINDOW = 128
ATTN_BLOCK = 128
ATTN_SCALE = HEAD_DIM ** -0.5
ROPE_BASE = 10000.0
D_LRU = D_MODEL // 4
LRU_HEADS = 4
LRU_BLOCK = D_LRU // LRU_HEADS
LRU_CONV = 4
LRU_C = 8.0
HG_HEADS = 4
HG_DK = 64
HG_DV = 64
D_HG = HG_HEADS * HG_DK
HG_SCALE = HG_DK ** -0.5
HG_BLK = 16
D_HY = D_MODEL // 4
HY_ORDER = 2
HY_SHORT = 3
HY_BANDS = 8
HY_EMB = 2 * HY_BANDS + 1
HY_FFN = 64
HY_TARGET = 1e-2
HY_DECAY_SLOW = -math.log(HY_TARGET) / 1.5
HY_DECAY_FAST = -math.log(HY_TARGET) / 0.3
N_IN = D_ATTN + 2 * D_KV + 2 * D_LRU + 5 * D_HG + 3 * D_HY

COL_Q = 0
COL_K = D_ATTN
COL_V = D_ATTN + D_KV
COL_RX = D_ATTN + 2 * D_KV
COL_RG = COL_RX + D_LRU
COL_HQ = COL_RG + D_LRU
COL_HFF = COL_HQ + D_HG
COL_HFB = COL_HFF + D_HG
COL_HI = COL_HFB + D_HG
COL_HGATE = COL_HI + D_HG
COL_HY = COL_HGATE + D_HG

FF_CHUNK = 256
N_FF_CHUNK = D_FF // FF_CHUNK
ROW_TILE = 512
VMEM_LIMIT = 56 * 1024 * 1024


def _bf(x):
    return x.astype(BF16)


def _dot(a, b):
    return jnp.dot(a, b, preferred_element_type=F32)


def _split3(x):
    hi = _bf(x)
    r1 = x - hi.astype(F32)
    mid = _bf(r1)
    lo = _bf(r1 - mid.astype(F32))
    return hi, mid, lo


def _dot_exact_rhs(mat_bf, x):
    hi, mid, lo = _split3(x)
    return _dot(mat_bf, hi) + _dot(mat_bf, mid) + _dot(mat_bf, lo)


def _sigmoid(x):
    return jax.nn.sigmoid(x)


def _silu(x):
    return x * _sigmoid(x)


def _norm_mod(x, lnw, shift, scale):
    ms = jnp.mean(x * x, axis=-1, keepdims=True)
    y = x * lax.rsqrt(ms + EPS) * lnw
    return y * (1.0 + scale) + shift


def _cparams(sem):
    return pltpu.CompilerParams(dimension_semantics=sem, vmem_limit_bytes=VMEM_LIMIT)


def _const_spec(shape):
    nd = len(shape)
    return pl.BlockSpec(shape, lambda *_: (0,) * nd)


def _ada_kernel(cond_ref, w_ref, b_ref, o_ref):
    c = cond_ref[...]
    o_ref[0] = _dot(_bf(_silu(c)), _bf(w_ref[0])) + b_ref[0]


def ada_mods(cond, w_ada, b_ada):
    r = cond.shape[0]
    cb = 1024
    ncol = (N_MOD * D_MODEL) // cb
    return pl.pallas_call(
        _ada_kernel,
        grid=(DEPTH, ncol),
        in_specs=[
            pl.BlockSpec((r, D_MODEL), lambda l, j: (0, 0)),
            pl.BlockSpec((1, D_MODEL, cb), lambda l, j: (l, 0, j)),
            pl.BlockSpec((1, 1, cb), lambda l, j: (l, 0, j)),
        ],
        out_specs=pl.BlockSpec((1, r, cb), lambda l, j: (l, 0, j)),
        out_shape=jax.ShapeDtypeStruct((DEPTH, r, N_MOD * D_MODEL), F32),
        compiler_params=_cparams(("parallel", "parallel")),
        name="ada_mods",
    )(cond, w_ada, b_ada.reshape(DEPTH, 1, N_MOD * D_MODEL))


def _seq_mod_index(seq_len, n_mod, tm):
    def idx(i):
        return jnp.minimum((i * tm) // seq_len, n_mod - 1)
    return idx


def _ffn_kernel(x_ref, mod_ref, ln_ref, wg_ref, wu_ref, wo_ref, fn_ref, o_ref, acc_ref, *, mod_base, final):
    x = x_ref[...]
    m = mod_ref[0]
    sh = m[mod_base:mod_base + 1]
    sc = m[mod_base + 1:mod_base + 2]
    g = m[mod_base + 2:mod_base + 3]
    h = _bf(_norm_mod(x, ln_ref[...], sh, sc))
    acc_ref[...] = jnp.zeros_like(acc_ref)

    def body(c, carry):
        gate = _dot(h, wg_ref[c])
        up = _dot(h, wu_ref[c])
        act = _bf(_silu(gate) * up)
        acc_ref[...] += _dot(act, wo_ref[c])
        return carry

    lax.fori_loop(0, N_FF_CHUNK, body, 0)
    y = x + 0.5 * g * acc_ref[...]
    if final:
        ms = jnp.mean(y * y, axis=-1, keepdims=True)
        y = y * lax.rsqrt(ms + EPS) * fn_ref[...]
    o_ref[...] = y


def ffn_half_step(x, mods, ln_w, w_in, w_out, final_w, *, mod_base, final, seq_len):
    t = x.shape[0]
    tm = ROW_TILE
    wg = _bf(w_in[:, :D_FF]).reshape(D_MODEL, N_FF_CHUNK, FF_CHUNK).transpose(1, 0, 2)
    wu = _bf(w_in[:, D_FF:]).reshape(D_MODEL, N_FF_CHUNK, FF_CHUNK).transpose(1, 0, 2)
    wo = _bf(w_out).reshape(N_FF_CHUNK, FF_CHUNK, D_MODEL)
    midx = _seq_mod_index(seq_len, mods.shape[0], tm)
    return pl.pallas_call(
        functools.partial(_ffn_kernel, mod_base=mod_base, final=final),
        grid=(t // tm,),
        in_specs=[
            pl.BlockSpec((tm, D_MODEL), lambda i: (i, 0)),
            pl.BlockSpec((1, N_MOD, D_MODEL), lambda i: (midx(i), 0, 0)),
            _const_spec((1, D_MODEL)),
            _const_spec((N_FF_CHUNK, D_MODEL, FF_CHUNK)),
            _const_spec((N_FF_CHUNK, D_MODEL, FF_CHUNK)),
            _const_spec((N_FF_CHUNK, FF_CHUNK, D_MODEL)),
            _const_spec((1, D_MODEL)),
        ],
        out_specs=pl.BlockSpec((tm, D_MODEL), lambda i: (i, 0)),
        out_shape=jax.ShapeDtypeStruct((t, D_MODEL), F32),
        scratch_shapes=[pltpu.VMEM((tm, D_MODEL), F32)],
        compiler_params=_cparams(("parallel",)),
        name="ffn_half_step",
    )(x, mods, ln_w.reshape(1, D_MODEL), wg, wu, wo, final_w.reshape(1, D_MODEL))


PROJ_CHUNK = 256


def _proj_kernel(x_ref, mod_ref, ln_ref, w_ref, o_ref):
    x = x_ref[...]
    m = mod_ref[0]
    u = _bf(_norm_mod(x, ln_ref[...], m[3:4], m[4:5]))
    for c in range(N_IN // PROJ_CHUNK):
        sl = slice(c * PROJ_CHUNK, (c + 1) * PROJ_CHUNK)
        o_ref[:, sl] = _dot(u, w_ref[:, sl])


def mixer_in_proj(x, mods, ln_w, w_in, *, seq_len):
    t = x.shape[0]
    tm = ROW_TILE
    midx = _seq_mod_index(seq_len, mods.shape[0], tm)
    return pl.pallas_call(
        _proj_kernel,
        grid=(t // tm,),
        in_specs=[
            pl.BlockSpec((tm, D_MODEL), lambda i: (i, 0)),
            pl.BlockSpec((1, N_MOD, D_MODEL), lambda i: (midx(i), 0, 0)),
            _const_spec((1, D_MODEL)),
            _const_spec((D_MODEL, N_IN)),
        ],
        out_specs=pl.BlockSpec((tm, N_IN), lambda i: (i, 0)),
        out_shape=jax.ShapeDtypeStruct((t, N_IN), F32),
        compiler_params=_cparams(("parallel",)),
        name="mixer_in_proj",
    )(x, mods, ln_w.reshape(1, D_MODEL), _bf(w_in))


MERGE_CHUNK = 256


def _merge_kernel(x_ref, mod_ref, ln_ref, att_ref, lru_ref, hg_ref, hy_ref,
                  wg_ref, wa_ref, wl_ref, wh_ref, wy_ref, wo_ref, o_ref, mix_ref):
    x = x_ref[...]
    m = mod_ref[0]
    u = _bf(_norm_mod(x, ln_ref[...], m[3:4], m[4:5]))
    branches = (_bf(att_ref[...]), _bf(lru_ref[...]), _bf(hg_ref[...]), _bf(hy_ref[...]))
    w_bo = (wa_ref, wl_ref, wh_ref, wy_ref)
    for c in range(D_MODEL // MERGE_CHUNK):
        sl = slice(c * MERGE_CHUNK, (c + 1) * MERGE_CHUNK)
        mixed = None
        for n in range(N_BRANCH):
            term = _sigmoid(_dot(u, wg_ref[n, :, sl])) * _dot(branches[n], w_bo[n][:, sl])
            mixed = term if mixed is None else mixed + term
        mix_ref[:, sl] = _bf(mixed)
    o_ref[...] = x + m[5:6] * _dot(mix_ref[...], wo_ref[...])


def gated_merge(x, mods, ln_w, att, lru, hg, hy, w_gate, w_bo_attn, w_bo_lru, w_bo_hg, w_bo_hy, w_out,
                *, seq_len):
    t = x.shape[0]
    tm = ROW_TILE
    midx = _seq_mod_index(seq_len, mods.shape[0], tm)
    row = lambda w: pl.BlockSpec((tm, w), lambda i: (i, 0))
    return pl.pallas_call(
        _merge_kernel,
        grid=(t // tm,),
        in_specs=[
            row(D_MODEL),
            pl.BlockSpec((1, N_MOD, D_MODEL), lambda i: (midx(i), 0, 0)),
            _const_spec((1, D_MODEL)),
            row(D_ATTN), row(D_LRU), row(D_HG), row(D_HY),
            _const_spec((N_BRANCH, D_MODEL, D_MODEL)),
            _const_spec((D_ATTN, D_MODEL)),
            _const_spec((D_LRU, D_MODEL)),
            _const_spec((D_HG, D_MODEL)),
            _const_spec((D_HY, D_MODEL)),
            _const_spec((D_MODEL, D_MODEL)),
        ],
        out_specs=row(D_MODEL),
        out_shape=jax.ShapeDtypeStruct((t, D_MODEL), F32),
        scratch_shapes=[pltpu.VMEM((tm, D_MODEL), BF16)],
        compiler_params=_cparams(("parallel",)),
        name="gated_merge",
    )(x, mods, ln_w.reshape(1, D_MODEL), att, lru, hg, hy,
      _bf(w_gate), _bf(w_bo_attn), _bf(w_bo_lru), _bf(w_bo_hg), _bf(w_bo_hy), _bf(w_out))


def _sink_column(sink_ref, kvh, rows):
    cols = [jnp.broadcast_to(sink_ref[kvh * GQA_GROUP + g:kvh * GQA_GROUP + g + 1, 0:1], (rows, 1))
            for g in range(GQA_GROUP)]
    return jnp.concatenate(cols, axis=0)


def _stack_group(q, kvh):
    return jnp.concatenate(
        [q[:, (kvh * GQA_GROUP + g) * HEAD_DIM:(kvh * GQA_GROUP + g + 1) * HEAD_DIM] for g in range(GQA_GROUP)],
        axis=0)


def _dot_nt(a, b):
    return lax.dot_general(a, b, (((1,), (1,)), ((), ())), preferred_element_type=F32)


def _ctx_attn_kernel(q_ref, k_ref, v_ref, sink_ref, o_ref):
    rows = q_ref.shape[1]
    q = q_ref[0] * ATTN_SCALE
    k = k_ref[0]
    v = v_ref[0]
    for kvh in range(N_KV_HEADS):
        hs = slice(kvh * HEAD_DIM, (kvh + 1) * HEAD_DIM)
        s = _dot_nt(_bf(_stack_group(q, kvh)), _bf(k[:, hs]))
        sink = _sink_column(sink_ref, kvh, rows)
        m = jnp.maximum(jnp.max(s, axis=-1, keepdims=True), sink)
        p = jnp.exp(s - m)
        den = jnp.sum(p, axis=-1, keepdims=True) + jnp.exp(sink - m)
        o = _dot(_bf(p), _bf(v[:, hs])) / den
        for g in range(GQA_GROUP):
            head = kvh * GQA_GROUP + g
            o_ref[0, :, head * HEAD_DIM:(head + 1) * HEAD_DIM] = o[g * rows:(g + 1) * rows]


def _sink_table(sink):
    return jnp.broadcast_to(sink.astype(F32)[:, None], (N_HEADS, 128))


def context_attention(p_ctx, sink):
    b, seq, _ = p_ctx.shape
    return pl.pallas_call(
        _ctx_attn_kernel,
        grid=(b,),
        in_specs=[
            pl.BlockSpec((1, seq, D_ATTN), lambda i: (i, 0, 0)),
            pl.BlockSpec((1, seq, D_KV), lambda i: (i, 0, COL_K // D_KV)),
            pl.BlockSpec((1, seq, D_KV), lambda i: (i, 0, COL_V // D_KV)),
            _const_spec((N_HEADS, 128)),
        ],
        out_specs=pl.BlockSpec((1, seq, D_ATTN), lambda i: (i, 0, 0)),
        out_shape=jax.ShapeDtypeStruct((b, seq, D_ATTN), F32),
        compiler_params=_cparams(("parallel",)),
        name="context_attention",
    )(p_ctx, p_ctx, p_ctx, _sink_table(sink))


def _rope(x, cos, sin_signed):
    n = x.shape[-1]
    lane = lax.broadcasted_iota(jnp.int32, x.shape, x.ndim - 1)
    first = (lane % (HEAD_DIM // 2)) < (HEAD_DIM // 4)
    partner = jnp.where(first, pltpu.roll(x, n - HEAD_DIM // 4, axis=x.ndim - 1),
                        pltpu.roll(x, HEAD_DIM // 4, axis=x.ndim - 1))
    return x * cos + partner * sin_signed


def _rope_tables(seq):
    rows = seq // GRID_W
    row = np.repeat(np.arange(rows, dtype=np.float32), GRID_W)
    col = np.tile(np.arange(GRID_W, dtype=np.float32), rows)
    n = HEAD_DIM // 4
    inv = (ROPE_BASE ** (-jnp.arange(n, dtype=F32) / n))
    ang_r = jnp.asarray(row)[:, None] * inv
    ang_c = jnp.asarray(col)[:, None] * inv
    cos = jnp.concatenate([jnp.cos(ang_r), jnp.cos(ang_r), jnp.cos(ang_c), jnp.cos(ang_c)], axis=-1)
    sin = jnp.concatenate([-jnp.sin(ang_r), jnp.sin(ang_r), -jnp.sin(ang_c), jnp.sin(ang_c)], axis=-1)
    return jnp.tile(cos, (1, 2)), jnp.tile(sin, (1, 2))


def _lat_attn_kernel(q_ref, k_ref, v_ref, kc_ref, vc_ref, cos_ref, sin_ref, sink_ref, o_ref):
    seq = k_ref.shape[1]
    blk = ATTN_BLOCK
    band = 3 * blk
    i = pl.program_id(1)
    start = pl.multiple_of(jnp.clip((i - 1) * blk, 0, seq - band), blk)
    q0 = pl.multiple_of(i * blk, blk)
    cos_q = cos_ref[pl.ds(q0, blk), :]
    sin_q = sin_ref[pl.ds(q0, blk), :]
    rep = D_ATTN // 128
    q = _rope(q_ref[0], jnp.tile(cos_q, (1, rep)), jnp.tile(sin_q, (1, rep))) * ATTN_SCALE
    kb = _rope(k_ref[0, pl.ds(start, band), :], cos_ref[pl.ds(start, band), :], sin_ref[pl.ds(start, band), :])
    vb = v_ref[0, pl.ds(start, band), :]
    kc = kc_ref[0]
    vc = vc_ref[0]
    qpos = q0 + lax.broadcasted_iota(jnp.int32, (blk, band), 0)
    kpos = start + lax.broadcasted_iota(jnp.int32, (blk, band), 1)
    mask1 = jnp.abs(qpos - kpos) <= WINDOW
    mask = jnp.concatenate([mask1] * GQA_GROUP, axis=0)
    for kvh in range(N_KV_HEADS):
        hs = slice(kvh * HEAD_DIM, (kvh + 1) * HEAD_DIM)
        qs = _bf(_stack_group(q, kvh))
        s_loc = jnp.where(mask, _dot_nt(qs, _bf(kb[:, hs])), -jnp.inf)
        s_ctx = _dot_nt(qs, _bf(kc[:, hs]))
        sink = _sink_column(sink_ref, kvh, blk)
        m = jnp.maximum(jnp.maximum(jnp.max(s_loc, axis=-1, keepdims=True), jnp.max(s_ctx, axis=-1, keepdims=True)),
                        sink)
        p_loc = jnp.exp(s_loc - m)
        p_ctx = jnp.exp(s_ctx - m)
        den = (jnp.sum(p_loc, axis=-1, keepdims=True) + jnp.sum(p_ctx, axis=-1, keepdims=True)
               + jnp.exp(sink - m))
        o = (_dot(_bf(p_loc), _bf(vb[:, hs])) + _dot(_bf(p_ctx), _bf(vc[:, hs]))) / den
        for g in range(GQA_GROUP):
            head = kvh * GQA_GROUP + g
            o_ref[0, :, head * HEAD_DIM:(head + 1) * HEAD_DIM] = o[g * blk:(g + 1) * blk]


def latent_attention(p_lat, k_ctx, v_ctx, sink):
    b, seq, _ = p_lat.shape
    past = k_ctx.shape[1]
    cos, sin = _rope_tables(seq)
    return pl.pallas_call(
        _lat_attn_kernel,
        grid=(b, seq // ATTN_BLOCK),
        in_specs=[
            pl.BlockSpec((1, ATTN_BLOCK, D_ATTN), lambda bi, i: (bi, i, 0)),
            pl.BlockSpec((1, seq, D_KV), lambda bi, i: (bi, 0, COL_K // D_KV)),
            pl.BlockSpec((1, seq, D_KV), lambda bi, i: (bi, 0, COL_V // D_KV)),
            pl.BlockSpec((1, past, D_KV), lambda bi, i: (bi, 0, 0)),
            pl.BlockSpec((1, past, D_KV), lambda bi, i: (bi, 0, 0)),
            _const_spec((seq, 128)),
            _const_spec((seq, 128)),
            _const_spec((N_HEADS, 128)),
        ],
        out_specs=pl.BlockSpec((1, ATTN_BLOCK, D_ATTN), lambda bi, i: (bi, i, 0)),
        out_shape=jax.ShapeDtypeStruct((b, seq, D_ATTN), F32),
        compiler_params=_cparams(("parallel", "arbitrary")),
        name="latent_attention",
    )(p_lat, p_lat, p_lat, k_ctx, v_ctx, cos, sin, _sink_table(sink))


LRU_HALF = D_LRU // 2
LRU_ROWS = 256
LRU_PAD = 8


def _neg_expm1(y):
    series = -(y * (1.0 + y * (1.0 / 2 + y * (1.0 / 6 + y * (1.0 / 24 + y * (1.0 / 120 + y * (1.0 / 720)))))))
    return jnp.where(y > -0.25, series, 1.0 - jnp.exp(y))


def _softplus(x):
    return jnp.maximum(x, 0.0) + jnp.log1p(jnp.exp(-jnp.abs(x)))


def _gelu_tanh(x):
    return 0.5 * x * (1.0 + jnp.tanh(math.sqrt(2.0 / math.pi) * (x + 0.044715 * (x * x * x))))


def _lru_kernel(x_ref, g_ref, cw_ref, cb_ref, wr_ref, br_ref, wi_ref, bi_ref, lam_ref, h0_ref,
                o_ref, st_ref, pad_ref, a_ref, bx_ref, hb_ref):
    seq = x_ref.shape[1]
    zeros = jnp.zeros((LRU_PAD, LRU_HALF), F32)
    pad_ref[0:LRU_PAD, :] = zeros
    pad_ref[LRU_PAD + seq:2 * LRU_PAD + seq, :] = zeros
    pad_ref[LRU_PAD:LRU_PAD + seq, :] = x_ref[0]
    left = LRU_CONV // 2
    for ti in range(seq // LRU_ROWS):
        r0 = ti * LRU_ROWS
        xc = cb_ref[...]
        for k in range(LRU_CONV):
            xc = xc + cw_ref[k:k + 1, :] * pad_ref[r0 + LRU_PAD - left + k:r0 + LRU_PAD - left + k + LRU_ROWS, :]
        xcb = _bf(xc)
        for d in range(2):
            r = _sigmoid(_dot(xcb, wr_ref[d]) + br_ref[d:d + 1, :])
            i = _sigmoid(_dot(xcb, wi_ref[d]) + bi_ref[d:d + 1, :])
            log_a = (-LRU_C) * r * _softplus(-lam_ref[d:d + 1, :])
            a_ref[d, r0:r0 + LRU_ROWS, :] = jnp.exp(log_a)
            bx_ref[d, r0:r0 + LRU_ROWS, :] = jnp.sqrt(_neg_expm1(2.0 * log_a)) * (i * xc)

    def step(blk, carry):
        hf, hb = carry
        base = pl.multiple_of(blk * 8, 8)
        for j in range(8):
            tf = base + j
            tb = seq - 1 - tf
            hf = a_ref[0, pl.ds(tf, 1), :] * hf + bx_ref[0, pl.ds(tf, 1), :]
            o_ref[0, pl.ds(tf, 1), :] = hf
            hb = a_ref[1, pl.ds(tb, 1), :] * hb + bx_ref[1, pl.ds(tb, 1), :]
            hb_ref[pl.ds(tb, 1), :] = hb
        return hf, hb

    hf, hb = lax.fori_loop(0, seq // 8, step, (h0_ref[0, 0:1, :], h0_ref[0, 1:2, :]))
    st_ref[0, 0:1, :] = hf
    st_ref[0, 1:2, :] = hb
    for ti in range(seq // LRU_ROWS):
        rows = slice(ti * LRU_ROWS, (ti + 1) * LRU_ROWS)
        o_ref[0, rows, :] = (o_ref[0, rows, :] + hb_ref[rows, :]) * _gelu_tanh(g_ref[0, rows, :])


def _block_diag(w):
    n, blk, _ = w.shape
    eye = jnp.eye(n, dtype=w.dtype)
    return (eye[:, None, :, None] * w[:, :, None, :]).reshape(n * blk, n * blk)


def rglru_mixer(p, conv_w, conv_b, w_r, b_r, w_i, b_i, lam, h0):
    b, seq, _ = p.shape
    wr = _bf(jnp.stack([_block_diag(w_r[d]) for d in range(2)]))
    wi = _bf(jnp.stack([_block_diag(w_i[d]) for d in range(2)]))
    hw = LRU_HALF
    vec = lambda rows: pl.BlockSpec((rows, hw), lambda bi, h: (0, h))
    return pl.pallas_call(
        _lru_kernel,
        grid=(b, 2),
        in_specs=[
            pl.BlockSpec((1, seq, hw), lambda bi, h: (bi, 0, COL_RX // hw + h)),
            pl.BlockSpec((1, seq, hw), lambda bi, h: (bi, 0, COL_RG // hw + h)),
            vec(LRU_CONV), vec(1),
            pl.BlockSpec((2, hw, hw), lambda bi, h: (0, h, h)), vec(2),
            pl.BlockSpec((2, hw, hw), lambda bi, h: (0, h, h)), vec(2),
            vec(2),
            pl.BlockSpec((1, 2, hw), lambda bi, h: (bi, 0, h)),
        ],
        out_specs=[
            pl.BlockSpec((1, seq, hw), lambda bi, h: (bi, 0, h)),
            pl.BlockSpec((1, 2, hw), lambda bi, h: (bi, 0, h)),
        ],
        out_shape=[jax.ShapeDtypeStruct((b, seq, D_LRU), F32), jax.ShapeDtypeStruct((b, 2, D_LRU), F32)],
        scratch_shapes=[
            pltpu.VMEM((seq + 2 * LRU_PAD, hw), F32),
            pltpu.VMEM((2, seq, hw), F32),
            pltpu.VMEM((2, seq, hw), F32),
            pltpu.VMEM((seq, hw), F32),
        ],
        compiler_params=_cparams(("parallel", "parallel")),
        name="rglru_mixer",
    )(p, p, conv_w, conv_b.reshape(1, D_LRU), wr, b_r, wi, b_i, lam, h0)


HG_TILE = 256


def _hg_constants():
    idx = np.arange(HG_TILE)
    same_blk = (idx[:, None] // HG_BLK) == (idx[None, :] // HG_BLK)
    tri_f = same_blk & (idx[None, :] <= idx[:, None])
    tri_b = same_blk & (idx[None, :] >= idx[:, None])
    head = (idx[:, None] // HG_DK) == (idx[None, :] // HG_DK)
    return (jnp.asarray(np.stack([tri_f, tri_b]), BF16), jnp.asarray(same_blk, BF16),
            jnp.asarray(head, BF16), jnp.asarray(head, F32))


def _hgrn_kernel(q_ref, z_ref, v_ref, lb_ref, tri_ref, ones_ref, headb_ref, headf_ref, s0_ref,
                 o_ref, st_ref, state, qs, ks, vs, cums, qds, kns, tots):
    seg_rows = q_ref.shape[1]
    d = pl.program_id(1)
    s = pl.program_id(2)
    nseg = pl.num_programs(2)

    @pl.when(s == 0)
    def _():
        state[...] = s0_ref[0, 0]

    lb = lb_ref[0]
    for ti in range(seg_rows // HG_TILE):
        rows = slice(ti * HG_TILE, (ti + 1) * HG_TILE)
        f = lb + (1.0 - lb) * _sigmoid(z_ref[0, rows, :])
        g = jnp.log(f)
        kk = 1.0 - f
        q = _silu(q_ref[0, rows, :]) * HG_SCALE
        cum = _dot_exact_rhs(tri_ref[0], g)
        tot = _dot_exact_rhs(ones_ref[...], g)
        qs[rows, :] = q
        ks[rows, :] = kk
        vs[rows, :] = v_ref[0, rows, :]
        cums[rows, :] = cum
        tots[rows, :] = tot
        qds[rows, :] = q * jnp.exp(cum)
        kns[rows, :] = kk * jnp.exp(tot - cum)

    nblk = seg_rows // HG_BLK
    tt = lax.broadcasted_iota(jnp.int32, (HG_BLK, HG_BLK, D_HG), 0)
    ss = lax.broadcasted_iota(jnp.int32, (HG_BLK, HG_BLK, D_HG), 1)
    keep = (tt - ss) * jnp.where(d == 0, 1, -1) >= 0

    def block(j, carry):
        je = jnp.where(d == 0, j, nblk - 1 - j)
        r0 = pl.multiple_of(je * HG_BLK, HG_BLK)
        rows = pl.ds(r0, HG_BLK)
        cum = cums[rows, :]
        vb = vs[rows, :]
        diff = cum[:, None, :] - cum[None, :, :]
        dec = jnp.exp(jnp.minimum(diff, 0.0)) * qs[rows, :][:, None, :] * ks[rows, :][None, :, :]
        dm = jnp.where(keep, dec, 0.0).reshape(HG_BLK * HG_BLK, D_HG)
        att = _dot(_bf(dm), headb_ref[...]).reshape(HG_BLK, HG_BLK, D_HG)
        o_diag = jnp.sum(att * vb[None, :, :], axis=1)
        st = state[...]
        o_ref[0, 0, rows, :] = o_diag + _dot_nt(_bf(qds[rows, :]), _bf(st))
        upd = lax.dot_general(_bf(vb), _bf(kns[rows, :]), (((0,), (0,)), ((), ())), preferred_element_type=F32)
        state[...] = jnp.exp(tots[pl.ds(r0, 1), :]) * st + upd * headf_ref[...]
        return carry

    lax.fori_loop(0, nblk, block, 0)

    @pl.when(s == nseg - 1)
    def _():
        st_ref[0, 0] = state[...]


def hgrn2_scan(p, lb, s0_t, seg_rows):
    b, seq, _ = p.shape
    nseg = seq // seg_rows
    tri, ones, headb, headf = _hg_constants()
    seg = lambda d, s: jnp.where(d == 0, s, nseg - 1 - s)
    col = lambda c: pl.BlockSpec((1, seg_rows, D_HG), lambda bi, d, s: (bi, seg(d, s), c // D_HG))
    scr = lambda: pltpu.VMEM((seg_rows, D_HG), F32)
    return pl.pallas_call(
        _hgrn_kernel,
        grid=(b, 2, nseg),
        in_specs=[
            col(COL_HQ),
            pl.BlockSpec((1, seg_rows, D_HG), lambda bi, d, s: (bi, seg(d, s), COL_HFF // D_HG + d)),
            col(COL_HI),
            pl.BlockSpec((1, 1, D_HG), lambda bi, d, s: (d, 0, 0)),
            pl.BlockSpec((1, HG_TILE, HG_TILE), lambda bi, d, s: (d, 0, 0)),
            _const_spec((HG_TILE, HG_TILE)),
            _const_spec((D_HG, D_HG)),
            _const_spec((D_HG, D_HG)),
            pl.BlockSpec((1, 1, D_HG, D_HG), lambda bi, d, s: (bi, d, 0, 0)),
        ],
        out_specs=[
            pl.BlockSpec((1, 1, seg_rows, D_HG), lambda bi, d, s: (d, bi, seg(d, s), 0)),
            pl.BlockSpec((1, 1, D_HG, D_HG), lambda bi, d, s: (bi, d, 0, 0)),
        ],
        out_shape=[jax.ShapeDtypeStruct((2, b, seq, D_HG), F32), jax.ShapeDtypeStruct((b, 2, D_HG, D_HG), F32)],
        scratch_shapes=[pltpu.VMEM((D_HG, D_HG), F32)] + [scr() for _ in range(7)],
        compiler_params=_cparams(("parallel", "parallel", "arbitrary")),
        name="hgrn2_scan",
    )(p, p, p, lb.reshape(2, 1, D_HG), tri, ones, headb, headf, s0_t)


def _hg_post_kernel(of_ref, ob_ref, g_ref, nw_ref, headb_ref, o_ref):
    o = of_ref[0] + ob_ref[0]
    sq = o * o
    hi = _bf(sq)
    lo = _bf(sq - hi.astype(F32))
    ms = (_dot(hi, headb_ref[...]) + _dot(lo, headb_ref[...])) * (1.0 / HG_DV)
    o_ref[...] = o * lax.rsqrt(ms + EPS) * nw_ref[...] * _silu(g_ref[...])


def hgrn2_output(o_dirs, p2d, norm_w):
    t = p2d.shape[0]
    tm = ROW_TILE
    _, _, headb, _ = _hg_constants()
    return pl.pallas_call(
        _hg_post_kernel,
        grid=(t // tm,),
        in_specs=[
            pl.BlockSpec((1, tm, D_HG), lambda i: (0, i, 0)),
            pl.BlockSpec((1, tm, D_HG), lambda i: (1, i, 0)),
            pl.BlockSpec((tm, D_HG), lambda i: (i, COL_HGATE // D_HG)),
            _const_spec((1, D_HG)),
            _const_spec((D_HG, D_HG)),
        ],
        out_specs=pl.BlockSpec((tm, D_HG), lambda i: (i, 0)),
        out_shape=jax.ShapeDtypeStruct((t, D_HG), F32),
        compiler_params=_cparams(("parallel",)),
        name="hgrn2_output",
    )(o_dirs, o_dirs, p2d, norm_w.reshape(1, D_HG), headb)


def _hg_state_to_t(s):
    b = s.shape[0]
    st = jnp.swapaxes(s, -1, -2)
    eye = jnp.eye(HG_HEADS, dtype=s.dtype)
    full = eye[None, None, :, None, :, None] * st[:, :, :, :, None, :]
    return full.reshape(b, 2, D_HG, D_HG)


def _hg_state_from_t(st):
    b = st.shape[0]
    r = st.reshape(b, 2, HG_HEADS, HG_DV, HG_HEADS, HG_DK)
    blocks = jnp.stack([r[:, :, h, :, h, :] for h in range(HG_HEADS)], axis=2)
    return jnp.swapaxes(blocks, -1, -2)


HY_COLS = 2 * HY_ORDER * D_HY
HY_FEAT_PAD = 128
HY_GEN_ROWS = 256


def _dot_hi(a, b):
    ah = _bf(a)
    al = _bf(a - ah.astype(F32))
    bh = _bf(b)
    bl = _bf(b - bh.astype(F32))
    return _dot(ah, bh) + _dot(ah, bl) + _dot(al, bh)


def _filter_gen_kernel(feat_ref, w1_ref, b1_ref, w2_ref, b2_ref, w3_ref, fr_ref, dl_ref, h_ref, ssq_ref):
    i = pl.program_id(0)
    feat = feat_ref[...]
    fr = fr_ref[...]
    h = jnp.sin(fr * (_dot_hi(feat, w1_ref[...]) + b1_ref[...]))
    h = jnp.sin(fr * (_dot_hi(h, w2_ref[...]) + b2_ref[...]))
    h = _dot_hi(h, w3_ref[...])
    decay = jnp.exp(-feat[:, 0:1] * dl_ref[...])
    h = h * jnp.concatenate([decay] * (HY_COLS // D_HY), axis=1)
    h_ref[...] = h

    @pl.when(i == 0)
    def _():
        ssq_ref[...] = jnp.zeros_like(ssq_ref)

    ssq_ref[...] += jnp.sum(h * h, axis=0, keepdims=True)


def _hyena_features(seq):
    pos = np.arange(seq, dtype=np.float32)
    t = pos / np.float32(max(seq - 1, 1))
    bands = np.linspace(1e-4, HY_BANDS - 1, HY_BANDS, dtype=np.float32)
    ang = np.float32(2.0 * math.pi / seq) * pos[:, None] * bands[None, :]
    feat = np.concatenate([t[:, None], np.cos(ang), np.sin(ang)], axis=-1).astype(np.float32)
    out = np.zeros((seq, HY_FEAT_PAD), np.float32)
    out[:, :HY_EMB] = feat
    return jnp.asarray(out)


def hyena_filter_gen(seq, w1, b1, w2, b2, w3, freq):
    feat = _hyena_features(seq)
    w1p = jnp.zeros((HY_FEAT_PAD, HY_FFN), F32).at[:HY_EMB].set(w1.astype(F32))
    deltas = jnp.asarray(np.linspace(HY_DECAY_SLOW, HY_DECAY_FAST, D_HY, dtype=np.float32)).reshape(1, D_HY)
    tm = HY_GEN_ROWS
    return pl.pallas_call(
        _filter_gen_kernel,
        grid=(seq // tm,),
        in_specs=[
            pl.BlockSpec((tm, HY_FEAT_PAD), lambda i: (i, 0)),
            _const_spec((HY_FEAT_PAD, HY_FFN)), _const_spec((1, HY_FFN)),
            _const_spec((HY_FFN, HY_FFN)), _const_spec((1, HY_FFN)),
            _const_spec((HY_FFN, HY_COLS)), _const_spec((1, HY_FFN)), _const_spec((1, D_HY)),
        ],
        out_specs=[pl.BlockSpec((tm, HY_COLS), lambda i: (i, 0)), _const_spec((1, HY_COLS))],
        out_shape=[jax.ShapeDtypeStruct((seq, HY_COLS), F32), jax.ShapeDtypeStruct((1, HY_COLS), F32)],
        compiler_params=_cparams(("arbitrary",)),
        name="hyena_filter_gen",
    )(feat, w1p, b1.reshape(1, HY_FFN), w2, b2.reshape(1, HY_FFN), w3, freq.reshape(1, HY_FFN), deltas)


def _circular_filters(h_raw):
    seq = h_raw.shape[0]
    h = h_raw.reshape(seq, 2, HY_ORDER, D_HY)
    hf = h[:, 0].transpose(1, 0, 2)
    hb = h[:, 1].transpose(1, 0, 2)
    return jnp.concatenate([hf, jnp.zeros((HY_ORDER, 1, D_HY), F32), jnp.flip(hb[:, 1:], axis=1)], axis=1)


def _filter_norm(ssq_ref, n):
    tot = ssq_ref[:, n * D_HY:(n + 1) * D_HY] + ssq_ref[:, (HY_ORDER + n) * D_HY:(HY_ORDER + n + 1) * D_HY]
    return lax.rsqrt(tot)


def _conv3(pad_ref, w_ref, b_ref, r0, rows, pad):
    acc = b_ref[...]
    for k in range(HY_SHORT):
        lo = r0 + pad - HY_SHORT // 2 + k
        acc = acc + w_ref[k:k + 1, :] * pad_ref[lo:lo + rows, :]
    return acc


def _dft_constants(seq):
    n = 2 * seq
    k = np.arange(n)
    ang = 2.0 * np.pi * np.outer(k, k) / n
    fwd = np.concatenate([np.cos(ang), -np.sin(ang)], axis=0)
    inv = np.concatenate([np.cos(ang[:seq]), -np.sin(ang[:seq])], axis=1) / n
    return jnp.asarray(fwd, BF16), jnp.asarray(inv, BF16)


def _ctx_spectrum_kernel(filt_ref, ssq_ref, fwd_ref, h_ref):
    n = pl.program_id(0)
    norm = jnp.where(n == 0, _filter_norm(ssq_ref, 0), _filter_norm(ssq_ref, 1))
    x = filt_ref[0] * norm
    xh = _bf(x)
    xl = _bf(x - xh.astype(F32))
    h_ref[0] = _dot(fwd_ref[...], xh) + _dot(fwd_ref[...], xl)


def hyena_ctx_spectrum(filt, ssq):
    n = filt.shape[1]
    fwd, _ = _dft_constants(n // 2)
    return pl.pallas_call(
        _ctx_spectrum_kernel,
        grid=(HY_ORDER,),
        in_specs=[pl.BlockSpec((1, n, D_HY), lambda o: (o, 0, 0)), _const_spec((1, HY_COLS)),
                  _const_spec((2 * n, n))],
        out_specs=pl.BlockSpec((1, 2 * n, D_HY), lambda o: (o, 0, 0)),
        out_shape=jax.ShapeDtypeStruct((HY_ORDER, 2 * n, D_HY), F32),
        compiler_params=_cparams(("parallel",)),
        name="hyena_ctx_spectrum",
    )(filt, ssq, fwd)


HY_PAD = 8


def _hyena_ctx_kernel(v_ref, x1_ref, x2_ref, cw_ref, cb_ref, bias_ref, spec_ref, fwd_ref, inv_ref, o_ref, pad_ref):
    seq = v_ref.shape[1]
    n = 2 * seq
    zeros = jnp.zeros((HY_PAD, D_HY), F32)
    pad_ref[0:HY_PAD, :] = zeros
    pad_ref[HY_PAD + seq:2 * HY_PAD + seq, :] = zeros

    def short_conv(src_ref, part):
        pad_ref[HY_PAD:HY_PAD + seq, :] = src_ref[0]
        return _conv3(pad_ref, cw_ref.at[part], cb_ref.at[part], 0, seq, HY_PAD)

    z = short_conv(v_ref, 0)
    for order, gate_ref in enumerate((x1_ref, x2_ref)):
        spec = _dot(fwd_ref[:, 0:seq], _bf(z))
        xr, xi = spec[:n], spec[n:]
        hr, hi = spec_ref[order, 0:n, :], spec_ref[order, n:2 * n, :]
        prod = jnp.concatenate([xr * hr - xi * hi, xr * hi + xi * hr], axis=0)
        conv = _dot(inv_ref[...], _bf(prod))
        z = short_conv(gate_ref, order + 1) * (conv + bias_ref[order:order + 1, :] * z)
    o_ref[0] = z


def hyena_ctx(p, conv_w, conv_b, bias, spec):
    b, seq, _ = p.shape
    fwd, inv = _dft_constants(seq)
    col = lambda c: pl.BlockSpec((1, seq, D_HY), lambda i: (i, 0, c))
    c0 = COL_HY // D_HY
    return pl.pallas_call(
        _hyena_ctx_kernel,
        grid=(b,),
        in_specs=[
            col(c0), col(c0 + 1), col(c0 + 2),
            _const_spec((3, HY_SHORT, D_HY)), _const_spec((3, 1, D_HY)), _const_spec((HY_ORDER, D_HY)),
            _const_spec((HY_ORDER, 4 * seq, D_HY)), _const_spec((4 * seq, 2 * seq)), _const_spec((seq, 4 * seq)),
        ],
        out_specs=pl.BlockSpec((1, seq, D_HY), lambda i: (i, 0, 0)),
        out_shape=jax.ShapeDtypeStruct((b, seq, D_HY), F32),
        scratch_shapes=[pltpu.VMEM((seq + 2 * HY_PAD, D_HY), F32)],
        compiler_params=_cparams(("parallel",)),
        name="hyena_ctx",
    )(p, p, p, conv_w.reshape(HY_SHORT, 3, D_HY).transpose(1, 0, 2), conv_b.reshape(3, 1, D_HY), bias, spec,
      fwd, inv)


FFT_N1 = 64
FFT_N2 = 128
FFT_N = FFT_N1 * FFT_N2
HY_LANES = 128


def _fft_constants():
    n1, n2, n = FFT_N1, FFT_N2, FFT_N
    w1 = np.exp(-2j * np.pi * np.outer(np.arange(n1), np.arange(n1)) / n1)
    tw = np.exp(-2j * np.pi * np.outer(np.arange(n2), np.arange(n1)) / n)
    f1c = tw[:, :, None] * w1[None, :, :]
    f1t = np.concatenate([f1c.real, f1c.imag], axis=1).transpose(0, 2, 1)
    w2 = np.exp(-2j * np.pi * np.outer(np.arange(n2), np.arange(n2)) / n2)
    f2 = np.block([[w2.real, -w2.imag], [w2.imag, w2.real]])
    g2c = np.conj(w2)
    g2 = np.block([[g2c.real, -g2c.imag], [g2c.imag, g2c.real]])
    mc = np.conj(w1.T)[None, :n1 // 2, :] * np.conj(tw)[:, None, :] / n
    g1 = np.concatenate([mc.real, -mc.imag], axis=2)
    return (jnp.asarray(f1t, BF16), jnp.asarray(f2, BF16), jnp.asarray(g2, BF16), jnp.asarray(g1, BF16))


def _dot_tn(a, b):
    return lax.dot_general(a, b, (((0,), (0,)), ((), ())), preferred_element_type=F32)


def _fft_stage1(x_ref, f1t_ref, a_ref, n1_in):
    def body(n2, carry):
        rows = x_ref[pl.ds(n2, n1_in, stride=FFT_N2), :]
        out = _dot_tn(f1t_ref[n2, 0:n1_in, :], _bf(rows))
        a_ref[0, pl.ds(n2, FFT_N1, stride=FFT_N2), :] = out[:FFT_N1]
        a_ref[1, pl.ds(n2, FFT_N1, stride=FFT_N2), :] = out[FFT_N1:]
        return carry
    lax.fori_loop(0, FFT_N2, body, 0)


def _lat_spectrum_kernel(filt_ref, ssq_ref, f1t_ref, f2_ref, h_ref, a_ref):
    n = pl.program_id(0)
    half = pl.program_id(1)
    norm_full = jnp.where(n == 0, _filter_norm(ssq_ref, 0), _filter_norm(ssq_ref, 1))
    norm = jnp.where(half == 0, norm_full[:, :HY_LANES], norm_full[:, HY_LANES:])
    _fft_stage1(filt_ref.at[0], f1t_ref, a_ref, FFT_N1)

    def stage2(k1, carry):
        r0 = pl.multiple_of(k1 * FFT_N2, FFT_N2)
        rows = pl.ds(r0, FFT_N2)
        a = jnp.concatenate([a_ref[0, rows, :], a_ref[1, rows, :]], axis=0)
        b = _dot(f2_ref[...], _bf(a)) * norm
        h_ref[0, 0, rows, :] = _bf(b[:FFT_N2])
        h_ref[0, 1, rows, :] = _bf(b[FFT_N2:])
        return carry
    lax.fori_loop(0, FFT_N1, stage2, 0)


def hyena_lat_spectrum(filt, ssq):
    f1t, f2, _, _ = _fft_constants()
    return pl.pallas_call(
        _lat_spectrum_kernel,
        grid=(HY_ORDER, D_HY // HY_LANES),
        in_specs=[
            pl.BlockSpec((1, FFT_N, HY_LANES), lambda o, h: (o, 0, h)),
            _const_spec((1, HY_COLS)),
            _const_spec((FFT_N2, FFT_N1, 2 * FFT_N1)),
            _const_spec((2 * FFT_N2, 2 * FFT_N2)),
        ],
        out_specs=pl.BlockSpec((1, 2, FFT_N, HY_LANES), lambda o, h: (o, 0, 0, h)),
        out_shape=jax.ShapeDtypeStruct((HY_ORDER, 2, FFT_N, D_HY), BF16),
        scratch_shapes=[pltpu.VMEM((2, FFT_N, HY_LANES), F32)],
        compiler_params=_cparams(("parallel", "parallel")),
        name="hyena_lat_spectrum",
    )(filt, ssq, f1t, f2)


HY_ROWS = 512


def _hyena_lat_kernel(v_ref, x1_ref, x2_ref, cw_ref, cb_ref, bias_ref, spec_ref, f1t_ref, f2_ref, g2_ref, g1_ref,
                      o_ref, pad_ref, z_ref, a_ref):
    seq = v_ref.shape[1]
    zeros = jnp.zeros((HY_PAD, HY_LANES), F32)
    pad_ref[0:HY_PAD, :] = zeros
    pad_ref[HY_PAD + seq:2 * HY_PAD + seq, :] = zeros
    pad_ref[HY_PAD:HY_PAD + seq, :] = v_ref[0]
    for ti in range(seq // HY_ROWS):
        r0 = ti * HY_ROWS
        z_ref[r0:r0 + HY_ROWS, :] = _conv3(pad_ref, cw_ref.at[0], cb_ref.at[0], r0, HY_ROWS, HY_PAD)

    for order, gate_ref in enumerate((x1_ref, x2_ref)):
        _fft_stage1(z_ref, f1t_ref, a_ref, FFT_N1 // 2)

        def stage2(k1, carry):
            r0 = pl.multiple_of(k1 * FFT_N2, FFT_N2)
            rows = pl.ds(r0, FFT_N2)
            a = jnp.concatenate([a_ref[0, rows, :], a_ref[1, rows, :]], axis=0)
            b = _dot(f2_ref[...], _bf(a))
            br, bi = b[:FFT_N2], b[FFT_N2:]
            hr = spec_ref[order, 0, rows, :].astype(F32)
            hi = spec_ref[order, 1, rows, :].astype(F32)
            y = jnp.concatenate([br * hr - bi * hi, br * hi + bi * hr], axis=0)
            c = _dot(g2_ref[...], _bf(y))
            a_ref[0, rows, :] = c[:FFT_N2]
            a_ref[1, rows, :] = c[FFT_N2:]
            return carry
        lax.fori_loop(0, FFT_N1, stage2, 0)

        def stage1_inv(n2, carry):
            c = jnp.concatenate([a_ref[0, pl.ds(n2, FFT_N1, stride=FFT_N2), :],
                                 a_ref[1, pl.ds(n2, FFT_N1, stride=FFT_N2), :]], axis=0)
            o_ref[0, pl.ds(n2, FFT_N1 // 2, stride=FFT_N2), :] = _dot(g1_ref[n2], _bf(c))
            return carry
        lax.fori_loop(0, FFT_N2, stage1_inv, 0)

        pad_ref[HY_PAD:HY_PAD + seq, :] = gate_ref[0]
        for ti in range(seq // HY_ROWS):
            r0 = ti * HY_ROWS
            rows = slice(r0, r0 + HY_ROWS)
            gate = _conv3(pad_ref, cw_ref.at[order + 1], cb_ref.at[order + 1], r0, HY_ROWS, HY_PAD)
            z = gate * (o_ref[0, rows, :] + bias_ref[order:order + 1, :] * z_ref[rows, :])
            if order + 1 < HY_ORDER:
                z_ref[rows, :] = z
            else:
                o_ref[0, rows, :] = z


def hyena_lat(p, conv_w, conv_b, bias, spec):
    b, seq, _ = p.shape
    assert 2 * seq == FFT_N
    f1t, f2, g2, g1 = _fft_constants()
    hw = HY_LANES
    col = lambda c: pl.BlockSpec((1, seq, hw), lambda i, h: (i, 0, (COL_HY + c * D_HY) // hw + h))
    return pl.pallas_call(
        _hyena_lat_kernel,
        grid=(b, D_HY // hw),
        in_specs=[
            col(0), col(1), col(2),
            pl.BlockSpec((3, HY_SHORT, hw), lambda i, h: (0, 0, h)),
            pl.BlockSpec((3, 1, hw), lambda i, h: (0, 0, h)),
            pl.BlockSpec((HY_ORDER, hw), lambda i, h: (0, h)),
            pl.BlockSpec((HY_ORDER, 2, FFT_N, hw), lambda i, h: (0, 0, 0, h)),
            _const_spec((FFT_N2, FFT_N1, 2 * FFT_N1)),
            _const_spec((2 * FFT_N2, 2 * FFT_N2)),
            _const_spec((2 * FFT_N2, 2 * FFT_N2)),
            _const_spec((FFT_N2, FFT_N1 // 2, 2 * FFT_N1)),
        ],
        out_specs=pl.BlockSpec((1, seq, hw), lambda i, h: (i, 0, h)),
        out_shape=jax.ShapeDtypeStruct((b, seq, D_HY), F32),
        scratch_shapes=[
            pltpu.VMEM((seq + 2 * HY_PAD, hw), F32),
            pltpu.VMEM((seq, hw), F32),
            pltpu.VMEM((2, FFT_N, hw), F32),
        ],
        compiler_params=_cparams(("parallel", "parallel")),
        name="hyena_lat",
    )(p, p, p, conv_w.reshape(HY_SHORT, 3, D_HY).transpose(1, 0, 2), conv_b.reshape(3, 1, D_HY), bias, spec,
      f1t, f2, g2, g1)


HG_SEG_ROWS = 512
COND_ROWS = 16


def _trunk_layer(x, mods, lp, lb, batch, seq, ctx, final):
    x = ffn_half_step(x, mods, lp['ln_ffn1'], lp['w_ffn1_in'], lp['w_ffn1_out'], lp['final_norm'],
                      mod_base=0, final=False, seq_len=seq)
    p2d = mixer_in_proj(x, mods, lp['ln_mix'], lp['w_in'], seq_len=seq)
    p = p2d.reshape(batch, seq, N_IN)

    h_raw, ssq = hyena_filter_gen(seq, lp['hy_w1'], lp['hy_b1'], lp['hy_w2'], lp['hy_b2'], lp['hy_w3'],
                                  lp['hy_freq'])
    filt = _circular_filters(h_raw)
    if ctx is None:
        att = context_attention(p, lp['attn_sink'])
        lru_h0 = jnp.zeros((batch, 2, D_LRU), F32)
        hg_s0 = jnp.zeros((batch, 2, D_HG, D_HG), F32)
        hy = hyena_ctx(p, lp['hy_conv_w'], lp['hy_conv_b'], lp['hy_bias'], hyena_ctx_spectrum(filt, ssq))
    else:
        k_ctx, v_ctx, lru_h0, hg_state = ctx
        att = latent_attention(p, k_ctx, v_ctx, lp['attn_sink'])
        hg_s0 = _hg_state_to_t(hg_state.astype(F32))
        hy = hyena_lat(p, lp['hy_conv_w'], lp['hy_conv_b'], lp['hy_bias'], hyena_lat_spectrum(filt, ssq))
    lru, lru_state = rglru_mixer(p, lp['lru_conv_w'], lp['lru_conv_b'], lp['lru_w_r'], lp['lru_b_r'],
                                 lp['lru_w_i'], lp['lru_b_i'], lp['lru_lambda'], lru_h0.astype(F32))
    o_dirs, hg_state_t = hgrn2_scan(p, lb, hg_s0, min(seq, HG_SEG_ROWS))
    t = batch * seq
    hg = hgrn2_output(o_dirs.reshape(2, t, D_HG), p2d, lp['hg_norm'])

    x = gated_merge(x, mods, lp['ln_mix'], att.reshape(t, D_ATTN), lru.reshape(t, D_LRU), hg,
                    hy.reshape(t, D_HY), lp['w_gate'], lp['w_bo_attn'], lp['w_bo_lru'], lp['w_bo_hg'],
                    lp['w_bo_hy'], lp['w_out'], seq_len=seq)
    x = ffn_half_step(x, mods, lp['ln_ffn2'], lp['w_ffn2_in'], lp['w_ffn2_out'], lp['final_norm'],
                      mod_base=6, final=final, seq_len=seq)
    state = None
    if ctx is None:
        k = p[:, :, COL_K:COL_K + D_KV].reshape(batch, seq, N_KV_HEADS, HEAD_DIM)
        v = p[:, :, COL_V:COL_V + D_KV].reshape(batch, seq, N_KV_HEADS, HEAD_DIM)
        state = (k, v, lru_state, _hg_state_from_t(hg_state_t))
    return x, state


def kernel(x_prompt, x_sample, cache_k, cache_v, state_lru, state_hgrn, c, c_ctx, ln_ffn1, ln_mix, ln_ffn2, w_ada, b_ada, w_ffn1_in, w_ffn1_out, w_ffn2_in, w_ffn2_out, w_in, attn_sink, lru_conv_w, lru_conv_b, lru_w_r, lru_b_r, lru_w_i, lru_b_i, lru_lambda, hg_lb_logits, hg_norm, hy_conv_w, hy_conv_b, hy_w1, hy_b1, hy_w2, hy_b2, hy_w3, hy_freq, hy_bias, w_bo_attn, w_bo_lru, w_bo_hg, w_bo_hy, w_gate, w_out, final_norm):
    batch, seq, _ = x_prompt.shape
    dec_batch, dec_seq, _ = x_sample.shape
    lb_soft = jax.nn.softmax(hg_lb_logits.astype(F32), axis=0)
    lb_all = jnp.cumsum(lb_soft, axis=0) - lb_soft[0]

    stacked = dict(ln_ffn1=ln_ffn1, ln_mix=ln_mix, ln_ffn2=ln_ffn2, w_ffn1_in=w_ffn1_in, w_ffn1_out=w_ffn1_out,
                   w_ffn2_in=w_ffn2_in, w_ffn2_out=w_ffn2_out, w_in=w_in, attn_sink=attn_sink,
                   lru_conv_w=lru_conv_w, lru_conv_b=lru_conv_b, lru_w_r=lru_w_r, lru_b_r=lru_b_r,
                   lru_w_i=lru_w_i, lru_b_i=lru_b_i, lru_lambda=lru_lambda, hg_norm=hg_norm,
                   hy_conv_w=hy_conv_w, hy_conv_b=hy_conv_b, hy_w1=hy_w1, hy_b1=hy_b1, hy_w2=hy_w2, hy_b2=hy_b2,
                   hy_w3=hy_w3, hy_freq=hy_freq, hy_bias=hy_bias, w_bo_attn=w_bo_attn, w_bo_lru=w_bo_lru,
                   w_bo_hg=w_bo_hg, w_bo_hy=w_bo_hy, w_gate=w_gate, w_out=w_out)

    cond = jnp.zeros((COND_ROWS, D_MODEL), F32).at[0].set(c_ctx).at[1:1 + dec_batch].set(c)
    mods = ada_mods(cond, w_ada, b_ada).reshape(DEPTH, COND_ROWS, N_MOD, D_MODEL)

    h = x_prompt.reshape(batch * seq, D_MODEL)
    z = x_sample.reshape(dec_batch * dec_seq, D_MODEL)
    ks, vs, lrus, hgs = [], [], [], []
    for l in range(DEPTH):
        lp = {name: w[l] for name, w in stacked.items()}
        lp['final_norm'] = final_norm
        final = l == DEPTH - 1
        h, (k_l, v_l, lru_l, hg_l) = _trunk_layer(h, mods[l, 0:1], lp, lb_all[l], batch, seq, None, final)
        ks.append(k_l)
        vs.append(v_l)
        lrus.append(lru_l)
        hgs.append(hg_l)
        ctx = (cache_k[:, l].reshape(dec_batch, -1, D_KV), cache_v[:, l].reshape(dec_batch, -1, D_KV),
               state_lru[:, l], state_hgrn[:, l])
        z, _ = _trunk_layer(z, mods[l, 1:1 + dec_batch], lp, lb_all[l], dec_batch, dec_seq, ctx, final)

    y_prompt = h.reshape(batch, seq, D_MODEL)
    y_sample = z.reshape(dec_batch, dec_seq, D_MODEL)
    return (y_prompt, y_sample, jnp.stack(ks, axis=1), jnp.stack(vs, axis=1), jnp.stack(lrus, axis=1),
            jnp.stack(hgs, axis=1))
```

```python
import functools
import math

import numpy as np
import jax
import jax.numpy as jnp
from jax import lax
from jax.experimental import pallas as pl
from jax.experimental.pallas import tpu as pltpu

F32 = jnp.float32
BF16 = jnp.bfloat16

D_MODEL = 1024
DEPTH = 2
GRID_W = 64
EPS = 1e-6
N_MOD = 9
N_BRANCH = 4
D_FF = 2816
N_HEADS = 8
N_KV_HEADS = 2
HEAD_DIM = 64
GQA_GROUP = N_HEADS // N_KV_HEADS
D_ATTN = N_HEADS * HEAD_DIM
D_KV = N_KV_HEADS * HEAD_DIM
WINDOW = 128
ATTN_BLOCK = 128
ATTN_SCALE = HEAD_DIM ** -0.5
ROPE_BASE = 10000.0
D_LRU = D_MODEL // 4
LRU_HEADS = 4
LRU_BLOCK = D_LRU // LRU_HEADS
LRU_CONV = 4
LRU_C = 8.0
HG_HEADS = 4
HG_DK = 64
HG_DV = 64
D_HG = HG_HEADS * HG_DK
HG_SCALE = HG_DK ** -0.5
HG_BLK = 16
D_HY = D_MODEL // 4
HY_ORDER = 2
HY_SHORT = 3
HY_BANDS = 8
HY_EMB = 2 * HY_BANDS + 1
HY_FFN = 64
HY_TARGET = 1e-2
HY_DECAY_SLOW = -math.log(HY_TARGET) / 1.5
HY_DECAY_FAST = -math.log(HY_TARGET) / 0.3
N_IN = D_ATTN + 2 * D_KV + 2 * D_LRU + 5 * D_HG + 3 * D_HY

COL_Q = 0
COL_K = D_ATTN
COL_V = D_ATTN + D_KV
COL_RX = D_ATTN + 2 * D_KV
COL_RG = COL_RX + D_LRU
COL_HQ = COL_RG + D_LRU
COL_HFF = COL_HQ + D_HG
COL_HFB = COL_HFF + D_HG
COL_HI = COL_HFB + D_HG
COL_HGATE = COL_HI + D_HG
COL_HY = COL_HGATE + D_HG

FF_CHUNK = 256
N_FF_CHUNK = D_FF // FF_CHUNK
ROW_TILE = 512
VMEM_LIMIT = 56 * 1024 * 1024


def _bf(x):
    return x.astype(BF16)


def _dot(a, b):
    return jnp.dot(a, b, preferred_element_type=F32)


def _split3(x):
    hi = _bf(x)
    r1 = x - hi.astype(F32)
    mid = _bf(r1)
    lo = _bf(r1 - mid.astype(F32))
    return hi, mid, lo


def _dot_exact_rhs(mat_bf, x):
    hi, mid, lo = _split3(x)
    return _dot(mat_bf, hi) + _dot(mat_bf, mid) + _dot(mat_bf, lo)


def _sigmoid(x):
    return jax.nn.sigmoid(x)


def _silu(x):
    return x * _sigmoid(x)


def _norm_mod(x, lnw, shift, scale):
    ms = jnp.mean(x * x, axis=-1, keepdims=True)
    y = x * lax.rsqrt(ms + EPS) * lnw
    return y * (1.0 + scale) + shift


def _cparams(sem):
    return pltpu.CompilerParams(dimension_semantics=sem, vmem_limit_bytes=VMEM_LIMIT)


def _const_spec(shape):
    nd = len(shape)
    return pl.BlockSpec(shape, lambda *_: (0,) * nd)


def _ada_kernel(cond_ref, w_ref, b_ref, o_ref):
    c = cond_ref[...]
    o_ref[0] = _dot(_bf(_silu(c)), _bf(w_ref[0])) + b_ref[0]


def ada_mods(cond, w_ada, b_ada):
    r = cond.shape[0]
    cb = 1024
    ncol = (N_MOD * D_MODEL) // cb
    return pl.pallas_call(
        _ada_kernel,
        grid=(DEPTH, ncol),
        in_specs=[
            pl.BlockSpec((r, D_MODEL), lambda l, j: (0, 0)),
            pl.BlockSpec((1, D_MODEL, cb), lambda l, j: (l, 0, j)),
            pl.BlockSpec((1, 1, cb), lambda l, j: (l, 0, j)),
        ],
        out_specs=pl.BlockSpec((1, r, cb), lambda l, j: (l, 0, j)),
        out_shape=jax.ShapeDtypeStruct((DEPTH, r, N_MOD * D_MODEL), F32),
        compiler_params=_cparams(("parallel", "parallel")),
        name="ada_mods",
    )(cond, w_ada, b_ada.reshape(DEPTH, 1, N_MOD * D_MODEL))


def _seq_mod_index(seq_len, n_mod, tm):
    def idx(i):
        return jnp.minimum((i * tm) // seq_len, n_mod - 1)
    return idx


def _ffn_kernel(x_ref, mod_ref, ln_ref, wg_ref, wu_ref, wo_ref, fn_ref, o_ref, acc_ref, *, mod_base, final):
    x = x_ref[...]
    m = mod_ref[0]
    sh = m[mod_base:mod_base + 1]
    sc = m[mod_base + 1:mod_base + 2]
    g = m[mod_base + 2:mod_base + 3]
    h = _bf(_norm_mod(x, ln_ref[...], sh, sc))
    acc_ref[...] = jnp.zeros_like(acc_ref)

    def body(c, carry):
        gate = _dot(h, wg_ref[c])
        up = _dot(h, wu_ref[c])
        act = _bf(_silu(gate) * up)
        acc_ref[...] += _dot(act, wo_ref[c])
        return carry

    lax.fori_loop(0, N_FF_CHUNK, body, 0)
    y = x + 0.5 * g * acc_ref[...]
    if final:
        ms = jnp.mean(y * y, axis=-1, keepdims=True)
        y = y * lax.rsqrt(ms + EPS) * fn_ref[...]
    o_ref[...] = y


def ffn_half_step(x, mods, ln_w, w_in, w_out, final_w, *, mod_base, final, seq_len):
    t = x.shape[0]
    tm = ROW_TILE
    wg = _bf(w_in[:, :D_FF]).reshape(D_MODEL, N_FF_CHUNK, FF_CHUNK).transpose(1, 0, 2)
    wu = _bf(w_in[:, D_FF:]).reshape(D_MODEL, N_FF_CHUNK, FF_CHUNK).transpose(1, 0, 2)
    wo = _bf(w_out).reshape(N_FF_CHUNK, FF_CHUNK, D_MODEL)
    midx = _seq_mod_index(seq_len, mods.shape[0], tm)
    return pl.pallas_call(
        functools.partial(_ffn_kernel, mod_base=mod_base, final=final),
        grid=(t // tm,),
        in_specs=[
            pl.BlockSpec((tm, D_MODEL), lambda i: (i, 0)),
            pl.BlockSpec((1, N_MOD, D_MODEL), lambda i: (midx(i), 0, 0)),
            _const_spec((1, D_MODEL)),
            _const_spec((N_FF_CHUNK, D_MODEL, FF_CHUNK)),
            _const_spec((N_FF_CHUNK, D_MODEL, FF_CHUNK)),
            _const_spec((N_FF_CHUNK, FF_CHUNK, D_MODEL)),
            _const_spec((1, D_MODEL)),
        ],
        out_specs=pl.BlockSpec((tm, D_MODEL), lambda i: (i, 0)),
        out_shape=jax.ShapeDtypeStruct((t, D_MODEL), F32),
        scratch_shapes=[pltpu.VMEM((tm, D_MODEL), F32)],
        compiler_params=_cparams(("parallel",)),
        name="ffn_half_step",
    )(x, mods, ln_w.reshape(1, D_MODEL), wg, wu, wo, final_w.reshape(1, D_MODEL))


PROJ_CHUNK = 256


def _proj_kernel(x_ref, mod_ref, ln_ref, w_ref, o_ref):
    x = x_ref[...]
    m = mod_ref[0]
    u = _bf(_norm_mod(x, ln_ref[...], m[3:4], m[4:5]))
    for c in range(N_IN // PROJ_CHUNK):
        sl = slice(c * PROJ_CHUNK, (c + 1) * PROJ_CHUNK)
        o_ref[:, sl] = _dot(u, w_ref[:, sl])


def mixer_in_proj(x, mods, ln_w, w_in, *, seq_len):
    t = x.shape[0]
    tm = ROW_TILE
    midx = _seq_mod_index(seq_len, mods.shape[0], tm)
    return pl.pallas_call(
        _proj_kernel,
        grid=(t // tm,),
        in_specs=[
            pl.BlockSpec((tm, D_MODEL), lambda i: (i, 0)),
            pl.BlockSpec((1, N_MOD, D_MODEL), lambda i: (midx(i), 0, 0)),
            _const_spec((1, D_MODEL)),
            _const_spec((D_MODEL, N_IN)),
        ],
        out_specs=pl.BlockSpec((tm, N_IN), lambda i: (i, 0)),
        out_shape=jax.ShapeDtypeStruct((t, N_IN), F32),
        compiler_params=_cparams(("parallel",)),
        name="mixer_in_proj",
    )(x, mods, ln_w.reshape(1, D_MODEL), _bf(w_in))


MERGE_CHUNK = 256


def _merge_kernel(x_ref, mod_ref, ln_ref, att_ref, lru_ref, hg_ref, hy_ref,
                  wg_ref, wa_ref, wl_ref, wh_ref, wy_ref, wo_ref, o_ref, mix_ref):
    x = x_ref[...]
    m = mod_ref[0]
    u = _bf(_norm_mod(x, ln_ref[...], m[3:4], m[4:5]))
    branches = (_bf(att_ref[...]), _bf(lru_ref[...]), _bf(hg_ref[...]), _bf(hy_ref[...]))
    w_bo = (wa_ref, wl_ref, wh_ref, wy_ref)
    for c in range(D_MODEL // MERGE_CHUNK):
        sl = slice(c * MERGE_CHUNK, (c + 1) * MERGE_CHUNK)
        mixed = None
        for n in range(N_BRANCH):
            term = _sigmoid(_dot(u, wg_ref[n, :, sl])) * _dot(branches[n], w_bo[n][:, sl])
            mixed = term if mixed is None else mixed + term
        mix_ref[:, sl] = _bf(mixed)
    o_ref[...] = x + m[5:6] * _dot(mix_ref[...], wo_ref[...])


def gated_merge(x, mods, ln_w, att, lru, hg, hy, w_gate, w_bo_attn, w_bo_lru, w_bo_hg, w_bo_hy, w_out,
                *, seq_len):
    t = x.shape[0]
    tm = ROW_TILE
    midx = _seq_mod_index(seq_len, mods.shape[0], tm)
    row = lambda w: pl.BlockSpec((tm, w), lambda i: (i, 0))
    return pl.pallas_call(
        _merge_kernel,
        grid=(t // tm,),
        in_specs=[
            row(D_MODEL),
            pl.BlockSpec((1, N_MOD, D_MODEL), lambda i: (midx(i), 0, 0)),
            _const_spec((1, D_MODEL)),
            row(D_ATTN), row(D_LRU), row(D_HG), row(D_HY),
            _const_spec((N_BRANCH, D_MODEL, D_MODEL)),
            _const_spec((D_ATTN, D_MODEL)),
            _const_spec((D_LRU, D_MODEL)),
            _const_spec((D_HG, D_MODEL)),
            _const_spec((D_HY, D_MODEL)),
            _const_spec((D_MODEL, D_MODEL)),
        ],
        out_specs=row(D_MODEL),
        out_shape=jax.ShapeDtypeStruct((t, D_MODEL), F32),
        scratch_shapes=[pltpu.VMEM((tm, D_MODEL), BF16)],
        compiler_params=_cparams(("parallel",)),
        name="gated_merge",
    )(x, mods, ln_w.reshape(1, D_MODEL), att, lru, hg, hy,
      _bf(w_gate), _bf(w_bo_attn), _bf(w_bo_lru), _bf(w_bo_hg), _bf(w_bo_hy), _bf(w_out))


def _sink_column(sink_ref, kvh, rows):
    cols = [jnp.broadcast_to(sink_ref[kvh * GQA_GROUP + g:kvh * GQA_GROUP + g + 1, 0:1], (rows, 1))
            for g in range(GQA_GROUP)]
    return jnp.concatenate(cols, axis=0)


def _stack_group(q, kvh):
    return jnp.concatenate(
        [q[:, (kvh * GQA_GROUP + g) * HEAD_DIM:(kvh * GQA_GROUP + g + 1) * HEAD_DIM] for g in range(GQA_GROUP)],
        axis=0)


def _dot_nt(a, b):
    return lax.dot_general(a, b, (((1,), (1,)), ((), ())), preferred_element_type=F32)


def _ctx_attn_kernel(q_ref, k_ref, v_ref, sink_ref, o_ref):
    rows = q_ref.shape[1]
    q = q_ref[0] * ATTN_SCALE
    k = k_ref[0]
    v = v_ref[0]
    for kvh in range(N_KV_HEADS):
        hs = slice(kvh * HEAD_DIM, (kvh + 1) * HEAD_DIM)
        s = _dot_nt(_bf(_stack_group(q, kvh)), _bf(k[:, hs]))
        sink = _sink_column(sink_ref, kvh, rows)
        m = jnp.maximum(jnp.max(s, axis=-1, keepdims=True), sink)
        p = jnp.exp(s - m)
        den = jnp.sum(p, axis=-1, keepdims=True) + jnp.exp(sink - m)
        o = _dot(_bf(p), _bf(v[:, hs])) / den
        for g in range(GQA_GROUP):
            head = kvh * GQA_GROUP + g
            o_ref[0, :, head * HEAD_DIM:(head + 1) * HEAD_DIM] = o[g * rows:(g + 1) * rows]


def _sink_table(sink):
    return jnp.broadcast_to(sink.astype(F32)[:, None], (N_HEADS, 128))


def context_attention(p_ctx, sink):
    b, seq, _ = p_ctx.shape
    return pl.pallas_call(
        _ctx_attn_kernel,
        grid=(b,),
        in_specs=[
            pl.BlockSpec((1, seq, D_ATTN), lambda i: (i, 0, 0)),
            pl.BlockSpec((1, seq, D_KV), lambda i: (i, 0, COL_K // D_KV)),
            pl.BlockSpec((1, seq, D_KV), lambda i: (i, 0, COL_V // D_KV)),
            _const_spec((N_HEADS, 128)),
        ],
        out_specs=pl.BlockSpec((1, seq, D_ATTN), lambda i: (i, 0, 0)),
        out_shape=jax.ShapeDtypeStruct((b, seq, D_ATTN), F32),
        compiler_params=_cparams(("parallel",)),
        name="context_attention",
    )(p_ctx, p_ctx, p_ctx, _sink_table(sink))


def _rope(x, cos, sin_signed):
    n = x.shape[-1]
    lane = lax.broadcasted_iota(jnp.int32, x.shape, x.ndim - 1)
    first = (lane % (HEAD_DIM // 2)) < (HEAD_DIM // 4)
    partner = jnp.where(first, pltpu.roll(x, n - HEAD_DIM // 4, axis=x.ndim - 1),
                        pltpu.roll(x, HEAD_DIM // 4, axis=x.ndim - 1))
    return x * cos + partner * sin_signed


def _rope_tables(seq):
    rows = seq // GRID_W
    row = np.repeat(np.arange(rows, dtype=np.float32), GRID_W)
    col = np.tile(np.arange(GRID_W, dtype=np.float32), rows)
    n = HEAD_DIM // 4
    inv = (ROPE_BASE ** (-jnp.arange(n, dtype=F32) / n))
    ang_r = jnp.asarray(row)[:, None] * inv
    ang_c = jnp.asarray(col)[:, None] * inv
    cos = jnp.concatenate([jnp.cos(ang_r), jnp.cos(ang_r), jnp.cos(ang_c), jnp.cos(ang_c)], axis=-1)
    sin = jnp.concatenate([-jnp.sin(ang_r), jnp.sin(ang_r), -jnp.sin(ang_c), jnp.sin(ang_c)], axis=-1)
    return jnp.tile(cos, (1, 2)), jnp.tile(sin, (1, 2))


ATTN_PREP_ROWS = 512


def _dup_heads(x):
    lane = lax.broadcasted_iota(jnp.int32, x.shape, 1)
    swapped = pltpu.roll(x, HEAD_DIM, axis=1)
    low = lane < HEAD_DIM
    return _bf(jnp.where(low, x, swapped)), _bf(jnp.where(low, swapped, x))


def _values_t(x):
    xt = jnp.transpose(x)
    ones = jnp.ones((HEAD_DIM, x.shape[0]), F32)
    return (_bf(jnp.concatenate([xt[:HEAD_DIM], ones], axis=0)),
            _bf(jnp.concatenate([xt[HEAD_DIM:], ones], axis=0)))


def _lat_attn_kernel(q_ref, k_ref, v_ref, kc_ref, vc_ref, cos_ref, sin_ref, sink_ref, o_ref,
                     kd_ref, vt_ref, kcd_ref, vct_ref):
    seq = k_ref.shape[1]
    blk = ATTN_BLOCK
    band = 3 * blk
    past = kc_ref.shape[1]
    i = pl.program_id(1)

    @pl.when(i == 0)
    def _():
        for ti in range(seq // ATTN_PREP_ROWS):
            rows = slice(ti * ATTN_PREP_ROWS, (ti + 1) * ATTN_PREP_ROWS)
            k0, k1 = _dup_heads(_rope(k_ref[0, rows, :], cos_ref[rows, :], sin_ref[rows, :]))
            kd_ref[0, rows, :] = k0
            kd_ref[1, rows, :] = k1
            v0, v1 = _values_t(v_ref[0, rows, :])
            vt_ref[0, :, rows] = v0
            vt_ref[1, :, rows] = v1
        kc0, kc1 = _dup_heads(kc_ref[0])
        kcd_ref[0] = kc0
        kcd_ref[1] = kc1
        vc0, vc1 = _values_t(vc_ref[0])
        vct_ref[0] = vc0
        vct_ref[1] = vc1

    start = pl.multiple_of(jnp.clip((i - 1) * blk, 0, seq - band), blk)
    q0 = pl.multiple_of(i * blk, blk)
    cos_q = cos_ref[pl.ds(q0, blk), :]
    sin_q = sin_ref[pl.ds(q0, blk), :]
    key = lax.broadcasted_iota(jnp.int32, (band + past, 2 * blk), 0)
    qry = lax.broadcasted_iota(jnp.int32, (band + past, 2 * blk), 1) % blk
    mask = (jnp.abs((q0 + qry) - (start + key)) <= WINDOW) | (key >= band)
    low = lax.broadcasted_iota(jnp.int32, (blk, 2 * HEAD_DIM), 1) < HEAD_DIM
    for pair in range(N_HEADS // 2):
        kvh = (2 * pair) // GQA_GROUP
        lanes = slice(pair * 2 * HEAD_DIM, (pair + 1) * 2 * HEAD_DIM)
        q = _rope(q_ref[0, :, lanes], cos_q, sin_q) * ATTN_SCALE
        q_heads = _bf(jnp.concatenate([jnp.where(low, q, 0.0), jnp.where(low, 0.0, q)], axis=0))
        keys = jnp.concatenate([kd_ref[kvh, pl.ds(start, band), :], kcd_ref[kvh]], axis=0)
        s = jnp.where(mask, _dot_nt(keys, q_heads), -jnp.inf)
        sink = jnp.concatenate([sink_ref[2 * pair:2 * pair + 1, :], sink_ref[2 * pair + 1:2 * pair + 2, :]], axis=1)
        m = jnp.maximum(jnp.max(s, axis=0, keepdims=True), sink)
        vals_t = jnp.concatenate([vt_ref[kvh, :, pl.ds(start, band)], vct_ref[kvh]], axis=1)
        acc = _dot(vals_t, _bf(jnp.exp(s - m)))
        den = acc[HEAD_DIM:HEAD_DIM + 1, :] + jnp.exp(sink - m)
        out_t = acc[:HEAD_DIM, :] / den
        o_ref[0, :, lanes] = jnp.transpose(jnp.concatenate([out_t[:, :blk], out_t[:, blk:]], axis=0))


def latent_attention(p_lat, k_ctx, v_ctx, sink):
    b, seq, _ = p_lat.shape
    past = k_ctx.shape[1]
    cos, sin = _rope_tables(seq)
    return pl.pallas_call(
        _lat_attn_kernel,
        grid=(b, seq // ATTN_BLOCK),
        in_specs=[
            pl.BlockSpec((1, ATTN_BLOCK, D_ATTN), lambda bi, i: (bi, i, 0)),
            pl.BlockSpec((1, seq, D_KV), lambda bi, i: (bi, 0, COL_K // D_KV)),
            pl.BlockSpec((1, seq, D_KV), lambda bi, i: (bi, 0, COL_V // D_KV)),
            pl.BlockSpec((1, past, D_KV), lambda bi, i: (bi, 0, 0)),
            pl.BlockSpec((1, past, D_KV), lambda bi, i: (bi, 0, 0)),
            _const_spec((seq, 128)),
            _const_spec((seq, 128)),
            _const_spec((N_HEADS, 128)),
        ],
        out_specs=pl.BlockSpec((1, ATTN_BLOCK, D_ATTN), lambda bi, i: (bi, i, 0)),
        out_shape=jax.ShapeDtypeStruct((b, seq, D_ATTN), F32),
        scratch_shapes=[
            pltpu.VMEM((N_KV_HEADS, seq, D_KV), BF16),
            pltpu.VMEM((N_KV_HEADS, D_KV, seq), BF16),
            pltpu.VMEM((N_KV_HEADS, past, D_KV), BF16),
            pltpu.VMEM((N_KV_HEADS, D_KV, past), BF16),
        ],
        compiler_params=_cparams(("parallel", "arbitrary")),
        name="latent_attention",
    )(p_lat, p_lat, p_lat, k_ctx, v_ctx, cos, sin, _sink_table(sink))


LRU_HALF = D_LRU // 2
LRU_ROWS = 256
LRU_PAD = 8


def _neg_expm1(y):
    series = -(y * (1.0 + y * (1.0 / 2 + y * (1.0 / 6 + y * (1.0 / 24 + y * (1.0 / 120 + y * (1.0 / 720)))))))
    return jnp.where(y > -0.25, series, 1.0 - jnp.exp(y))


def _softplus(x):
    return jnp.maximum(x, 0.0) + jnp.log1p(jnp.exp(-jnp.abs(x)))


def _gelu_tanh(x):
    return 0.5 * x * (1.0 + jnp.tanh(math.sqrt(2.0 / math.pi) * (x + 0.044715 * (x * x * x))))


LRU_SCAN = 8


def _affine_scan8(a, b, *, reverse):
    row = lax.broadcasted_iota(jnp.int32, a.shape, 0)
    k = 1
    while k < LRU_SCAN:
        if reverse:
            valid = row < LRU_SCAN - k
            shift = LRU_SCAN - k
        else:
            valid = row >= k
            shift = k
        a_prev = jnp.where(valid, pltpu.roll(a, shift, axis=0), 1.0)
        b_prev = jnp.where(valid, pltpu.roll(b, shift, axis=0), 0.0)
        b = b + a * b_prev
        a = a * a_prev
        k *= 2
    return a, b


def _lru_kernel(x_ref, g_ref, cw_ref, cb_ref, wr_ref, br_ref, wi_ref, bi_ref, lam_ref, h0_ref,
                o_ref, st_ref, pad_ref, a_ref, bx_ref, hb_ref):
    seq = x_ref.shape[1]
    zeros = jnp.zeros((LRU_PAD, LRU_HALF), F32)
    pad_ref[0:LRU_PAD, :] = zeros
    pad_ref[LRU_PAD + seq:2 * LRU_PAD + seq, :] = zeros
    pad_ref[LRU_PAD:LRU_PAD + seq, :] = x_ref[0]
    left = LRU_CONV // 2
    for ti in range(seq // LRU_ROWS):
        r0 = ti * LRU_ROWS
        xc = cb_ref[...]
        for k in range(LRU_CONV):
            xc = xc + cw_ref[k:k + 1, :] * pad_ref[r0 + LRU_PAD - left + k:r0 + LRU_PAD - left + k + LRU_ROWS, :]
        xcb = _bf(xc)
        for d in range(2):
            r = _sigmoid(_dot(xcb, wr_ref[d]) + br_ref[d:d + 1, :])
            i = _sigmoid(_dot(xcb, wi_ref[d]) + bi_ref[d:d + 1, :])
            log_a = (-LRU_C) * r * _softplus(-lam_ref[d:d + 1, :])
            a_ref[d, r0:r0 + LRU_ROWS, :] = jnp.exp(log_a)
            bx_ref[d, r0:r0 + LRU_ROWS, :] = jnp.sqrt(_neg_expm1(2.0 * log_a)) * (i * xc)

    nt = seq // LRU_SCAN

    def step(blk, carry):
        hf, hb = carry
        rf = pl.ds(pl.multiple_of(blk * LRU_SCAN, LRU_SCAN), LRU_SCAN)
        rb = pl.ds(pl.multiple_of((nt - 1 - blk) * LRU_SCAN, LRU_SCAN), LRU_SCAN)
        af, bf_ = _affine_scan8(a_ref[0, rf, :], bx_ref[0, rf, :], reverse=False)
        ab, bb = _affine_scan8(a_ref[1, rb, :], bx_ref[1, rb, :], reverse=True)
        tile_f = af * hf + bf_
        tile_b = ab * hb + bb
        o_ref[0, rf, :] = tile_f
        hb_ref[rb, :] = tile_b
        return tile_f[LRU_SCAN - 1:LRU_SCAN, :], tile_b[0:1, :]

    hf, hb = lax.fori_loop(0, nt, step, (h0_ref[0, 0:1, :], h0_ref[0, 1:2, :]), unroll=2)
    st_ref[0, 0:1, :] = hf
    st_ref[0, 1:2, :] = hb
    for ti in range(seq // LRU_ROWS):
        rows = slice(ti * LRU_ROWS, (ti + 1) * LRU_ROWS)
        o_ref[0, rows, :] = (o_ref[0, rows, :] + hb_ref[rows, :]) * _gelu_tanh(g_ref[0, rows, :])


def _block_diag(w):
    n, blk, _ = w.shape
    eye = jnp.eye(n, dtype=w.dtype)
    return (eye[:, None, :, None] * w[:, :, None, :]).reshape(n * blk, n * blk)


def rglru_mixer(p, conv_w, conv_b, w_r, b_r, w_i, b_i, lam, h0):
    b, seq, _ = p.shape
    wr = _bf(jnp.stack([_block_diag(w_r[d]) for d in range(2)]))
    wi = _bf(jnp.stack([_block_diag(w_i[d]) for d in range(2)]))
    hw = LRU_HALF
    vec = lambda rows: pl.BlockSpec((rows, hw), lambda bi, h: (0, h))
    return pl.pallas_call(
        _lru_kernel,
        grid=(b, 2),
        in_specs=[
            pl.BlockSpec((1, seq, hw), lambda bi, h: (bi, 0, COL_RX // hw + h)),
            pl.BlockSpec((1, seq, hw), lambda bi, h: (bi, 0, COL_RG // hw + h)),
            vec(LRU_CONV), vec(1),
            pl.BlockSpec((2, hw, hw), lambda bi, h: (0, h, h)), vec(2),
            pl.BlockSpec((2, hw, hw), lambda bi, h: (0, h, h)), vec(2),
            vec(2),
            pl.BlockSpec((1, 2, hw), lambda bi, h: (bi, 0, h)),
        ],
        out_specs=[
            pl.BlockSpec((1, seq, hw), lambda bi, h: (bi, 0, h)),
            pl.BlockSpec((1, 2, hw), lambda bi, h: (bi, 0, h)),
        ],
        out_shape=[jax.ShapeDtypeStruct((b, seq, D_LRU), F32), jax.ShapeDtypeStruct((b, 2, D_LRU), F32)],
        scratch_shapes=[
            pltpu.VMEM((seq + 2 * LRU_PAD, hw), F32),
            pltpu.VMEM((2, seq, hw), F32),
            pltpu.VMEM((2, seq, hw), F32),
            pltpu.VMEM((seq, hw), F32),
        ],
        compiler_params=_cparams(("parallel", "parallel")),
        name="rglru_mixer",
    )(p, p, conv_w, conv_b.reshape(1, D_LRU), wr, b_r, wi, b_i, lam, h0)


HG_TILE = 256


def _hg_constants():
    idx = np.arange(HG_TILE)
    same_blk = (idx[:, None] // HG_BLK) == (idx[None, :] // HG_BLK)
    tri_f = same_blk & (idx[None, :] <= idx[:, None])
    tri_b = same_blk & (idx[None, :] >= idx[:, None])
    head = (idx[:, None] // HG_DK) == (idx[None, :] // HG_DK)
    return (jnp.asarray(np.stack([tri_f, tri_b]), BF16), jnp.asarray(same_blk, BF16),
            jnp.asarray(head, BF16), jnp.asarray(head, F32))


def _hgrn_kernel(q_ref, z_ref, v_ref, lb_ref, tri_ref, ones_ref, headb_ref, headf_ref, s0_ref,
                 o_ref, st_ref, state, qs, ks, vs, cums, qds, kns, tots):
    seg_rows = q_ref.shape[1]
    d = pl.program_id(1)
    s = pl.program_id(2)
    nseg = pl.num_programs(2)

    @pl.when(s == 0)
    def _():
        state[...] = s0_ref[0, 0]

    lb = lb_ref[0]
    for ti in range(seg_rows // HG_TILE):
        rows = slice(ti * HG_TILE, (ti + 1) * HG_TILE)
        f = lb + (1.0 - lb) * _sigmoid(z_ref[0, rows, :])
        g = jnp.log(f)
        kk = 1.0 - f
        q = _silu(q_ref[0, rows, :]) * HG_SCALE
        cum = _dot_exact_rhs(tri_ref[0], g)
        tot = _dot_exact_rhs(ones_ref[...], g)
        qs[rows, :] = q
        ks[rows, :] = kk
        vs[rows, :] = v_ref[0, rows, :]
        cums[rows, :] = cum
        tots[rows, :] = tot
        qds[rows, :] = q * jnp.exp(cum)
        kns[rows, :] = kk * jnp.exp(tot - cum)

    nblk = seg_rows // HG_BLK
    ss = lax.broadcasted_iota(jnp.int32, (HG_BLK, HG_BLK, D_HG), 0)
    tt = lax.broadcasted_iota(jnp.int32, (HG_BLK, HG_BLK, D_HG), 1)
    keep = (tt - ss) * jnp.where(d == 0, 1, -1) >= 0

    def block(j, carry):
        je = jnp.where(d == 0, j, nblk - 1 - j)
        r0 = pl.multiple_of(je * HG_BLK, HG_BLK)
        rows = pl.ds(r0, HG_BLK)
        cum = cums[rows, :]
        vb = vs[rows, :]
        dec = jnp.exp(cum[None, :, :] - cum[:, None, :]) * qs[rows, :][None, :, :] * ks[rows, :][:, None, :]
        dm = jnp.where(keep, dec, 0.0).reshape(HG_BLK * HG_BLK, D_HG)
        att = _dot(_bf(dm), headb_ref[...]).reshape(HG_BLK, HG_BLK, D_HG)
        o_diag = jnp.sum(att * vb[:, None, :], axis=0)
        st = state[...]
        o_ref[0, 0, rows, :] = o_diag + _dot_nt(_bf(qds[rows, :]), _bf(st))
        upd = lax.dot_general(_bf(vb), _bf(kns[rows, :]), (((0,), (0,)), ((), ())), preferred_element_type=F32)
        state[...] = jnp.exp(tots[pl.ds(r0, 1), :]) * st + upd * headf_ref[...]
        return carry

    lax.fori_loop(0, nblk, block, 0, unroll=2)

    @pl.when(s == nseg - 1)
    def _():
        st_ref[0, 0] = state[...]


def hgrn2_scan(p, lb, s0_t, seg_rows):
    b, seq, _ = p.shape
    nseg = seq // seg_rows
    tri, ones, headb, headf = _hg_constants()
    seg = lambda d, s: jnp.where(d == 0, s, nseg - 1 - s)
    col = lambda c: pl.BlockSpec((1, seg_rows, D_HG), lambda bi, d, s: (bi, seg(d, s), c // D_HG))
    scr = lambda: pltpu.VMEM((seg_rows, D_HG), F32)
    return pl.pallas_call(
        _hgrn_kernel,
        grid=(b, 2, nseg),
        in_specs=[
            col(COL_HQ),
            pl.BlockSpec((1, seg_rows, D_HG), lambda bi, d, s: (bi, seg(d, s), COL_HFF // D_HG + d)),
            col(COL_HI),
            pl.BlockSpec((1, 1, D_HG), lambda bi, d, s: (d, 0, 0)),
            pl.BlockSpec((1, HG_TILE, HG_TILE), lambda bi, d, s: (d, 0, 0)),
            _const_spec((HG_TILE, HG_TILE)),
            _const_spec((D_HG, D_HG)),
            _const_spec((D_HG, D_HG)),
            pl.BlockSpec((1, 1, D_HG, D_HG), lambda bi, d, s: (bi, d, 0, 0)),
        ],
        out_specs=[
            pl.BlockSpec((1, 1, seg_rows, D_HG), lambda bi, d, s: (d, bi, seg(d, s), 0)),
            pl.BlockSpec((1, 1, D_HG, D_HG), lambda bi, d, s: (bi, d, 0, 0)),
        ],
        out_shape=[jax.ShapeDtypeStruct((2, b, seq, D_HG), F32), jax.ShapeDtypeStruct((b, 2, D_HG, D_HG), F32)],
        scratch_shapes=[pltpu.VMEM((D_HG, D_HG), F32)] + [scr() for _ in range(7)],
        compiler_params=_cparams(("parallel", "parallel", "arbitrary")),
        name="hgrn2_scan",
    )(p, p, p, lb.reshape(2, 1, D_HG), tri, ones, headb, headf, s0_t)


def _hg_post_kernel(of_ref, ob_ref, g_ref, nw_ref, headb_ref, o_ref):
    o = of_ref[0] + ob_ref[0]
    sq = o * o
    hi = _bf(sq)
    lo = _bf(sq - hi.astype(F32))
    ms = (_dot(hi, headb_ref[...]) + _dot(lo, headb_ref[...])) * (1.0 / HG_DV)
    o_ref[...] = o * lax.rsqrt(ms + EPS) * nw_ref[...] * _silu(g_ref[...])


def hgrn2_output(o_dirs, p2d, norm_w):
    t = p2d.shape[0]
    tm = ROW_TILE
    _, _, headb, _ = _hg_constants()
    return pl.pallas_call(
        _hg_post_kernel,
        grid=(t // tm,),
        in_specs=[
            pl.BlockSpec((1, tm, D_HG), lambda i: (0, i, 0)),
            pl.BlockSpec((1, tm, D_HG), lambda i: (1, i, 0)),
            pl.BlockSpec((tm, D_HG), lambda i: (i, COL_HGATE // D_HG)),
            _const_spec((1, D_HG)),
            _const_spec((D_HG, D_HG)),
        ],
        out_specs=pl.BlockSpec((tm, D_HG), lambda i: (i, 0)),
        out_shape=jax.ShapeDtypeStruct((t, D_HG), F32),
        compiler_params=_cparams(("parallel",)),
        name="hgrn2_output",
    )(o_dirs, o_dirs, p2d, norm_w.reshape(1, D_HG), headb)


def _hg_state_to_t(s):
    b = s.shape[0]
    st = jnp.swapaxes(s, -1, -2)
    eye = jnp.eye(HG_HEADS, dtype=s.dtype)
    full = eye[None, None, :, None, :, None] * st[:, :, :, :, None, :]
    return full.reshape(b, 2, D_HG, D_HG)


def _hg_state_from_t(st):
    b = st.shape[0]
    r = st.reshape(b, 2, HG_HEADS, HG_DV, HG_HEADS, HG_DK)
    blocks = jnp.stack([r[:, :, h, :, h, :] for h in range(HG_HEADS)], axis=2)
    return jnp.swapaxes(blocks, -1, -2)


HY_COLS = 2 * HY_ORDER * D_HY
HY_FEAT_PAD = 128
HY_GEN_ROWS = 256


def _dot_hi(a, b):
    ah = _bf(a)
    al = _bf(a - ah.astype(F32))
    bh = _bf(b)
    bl = _bf(b - bh.astype(F32))
    return _dot(ah, bh) + _dot(ah, bl) + _dot(al, bh)


HY_HALF_COLS = HY_ORDER * D_HY


def _filter_gen_kernel(feat_ref, w1_ref, b1_ref, w2_ref, b2_ref, w3_ref, fr_ref, dl_ref, h_ref, ssq_ref, *, seq):
    i = pl.program_id(0)
    tm = feat_ref.shape[0]
    feat = feat_ref[...]
    fr = fr_ref[...]
    h = jnp.sin(fr * (_dot_hi(feat, w1_ref[...]) + b1_ref[...]))
    h = jnp.sin(fr * (_dot_hi(h, w2_ref[...]) + b2_ref[...]))
    h = _dot_hi(h, w3_ref[...])
    decay = jnp.exp(-feat[:, 0:1] * dl_ref[...])
    decay = jnp.concatenate([decay] * HY_ORDER, axis=1)
    row = i * tm + lax.broadcasted_iota(jnp.int32, (tm, HY_HALF_COLS), 0)
    taps = jnp.where(row < seq, h[:, :HY_HALF_COLS], h[:, HY_HALF_COLS:]) * decay
    h_ref[...] = jnp.where(row == seq, 0.0, taps)

    @pl.when(i == 0)
    def _():
        ssq_ref[...] = jnp.zeros_like(ssq_ref)

    ssq_ref[...] += jnp.sum(taps * taps, axis=0, keepdims=True)


def _hyena_features(seq):
    order = np.concatenate([np.arange(seq), [0], np.arange(seq - 1, 0, -1)])
    pos = order.astype(np.float32)
    t = pos / np.float32(max(seq - 1, 1))
    bands = np.linspace(1e-4, HY_BANDS - 1, HY_BANDS, dtype=np.float32)
    ang = np.float32(2.0 * math.pi / seq) * pos[:, None] * bands[None, :]
    feat = np.concatenate([t[:, None], np.cos(ang), np.sin(ang)], axis=-1).astype(np.float32)
    out = np.zeros((2 * seq, HY_FEAT_PAD), np.float32)
    out[:, :HY_EMB] = feat
    return jnp.asarray(out)


def hyena_filter_gen(seq, w1, b1, w2, b2, w3, freq):
    feat = _hyena_features(seq)
    w1p = jnp.zeros((HY_FEAT_PAD, HY_FFN), F32).at[:HY_EMB].set(w1.astype(F32))
    deltas = jnp.asarray(np.linspace(HY_DECAY_SLOW, HY_DECAY_FAST, D_HY, dtype=np.float32)).reshape(1, D_HY)
    tm = HY_GEN_ROWS
    return pl.pallas_call(
        functools.partial(_filter_gen_kernel, seq=seq),
        grid=(2 * seq // tm,),
        in_specs=[
            pl.BlockSpec((tm, HY_FEAT_PAD), lambda i: (i, 0)),
            _const_spec((HY_FEAT_PAD, HY_FFN)), _const_spec((1, HY_FFN)),
            _const_spec((HY_FFN, HY_FFN)), _const_spec((1, HY_FFN)),
            _const_spec((HY_FFN, HY_COLS)), _const_spec((1, HY_FFN)), _const_spec((1, D_HY)),
        ],
        out_specs=[pl.BlockSpec((tm, HY_HALF_COLS), lambda i: (i, 0)), _const_spec((1, HY_HALF_COLS))],
        out_shape=[jax.ShapeDtypeStruct((2 * seq, HY_HALF_COLS), F32),
                   jax.ShapeDtypeStruct((1, HY_HALF_COLS), F32)],
        compiler_params=_cparams(("arbitrary",)),
        name="hyena_filter_gen",
    )(feat, w1p, b1.reshape(1, HY_FFN), w2, b2.reshape(1, HY_FFN), w3, freq.reshape(1, HY_FFN), deltas)


def _filter_norm(ssq_ref, n):
    return lax.rsqrt(ssq_ref[:, n * D_HY:(n + 1) * D_HY])


def _conv3(pad_ref, w_ref, b_ref, r0, rows, pad):
    acc = b_ref[...]
    for k in range(HY_SHORT):
        lo = r0 + pad - HY_SHORT // 2 + k
        acc = acc + w_ref[k:k + 1, :] * pad_ref[lo:lo + rows, :]
    return acc


def _dft_constants(seq):
    n = 2 * seq
    k = np.arange(n)
    ang = 2.0 * np.pi * np.outer(k, k) / n
    fwd = np.concatenate([np.cos(ang), -np.sin(ang)], axis=0)
    inv = np.concatenate([np.cos(ang[:seq]), -np.sin(ang[:seq])], axis=1) / n
    return jnp.asarray(fwd, BF16), jnp.asarray(inv, BF16)


def _ctx_spectrum_kernel(filt_ref, ssq_ref, fwd_ref, h_ref):
    n = pl.program_id(0)
    norm = jnp.where(n == 0, _filter_norm(ssq_ref, 0), _filter_norm(ssq_ref, 1))
    x = filt_ref[...] * norm
    xh = _bf(x)
    xl = _bf(x - xh.astype(F32))
    h_ref[0] = _dot(fwd_ref[...], xh) + _dot(fwd_ref[...], xl)


def hyena_ctx_spectrum(filt, ssq):
    n = filt.shape[0]
    fwd, _ = _dft_constants(n // 2)
    return pl.pallas_call(
        _ctx_spectrum_kernel,
        grid=(HY_ORDER,),
        in_specs=[pl.BlockSpec((n, D_HY), lambda o: (0, o)), _const_spec((1, HY_HALF_COLS)),
                  _const_spec((2 * n, n))],
        out_specs=pl.BlockSpec((1, 2 * n, D_HY), lambda o: (o, 0, 0)),
        out_shape=jax.ShapeDtypeStruct((HY_ORDER, 2 * n, D_HY), F32),
        compiler_params=_cparams(("parallel",)),
        name="hyena_ctx_spectrum",
    )(filt, ssq, fwd)


HY_PAD = 8


def _hyena_ctx_kernel(v_ref, x1_ref, x2_ref, cw_ref, cb_ref, bias_ref, spec_ref, fwd_ref, inv_ref, o_ref, pad_ref):
    seq = v_ref.shape[1]
    n = 2 * seq
    zeros = jnp.zeros((HY_PAD, D_HY), F32)
    pad_ref[0:HY_PAD, :] = zeros
    pad_ref[HY_PAD + seq:2 * HY_PAD + seq, :] = zeros

    def short_conv(src_ref, part):
        pad_ref[HY_PAD:HY_PAD + seq, :] = src_ref[0]
        return _conv3(pad_ref, cw_ref.at[part], cb_ref.at[part], 0, seq, HY_PAD)

    z = short_conv(v_ref, 0)
    for order, gate_ref in enumerate((x1_ref, x2_ref)):
        spec = _dot(fwd_ref[:, 0:seq], _bf(z))
        xr, xi = spec[:n], spec[n:]
        hr, hi = spec_ref[order, 0:n, :], spec_ref[order, n:2 * n, :]
        prod = jnp.concatenate([xr * hr - xi * hi, xr * hi + xi * hr], axis=0)
        conv = _dot(inv_ref[...], _bf(prod))
        z = short_conv(gate_ref, order + 1) * (conv + bias_ref[order:order + 1, :] * z)
    o_ref[0] = z


def hyena_ctx(p, conv_w, conv_b, bias, spec):
    b, seq, _ = p.shape
    fwd, inv = _dft_constants(seq)
    col = lambda c: pl.BlockSpec((1, seq, D_HY), lambda i: (i, 0, c))
    c0 = COL_HY // D_HY
    return pl.pallas_call(
        _hyena_ctx_kernel,
        grid=(b,),
        in_specs=[
            col(c0), col(c0 + 1), col(c0 + 2),
            _const_spec((3, HY_SHORT, D_HY)), _const_spec((3, 1, D_HY)), _const_spec((HY_ORDER, D_HY)),
            _const_spec((HY_ORDER, 4 * seq, D_HY)), _const_spec((4 * seq, 2 * seq)), _const_spec((seq, 4 * seq)),
        ],
        out_specs=pl.BlockSpec((1, seq, D_HY), lambda i: (i, 0, 0)),
        out_shape=jax.ShapeDtypeStruct((b, seq, D_HY), F32),
        scratch_shapes=[pltpu.VMEM((seq + 2 * HY_PAD, D_HY), F32)],
        compiler_params=_cparams(("parallel",)),
        name="hyena_ctx",
    )(p, p, p, conv_w.reshape(HY_SHORT, 3, D_HY).transpose(1, 0, 2), conv_b.reshape(3, 1, D_HY), bias, spec,
      fwd, inv)


FFT_N1 = 64
FFT_N2 = 128
FFT_N = FFT_N1 * FFT_N2
FFT_K1 = FFT_N1 // 2 + 1
FFT_K1_PAD = 40
FFT_ROWS = FFT_K1_PAD * FFT_N2
HY_LANES = 128
FFT_UNROLL1 = 4
FFT_UNROLL2 = 3


def _fft_constants():
    n1, n2, n, kp = FFT_N1, FFT_N2, FFT_N, FFT_K1_PAD
    k1 = np.arange(kp)
    w1 = np.exp(-2j * np.pi * np.outer(k1, np.arange(n1)) / n1)
    tw = np.exp(-2j * np.pi * np.outer(np.arange(n2), k1) / n)
    f1c = tw[:, :, None] * w1[None, :, :]
    f1t = np.concatenate([f1c.real, f1c.imag], axis=1).transpose(0, 2, 1)
    w2 = np.exp(-2j * np.pi * np.outer(np.arange(n2), np.arange(n2)) / n2)
    f2 = np.block([[w2.real, -w2.imag], [w2.imag, w2.real]])
    g2c = np.conj(w2)
    g2 = np.block([[g2c.real, -g2c.imag], [g2c.imag, g2c.real]])
    weight = np.where((k1 == 0) | (k1 == n1 // 2), 1.0, 2.0) * (k1 < FFT_K1)
    mc = np.conj(w1.T)[None, :n1 // 2, :] * np.conj(tw)[:, None, :] * weight / n
    g1 = np.concatenate([mc.real, -mc.imag], axis=2)
    return (jnp.asarray(f1t, BF16), jnp.asarray(f2, BF16), jnp.asarray(g2, BF16), jnp.asarray(g1, BF16))


def _dot_tn(a, b):
    return lax.dot_general(a, b, (((0,), (0,)), ((), ())), preferred_element_type=F32)


def _fft_stage1(x_ref, f1t_ref, a_ref, n1_in):
    def body(n2, carry):
        rows = x_ref[pl.ds(n2, n1_in, stride=FFT_N2), :]
        out = _dot_tn(f1t_ref[n2, 0:n1_in, :], _bf(rows))
        a_ref[0, pl.ds(n2, FFT_K1, stride=FFT_N2), :] = out[:FFT_K1]
        a_ref[1, pl.ds(n2, FFT_K1, stride=FFT_N2), :] = out[FFT_K1_PAD:FFT_K1_PAD + FFT_K1]
        return carry
    lax.fori_loop(0, FFT_N2, body, 0, unroll=FFT_UNROLL1)


def _lat_spectrum_kernel(filt_ref, ssq_ref, f1t_ref, f2_ref, h_ref, a_ref):
    n = pl.program_id(0)
    half = pl.program_id(1)
    norm_full = jnp.where(n == 0, _filter_norm(ssq_ref, 0), _filter_norm(ssq_ref, 1))
    norm = jnp.where(half == 0, norm_full[:, :HY_LANES], norm_full[:, HY_LANES:])
    _fft_stage1(filt_ref, f1t_ref, a_ref, FFT_N1)

    def stage2(k1, carry):
        r0 = pl.multiple_of(k1 * FFT_N2, FFT_N2)
        rows = pl.ds(r0, FFT_N2)
        a = jnp.concatenate([a_ref[0, rows, :], a_ref[1, rows, :]], axis=0)
        b = _dot(f2_ref[...], _bf(a)) * norm
        h_ref[0, 0, rows, :] = _bf(b[:FFT_N2])
        h_ref[0, 1, rows, :] = _bf(b[FFT_N2:])
        return carry
    lax.fori_loop(0, FFT_K1, stage2, 0)


def hyena_lat_spectrum(filt, ssq):
    f1t, f2, _, _ = _fft_constants()
    rows = FFT_K1 * FFT_N2
    halves = D_HY // HY_LANES
    return pl.pallas_call(
        _lat_spectrum_kernel,
        grid=(HY_ORDER, halves),
        in_specs=[
            pl.BlockSpec((FFT_N, HY_LANES), lambda o, h: (0, o * halves + h)),
            _const_spec((1, HY_HALF_COLS)),
            _const_spec((FFT_N2, FFT_N1, 2 * FFT_K1_PAD)),
            _const_spec((2 * FFT_N2, 2 * FFT_N2)),
        ],
        out_specs=pl.BlockSpec((1, 2, rows, HY_LANES), lambda o, h: (o, 0, 0, h)),
        out_shape=jax.ShapeDtypeStruct((HY_ORDER, 2, rows, D_HY), BF16),
        scratch_shapes=[pltpu.VMEM((2, FFT_ROWS, HY_LANES), F32)],
        compiler_params=_cparams(("parallel", "parallel")),
        name="hyena_lat_spectrum",
    )(filt, ssq, f1t, f2)


HY_ROWS = 512


def _hyena_lat_kernel(v_ref, x1_ref, x2_ref, cw_ref, cb_ref, bias_ref, spec_ref, f1t_ref, f2_ref, g2_ref, g1_ref,
                      o_ref, pad_ref, z_ref, a_ref):
    seq = v_ref.shape[1]
    zeros = jnp.zeros((HY_PAD, HY_LANES), F32)
    pad_ref[0:HY_PAD, :] = zeros
    pad_ref[HY_PAD + seq:2 * HY_PAD + seq, :] = zeros
    pad_ref[HY_PAD:HY_PAD + seq, :] = v_ref[0]
    for ti in range(seq // HY_ROWS):
        r0 = ti * HY_ROWS
        z_ref[r0:r0 + HY_ROWS, :] = _conv3(pad_ref, cw_ref.at[0], cb_ref.at[0], r0, HY_ROWS, HY_PAD)
    tail = jnp.zeros((FFT_ROWS - FFT_K1 * FFT_N2, HY_LANES), F32)
    a_ref[0, FFT_K1 * FFT_N2:FFT_ROWS, :] = tail
    a_ref[1, FFT_K1 * FFT_N2:FFT_ROWS, :] = tail

    for order, gate_ref in enumerate((x1_ref, x2_ref)):
        _fft_stage1(z_ref, f1t_ref, a_ref, FFT_N1 // 2)

        def stage2(k1, carry):
            r0 = pl.multiple_of(k1 * FFT_N2, FFT_N2)
            rows = pl.ds(r0, FFT_N2)
            a = jnp.concatenate([a_ref[0, rows, :], a_ref[1, rows, :]], axis=0)
            b = _dot(f2_ref[...], _bf(a))
            br, bi = b[:FFT_N2], b[FFT_N2:]
            hr = spec_ref[order, 0, rows, :].astype(F32)
            hi = spec_ref[order, 1, rows, :].astype(F32)
            y = jnp.concatenate([br * hr - bi * hi, br * hi + bi * hr], axis=0)
            c = _dot(g2_ref[...], _bf(y))
            a_ref[0, rows, :] = c[:FFT_N2]
            a_ref[1, rows, :] = c[FFT_N2:]
            return carry
        lax.fori_loop(0, FFT_K1, stage2, 0, unroll=FFT_UNROLL2)

        def stage1_inv(n2, carry):
            c = jnp.concatenate([a_ref[0, pl.ds(n2, FFT_K1_PAD, stride=FFT_N2), :],
                                 a_ref[1, pl.ds(n2, FFT_K1_PAD, stride=FFT_N2), :]], axis=0)
            o_ref[0, pl.ds(n2, FFT_N1 // 2, stride=FFT_N2), :] = _dot(g1_ref[n2], _bf(c))
            return carry
        lax.fori_loop(0, FFT_N2, stage1_inv, 0, unroll=FFT_UNROLL1)

        pad_ref[HY_PAD:HY_PAD + seq, :] = gate_ref[0]
        for ti in range(seq // HY_ROWS):
            r0 = ti * HY_ROWS
            rows = slice(r0, r0 + HY_ROWS)
            gate = _conv3(pad_ref, cw_ref.at[order + 1], cb_ref.at[order + 1], r0, HY_ROWS, HY_PAD)
            z = gate * (o_ref[0, rows, :] + bias_ref[order:order + 1, :] * z_ref[rows, :])
            if order + 1 < HY_ORDER:
                z_ref[rows, :] = z
            else:
                o_ref[0, rows, :] = z


def hyena_lat(p, conv_w, conv_b, bias, spec):
    b, seq, _ = p.shape
    assert 2 * seq == FFT_N
    f1t, f2, g2, g1 = _fft_constants()
    hw = HY_LANES
    col = lambda c: pl.BlockSpec((1, seq, hw), lambda i, h: (i, 0, (COL_HY + c * D_HY) // hw + h))
    return pl.pallas_call(
        _hyena_lat_kernel,
        grid=(b, D_HY // hw),
        in_specs=[
            col(0), col(1), col(2),
            pl.BlockSpec((3, HY_SHORT, hw), lambda i, h: (0, 0, h)),
            pl.BlockSpec((3, 1, hw), lambda i, h: (0, 0, h)),
            pl.BlockSpec((HY_ORDER, hw), lambda i, h: (0, h)),
            pl.BlockSpec((HY_ORDER, 2, FFT_K1 * FFT_N2, hw), lambda i, h: (0, 0, 0, h)),
            _const_spec((FFT_N2, FFT_N1, 2 * FFT_K1_PAD)),
            _const_spec((2 * FFT_N2, 2 * FFT_N2)),
            _const_spec((2 * FFT_N2, 2 * FFT_N2)),
            _const_spec((FFT_N2, FFT_N1 // 2, 2 * FFT_K1_PAD)),
        ],
        out_specs=pl.BlockSpec((1, seq, hw), lambda i, h: (i, 0, h)),
        out_shape=jax.ShapeDtypeStruct((b, seq, D_HY), F32),
        scratch_shapes=[
            pltpu.VMEM((seq + 2 * HY_PAD, hw), F32),
            pltpu.VMEM((seq, hw), F32),
            pltpu.VMEM((2, FFT_ROWS, hw), F32),
        ],
        compiler_params=_cparams(("parallel", "parallel")),
        name="hyena_lat",
    )(p, p, p, conv_w.reshape(HY_SHORT, 3, D_HY).transpose(1, 0, 2), conv_b.reshape(3, 1, D_HY), bias, spec,
      f1t, f2, g2, g1)


HG_SEG_ROWS = 512
COND_ROWS = 16


def _trunk_layer(x, mods, lp, lb, batch, seq, ctx, final):
    x = ffn_half_step(x, mods, lp['ln_ffn1'], lp['w_ffn1_in'], lp['w_ffn1_out'], lp['final_norm'],
                      mod_base=0, final=False, seq_len=seq)
    p2d = mixer_in_proj(x, mods, lp['ln_mix'], lp['w_in'], seq_len=seq)
    p = p2d.reshape(batch, seq, N_IN)

    filt, ssq = hyena_filter_gen(seq, lp['hy_w1'], lp['hy_b1'], lp['hy_w2'], lp['hy_b2'], lp['hy_w3'],
                                 lp['hy_freq'])
    if ctx is None:
        att = context_attention(p, lp['attn_sink'])
        lru_h0 = jnp.zeros((batch, 2, D_LRU), F32)
        hg_s0 = jnp.zeros((batch, 2, D_HG, D_HG), F32)
        hy = hyena_ctx(p, lp['hy_conv_w'], lp['hy_conv_b'], lp['hy_bias'], hyena_ctx_spectrum(filt, ssq))
    else:
        k_ctx, v_ctx, lru_h0, hg_state = ctx
        att = latent_attention(p, k_ctx, v_ctx, lp['attn_sink'])
        hg_s0 = _hg_state_to_t(hg_state.astype(F32))
        hy = hyena_lat(p, lp['hy_conv_w'], lp['hy_conv_b'], lp['hy_bias'], hyena_lat_spectrum(filt, ssq))
    lru, lru_state = rglru_mixer(p, lp['lru_conv_w'], lp['lru_conv_b'], lp['lru_w_r'], lp['lru_b_r'],
                                 lp['lru_w_i'], lp['lru_b_i'], lp['lru_lambda'], lru_h0.astype(F32))
    o_dirs, hg_state_t = hgrn2_scan(p, lb, hg_s0, min(seq, HG_SEG_ROWS))
    t = batch * seq
    hg = hgrn2_output(o_dirs.reshape(2, t, D_HG), p2d, lp['hg_norm'])

    x = gated_merge(x, mods, lp['ln_mix'], att.reshape(t, D_ATTN), lru.reshape(t, D_LRU), hg,
                    hy.reshape(t, D_HY), lp['w_gate'], lp['w_bo_attn'], lp['w_bo_lru'], lp['w_bo_hg'],
                    lp['w_bo_hy'], lp['w_out'], seq_len=seq)
    x = ffn_half_step(x, mods, lp['ln_ffn2'], lp['w_ffn2_in'], lp['w_ffn2_out'], lp['final_norm'],
                      mod_base=6, final=final, seq_len=seq)
    state = None
    if ctx is None:
        k = p[:, :, COL_K:COL_K + D_KV].reshape(batch, seq, N_KV_HEADS, HEAD_DIM)
        v = p[:, :, COL_V:COL_V + D_KV].reshape(batch, seq, N_KV_HEADS, HEAD_DIM)
        state = (k, v, lru_state, _hg_state_from_t(hg_state_t))
    return x, state


def kernel(x_prompt, x_sample, cache_k, cache_v, state_lru, state_hgrn, c, c_ctx, ln_ffn1, ln_mix, ln_ffn2, w_ada, b_ada, w_ffn1_in, w_ffn1_out, w_ffn2_in, w_ffn2_out, w_in, attn_sink, lru_conv_w, lru_conv_b, lru_w_r, lru_b_r, lru_w_i, lru_b_i, lru_lambda, hg_lb_logits, hg_norm, hy_conv_w, hy_conv_b, hy_w1, hy_b1, hy_w2, hy_b2, hy_w3, hy_freq, hy_bias, w_bo_attn, w_bo_lru, w_bo_hg, w_bo_hy, w_gate, w_out, final_norm):
    batch, seq, _ = x_prompt.shape
    dec_batch, dec_seq, _ = x_sample.shape
    lb_soft = jax.nn.softmax(hg_lb_logits.astype(F32), axis=0)
    lb_all = jnp.cumsum(lb_soft, axis=0) - lb_soft[0]

    stacked = dict(ln_ffn1=ln_ffn1, ln_mix=ln_mix, ln_ffn2=ln_ffn2, w_ffn1_in=w_ffn1_in, w_ffn1_out=w_ffn1_out,
                   w_ffn2_in=w_ffn2_in, w_ffn2_out=w_ffn2_out, w_in=w_in, attn_sink=attn_sink,
                   lru_conv_w=lru_conv_w, lru_conv_b=lru_conv_b, lru_w_r=lru_w_r, lru_b_r=lru_b_r,
                   lru_w_i=lru_w_i, lru_b_i=lru_b_i, lru_lambda=lru_lambda, hg_norm=hg_norm,
                   hy_conv_w=hy_conv_w, hy_conv_b=hy_conv_b, hy_w1=hy_w1, hy_b1=hy_b1, hy_w2=hy_w2, hy_b2=hy_b2,
                   hy_w3=hy_w3, hy_freq=hy_freq, hy_bias=hy_bias, w_bo_attn=w_bo_attn, w_bo_lru=w_bo_lru,
                   w_bo_hg=w_bo_hg, w_bo_hy=w_bo_hy, w_gate=w_gate, w_out=w_out)

    cond = jnp.zeros((COND_ROWS, D_MODEL), F32).at[0].set(c_ctx).at[1:1 + dec_batch].set(c)
    mods = ada_mods(cond, w_ada, b_ada).reshape(DEPTH, COND_ROWS, N_MOD, D_MODEL)

    h = x_prompt.reshape(batch * seq, D_MODEL)
    z = x_sample.reshape(dec_batch * dec_seq, D_MODEL)
    ks, vs, lrus, hgs = [], [], [], []
    for l in range(DEPTH):
        lp = {name: w[l] for name, w in stacked.items()}
        lp['final_norm'] = final_norm
        final = l == DEPTH - 1
        h, (k_l, v_l, lru_l, hg_l) = _trunk_layer(h, mods[l, 0:1], lp, lb_all[l], batch, seq, None, final)
        ks.append(k_l)
        vs.append(v_l)
        lrus.append(lru_l)
        hgs.append(hg_l)
        ctx = (cache_k[:, l].reshape(dec_batch, -1, D_KV), cache_v[:, l].reshape(dec_batch, -1, D_KV),
               state_lru[:, l], state_hgrn[:, l])
        z, _ = _trunk_layer(z, mods[l, 1:1 + dec_batch], lp, lb_all[l], dec_batch, dec_seq, ctx, final)

    y_prompt = h.reshape(batch, seq, D_MODEL)
    y_sample = z.reshape(dec_batch, dec_seq, D_MODEL)
    return (y_prompt, y_sample, jnp.stack(ks, axis=1), jnp.stack(vs, axis=1), jnp.stack(lrus, axis=1),
            jnp.stack(hgs, axis=1))
```

```python
import functools
import math

import numpy as np
import jax
import jax.numpy as jnp
from jax import lax
from jax.experimental import pallas as pl
from jax.experimental.pallas import tpu as pltpu

F32 = jnp.float32
BF16 = jnp.bfloat16

D_MODEL = 1024
DEPTH = 2
GRID_W = 64
EPS = 1e-6
N_MOD = 9
N_BRANCH = 4
D_FF = 2816
N_HEADS = 8
N_KV_HEADS = 2
HEAD_DIM = 64
GQA_GROUP = N_HEADS // N_KV_HEADS
D_ATTN = N_HEADS * HEAD_DIM
D_KV = N_KV_HEADS * HEAD_DIM
WINDOW = 128
ATTN_BLOCK = 128
ATTN_SCALE = HEAD_DIM ** -0.5
ROPE_BASE = 10000.0
D_LRU = D_MODEL // 4
LRU_HEADS = 4
LRU_BLOCK = D_LRU // LRU_HEADS
LRU_CONV = 4
LRU_C = 8.0
HG_HEADS = 4
HG_DK = 64
HG_DV = 64
D_HG = HG_HEADS * HG_DK
HG_SCALE = HG_DK ** -0.5
HG_BLK = 16
D_HY = D_MODEL // 4
HY_ORDER = 2
HY_SHORT = 3
HY_BANDS = 8
HY_EMB = 2 * HY_BANDS + 1
HY_FFN = 64
HY_TARGET = 1e-2
HY_DECAY_SLOW = -math.log(HY_TARGET) / 1.5
HY_DECAY_FAST = -math.log(HY_TARGET) / 0.3
N_IN = D_ATTN + 2 * D_KV + 2 * D_LRU + 5 * D_HG + 3 * D_HY

COL_Q = 0
COL_K = D_ATTN
COL_V = D_ATTN + D_KV
COL_RX = D_ATTN + 2 * D_KV
COL_RG = COL_RX + D_LRU
COL_HQ = COL_RG + D_LRU
COL_HFF = COL_HQ + D_HG
COL_HFB = COL_HFF + D_HG
COL_HI = COL_HFB + D_HG
COL_HGATE = COL_HI + D_HG
COL_HY = COL_HGATE + D_HG

FF_CHUNK = 256
N_FF_CHUNK = D_FF // FF_CHUNK
ROW_TILE = 512
VMEM_LIMIT = 56 * 1024 * 1024


def _bf(x):
    return x.astype(BF16)


def _dot(a, b):
    return jnp.dot(a, b, preferred_element_type=F32)


def _split3(x):
    hi = _bf(x)
    r1 = x - hi.astype(F32)
    mid = _bf(r1)
    lo = _bf(r1 - mid.astype(F32))
    return hi, mid, lo


def _dot_exact_rhs(mat_bf, x):
    hi, mid, lo = _split3(x)
    return _dot(mat_bf, hi) + _dot(mat_bf, mid) + _dot(mat_bf, lo)


def _sigmoid(x):
    return jax.nn.sigmoid(x)


def _silu(x):
    return x * _sigmoid(x)


def _norm_mod(x, lnw, shift, scale):
    ms = jnp.mean(x * x, axis=-1, keepdims=True)
    y = x * lax.rsqrt(ms + EPS) * lnw
    return y * (1.0 + scale) + shift


def _cparams(sem):
    return pltpu.CompilerParams(dimension_semantics=sem, vmem_limit_bytes=VMEM_LIMIT)


def _const_spec(shape):
    nd = len(shape)
    return pl.BlockSpec(shape, lambda *_: (0,) * nd)


def _ada_kernel(cond_ref, w_ref, b_ref, o_ref):
    c = cond_ref[...]
    o_ref[0] = _dot(_bf(_silu(c)), _bf(w_ref[0])) + b_ref[0]


def ada_mods(cond, w_ada, b_ada):
    r = cond.shape[0]
    cb = 1024
    ncol = (N_MOD * D_MODEL) // cb
    return pl.pallas_call(
        _ada_kernel,
        grid=(DEPTH, ncol),
        in_specs=[
            pl.BlockSpec((r, D_MODEL), lambda l, j: (0, 0)),
            pl.BlockSpec((1, D_MODEL, cb), lambda l, j: (l, 0, j)),
            pl.BlockSpec((1, 1, cb), lambda l, j: (l, 0, j)),
        ],
        out_specs=pl.BlockSpec((1, r, cb), lambda l, j: (l, 0, j)),
        out_shape=jax.ShapeDtypeStruct((DEPTH, r, N_MOD * D_MODEL), F32),
        compiler_params=_cparams(("parallel", "parallel")),
        name="ada_mods",
    )(cond, w_ada, b_ada.reshape(DEPTH, 1, N_MOD * D_MODEL))


def _seq_mod_index(seq_len, n_mod, tm):
    def idx(i):
        return jnp.minimum((i * tm) // seq_len, n_mod - 1)
    return idx


def _ffn_kernel(x_ref, mod_ref, ln_ref, wg_ref, wu_ref, wo_ref, fn_ref, o_ref, *, mod_base, final):
    x = x_ref[...]
    m = mod_ref[0]
    sh = m[mod_base:mod_base + 1]
    sc = m[mod_base + 1:mod_base + 2]
    g = m[mod_base + 2:mod_base + 3]
    h = _bf(_norm_mod(x, ln_ref[...], sh, sc))
    acc = None
    for c in range(N_FF_CHUNK):
        gate = _dot(h, wg_ref[c])
        up = _dot(h, wu_ref[c])
        part = _dot(_bf(_silu(gate) * up), wo_ref[c])
        acc = part if acc is None else acc + part
    y = x + 0.5 * g * acc
    if final:
        ms = jnp.mean(y * y, axis=-1, keepdims=True)
        y = y * lax.rsqrt(ms + EPS) * fn_ref[...]
    o_ref[...] = y


def ffn_half_step(x, mods, ln_w, w_in, w_out, final_w, *, mod_base, final, seq_len):
    t = x.shape[0]
    tm = ROW_TILE
    wg = _bf(w_in[:, :D_FF]).reshape(D_MODEL, N_FF_CHUNK, FF_CHUNK).transpose(1, 0, 2)
    wu = _bf(w_in[:, D_FF:]).reshape(D_MODEL, N_FF_CHUNK, FF_CHUNK).transpose(1, 0, 2)
    wo = _bf(w_out).reshape(N_FF_CHUNK, FF_CHUNK, D_MODEL)
    midx = _seq_mod_index(seq_len, mods.shape[0], tm)
    return pl.pallas_call(
        functools.partial(_ffn_kernel, mod_base=mod_base, final=final),
        grid=(t // tm,),
        in_specs=[
            pl.BlockSpec((tm, D_MODEL), lambda i: (i, 0)),
            pl.BlockSpec((1, N_MOD, D_MODEL), lambda i: (midx(i), 0, 0)),
            _const_spec((1, D_MODEL)),
            _const_spec((N_FF_CHUNK, D_MODEL, FF_CHUNK)),
            _const_spec((N_FF_CHUNK, D_MODEL, FF_CHUNK)),
            _const_spec((N_FF_CHUNK, FF_CHUNK, D_MODEL)),
            _const_spec((1, D_MODEL)),
        ],
        out_specs=pl.BlockSpec((tm, D_MODEL), lambda i: (i, 0)),
        out_shape=jax.ShapeDtypeStruct((t, D_MODEL), F32),
        compiler_params=_cparams(("parallel",)),
        name="ffn_half_step",
    )(x, mods, ln_w.reshape(1, D_MODEL), wg, wu, wo, final_w.reshape(1, D_MODEL))


PROJ_CHUNK = 256


def _proj_kernel(x_ref, mod_ref, ln_ref, w_ref, o_ref):
    x = x_ref[...]
    m = mod_ref[0]
    u = _bf(_norm_mod(x, ln_ref[...], m[3:4], m[4:5]))
    for c in range(N_IN // PROJ_CHUNK):
        sl = slice(c * PROJ_CHUNK, (c + 1) * PROJ_CHUNK)
        o_ref[:, sl] = _dot(u, w_ref[:, sl])


def mixer_in_proj(x, mods, ln_w, w_in, *, seq_len):
    t = x.shape[0]
    tm = ROW_TILE
    midx = _seq_mod_index(seq_len, mods.shape[0], tm)
    return pl.pallas_call(
        _proj_kernel,
        grid=(t // tm,),
        in_specs=[
            pl.BlockSpec((tm, D_MODEL), lambda i: (i, 0)),
            pl.BlockSpec((1, N_MOD, D_MODEL), lambda i: (midx(i), 0, 0)),
            _const_spec((1, D_MODEL)),
            _const_spec((D_MODEL, N_IN)),
        ],
        out_specs=pl.BlockSpec((tm, N_IN), lambda i: (i, 0)),
        out_shape=jax.ShapeDtypeStruct((t, N_IN), F32),
        compiler_params=_cparams(("parallel",)),
        name="mixer_in_proj",
    )(x, mods, ln_w.reshape(1, D_MODEL), _bf(w_in))


MERGE_CHUNK = 256


def _merge_kernel(x_ref, mod_ref, ln_ref, att_ref, lru_ref, hg_ref, hy_ref,
                  wg_ref, wa_ref, wl_ref, wh_ref, wy_ref, wo_ref, o_ref, mix_ref):
    x = x_ref[...]
    m = mod_ref[0]
    u = _bf(_norm_mod(x, ln_ref[...], m[3:4], m[4:5]))
    branches = (_bf(att_ref[...]), _bf(lru_ref[...]), _bf(hg_ref[...]), _bf(hy_ref[...]))
    w_bo = (wa_ref, wl_ref, wh_ref, wy_ref)
    for c in range(D_MODEL // MERGE_CHUNK):
        sl = slice(c * MERGE_CHUNK, (c + 1) * MERGE_CHUNK)
        mixed = None
        for n in range(N_BRANCH):
            term = _sigmoid(_dot(u, wg_ref[n, :, sl])) * _dot(branches[n], w_bo[n][:, sl])
            mixed = term if mixed is None else mixed + term
        mix_ref[:, sl] = _bf(mixed)
    o_ref[...] = x + m[5:6] * _dot(mix_ref[...], wo_ref[...])


def gated_merge(x, mods, ln_w, att, lru, hg, hy, w_gate, w_bo_attn, w_bo_lru, w_bo_hg, w_bo_hy, w_out,
                *, seq_len):
    t = x.shape[0]
    tm = ROW_TILE
    midx = _seq_mod_index(seq_len, mods.shape[0], tm)
    row = lambda w: pl.BlockSpec((tm, w), lambda i: (i, 0))
    return pl.pallas_call(
        _merge_kernel,
        grid=(t // tm,),
        in_specs=[
            row(D_MODEL),
            pl.BlockSpec((1, N_MOD, D_MODEL), lambda i: (midx(i), 0, 0)),
            _const_spec((1, D_MODEL)),
            row(D_ATTN), row(D_LRU), row(D_HG), row(D_HY),
            _const_spec((N_BRANCH, D_MODEL, D_MODEL)),
            _const_spec((D_ATTN, D_MODEL)),
            _const_spec((D_LRU, D_MODEL)),
            _const_spec((D_HG, D_MODEL)),
            _const_spec((D_HY, D_MODEL)),
            _const_spec((D_MODEL, D_MODEL)),
        ],
        out_specs=row(D_MODEL),
        out_shape=jax.ShapeDtypeStruct((t, D_MODEL), F32),
        scratch_shapes=[pltpu.VMEM((tm, D_MODEL), BF16)],
        compiler_params=_cparams(("parallel",)),
        name="gated_merge",
    )(x, mods, ln_w.reshape(1, D_MODEL), att, lru, hg, hy,
      _bf(w_gate), _bf(w_bo_attn), _bf(w_bo_lru), _bf(w_bo_hg), _bf(w_bo_hy), _bf(w_out))


def _sink_column(sink_ref, kvh, rows):
    cols = [jnp.broadcast_to(sink_ref[kvh * GQA_GROUP + g:kvh * GQA_GROUP + g + 1, 0:1], (rows, 1))
            for g in range(GQA_GROUP)]
    return jnp.concatenate(cols, axis=0)


def _stack_group(q, kvh):
    return jnp.concatenate(
        [q[:, (kvh * GQA_GROUP + g) * HEAD_DIM:(kvh * GQA_GROUP + g + 1) * HEAD_DIM] for g in range(GQA_GROUP)],
        axis=0)


def _dot_nt(a, b):
    return lax.dot_general(a, b, (((1,), (1,)), ((), ())), preferred_element_type=F32)


def _ctx_attn_kernel(q_ref, k_ref, v_ref, sink_ref, o_ref):
    rows = q_ref.shape[1]
    q = q_ref[0] * ATTN_SCALE
    k = k_ref[0]
    v = v_ref[0]
    for kvh in range(N_KV_HEADS):
        hs = slice(kvh * HEAD_DIM, (kvh + 1) * HEAD_DIM)
        s = _dot_nt(_bf(_stack_group(q, kvh)), _bf(k[:, hs]))
        sink = _sink_column(sink_ref, kvh, rows)
        m = jnp.maximum(jnp.max(s, axis=-1, keepdims=True), sink)
        p = jnp.exp(s - m)
        den = jnp.sum(p, axis=-1, keepdims=True) + jnp.exp(sink - m)
        o = _dot(_bf(p), _bf(v[:, hs])) / den
        for g in range(GQA_GROUP):
            head = kvh * GQA_GROUP + g
            o_ref[0, :, head * HEAD_DIM:(head + 1) * HEAD_DIM] = o[g * rows:(g + 1) * rows]


def _sink_table(sink):
    return jnp.broadcast_to(sink.astype(F32)[:, None], (N_HEADS, 128))


def context_attention(p_ctx, sink):
    b, seq, _ = p_ctx.shape
    return pl.pallas_call(
        _ctx_attn_kernel,
        grid=(b,),
        in_specs=[
            pl.BlockSpec((1, seq, D_ATTN), lambda i: (i, 0, 0)),
            pl.BlockSpec((1, seq, D_KV), lambda i: (i, 0, COL_K // D_KV)),
            pl.BlockSpec((1, seq, D_KV), lambda i: (i, 0, COL_V // D_KV)),
            _const_spec((N_HEADS, 128)),
        ],
        out_specs=pl.BlockSpec((1, seq, D_ATTN), lambda i: (i, 0, 0)),
        out_shape=jax.ShapeDtypeStruct((b, seq, D_ATTN), F32),
        compiler_params=_cparams(("parallel",)),
        name="context_attention",
    )(p_ctx, p_ctx, p_ctx, _sink_table(sink))


def _rope(x, cos, sin_signed):
    n = x.shape[-1]
    lane = lax.broadcasted_iota(jnp.int32, x.shape, x.ndim - 1)
    first = (lane % (HEAD_DIM // 2)) < (HEAD_DIM // 4)
    partner = jnp.where(first, pltpu.roll(x, n - HEAD_DIM // 4, axis=x.ndim - 1),
                        pltpu.roll(x, HEAD_DIM // 4, axis=x.ndim - 1))
    return x * cos + partner * sin_signed


def _rope_tables(seq):
    rows = seq // GRID_W
    row = np.repeat(np.arange(rows, dtype=np.float32), GRID_W)
    col = np.tile(np.arange(GRID_W, dtype=np.float32), rows)
    n = HEAD_DIM // 4
    inv = (ROPE_BASE ** (-jnp.arange(n, dtype=F32) / n))
    ang_r = jnp.asarray(row)[:, None] * inv
    ang_c = jnp.asarray(col)[:, None] * inv
    cos = jnp.concatenate([jnp.cos(ang_r), jnp.cos(ang_r), jnp.cos(ang_c), jnp.cos(ang_c)], axis=-1)
    sin = jnp.concatenate([-jnp.sin(ang_r), jnp.sin(ang_r), -jnp.sin(ang_c), jnp.sin(ang_c)], axis=-1)
    return jnp.tile(cos, (1, 2)), jnp.tile(sin, (1, 2))


ATTN_PREP_ROWS = 512


def _dup_heads(x):
    lane = lax.broadcasted_iota(jnp.int32, x.shape, 1)
    swapped = pltpu.roll(x, HEAD_DIM, axis=1)
    low = lane < HEAD_DIM
    return _bf(jnp.where(low, x, swapped)), _bf(jnp.where(low, swapped, x))


def _values_t(x):
    xt = jnp.transpose(x)
    ones = jnp.ones((HEAD_DIM, x.shape[0]), F32)
    return (_bf(jnp.concatenate([xt[:HEAD_DIM], ones], axis=0)),
            _bf(jnp.concatenate([xt[HEAD_DIM:], ones], axis=0)))


def _lat_attn_kernel(q_ref, k_ref, v_ref, kc_ref, vc_ref, cos_ref, sin_ref, sink_ref, o_ref,
                     kd_ref, vt_ref, kcd_ref, vct_ref):
    seq = k_ref.shape[1]
    blk = ATTN_BLOCK
    band = 3 * blk
    past = kc_ref.shape[1]
    i = pl.program_id(1)

    @pl.when(i == 0)
    def _():
        for ti in range(seq // ATTN_PREP_ROWS):
            rows = slice(ti * ATTN_PREP_ROWS, (ti + 1) * ATTN_PREP_ROWS)
            k0, k1 = _dup_heads(_rope(k_ref[0, rows, :], cos_ref[rows, :], sin_ref[rows, :]))
            kd_ref[0, rows, :] = k0
            kd_ref[1, rows, :] = k1
            v0, v1 = _values_t(v_ref[0, rows, :])
            vt_ref[0, :, rows] = v0
            vt_ref[1, :, rows] = v1
        kc0, kc1 = _dup_heads(kc_ref[0])
        kcd_ref[0] = kc0
        kcd_ref[1] = kc1
        vc0, vc1 = _values_t(vc_ref[0])
        vct_ref[0] = vc0
        vct_ref[1] = vc1

    start = pl.multiple_of(jnp.clip((i - 1) * blk, 0, seq - band), blk)
    q0 = pl.multiple_of(i * blk, blk)
    cos_q = cos_ref[pl.ds(q0, blk), :]
    sin_q = sin_ref[pl.ds(q0, blk), :]
    key = lax.broadcasted_iota(jnp.int32, (band + past, 2 * blk), 0)
    qry = lax.broadcasted_iota(jnp.int32, (band + past, 2 * blk), 1) % blk
    mask = (jnp.abs((q0 + qry) - (start + key)) <= WINDOW) | (key >= band)
    low = lax.broadcasted_iota(jnp.int32, (blk, 2 * HEAD_DIM), 1) < HEAD_DIM
    pairs = range(N_HEADS // 2)
    keys = [jnp.concatenate([kd_ref[kvh, pl.ds(start, band), :], kcd_ref[kvh]], axis=0)
            for kvh in range(N_KV_HEADS)]
    vals_t = [jnp.concatenate([vt_ref[kvh, :, pl.ds(start, band)], vct_ref[kvh]], axis=1)
              for kvh in range(N_KV_HEADS)]
    scores = []
    for pair in pairs:
        lanes = slice(pair * 2 * HEAD_DIM, (pair + 1) * 2 * HEAD_DIM)
        q = _rope(q_ref[0, :, lanes], cos_q, sin_q) * ATTN_SCALE
        q_heads = _bf(jnp.concatenate([jnp.where(low, q, 0.0), jnp.where(low, 0.0, q)], axis=0))
        scores.append(_dot_nt(keys[(2 * pair) // GQA_GROUP], q_heads))
    probs = []
    for pair in pairs:
        s = jnp.where(mask, scores[pair], -jnp.inf)
        sink = jnp.concatenate([sink_ref[2 * pair:2 * pair + 1, :], sink_ref[2 * pair + 1:2 * pair + 2, :]], axis=1)
        m = jnp.maximum(jnp.max(s, axis=0, keepdims=True), sink)
        probs.append((_bf(jnp.exp(s - m)), jnp.exp(sink - m)))
    for pair in pairs:
        lanes = slice(pair * 2 * HEAD_DIM, (pair + 1) * 2 * HEAD_DIM)
        p, sink_term = probs[pair]
        acc = _dot(vals_t[(2 * pair) // GQA_GROUP], p)
        out_t = acc[:HEAD_DIM, :] / (acc[HEAD_DIM:HEAD_DIM + 1, :] + sink_term)
        o_ref[0, :, lanes] = jnp.transpose(jnp.concatenate([out_t[:, :blk], out_t[:, blk:]], axis=0))


def latent_attention(p_lat, k_ctx, v_ctx, sink):
    b, seq, _ = p_lat.shape
    past = k_ctx.shape[1]
    cos, sin = _rope_tables(seq)
    return pl.pallas_call(
        _lat_attn_kernel,
        grid=(b, seq // ATTN_BLOCK),
        in_specs=[
            pl.BlockSpec((1, ATTN_BLOCK, D_ATTN), lambda bi, i: (bi, i, 0)),
            pl.BlockSpec((1, seq, D_KV), lambda bi, i: (bi, 0, COL_K // D_KV)),
            pl.BlockSpec((1, seq, D_KV), lambda bi, i: (bi, 0, COL_V // D_KV)),
            pl.BlockSpec((1, past, D_KV), lambda bi, i: (bi, 0, 0)),
            pl.BlockSpec((1, past, D_KV), lambda bi, i: (bi, 0, 0)),
            _const_spec((seq, 128)),
            _const_spec((seq, 128)),
            _const_spec((N_HEADS, 128)),
        ],
        out_specs=pl.BlockSpec((1, ATTN_BLOCK, D_ATTN), lambda bi, i: (bi, i, 0)),
        out_shape=jax.ShapeDtypeStruct((b, seq, D_ATTN), F32),
        scratch_shapes=[
            pltpu.VMEM((N_KV_HEADS, seq, D_KV), BF16),
            pltpu.VMEM((N_KV_HEADS, D_KV, seq), BF16),
            pltpu.VMEM((N_KV_HEADS, past, D_KV), BF16),
            pltpu.VMEM((N_KV_HEADS, D_KV, past), BF16),
        ],
        compiler_params=_cparams(("parallel", "arbitrary")),
        name="latent_attention",
    )(p_lat, p_lat, p_lat, k_ctx, v_ctx, cos, sin, _sink_table(sink))


LRU_HALF = D_LRU // 2
LRU_ROWS = 256
LRU_PAD = 8


def _neg_expm1(y):
    series = -(y * (1.0 + y * (1.0 / 2 + y * (1.0 / 6 + y * (1.0 / 24 + y * (1.0 / 120 + y * (1.0 / 720)))))))
    return jnp.where(y > -0.25, series, 1.0 - jnp.exp(y))


def _softplus(x):
    return jnp.maximum(x, 0.0) + jnp.log1p(jnp.exp(-jnp.abs(x)))


def _gelu_tanh(x):
    return 0.5 * x * (1.0 + jnp.tanh(math.sqrt(2.0 / math.pi) * (x + 0.044715 * (x * x * x))))


LRU_SCAN = 8


def _affine_scan8(a, b, *, reverse):
    row = lax.broadcasted_iota(jnp.int32, a.shape, 0)
    k = 1
    while k < LRU_SCAN:
        if reverse:
            valid = row < LRU_SCAN - k
            shift = LRU_SCAN - k
        else:
            valid = row >= k
            shift = k
        a_prev = jnp.where(valid, pltpu.roll(a, shift, axis=0), 1.0)
        b_prev = jnp.where(valid, pltpu.roll(b, shift, axis=0), 0.0)
        b = b + a * b_prev
        a = a * a_prev
        k *= 2
    return a, b


def _lru_kernel(x_ref, g_ref, cw_ref, cb_ref, wr_ref, br_ref, wi_ref, bi_ref, lam_ref, h0_ref,
                o_ref, st_ref, pad_ref, a_ref, bx_ref, hb_ref):
    seq = x_ref.shape[1]
    zeros = jnp.zeros((LRU_PAD, LRU_HALF), F32)
    pad_ref[0:LRU_PAD, :] = zeros
    pad_ref[LRU_PAD + seq:2 * LRU_PAD + seq, :] = zeros
    pad_ref[LRU_PAD:LRU_PAD + seq, :] = x_ref[0]
    left = LRU_CONV // 2
    for ti in range(seq // LRU_ROWS):
        r0 = ti * LRU_ROWS
        xc = cb_ref[...]
        for k in range(LRU_CONV):
            xc = xc + cw_ref[k:k + 1, :] * pad_ref[r0 + LRU_PAD - left + k:r0 + LRU_PAD - left + k + LRU_ROWS, :]
        xcb = _bf(xc)
        for d in range(2):
            r = _sigmoid(_dot(xcb, wr_ref[d]) + br_ref[d:d + 1, :])
            i = _sigmoid(_dot(xcb, wi_ref[d]) + bi_ref[d:d + 1, :])
            log_a = (-LRU_C) * r * _softplus(-lam_ref[d:d + 1, :])
            a_ref[d, r0:r0 + LRU_ROWS, :] = jnp.exp(log_a)
            bx_ref[d, r0:r0 + LRU_ROWS, :] = jnp.sqrt(_neg_expm1(2.0 * log_a)) * (i * xc)

    nt = seq // LRU_SCAN

    def step(blk, carry):
        hf, hb = carry
        rf = pl.ds(pl.multiple_of(blk * LRU_SCAN, LRU_SCAN), LRU_SCAN)
        rb = pl.ds(pl.multiple_of((nt - 1 - blk) * LRU_SCAN, LRU_SCAN), LRU_SCAN)
        af, bf_ = _affine_scan8(a_ref[0, rf, :], bx_ref[0, rf, :], reverse=False)
        ab, bb = _affine_scan8(a_ref[1, rb, :], bx_ref[1, rb, :], reverse=True)
        tile_f = af * hf + bf_
        tile_b = ab * hb + bb
        o_ref[0, rf, :] = tile_f
        hb_ref[rb, :] = tile_b
        return tile_f[LRU_SCAN - 1:LRU_SCAN, :], tile_b[0:1, :]

    hf, hb = lax.fori_loop(0, nt, step, (h0_ref[0, 0:1, :], h0_ref[0, 1:2, :]), unroll=2)
    st_ref[0, 0:1, :] = hf
    st_ref[0, 1:2, :] = hb
    for ti in range(seq // LRU_ROWS):
        rows = slice(ti * LRU_ROWS, (ti + 1) * LRU_ROWS)
        o_ref[0, rows, :] = (o_ref[0, rows, :] + hb_ref[rows, :]) * _gelu_tanh(g_ref[0, rows, :])


def _block_diag(w):
    n, blk, _ = w.shape
    eye = jnp.eye(n, dtype=w.dtype)
    return (eye[:, None, :, None] * w[:, :, None, :]).reshape(n * blk, n * blk)


def rglru_mixer(p, conv_w, conv_b, w_r, b_r, w_i, b_i, lam, h0):
    b, seq, _ = p.shape
    wr = _bf(jnp.stack([_block_diag(w_r[d]) for d in range(2)]))
    wi = _bf(jnp.stack([_block_diag(w_i[d]) for d in range(2)]))
    hw = LRU_HALF
    vec = lambda rows: pl.BlockSpec((rows, hw), lambda bi, h: (0, h))
    return pl.pallas_call(
        _lru_kernel,
        grid=(b, 2),
        in_specs=[
            pl.BlockSpec((1, seq, hw), lambda bi, h: (bi, 0, COL_RX // hw + h)),
            pl.BlockSpec((1, seq, hw), lambda bi, h: (bi, 0, COL_RG // hw + h)),
            vec(LRU_CONV), vec(1),
            pl.BlockSpec((2, hw, hw), lambda bi, h: (0, h, h)), vec(2),
            pl.BlockSpec((2, hw, hw), lambda bi, h: (0, h, h)), vec(2),
            vec(2),
            pl.BlockSpec((1, 2, hw), lambda bi, h: (bi, 0, h)),
        ],
        out_specs=[
            pl.BlockSpec((1, seq, hw), lambda bi, h: (bi, 0, h)),
            pl.BlockSpec((1, 2, hw), lambda bi, h: (bi, 0, h)),
        ],
        out_shape=[jax.ShapeDtypeStruct((b, seq, D_LRU), F32), jax.ShapeDtypeStruct((b, 2, D_LRU), F32)],
        scratch_shapes=[
            pltpu.VMEM((seq + 2 * LRU_PAD, hw), F32),
            pltpu.VMEM((2, seq, hw), F32),
            pltpu.VMEM((2, seq, hw), F32),
            pltpu.VMEM((seq, hw), F32),
        ],
        compiler_params=_cparams(("parallel", "parallel")),
        name="rglru_mixer",
    )(p, p, conv_w, conv_b.reshape(1, D_LRU), wr, b_r, wi, b_i, lam, h0)


HG_TILE = 256
LOG2_E = math.log2(math.e)


def _hg_constants():
    idx = np.arange(HG_TILE)
    same_blk = (idx[:, None] // HG_BLK) == (idx[None, :] // HG_BLK)
    tri_f = same_blk & (idx[None, :] <= idx[:, None])
    tri_b = same_blk & (idx[None, :] >= idx[:, None])
    head = (idx[:, None] // HG_DK) == (idx[None, :] // HG_DK)
    return (jnp.asarray(np.stack([tri_f, tri_b]), BF16), jnp.asarray(same_blk, BF16),
            jnp.asarray(head, BF16), jnp.asarray(head, F32))


def _hgrn_kernel(q_ref, z_ref, v_ref, lb_ref, tri_ref, ones_ref, headb_ref, headf_ref, s0_ref,
                 o_ref, st_ref, state, att0, att1, upd0, upd1, qs, srcs, vs, cums, qds, kns, tots):
    att_bufs = (att0, att1)
    upd_bufs = (upd0, upd1)
    seg_rows = q_ref.shape[1]
    d = pl.program_id(1)
    s = pl.program_id(2)
    nseg = pl.num_programs(2)

    @pl.when(s == 0)
    def _():
        state[...] = s0_ref[0, 0]

    lb = lb_ref[0]
    for ti in range(seg_rows // HG_TILE):
        rows = slice(ti * HG_TILE, (ti + 1) * HG_TILE)
        f = lb + (1.0 - lb) * _sigmoid(z_ref[0, rows, :])
        g = jnp.log(f)
        kk = 1.0 - f
        q = _silu(q_ref[0, rows, :]) * HG_SCALE
        cum = _dot_exact_rhs(tri_ref[0], g)
        tot = _dot_exact_rhs(ones_ref[...], g)
        qs[rows, :] = q
        srcs[rows, :] = (cum - jnp.log(kk)) * LOG2_E
        vs[rows, :] = v_ref[0, rows, :]
        cums[rows, :] = cum * LOG2_E
        tots[rows, :] = tot
        qds[rows, :] = q * jnp.exp(cum)
        kns[rows, :] = kk * jnp.exp(tot - cum)

    nblk = seg_rows // HG_BLK
    sgn = jnp.where(d == 0, 1, -1)
    t_signed = [(lax.broadcasted_iota(jnp.int32, (8, D_HG), 0) + 8 * h) * sgn for h in range(HG_BLK // 8)]

    n_half = HG_BLK // 8

    def block_rows(j):
        je = jnp.where(d == 0, j, nblk - 1 - j)
        return pl.multiple_of(je * HG_BLK, HG_BLK)

    def prepare(j, slot):
        r0 = block_rows(j)
        rows = pl.ds(r0, HG_BLK)
        cum_h = [cums[pl.ds(r0 + 8 * h, 8), :] for h in range(n_half)]
        q_h = [qs[pl.ds(r0 + 8 * h, 8), :] for h in range(n_half)]
        tiles = []
        for src in range(HG_BLK):
            src_row = srcs[pl.ds(r0 + src, 1), :]
            for h in range(n_half):
                pair = jnp.exp2(cum_h[h] - src_row) * q_h[h]
                tiles.append(jnp.where(t_signed[h] >= src * sgn, pair, 0.0))
        att_bufs[slot][...] = _dot(_bf(jnp.concatenate(tiles, axis=0)), headb_ref[...])
        upd = lax.dot_general(_bf(vs[rows, :]), _bf(kns[rows, :]), (((0,), (0,)), ((), ())),
                              preferred_element_type=F32)
        upd_bufs[slot][...] = upd * headf_ref[...]

    def finish(j, slot):
        r0 = block_rows(j)
        rows = pl.ds(r0, HG_BLK)
        o_halves = [None] * n_half
        for src in range(HG_BLK):
            v_row = vs[pl.ds(r0 + src, 1), :]
            for h in range(n_half):
                term = att_bufs[slot][src * HG_BLK + h * 8:src * HG_BLK + (h + 1) * 8, :] * v_row
                o_halves[h] = term if o_halves[h] is None else o_halves[h] + term
        st = state[...]
        o_ref[0, 0, rows, :] = jnp.concatenate(o_halves, axis=0) + _dot_nt(_bf(qds[rows, :]), _bf(st))
        state[...] = jnp.exp(tots[pl.ds(r0, 1), :]) * st + upd_bufs[slot][...]

    prepare(0, 0)

    def block_pair(i, carry):
        j = 2 * i
        prepare(j + 1, 1)
        finish(j, 0)
        prepare(jnp.minimum(j + 2, nblk - 1), 0)
        finish(j + 1, 1)
        return carry

    lax.fori_loop(0, nblk // 2, block_pair, 0)

    @pl.when(s == nseg - 1)
    def _():
        st_ref[0, 0] = state[...]


def hgrn2_scan(p, lb, s0_t, seg_rows):
    b, seq, _ = p.shape
    nseg = seq // seg_rows
    tri, ones, headb, headf = _hg_constants()
    seg = lambda d, s: jnp.where(d == 0, s, nseg - 1 - s)
    col = lambda c: pl.BlockSpec((1, seg_rows, D_HG), lambda bi, d, s: (bi, seg(d, s), c // D_HG))
    scr = lambda: pltpu.VMEM((seg_rows, D_HG), F32)
    return pl.pallas_call(
        _hgrn_kernel,
        grid=(b, 2, nseg),
        in_specs=[
            col(COL_HQ),
            pl.BlockSpec((1, seg_rows, D_HG), lambda bi, d, s: (bi, seg(d, s), COL_HFF // D_HG + d)),
            col(COL_HI),
            pl.BlockSpec((1, 1, D_HG), lambda bi, d, s: (d, 0, 0)),
            pl.BlockSpec((1, HG_TILE, HG_TILE), lambda bi, d, s: (d, 0, 0)),
            _const_spec((HG_TILE, HG_TILE)),
            _const_spec((D_HG, D_HG)),
            _const_spec((D_HG, D_HG)),
            pl.BlockSpec((1, 1, D_HG, D_HG), lambda bi, d, s: (bi, d, 0, 0)),
        ],
        out_specs=[
            pl.BlockSpec((1, 1, seg_rows, D_HG), lambda bi, d, s: (d, bi, seg(d, s), 0)),
            pl.BlockSpec((1, 1, D_HG, D_HG), lambda bi, d, s: (bi, d, 0, 0)),
        ],
        out_shape=[jax.ShapeDtypeStruct((2, b, seq, D_HG), F32), jax.ShapeDtypeStruct((b, 2, D_HG, D_HG), F32)],
        scratch_shapes=[pltpu.VMEM((D_HG, D_HG), F32)]
        + [pltpu.VMEM((HG_BLK * HG_BLK, D_HG), F32) for _ in range(2)]
        + [pltpu.VMEM((D_HG, D_HG), F32) for _ in range(2)] + [scr() for _ in range(7)],
        compiler_params=_cparams(("parallel", "parallel", "arbitrary")),
        name="hgrn2_scan",
    )(p, p, p, lb.reshape(2, 1, D_HG), tri, ones, headb, headf, s0_t)


def _hg_post_kernel(of_ref, ob_ref, g_ref, nw_ref, headb_ref, o_ref):
    o = of_ref[0] + ob_ref[0]
    sq = o * o
    hi = _bf(sq)
    lo = _bf(sq - hi.astype(F32))
    ms = (_dot(hi, headb_ref[...]) + _dot(lo, headb_ref[...])) * (1.0 / HG_DV)
    o_ref[...] = o * lax.rsqrt(ms + EPS) * nw_ref[...] * _silu(g_ref[...])


def hgrn2_output(o_dirs, p2d, norm_w):
    t = p2d.shape[0]
    tm = ROW_TILE
    _, _, headb, _ = _hg_constants()
    return pl.pallas_call(
        _hg_post_kernel,
        grid=(t // tm,),
        in_specs=[
            pl.BlockSpec((1, tm, D_HG), lambda i: (0, i, 0)),
            pl.BlockSpec((1, tm, D_HG), lambda i: (1, i, 0)),
            pl.BlockSpec((tm, D_HG), lambda i: (i, COL_HGATE // D_HG)),
            _const_spec((1, D_HG)),
            _const_spec((D_HG, D_HG)),
        ],
        out_specs=pl.BlockSpec((tm, D_HG), lambda i: (i, 0)),
        out_shape=jax.ShapeDtypeStruct((t, D_HG), F32),
        compiler_params=_cparams(("parallel",)),
        name="hgrn2_output",
    )(o_dirs, o_dirs, p2d, norm_w.reshape(1, D_HG), headb)


def _hg_state_to_t(s):
    b = s.shape[0]
    st = jnp.swapaxes(s, -1, -2)
    eye = jnp.eye(HG_HEADS, dtype=s.dtype)
    full = eye[None, None, :, None, :, None] * st[:, :, :, :, None, :]
    return full.reshape(b, 2, D_HG, D_HG)


def _hg_state_from_t(st):
    b = st.shape[0]
    r = st.reshape(b, 2, HG_HEADS, HG_DV, HG_HEADS, HG_DK)
    blocks = jnp.stack([r[:, :, h, :, h, :] for h in range(HG_HEADS)], axis=2)
    return jnp.swapaxes(blocks, -1, -2)


HY_COLS = 2 * HY_ORDER * D_HY
HY_FEAT_PAD = 128
HY_GEN_ROWS = 256


def _dot_hi(a, b):
    ah = _bf(a)
    al = _bf(a - ah.astype(F32))
    bh = _bf(b)
    bl = _bf(b - bh.astype(F32))
    return _dot(ah, bh) + _dot(ah, bl) + _dot(al, bh)


HY_HALF_COLS = HY_ORDER * D_HY


def _filter_gen_kernel(feat_ref, w1_ref, b1_ref, w2_ref, b2_ref, w3_ref, fr_ref, dl_ref, h_ref, ssq_ref, *, seq):
    i = pl.program_id(0)
    tm = feat_ref.shape[0]
    feat = feat_ref[...]
    fr = fr_ref[...]
    h = jnp.sin(fr * (_dot_hi(feat, w1_ref[...]) + b1_ref[...]))
    h = jnp.sin(fr * (_dot_hi(h, w2_ref[...]) + b2_ref[...]))
    h = _dot_hi(h, w3_ref[...])
    decay = jnp.exp(-feat[:, 0:1] * dl_ref[...])
    decay = jnp.concatenate([decay] * HY_ORDER, axis=1)
    row = i * tm + lax.broadcasted_iota(jnp.int32, (tm, HY_HALF_COLS), 0)
    taps = jnp.where(row < seq, h[:, :HY_HALF_COLS], h[:, HY_HALF_COLS:]) * decay
    h_ref[...] = jnp.where(row == seq, 0.0, taps)

    @pl.when(i == 0)
    def _():
        ssq_ref[...] = jnp.zeros_like(ssq_ref)

    ssq_ref[...] += jnp.sum(taps * taps, axis=0, keepdims=True)


def _hyena_features(seq):
    order = np.concatenate([np.arange(seq), [0], np.arange(seq - 1, 0, -1)])
    pos = order.astype(np.float32)
    t = pos / np.float32(max(seq - 1, 1))
    bands = np.linspace(1e-4, HY_BANDS - 1, HY_BANDS, dtype=np.float32)
    ang = np.float32(2.0 * math.pi / seq) * pos[:, None] * bands[None, :]
    feat = np.concatenate([t[:, None], np.cos(ang), np.sin(ang)], axis=-1).astype(np.float32)
    out = np.zeros((2 * seq, HY_FEAT_PAD), np.float32)
    out[:, :HY_EMB] = feat
    return jnp.asarray(out)


def hyena_filter_gen(seq, w1, b1, w2, b2, w3, freq):
    feat = _hyena_features(seq)
    w1p = jnp.zeros((HY_FEAT_PAD, HY_FFN), F32).at[:HY_EMB].set(w1.astype(F32))
    deltas = jnp.asarray(np.linspace(HY_DECAY_SLOW, HY_DECAY_FAST, D_HY, dtype=np.float32)).reshape(1, D_HY)
    tm = HY_GEN_ROWS
    return pl.pallas_call(
        functools.partial(_filter_gen_kernel, seq=seq),
        grid=(2 * seq // tm,),
        in_specs=[
            pl.BlockSpec((tm, HY_FEAT_PAD), lambda i: (i, 0)),
            _const_spec((HY_FEAT_PAD, HY_FFN)), _const_spec((1, HY_FFN)),
            _const_spec((HY_FFN, HY_FFN)), _const_spec((1, HY_FFN)),
            _const_spec((HY_FFN, HY_COLS)), _const_spec((1, HY_FFN)), _const_spec((1, D_HY)),
        ],
        out_specs=[pl.BlockSpec((tm, HY_HALF_COLS), lambda i: (i, 0)), _const_spec((1, HY_HALF_COLS))],
        out_shape=[jax.ShapeDtypeStruct((2 * seq, HY_HALF_COLS), F32),
                   jax.ShapeDtypeStruct((1, HY_HALF_COLS), F32)],
        compiler_params=_cparams(("arbitrary",)),
        name="hyena_filter_gen",
    )(feat, w1p, b1.reshape(1, HY_FFN), w2, b2.reshape(1, HY_FFN), w3, freq.reshape(1, HY_FFN), deltas)


def _filter_norm(ssq_ref, n):
    return lax.rsqrt(ssq_ref[:, n * D_HY:(n + 1) * D_HY])


def _conv3(pad_ref, w_ref, b_ref, r0, rows, pad):
    acc = b_ref[...]
    for k in range(HY_SHORT):
        lo = r0 + pad - HY_SHORT // 2 + k
        acc = acc + w_ref[k:k + 1, :] * pad_ref[lo:lo + rows, :]
    return acc


def _dft_constants(seq):
    n = 2 * seq
    k = np.arange(n)
    ang = 2.0 * np.pi * np.outer(k, k) / n
    fwd = np.concatenate([np.cos(ang), -np.sin(ang)], axis=0)
    inv = np.concatenate([np.cos(ang[:seq]), -np.sin(ang[:seq])], axis=1) / n
    return jnp.asarray(fwd, BF16), jnp.asarray(inv, BF16)


def _ctx_spectrum_kernel(filt_ref, ssq_ref, fwd_ref, h_ref):
    n = pl.program_id(0)
    norm = jnp.where(n == 0, _filter_norm(ssq_ref, 0), _filter_norm(ssq_ref, 1))
    x = filt_ref[...] * norm
    xh = _bf(x)
    xl = _bf(x - xh.astype(F32))
    h_ref[0] = _dot(fwd_ref[...], xh) + _dot(fwd_ref[...], xl)


def hyena_ctx_spectrum(filt, ssq):
    n = filt.shape[0]
    fwd, _ = _dft_constants(n // 2)
    return pl.pallas_call(
        _ctx_spectrum_kernel,
        grid=(HY_ORDER,),
        in_specs=[pl.BlockSpec((n, D_HY), lambda o: (0, o)), _const_spec((1, HY_HALF_COLS)),
                  _const_spec((2 * n, n))],
        out_specs=pl.BlockSpec((1, 2 * n, D_HY), lambda o: (o, 0, 0)),
        out_shape=jax.ShapeDtypeStruct((HY_ORDER, 2 * n, D_HY), F32),
        compiler_params=_cparams(("parallel",)),
        name="hyena_ctx_spectrum",
    )(filt, ssq, fwd)


HY_PAD = 8


def _hyena_ctx_kernel(v_ref, x1_ref, x2_ref, cw_ref, cb_ref, bias_ref, spec_ref, fwd_ref, inv_ref, o_ref, pad_ref):
    seq = v_ref.shape[1]
    n = 2 * seq
    zeros = jnp.zeros((HY_PAD, D_HY), F32)
    pad_ref[0:HY_PAD, :] = zeros
    pad_ref[HY_PAD + seq:2 * HY_PAD + seq, :] = zeros

    def short_conv(src_ref, part):
        pad_ref[HY_PAD:HY_PAD + seq, :] = src_ref[0]
        return _conv3(pad_ref, cw_ref.at[part], cb_ref.at[part], 0, seq, HY_PAD)

    z = short_conv(v_ref, 0)
    for order, gate_ref in enumerate((x1_ref, x2_ref)):
        spec = _dot(fwd_ref[:, 0:seq], _bf(z))
        xr, xi = spec[:n], spec[n:]
        hr, hi = spec_ref[order, 0:n, :], spec_ref[order, n:2 * n, :]
        prod = jnp.concatenate([xr * hr - xi * hi, xr * hi + xi * hr], axis=0)
        conv = _dot(inv_ref[...], _bf(prod))
        z = short_conv(gate_ref, order + 1) * (conv + bias_ref[order:order + 1, :] * z)
    o_ref[0] = z


def hyena_ctx(p, conv_w, conv_b, bias, spec):
    b, seq, _ = p.shape
    fwd, inv = _dft_constants(seq)
    col = lambda c: pl.BlockSpec((1, seq, D_HY), lambda i: (i, 0, c))
    c0 = COL_HY // D_HY
    return pl.pallas_call(
        _hyena_ctx_kernel,
        grid=(b,),
        in_specs=[
            col(c0), col(c0 + 1), col(c0 + 2),
            _const_spec((3, HY_SHORT, D_HY)), _const_spec((3, 1, D_HY)), _const_spec((HY_ORDER, D_HY)),
            _const_spec((HY_ORDER, 4 * seq, D_HY)), _const_spec((4 * seq, 2 * seq)), _const_spec((seq, 4 * seq)),
        ],
        out_specs=pl.BlockSpec((1, seq, D_HY), lambda i: (i, 0, 0)),
        out_shape=jax.ShapeDtypeStruct((b, seq, D_HY), F32),
        scratch_shapes=[pltpu.VMEM((seq + 2 * HY_PAD, D_HY), F32)],
        compiler_params=_cparams(("parallel",)),
        name="hyena_ctx",
    )(p, p, p, conv_w.reshape(HY_SHORT, 3, D_HY).transpose(1, 0, 2), conv_b.reshape(3, 1, D_HY), bias, spec,
      fwd, inv)


FFT_N1 = 64
FFT_N2 = 128
FFT_N = FFT_N1 * FFT_N2
FFT_K1 = FFT_N1 // 2 + 1
FFT_K1_PAD = 40
FFT_ROWS = FFT_K1_PAD * FFT_N2
HY_LANES = 128
FFT_UNROLL1 = 16
FFT_UNROLL2 = 11


def _fft_constants():
    n1, n2, n, kp = FFT_N1, FFT_N2, FFT_N, FFT_K1_PAD
    k1 = np.arange(kp)
    w1 = np.exp(-2j * np.pi * np.outer(k1, np.arange(n1)) / n1)
    tw = np.exp(-2j * np.pi * np.outer(np.arange(n2), k1) / n)
    f1c = tw[:, :, None] * w1[None, :, :]
    f1t = np.concatenate([f1c.real, f1c.imag], axis=1).transpose(0, 2, 1)
    w2 = np.exp(-2j * np.pi * np.outer(np.arange(n2), np.arange(n2)) / n2)
    f2 = np.block([[w2.real, -w2.imag], [w2.imag, w2.real]])
    g2c = np.conj(w2)
    g2 = np.block([[g2c.real, -g2c.imag], [g2c.imag, g2c.real]])
    weight = np.where((k1 == 0) | (k1 == n1 // 2), 1.0, 2.0) * (k1 < FFT_K1)
    mc = np.conj(w1.T)[None, :n1 // 2, :] * np.conj(tw)[:, None, :] * weight / n
    g1 = np.concatenate([mc.real, -mc.imag], axis=2)
    return (jnp.asarray(f1t, BF16), jnp.asarray(f2, BF16), jnp.asarray(g2, BF16), jnp.asarray(g1, BF16))


def _dot_tn(a, b):
    return lax.dot_general(a, b, (((0,), (0,)), ((), ())), preferred_element_type=F32)


def _fft_stage1(x_ref, f1t_ref, a_ref, n1_in):
    def body(n2, carry):
        rows = x_ref[pl.ds(n2, n1_in, stride=FFT_N2), :]
        out = _dot_tn(f1t_ref[n2, 0:n1_in, :], _bf(rows))
        a_ref[0, pl.ds(n2, FFT_K1, stride=FFT_N2), :] = out[:FFT_K1]
        a_ref[1, pl.ds(n2, FFT_K1, stride=FFT_N2), :] = out[FFT_K1_PAD:FFT_K1_PAD + FFT_K1]
        return carry
    lax.fori_loop(0, FFT_N2, body, 0, unroll=FFT_UNROLL1)


def _lat_spectrum_kernel(filt_ref, ssq_ref, f1t_ref, f2_ref, h_ref, a_ref):
    n = pl.program_id(0)
    half = pl.program_id(1)
    norm_full = jnp.where(n == 0, _filter_norm(ssq_ref, 0), _filter_norm(ssq_ref, 1))
    norm = jnp.where(half == 0, norm_full[:, :HY_LANES], norm_full[:, HY_LANES:])
    _fft_stage1(filt_ref, f1t_ref, a_ref, FFT_N1)

    def stage2(k1, carry):
        r0 = pl.multiple_of(k1 * FFT_N2, FFT_N2)
        rows = pl.ds(r0, FFT_N2)
        a = jnp.concatenate([a_ref[0, rows, :], a_ref[1, rows, :]], axis=0)
        b = _dot(f2_ref[...], _bf(a)) * norm
        h_ref[0, 0, rows, :] = _bf(b[:FFT_N2])
        h_ref[0, 1, rows, :] = _bf(b[FFT_N2:])
        return carry
    lax.fori_loop(0, FFT_K1, stage2, 0)


def hyena_lat_spectrum(filt, ssq):
    f1t, f2, _, _ = _fft_constants()
    rows = FFT_K1 * FFT_N2
    halves = D_HY // HY_LANES
    return pl.pallas_call(
        _lat_spectrum_kernel,
        grid=(HY_ORDER, halves),
        in_specs=[
            pl.BlockSpec((FFT_N, HY_LANES), lambda o, h: (0, o * halves + h)),
            _const_spec((1, HY_HALF_COLS)),
            _const_spec((FFT_N2, FFT_N1, 2 * FFT_K1_PAD)),
            _const_spec((2 * FFT_N2, 2 * FFT_N2)),
        ],
        out_specs=pl.BlockSpec((1, 2, rows, HY_LANES), lambda o, h: (o, 0, 0, h)),
        out_shape=jax.ShapeDtypeStruct((HY_ORDER, 2, rows, D_HY), BF16),
        scratch_shapes=[pltpu.VMEM((2, FFT_ROWS, HY_LANES), F32)],
        compiler_params=_cparams(("parallel", "parallel")),
        name="hyena_lat_spectrum",
    )(filt, ssq, f1t, f2)


HY_ROWS = 512


def _hyena_lat_kernel(v_ref, x1_ref, x2_ref, cw_ref, cb_ref, bias_ref, spec_ref, f1t_ref, f2_ref, g2_ref, g1_ref,
                      o_ref, pad_ref, z_ref, a_ref):
    seq = v_ref.shape[1]
    zeros = jnp.zeros((HY_PAD, HY_LANES), F32)
    pad_ref[0:HY_PAD, :] = zeros
    pad_ref[HY_PAD + seq:2 * HY_PAD + seq, :] = zeros
    pad_ref[HY_PAD:HY_PAD + seq, :] = v_ref[0]
    for ti in range(seq // HY_ROWS):
        r0 = ti * HY_ROWS
        z_ref[r0:r0 + HY_ROWS, :] = _conv3(pad_ref, cw_ref.at[0], cb_ref.at[0], r0, HY_ROWS, HY_PAD)
    tail = jnp.zeros((FFT_ROWS - FFT_K1 * FFT_N2, HY_LANES), F32)
    a_ref[0, FFT_K1 * FFT_N2:FFT_ROWS, :] = tail
    a_ref[1, FFT_K1 * FFT_N2:FFT_ROWS, :] = tail

    for order, gate_ref in enumerate((x1_ref, x2_ref)):
        _fft_stage1(z_ref, f1t_ref, a_ref, FFT_N1 // 2)

        def stage2(k1, carry):
            r0 = pl.multiple_of(k1 * FFT_N2, FFT_N2)
            rows = pl.ds(r0, FFT_N2)
            a = jnp.concatenate([a_ref[0, rows, :], a_ref[1, rows, :]], axis=0)
            b = _dot(f2_ref[...], _bf(a))
            br, bi = b[:FFT_N2], b[FFT_N2:]
            hr = spec_ref[order, 0, rows, :].astype(F32)
            hi = spec_ref[order, 1, rows, :].astype(F32)
            y = jnp.concatenate([br * hr - bi * hi, br * hi + bi * hr], axis=0)
            c = _dot(g2_ref[...], _bf(y))
            a_ref[0, rows, :] = c[:FFT_N2]
            a_ref[1, rows, :] = c[FFT_N2:]
            return carry
        lax.fori_loop(0, FFT_K1, stage2, 0, unroll=FFT_UNROLL2)

        def stage1_inv(n2, carry):
            c = jnp.concatenate([a_ref[0, pl.ds(n2, FFT_K1_PAD, stride=FFT_N2), :],
                                 a_ref[1, pl.ds(n2, FFT_K1_PAD, stride=FFT_N2), :]], axis=0)
            o_ref[0, pl.ds(n2, FFT_N1 // 2, stride=FFT_N2), :] = _dot(g1_ref[n2], _bf(c))
            return carry
        lax.fori_loop(0, FFT_N2, stage1_inv, 0, unroll=FFT_UNROLL1)

        pad_ref[HY_PAD:HY_PAD + seq, :] = gate_ref[0]
        for ti in range(seq // HY_ROWS):
            r0 = ti * HY_ROWS
            rows = slice(r0, r0 + HY_ROWS)
            gate = _conv3(pad_ref, cw_ref.at[order + 1], cb_ref.at[order + 1], r0, HY_ROWS, HY_PAD)
            z = gate * (o_ref[0, rows, :] + bias_ref[order:order + 1, :] * z_ref[rows, :])
            if order + 1 < HY_ORDER:
                z_ref[rows, :] = z
            else:
                o_ref[0, rows, :] = z


def hyena_lat(p, conv_w, conv_b, bias, spec):
    b, seq, _ = p.shape
    assert 2 * seq == FFT_N
    f1t, f2, g2, g1 = _fft_constants()
    hw = HY_LANES
    col = lambda c: pl.BlockSpec((1, seq, hw), lambda i, h: (i, 0, (COL_HY + c * D_HY) // hw + h))
    return pl.pallas_call(
        _hyena_lat_kernel,
        grid=(b, D_HY // hw),
        in_specs=[
            col(0), col(1), col(2),
            pl.BlockSpec((3, HY_SHORT, hw), lambda i, h: (0, 0, h)),
            pl.BlockSpec((3, 1, hw), lambda i, h: (0, 0, h)),
            pl.BlockSpec((HY_ORDER, hw), lambda i, h: (0, h)),
            pl.BlockSpec((HY_ORDER, 2, FFT_K1 * FFT_N2, hw), lambda i, h: (0, 0, 0, h)),
            _const_spec((FFT_N2, FFT_N1, 2 * FFT_K1_PAD)),
            _const_spec((2 * FFT_N2, 2 * FFT_N2)),
            _const_spec((2 * FFT_N2, 2 * FFT_N2)),
            _const_spec((FFT_N2, FFT_N1 // 2, 2 * FFT_K1_PAD)),
        ],
        out_specs=pl.BlockSpec((1, seq, hw), lambda i, h: (i, 0, h)),
        out_shape=jax.ShapeDtypeStruct((b, seq, D_HY), F32),
        scratch_shapes=[
            pltpu.VMEM((seq + 2 * HY_PAD, hw), F32),
            pltpu.VMEM((seq, hw), F32),
            pltpu.VMEM((2, FFT_ROWS, hw), F32),
        ],
        compiler_params=_cparams(("parallel", "parallel")),
        name="hyena_lat",
    )(p, p, p, conv_w.reshape(HY_SHORT, 3, D_HY).transpose(1, 0, 2), conv_b.reshape(3, 1, D_HY), bias, spec,
      f1t, f2, g2, g1)


HG_SEG_ROWS = 512
COND_ROWS = 16


def _trunk_layer(x, mods, lp, lb, batch, seq, ctx, final):
    x = ffn_half_step(x, mods, lp['ln_ffn1'], lp['w_ffn1_in'], lp['w_ffn1_out'], lp['final_norm'],
                      mod_base=0, final=False, seq_len=seq)
    p2d = mixer_in_proj(x, mods, lp['ln_mix'], lp['w_in'], seq_len=seq)
    p = p2d.reshape(batch, seq, N_IN)

    filt, ssq = hyena_filter_gen(seq, lp['hy_w1'], lp['hy_b1'], lp['hy_w2'], lp['hy_b2'], lp['hy_w3'],
                                 lp['hy_freq'])
    if ctx is None:
        att = context_attention(p, lp['attn_sink'])
        lru_h0 = jnp.zeros((batch, 2, D_LRU), F32)
        hg_s0 = jnp.zeros((batch, 2, D_HG, D_HG), F32)
        hy = hyena_ctx(p, lp['hy_conv_w'], lp['hy_conv_b'], lp['hy_bias'], hyena_ctx_spectrum(filt, ssq))
    else:
        k_ctx, v_ctx, lru_h0, hg_state = ctx
        att = latent_attention(p, k_ctx, v_ctx, lp['attn_sink'])
        hg_s0 = _hg_state_to_t(hg_state.astype(F32))
        hy = hyena_lat(p, lp['hy_conv_w'], lp['hy_conv_b'], lp['hy_bias'], hyena_lat_spectrum(filt, ssq))
    lru, lru_state = rglru_mixer(p, lp['lru_conv_w'], lp['lru_conv_b'], lp['lru_w_r'], lp['lru_b_r'],
                                 lp['lru_w_i'], lp['lru_b_i'], lp['lru_lambda'], lru_h0.astype(F32))
    o_dirs, hg_state_t = hgrn2_scan(p, lb, hg_s0, min(seq, HG_SEG_ROWS))
    t = batch * seq
    hg = hgrn2_output(o_dirs.reshape(2, t, D_HG), p2d, lp['hg_norm'])

    x = gated_merge(x, mods, lp['ln_mix'], att.reshape(t, D_ATTN), lru.reshape(t, D_LRU), hg,
                    hy.reshape(t, D_HY), lp['w_gate'], lp['w_bo_attn'], lp['w_bo_lru'], lp['w_bo_hg'],
                    lp['w_bo_hy'], lp['w_out'], seq_len=seq)
    x = ffn_half_step(x, mods, lp['ln_ffn2'], lp['w_ffn2_in'], lp['w_ffn2_out'], lp['final_norm'],
                      mod_base=6, final=final, seq_len=seq)
    state = None
    if ctx is None:
        k = p[:, :, COL_K:COL_K + D_KV].reshape(batch, seq, N_KV_HEADS, HEAD_DIM)
        v = p[:, :, COL_V:COL_V + D_KV].reshape(batch, seq, N_KV_HEADS, HEAD_DIM)
        state = (k, v, lru_state, _hg_state_from_t(hg_state_t))
    return x, state


def kernel(x_prompt, x_sample, cache_k, cache_v, state_lru, state_hgrn, c, c_ctx, ln_ffn1, ln_mix, ln_ffn2, w_ada, b_ada, w_ffn1_in, w_ffn1_out, w_ffn2_in, w_ffn2_out, w_in, attn_sink, lru_conv_w, lru_conv_b, lru_w_r, lru_b_r, lru_w_i, lru_b_i, lru_lambda, hg_lb_logits, hg_norm, hy_conv_w, hy_conv_b, hy_w1, hy_b1, hy_w2, hy_b2, hy_w3, hy_freq, hy_bias, w_bo_attn, w_bo_lru, w_bo_hg, w_bo_hy, w_gate, w_out, final_norm):
    batch, seq, _ = x_prompt.shape
    dec_batch, dec_seq, _ = x_sample.shape
    lb_soft = jax.nn.softmax(hg_lb_logits.astype(F32), axis=0)
    lb_all = jnp.cumsum(lb_soft, axis=0) - lb_soft[0]

    stacked = dict(ln_ffn1=ln_ffn1, ln_mix=ln_mix, ln_ffn2=ln_ffn2, w_ffn1_in=w_ffn1_in, w_ffn1_out=w_ffn1_out,
                   w_ffn2_in=w_ffn2_in, w_ffn2_out=w_ffn2_out, w_in=w_in, attn_sink=attn_sink,
                   lru_conv_w=lru_conv_w, lru_conv_b=lru_conv_b, lru_w_r=lru_w_r, lru_b_r=lru_b_r,
                   lru_w_i=lru_w_i, lru_b_i=lru_b_i, lru_lambda=lru_lambda, hg_norm=hg_norm,
                   hy_conv_w=hy_conv_w, hy_conv_b=hy_conv_b, hy_w1=hy_w1, hy_b1=hy_b1, hy_w2=hy_w2, hy_b2=hy_b2,
                   hy_w3=hy_w3, hy_freq=hy_freq, hy_bias=hy_bias, w_bo_attn=w_bo_attn, w_bo_lru=w_bo_lru,
                   w_bo_hg=w_bo_hg, w_bo_hy=w_bo_hy, w_gate=w_gate, w_out=w_out)

    cond = jnp.zeros((COND_ROWS, D_MODEL), F32).at[0].set(c_ctx).at[1:1 + dec_batch].set(c)
    mods = ada_mods(cond, w_ada, b_ada).reshape(DEPTH, COND_ROWS, N_MOD, D_MODEL)

    h = x_prompt.reshape(batch * seq, D_MODEL)
    z = x_sample.reshape(dec_batch * dec_seq, D_MODEL)
    ks, vs, lrus, hgs = [], [], [], []
    for l in range(DEPTH):
        lp = {name: w[l] for name, w in stacked.items()}
        lp['final_norm'] = final_norm
        final = l == DEPTH - 1
        h, (k_l, v_l, lru_l, hg_l) = _trunk_layer(h, mods[l, 0:1], lp, lb_all[l], batch, seq, None, final)
        ks.append(k_l)
        vs.append(v_l)
        lrus.append(lru_l)
        hgs.append(hg_l)
        ctx = (cache_k[:, l].reshape(dec_batch, -1, D_KV), cache_v[:, l].reshape(dec_batch, -1, D_KV),
               state_lru[:, l], state_hgrn[:, l])
        z, _ = _trunk_layer(z, mods[l, 1:1 + dec_batch], lp, lb_all[l], dec_batch, dec_seq, ctx, final)

    y_prompt = h.reshape(batch, seq, D_MODEL)
    y_sample = z.reshape(dec_batch, dec_seq, D_MODEL)
    return (y_prompt, y_sample, jnp.stack(ks, axis=1), jnp.stack(vs, axis=1), jnp.stack(lrus, axis=1),
            jnp.stack(hgs, axis=1))
```

```python
import functools
import math

import numpy as np
import jax
import jax.numpy as jnp
from jax import lax
from jax.experimental import pallas as pl
from jax.experimental.pallas import tpu as pltpu

F32 = jnp.float32
BF16 = jnp.bfloat16

D_MODEL = 1024
DEPTH = 2
GRID_W = 64
EPS = 1e-6
N_MOD = 9
N_BRANCH = 4
D_FF = 2816
N_HEADS = 8
N_KV_HEADS = 2
HEAD_DIM = 64
GQA_GROUP = N_HEADS // N_KV_HEADS
D_ATTN = N_HEADS * HEAD_DIM
D_KV = N_KV_HEADS * HEAD_DIM
WINDOW = 128
ATTN_BLOCK = 128
ATTN_SCALE = HEAD_DIM ** -0.5
ROPE_BASE = 10000.0
D_LRU = D_MODEL // 4
LRU_HEADS = 4
LRU_BLOCK = D_LRU // LRU_HEADS
LRU_CONV = 4
LRU_C = 8.0
HG_HEADS = 4
HG_DK = 64
HG_DV = 64
D_HG = HG_HEADS * HG_DK
HG_SCALE = HG_DK ** -0.5
HG_BLK = 16
D_HY = D_MODEL // 4
HY_ORDER = 2
HY_SHORT = 3
HY_BANDS = 8
HY_EMB = 2 * HY_BANDS + 1
HY_FFN = 64
HY_TARGET = 1e-2
HY_DECAY_SLOW = -math.log(HY_TARGET) / 1.5
HY_DECAY_FAST = -math.log(HY_TARGET) / 0.3
N_IN = D_ATTN + 2 * D_KV + 2 * D_LRU + 5 * D_HG + 3 * D_HY

COL_Q = 0
COL_K = D_ATTN
COL_V = D_ATTN + D_KV
COL_RX = D_ATTN + 2 * D_KV
COL_RG = COL_RX + D_LRU
COL_HQ = COL_RG + D_LRU
COL_HFF = COL_HQ + D_HG
COL_HFB = COL_HFF + D_HG
COL_HI = COL_HFB + D_HG
COL_HGATE = COL_HI + D_HG
COL_HY = COL_HGATE + D_HG

FF_CHUNK = 256
N_FF_CHUNK = D_FF // FF_CHUNK
ROW_TILE = 512
VMEM_LIMIT = 56 * 1024 * 1024


def _bf(x):
    return x.astype(BF16)


def _dot(a, b):
    return jnp.dot(a, b, preferred_element_type=F32)


def _split3(x):
    hi = _bf(x)
    r1 = x - hi.astype(F32)
    mid = _bf(r1)
    lo = _bf(r1 - mid.astype(F32))
    return hi, mid, lo


def _dot_exact_rhs(mat_bf, x):
    hi, mid, lo = _split3(x)
    return _dot(mat_bf, hi) + _dot(mat_bf, mid) + _dot(mat_bf, lo)


def _sigmoid(x):
    return jax.nn.sigmoid(x)


def _silu(x):
    return x * _sigmoid(x)


def _norm_mod(x, lnw, shift, scale):
    ms = jnp.mean(x * x, axis=-1, keepdims=True)
    y = x * lax.rsqrt(ms + EPS) * lnw
    return y * (1.0 + scale) + shift


def _cparams(sem):
    return pltpu.CompilerParams(dimension_semantics=sem, vmem_limit_bytes=VMEM_LIMIT)


def _const_spec(shape):
    nd = len(shape)
    return pl.BlockSpec(shape, lambda *_: (0,) * nd)


def _ada_kernel(cond_ref, w_ref, b_ref, o_ref):
    c = cond_ref[...]
    o_ref[0] = _dot(_bf(_silu(c)), _bf(w_ref[0])) + b_ref[0]


def ada_mods(cond, w_ada, b_ada):
    r = cond.shape[0]
    cb = 1024
    ncol = (N_MOD * D_MODEL) // cb
    return pl.pallas_call(
        _ada_kernel,
        grid=(DEPTH, ncol),
        in_specs=[
            pl.BlockSpec((r, D_MODEL), lambda l, j: (0, 0)),
            pl.BlockSpec((1, D_MODEL, cb), lambda l, j: (l, 0, j)),
            pl.BlockSpec((1, 1, cb), lambda l, j: (l, 0, j)),
        ],
        out_specs=pl.BlockSpec((1, r, cb), lambda l, j: (l, 0, j)),
        out_shape=jax.ShapeDtypeStruct((DEPTH, r, N_MOD * D_MODEL), F32),
        compiler_params=_cparams(("parallel", "parallel")),
        name="ada_mods",
    )(cond, w_ada, b_ada.reshape(DEPTH, 1, N_MOD * D_MODEL))


def _seq_mod_index(seq_len, n_mod, tm):
    def idx(i):
        return jnp.minimum((i * tm) // seq_len, n_mod - 1)
    return idx


def _ffn_kernel(x_ref, mod_ref, ln_ref, wg_ref, wu_ref, wo_ref, fn_ref, o_ref, *, mod_base, final):
    x = x_ref[...]
    m = mod_ref[0]
    sh = m[mod_base:mod_base + 1]
    sc = m[mod_base + 1:mod_base + 2]
    g = m[mod_base + 2:mod_base + 3]
    h = _bf(_norm_mod(x, ln_ref[...], sh, sc))
    acc = None
    for c in range(N_FF_CHUNK):
        gate = _dot(h, wg_ref[c])
        up = _dot(h, wu_ref[c])
        part = _dot(_bf(_silu(gate) * up), wo_ref[c])
        acc = part if acc is None else acc + part
    y = x + 0.5 * g * acc
    if final:
        ms = jnp.mean(y * y, axis=-1, keepdims=True)
        y = y * lax.rsqrt(ms + EPS) * fn_ref[...]
    o_ref[...] = y


def ffn_half_step(x, mods, ln_w, w_in, w_out, final_w, *, mod_base, final, seq_len):
    t = x.shape[0]
    tm = ROW_TILE
    wg = _bf(w_in[:, :D_FF]).reshape(D_MODEL, N_FF_CHUNK, FF_CHUNK).transpose(1, 0, 2)
    wu = _bf(w_in[:, D_FF:]).reshape(D_MODEL, N_FF_CHUNK, FF_CHUNK).transpose(1, 0, 2)
    wo = _bf(w_out).reshape(N_FF_CHUNK, FF_CHUNK, D_MODEL)
    midx = _seq_mod_index(seq_len, mods.shape[0], tm)
    return pl.pallas_call(
        functools.partial(_ffn_kernel, mod_base=mod_base, final=final),
        grid=(t // tm,),
        in_specs=[
            pl.BlockSpec((tm, D_MODEL), lambda i: (i, 0)),
            pl.BlockSpec((1, N_MOD, D_MODEL), lambda i: (midx(i), 0, 0)),
            _const_spec((1, D_MODEL)),
            _const_spec((N_FF_CHUNK, D_MODEL, FF_CHUNK)),
            _const_spec((N_FF_CHUNK, D_MODEL, FF_CHUNK)),
            _const_spec((N_FF_CHUNK, FF_CHUNK, D_MODEL)),
            _const_spec((1, D_MODEL)),
        ],
        out_specs=pl.BlockSpec((tm, D_MODEL), lambda i: (i, 0)),
        out_shape=jax.ShapeDtypeStruct((t, D_MODEL), F32),
        compiler_params=_cparams(("parallel",)),
        name="ffn_half_step",
    )(x, mods, ln_w.reshape(1, D_MODEL), wg, wu, wo, final_w.reshape(1, D_MODEL))


PROJ_CHUNK = 256


def _proj_kernel(x_ref, mod_ref, ln_ref, w_ref, o_ref):
    x = x_ref[...]
    m = mod_ref[0]
    u = _bf(_norm_mod(x, ln_ref[...], m[3:4], m[4:5]))
    for c in range(N_IN // PROJ_CHUNK):
        sl = slice(c * PROJ_CHUNK, (c + 1) * PROJ_CHUNK)
        o_ref[:, sl] = _dot(u, w_ref[:, sl])


def mixer_in_proj(x, mods, ln_w, w_in, *, seq_len):
    t = x.shape[0]
    tm = ROW_TILE
    midx = _seq_mod_index(seq_len, mods.shape[0], tm)
    return pl.pallas_call(
        _proj_kernel,
        grid=(t // tm,),
        in_specs=[
            pl.BlockSpec((tm, D_MODEL), lambda i: (i, 0)),
            pl.BlockSpec((1, N_MOD, D_MODEL), lambda i: (midx(i), 0, 0)),
            _const_spec((1, D_MODEL)),
            _const_spec((D_MODEL, N_IN)),
        ],
        out_specs=pl.BlockSpec((tm, N_IN), lambda i: (i, 0)),
        out_shape=jax.ShapeDtypeStruct((t, N_IN), F32),
        compiler_params=_cparams(("parallel",)),
        name="mixer_in_proj",
    )(x, mods, ln_w.reshape(1, D_MODEL), _bf(w_in))


MERGE_CHUNK = 256


def _merge_kernel(x_ref, mod_ref, ln_ref, att_ref, lru_ref, hgf_ref, hgb_ref, hgg_ref, hgn_ref, headb_ref, hy_ref,
                  wg_ref, wa_ref, wl_ref, wh_ref, wy_ref, wo_ref, o_ref, mix_ref):
    x = x_ref[...]
    m = mod_ref[0]
    u = _bf(_norm_mod(x, ln_ref[...], m[3:4], m[4:5]))
    o = hgf_ref[0] + hgb_ref[0]
    sq = o * o
    sq_hi = _bf(sq)
    sq_lo = _bf(sq - sq_hi.astype(F32))
    ms = (_dot(sq_hi, headb_ref[...]) + _dot(sq_lo, headb_ref[...])) * (1.0 / HG_DV)
    hg = o * lax.rsqrt(ms + EPS) * hgn_ref[...] * _silu(hgg_ref[...])
    branches = (_bf(att_ref[...]), _bf(lru_ref[...]), _bf(hg), _bf(hy_ref[...]))
    w_bo = (wa_ref, wl_ref, wh_ref, wy_ref)
    for c in range(D_MODEL // MERGE_CHUNK):
        sl = slice(c * MERGE_CHUNK, (c + 1) * MERGE_CHUNK)
        mixed = None
        for n in range(N_BRANCH):
            term = _sigmoid(_dot(u, wg_ref[n, :, sl])) * _dot(branches[n], w_bo[n][:, sl])
            mixed = term if mixed is None else mixed + term
        mix_ref[:, sl] = _bf(mixed)
    o_ref[...] = x + m[5:6] * _dot(mix_ref[...], wo_ref[...])


def gated_merge(x, mods, ln_w, att, lru, hg_dirs, p2d, hg_norm, hy, w_gate, w_bo_attn, w_bo_lru, w_bo_hg, w_bo_hy,
                w_out, *, seq_len):
    t = x.shape[0]
    tm = ROW_TILE
    midx = _seq_mod_index(seq_len, mods.shape[0], tm)
    row = lambda w: pl.BlockSpec((tm, w), lambda i: (i, 0))
    _, _, headb, _ = _hg_constants()
    return pl.pallas_call(
        _merge_kernel,
        grid=(t // tm,),
        in_specs=[
            row(D_MODEL),
            pl.BlockSpec((1, N_MOD, D_MODEL), lambda i: (midx(i), 0, 0)),
            _const_spec((1, D_MODEL)),
            row(D_ATTN), row(D_LRU),
            pl.BlockSpec((1, tm, D_HG), lambda i: (0, i, 0)),
            pl.BlockSpec((1, tm, D_HG), lambda i: (1, i, 0)),
            pl.BlockSpec((tm, D_HG), lambda i: (i, COL_HGATE // D_HG)),
            _const_spec((1, D_HG)),
            _const_spec((D_HG, D_HG)),
            row(D_HY),
            _const_spec((N_BRANCH, D_MODEL, D_MODEL)),
            _const_spec((D_ATTN, D_MODEL)),
            _const_spec((D_LRU, D_MODEL)),
            _const_spec((D_HG, D_MODEL)),
            _const_spec((D_HY, D_MODEL)),
            _const_spec((D_MODEL, D_MODEL)),
        ],
        out_specs=row(D_MODEL),
        out_shape=jax.ShapeDtypeStruct((t, D_MODEL), F32),
        scratch_shapes=[pltpu.VMEM((tm, D_MODEL), BF16)],
        compiler_params=_cparams(("parallel",)),
        name="gated_merge",
    )(x, mods, ln_w.reshape(1, D_MODEL), att, lru, hg_dirs, hg_dirs, p2d, hg_norm.reshape(1, D_HG), headb, hy,
      _bf(w_gate), _bf(w_bo_attn), _bf(w_bo_lru), _bf(w_bo_hg), _bf(w_bo_hy), _bf(w_out))


def _dot_nt(a, b):
    return lax.dot_general(a, b, (((1,), (1,)), ((), ())), preferred_element_type=F32)


def _ctx_attn_kernel(q_ref, k_ref, v_ref, sink_ref, o_ref):
    rows = q_ref.shape[1]
    blk = ATTN_BLOCK
    keys = _dup_heads(k_ref[0])
    vals_t = _values_t(v_ref[0])
    low = lax.broadcasted_iota(jnp.int32, (blk, 2 * HEAD_DIM), 1) < HEAD_DIM
    units = [(pair, qb) for qb in range(rows // blk) for pair in range(N_HEADS // 2)]

    def where(pair, qb):
        return slice(qb * blk, (qb + 1) * blk), slice(pair * 2 * HEAD_DIM, (pair + 1) * 2 * HEAD_DIM)

    scores = []
    for pair, qb in units:
        q_rows, lanes = where(pair, qb)
        q = q_ref[0, q_rows, lanes] * ATTN_SCALE
        q_heads = _bf(jnp.concatenate([jnp.where(low, q, 0.0), jnp.where(low, 0.0, q)], axis=0))
        scores.append(_dot_nt(keys[(2 * pair) // GQA_GROUP], q_heads))
    probs = []
    for (pair, qb), s in zip(units, scores):
        sink = jnp.concatenate([sink_ref[2 * pair:2 * pair + 1, :], sink_ref[2 * pair + 1:2 * pair + 2, :]], axis=1)
        m = jnp.maximum(jnp.max(s, axis=0, keepdims=True), sink)
        probs.append((_bf(jnp.exp(s - m)), jnp.exp(sink - m)))
    for (pair, qb), (p, sink_term) in zip(units, probs):
        q_rows, lanes = where(pair, qb)
        acc = _dot(vals_t[(2 * pair) // GQA_GROUP], p)
        out_t = acc[:HEAD_DIM, :] / (acc[HEAD_DIM:HEAD_DIM + 1, :] + sink_term)
        o_ref[0, q_rows, lanes] = jnp.transpose(jnp.concatenate([out_t[:, :blk], out_t[:, blk:]], axis=0))


def _sink_table(sink):
    return jnp.broadcast_to(sink.astype(F32)[:, None], (N_HEADS, 128))


def context_attention(p_ctx, sink):
    b, seq, _ = p_ctx.shape
    return pl.pallas_call(
        _ctx_attn_kernel,
        grid=(b,),
        in_specs=[
            pl.BlockSpec((1, seq, D_ATTN), lambda i: (i, 0, 0)),
            pl.BlockSpec((1, seq, D_KV), lambda i: (i, 0, COL_K // D_KV)),
            pl.BlockSpec((1, seq, D_KV), lambda i: (i, 0, COL_V // D_KV)),
            _const_spec((N_HEADS, 128)),
        ],
        out_specs=pl.BlockSpec((1, seq, D_ATTN), lambda i: (i, 0, 0)),
        out_shape=jax.ShapeDtypeStruct((b, seq, D_ATTN), F32),
        compiler_params=_cparams(("parallel",)),
        name="context_attention",
    )(p_ctx, p_ctx, p_ctx, _sink_table(sink))


def _rope(x, cos, sin_signed):
    n = x.shape[-1]
    lane = lax.broadcasted_iota(jnp.int32, x.shape, x.ndim - 1)
    first = (lane % (HEAD_DIM // 2)) < (HEAD_DIM // 4)
    partner = jnp.where(first, pltpu.roll(x, n - HEAD_DIM // 4, axis=x.ndim - 1),
                        pltpu.roll(x, HEAD_DIM // 4, axis=x.ndim - 1))
    return x * cos + partner * sin_signed


def _rope_tables(seq):
    rows = seq // GRID_W
    row = np.repeat(np.arange(rows, dtype=np.float32), GRID_W)
    col = np.tile(np.arange(GRID_W, dtype=np.float32), rows)
    n = HEAD_DIM // 4
    inv = (ROPE_BASE ** (-jnp.arange(n, dtype=F32) / n))
    ang_r = jnp.asarray(row)[:, None] * inv
    ang_c = jnp.asarray(col)[:, None] * inv
    cos = jnp.concatenate([jnp.cos(ang_r), jnp.cos(ang_r), jnp.cos(ang_c), jnp.cos(ang_c)], axis=-1)
    sin = jnp.concatenate([-jnp.sin(ang_r), jnp.sin(ang_r), -jnp.sin(ang_c), jnp.sin(ang_c)], axis=-1)
    return jnp.tile(cos, (1, 2)), jnp.tile(sin, (1, 2))


ATTN_PREP_ROWS = 512


def _dup_heads(x):
    lane = lax.broadcasted_iota(jnp.int32, x.shape, 1)
    swapped = pltpu.roll(x, HEAD_DIM, axis=1)
    low = lane < HEAD_DIM
    return _bf(jnp.where(low, x, swapped)), _bf(jnp.where(low, swapped, x))


def _values_t(x):
    xt = jnp.transpose(x)
    ones = jnp.ones((HEAD_DIM, x.shape[0]), F32)
    return (_bf(jnp.concatenate([xt[:HEAD_DIM], ones], axis=0)),
            _bf(jnp.concatenate([xt[HEAD_DIM:], ones], axis=0)))


def _lat_attn_kernel(q_ref, k_ref, v_ref, kc_ref, vc_ref, cos_ref, sin_ref, sink_ref, o_ref,
                     kd_ref, vt_ref, kcd_ref, vct_ref):
    seq = k_ref.shape[1]
    blk = ATTN_BLOCK
    band = 3 * blk
    past = kc_ref.shape[1]
    i = pl.program_id(1)

    @pl.when(i == 0)
    def _():
        for ti in range(seq // ATTN_PREP_ROWS):
            rows = slice(ti * ATTN_PREP_ROWS, (ti + 1) * ATTN_PREP_ROWS)
            k0, k1 = _dup_heads(_rope(k_ref[0, rows, :], cos_ref[rows, :], sin_ref[rows, :]))
            kd_ref[0, rows, :] = k0
            kd_ref[1, rows, :] = k1
            v0, v1 = _values_t(v_ref[0, rows, :])
            vt_ref[0, :, rows] = v0
            vt_ref[1, :, rows] = v1
        kc0, kc1 = _dup_heads(kc_ref[0])
        kcd_ref[0] = kc0
        kcd_ref[1] = kc1
        vc0, vc1 = _values_t(vc_ref[0])
        vct_ref[0] = vc0
        vct_ref[1] = vc1

    start = pl.multiple_of(jnp.clip((i - 1) * blk, 0, seq - band), blk)
    q0 = pl.multiple_of(i * blk, blk)
    cos_q = cos_ref[pl.ds(q0, blk), :]
    sin_q = sin_ref[pl.ds(q0, blk), :]
    key = lax.broadcasted_iota(jnp.int32, (band + past, 2 * blk), 0)
    qry = lax.broadcasted_iota(jnp.int32, (band + past, 2 * blk), 1) % blk
    mask = (jnp.abs((q0 + qry) - (start + key)) <= WINDOW) | (key >= band)
    low = lax.broadcasted_iota(jnp.int32, (blk, 2 * HEAD_DIM), 1) < HEAD_DIM
    pairs = range(N_HEADS // 2)
    keys = [jnp.concatenate([kd_ref[kvh, pl.ds(start, band), :], kcd_ref[kvh]], axis=0)
            for kvh in range(N_KV_HEADS)]
    vals_t = [jnp.concatenate([vt_ref[kvh, :, pl.ds(start, band)], vct_ref[kvh]], axis=1)
              for kvh in range(N_KV_HEADS)]
    scores = []
    for pair in pairs:
        lanes = slice(pair * 2 * HEAD_DIM, (pair + 1) * 2 * HEAD_DIM)
        q = _rope(q_ref[0, :, lanes], cos_q, sin_q) * ATTN_SCALE
        q_heads = _bf(jnp.concatenate([jnp.where(low, q, 0.0), jnp.where(low, 0.0, q)], axis=0))
        scores.append(_dot_nt(keys[(2 * pair) // GQA_GROUP], q_heads))
    probs = []
    for pair in pairs:
        s = jnp.where(mask, scores[pair], -jnp.inf)
        sink = jnp.concatenate([sink_ref[2 * pair:2 * pair + 1, :], sink_ref[2 * pair + 1:2 * pair + 2, :]], axis=1)
        m = jnp.maximum(jnp.max(s, axis=0, keepdims=True), sink)
        probs.append((_bf(jnp.exp(s - m)), jnp.exp(sink - m)))
    for pair in pairs:
        lanes = slice(pair * 2 * HEAD_DIM, (pair + 1) * 2 * HEAD_DIM)
        p, sink_term = probs[pair]
        acc = _dot(vals_t[(2 * pair) // GQA_GROUP], p)
        out_t = acc[:HEAD_DIM, :] / (acc[HEAD_DIM:HEAD_DIM + 1, :] + sink_term)
        o_ref[0, :, lanes] = jnp.transpose(jnp.concatenate([out_t[:, :blk], out_t[:, blk:]], axis=0))


def latent_attention(p_lat, k_ctx, v_ctx, sink):
    b, seq, _ = p_lat.shape
    past = k_ctx.shape[1]
    cos, sin = _rope_tables(seq)
    return pl.pallas_call(
        _lat_attn_kernel,
        grid=(b, seq // ATTN_BLOCK),
        in_specs=[
            pl.BlockSpec((1, ATTN_BLOCK, D_ATTN), lambda bi, i: (bi, i, 0)),
            pl.BlockSpec((1, seq, D_KV), lambda bi, i: (bi, 0, COL_K // D_KV)),
            pl.BlockSpec((1, seq, D_KV), lambda bi, i: (bi, 0, COL_V // D_KV)),
            pl.BlockSpec((1, past, D_KV), lambda bi, i: (bi, 0, 0)),
            pl.BlockSpec((1, past, D_KV), lambda bi, i: (bi, 0, 0)),
            _const_spec((seq, 128)),
            _const_spec((seq, 128)),
            _const_spec((N_HEADS, 128)),
        ],
        out_specs=pl.BlockSpec((1, ATTN_BLOCK, D_ATTN), lambda bi, i: (bi, i, 0)),
        out_shape=jax.ShapeDtypeStruct((b, seq, D_ATTN), F32),
        scratch_shapes=[
            pltpu.VMEM((N_KV_HEADS, seq, D_KV), BF16),
            pltpu.VMEM((N_KV_HEADS, D_KV, seq), BF16),
            pltpu.VMEM((N_KV_HEADS, past, D_KV), BF16),
            pltpu.VMEM((N_KV_HEADS, D_KV, past), BF16),
        ],
        compiler_params=_cparams(("parallel", "arbitrary")),
        name="latent_attention",
    )(p_lat, p_lat, p_lat, k_ctx, v_ctx, cos, sin, _sink_table(sink))


LRU_HALF = D_LRU // 2
LRU_ROWS = 256
LRU_PAD = 8


def _neg_expm1(y, exp_y):
    series = -(y * (1.0 + y * (1.0 / 2 + y * (1.0 / 6 + y * (1.0 / 24 + y * (1.0 / 120 + y * (1.0 / 720)))))))
    return jnp.where(y > -0.25, series, 1.0 - exp_y)


def _softplus(x):
    return jnp.maximum(x, 0.0) + jnp.log1p(jnp.exp(-jnp.abs(x)))


def _gelu_tanh(x):
    return 0.5 * x * (1.0 + jnp.tanh(math.sqrt(2.0 / math.pi) * (x + 0.044715 * (x * x * x))))


LRU_SCAN = 8


def _affine_scan8(a, b, *, reverse):
    row = lax.broadcasted_iota(jnp.int32, a.shape, 0)
    k = 1
    while k < LRU_SCAN:
        if reverse:
            valid = row < LRU_SCAN - k
            shift = LRU_SCAN - k
        else:
            valid = row >= k
            shift = k
        a_prev = jnp.where(valid, pltpu.roll(a, shift, axis=0), 1.0)
        b_prev = jnp.where(valid, pltpu.roll(b, shift, axis=0), 0.0)
        b = b + a * b_prev
        a = a * a_prev
        k *= 2
    return a, b


def _lru_kernel(x_ref, g_ref, cw_ref, cb_ref, wr_ref, br_ref, wi_ref, bi_ref, lam_ref, h0_ref,
                o_ref, st_ref, pad_ref, a_ref, bx_ref, hb_ref):
    seq = x_ref.shape[1]
    zeros = jnp.zeros((LRU_PAD, LRU_HALF), F32)
    pad_ref[0:LRU_PAD, :] = zeros
    pad_ref[LRU_PAD + seq:2 * LRU_PAD + seq, :] = zeros
    pad_ref[LRU_PAD:LRU_PAD + seq, :] = x_ref[0]
    left = LRU_CONV // 2
    for ti in range(seq // LRU_ROWS):
        r0 = ti * LRU_ROWS
        xc = cb_ref[...]
        for k in range(LRU_CONV):
            xc = xc + cw_ref[k:k + 1, :] * pad_ref[r0 + LRU_PAD - left + k:r0 + LRU_PAD - left + k + LRU_ROWS, :]
        xcb = _bf(xc)
        for d in range(2):
            r = _sigmoid(_dot(xcb, wr_ref[d]) + br_ref[d:d + 1, :])
            i = _sigmoid(_dot(xcb, wi_ref[d]) + bi_ref[d:d + 1, :])
            log_a = (-LRU_C) * r * _softplus(-lam_ref[d:d + 1, :])
            a = jnp.exp(log_a)
            a_ref[d, r0:r0 + LRU_ROWS, :] = a
            bx_ref[d, r0:r0 + LRU_ROWS, :] = jnp.sqrt(_neg_expm1(2.0 * log_a, a * a)) * (i * xc)

    nt = seq // LRU_SCAN

    def step(blk, carry):
        hf, hb = carry
        rf = pl.ds(pl.multiple_of(blk * LRU_SCAN, LRU_SCAN), LRU_SCAN)
        rb = pl.ds(pl.multiple_of((nt - 1 - blk) * LRU_SCAN, LRU_SCAN), LRU_SCAN)
        af, bf_ = _affine_scan8(a_ref[0, rf, :], bx_ref[0, rf, :], reverse=False)
        ab, bb = _affine_scan8(a_ref[1, rb, :], bx_ref[1, rb, :], reverse=True)
        tile_f = af * hf + bf_
        tile_b = ab * hb + bb
        o_ref[0, rf, :] = tile_f
        hb_ref[rb, :] = tile_b
        return tile_f[LRU_SCAN - 1:LRU_SCAN, :], tile_b[0:1, :]

    hf, hb = lax.fori_loop(0, nt, step, (h0_ref[0, 0:1, :], h0_ref[0, 1:2, :]), unroll=2)
    st_ref[0, 0:1, :] = hf
    st_ref[0, 1:2, :] = hb
    for ti in range(seq // LRU_ROWS):
        rows = slice(ti * LRU_ROWS, (ti + 1) * LRU_ROWS)
        o_ref[0, rows, :] = (o_ref[0, rows, :] + hb_ref[rows, :]) * _gelu_tanh(g_ref[0, rows, :])


def _block_diag(w):
    n, blk, _ = w.shape
    eye = jnp.eye(n, dtype=w.dtype)
    return (eye[:, None, :, None] * w[:, :, None, :]).reshape(n * blk, n * blk)


def rglru_mixer(p, conv_w, conv_b, w_r, b_r, w_i, b_i, lam, h0):
    b, seq, _ = p.shape
    wr = _bf(jnp.stack([_block_diag(w_r[d]) for d in range(2)]))
    wi = _bf(jnp.stack([_block_diag(w_i[d]) for d in range(2)]))
    hw = LRU_HALF
    vec = lambda rows: pl.BlockSpec((rows, hw), lambda bi, h: (0, h))
    return pl.pallas_call(
        _lru_kernel,
        grid=(b, 2),
        in_specs=[
            pl.BlockSpec((1, seq, hw), lambda bi, h: (bi, 0, COL_RX // hw + h)),
            pl.BlockSpec((1, seq, hw), lambda bi, h: (bi, 0, COL_RG // hw + h)),
            vec(LRU_CONV), vec(1),
            pl.BlockSpec((2, hw, hw), lambda bi, h: (0, h, h)), vec(2),
            pl.BlockSpec((2, hw, hw), lambda bi, h: (0, h, h)), vec(2),
            vec(2),
            pl.BlockSpec((1, 2, hw), lambda bi, h: (bi, 0, h)),
        ],
        out_specs=[
            pl.BlockSpec((1, seq, hw), lambda bi, h: (bi, 0, h)),
            pl.BlockSpec((1, 2, hw), lambda bi, h: (bi, 0, h)),
        ],
        out_shape=[jax.ShapeDtypeStruct((b, seq, D_LRU), F32), jax.ShapeDtypeStruct((b, 2, D_LRU), F32)],
        scratch_shapes=[
            pltpu.VMEM((seq + 2 * LRU_PAD, hw), F32),
            pltpu.VMEM((2, seq, hw), F32),
            pltpu.VMEM((2, seq, hw), F32),
            pltpu.VMEM((seq, hw), F32),
        ],
        compiler_params=_cparams(("parallel", "parallel")),
        name="rglru_mixer",
    )(p, p, conv_w, conv_b.reshape(1, D_LRU), wr, b_r, wi, b_i, lam, h0)


HG_TILE = 256
HG_GROUP = 8
LOG2_E = math.log2(math.e)


def _hg_constants():
    idx = np.arange(HG_TILE)
    same_blk = (idx[:, None] // HG_BLK) == (idx[None, :] // HG_BLK)
    tri_f = same_blk & (idx[None, :] <= idx[:, None])
    tri_b = same_blk & (idx[None, :] >= idx[:, None])
    head = (idx[:, None] // HG_DK) == (idx[None, :] // HG_DK)
    return (jnp.asarray(np.stack([tri_f, tri_b]), BF16), jnp.asarray(same_blk, BF16),
            jnp.asarray(head, BF16), jnp.asarray(head, F32))


def _hgrn_kernel(q_ref, z_ref, v_ref, lb_ref, tri_ref, ones_ref, headb_ref, headf_ref, s0_ref,
                 o_ref, st_ref, state, att0, att1, upd0, upd1, qs, srcs, vs, cums, qds, kns, tots):
    att_bufs = (att0, att1)
    upd_bufs = (upd0, upd1)
    seg_rows = q_ref.shape[1]
    d = pl.program_id(1)
    s = pl.program_id(2)
    nseg = pl.num_programs(2)

    @pl.when(s == 0)
    def _():
        state[...] = jnp.zeros_like(state)
        for h in range(HG_HEADS):
            state[h * HG_DV:(h + 1) * HG_DV, h * HG_DK:(h + 1) * HG_DK] = jnp.transpose(s0_ref[0, 0, h])

    lb = lb_ref[0]
    for ti in range(seg_rows // HG_TILE):
        rows = slice(ti * HG_TILE, (ti + 1) * HG_TILE)
        f = lb + (1.0 - lb) * _sigmoid(z_ref[0, rows, :])
        g = jnp.log(f)
        kk = 1.0 - f
        q = _silu(q_ref[0, rows, :]) * HG_SCALE
        cum = _dot_exact_rhs(tri_ref[0], g)
        tot = _dot_exact_rhs(ones_ref[...], g)
        qs[rows, :] = q
        srcs[rows, :] = (cum - jnp.log(kk)) * LOG2_E
        vs[rows, :] = v_ref[0, rows, :]
        cums[rows, :] = cum * LOG2_E
        tots[rows, :] = tot
        qds[rows, :] = q * jnp.exp(cum)
        kns[rows, :] = kk * jnp.exp(tot - cum)

    nblk = seg_rows // HG_BLK
    sgn = jnp.where(d == 0, 1, -1)
    t_signed = [(lax.broadcasted_iota(jnp.int32, (8, D_HG), 0) + 8 * h) * sgn for h in range(HG_BLK // 8)]

    n_half = HG_BLK // 8

    def block_rows(j):
        je = jnp.where(d == 0, j, nblk - 1 - j)
        return pl.multiple_of(je * HG_BLK, HG_BLK)

    def prepare(j, slot):
        r0 = block_rows(j)
        rows = pl.ds(r0, HG_BLK)
        cum_h = [cums[pl.ds(r0 + 8 * h, 8), :] for h in range(n_half)]
        q_h = [qs[pl.ds(r0 + 8 * h, 8), :] for h in range(n_half)]
        tiles = []
        for src in range(HG_BLK):
            src_row = srcs[pl.ds(r0 + src, 1), :]
            for h in range(n_half):
                pair = jnp.exp2(cum_h[h] - src_row) * q_h[h]
                tiles.append(jnp.where(t_signed[h] >= src * sgn, pair, 0.0))
        att_bufs[slot][...] = _dot(_bf(jnp.concatenate(tiles, axis=0)), headb_ref[...])
        upd = lax.dot_general(_bf(vs[rows, :]), _bf(kns[rows, :]), (((0,), (0,)), ((), ())),
                              preferred_element_type=F32)
        upd_bufs[slot][...] = upd * headf_ref[...]

    def finish(j, slot):
        r0 = block_rows(j)
        rows = pl.ds(r0, HG_BLK)
        o_halves = [None] * n_half
        for src in range(HG_BLK):
            v_row = vs[pl.ds(r0 + src, 1), :]
            for h in range(n_half):
                term = att_bufs[slot][src * HG_BLK + h * 8:src * HG_BLK + (h + 1) * 8, :] * v_row
                o_halves[h] = term if o_halves[h] is None else o_halves[h] + term
        st = state[...]
        o_ref[0, 0, rows, :] = jnp.concatenate(o_halves, axis=0) + _dot_nt(_bf(qds[rows, :]), _bf(st))
        state[...] = jnp.exp(tots[pl.ds(r0, 1), :]) * st + upd_bufs[slot][...]

    prepare(0, 0)

    def block_group(i, carry):
        j = HG_GROUP * i
        for k in range(HG_GROUP):
            nxt = j + k + 1 if k + 1 < HG_GROUP else jnp.minimum(j + k + 1, nblk - 1)
            prepare(nxt, (k + 1) % 2)
            finish(j + k, k % 2)
        return carry

    lax.fori_loop(0, nblk // HG_GROUP, block_group, 0)

    @pl.when(s == nseg - 1)
    def _():
        for h in range(HG_HEADS):
            st_ref[0, 0, h] = jnp.transpose(state[h * HG_DV:(h + 1) * HG_DV, h * HG_DK:(h + 1) * HG_DK])


def hgrn2_scan(p, lb, s0, seg_rows):
    b, seq, _ = p.shape
    nseg = seq // seg_rows
    tri, ones, headb, headf = _hg_constants()
    seg = lambda d, s: jnp.where(d == 0, s, nseg - 1 - s)
    col = lambda c: pl.BlockSpec((1, seg_rows, D_HG), lambda bi, d, s: (bi, seg(d, s), c // D_HG))
    scr = lambda: pltpu.VMEM((seg_rows, D_HG), F32)
    return pl.pallas_call(
        _hgrn_kernel,
        grid=(b, 2, nseg),
        in_specs=[
            col(COL_HQ),
            pl.BlockSpec((1, seg_rows, D_HG), lambda bi, d, s: (bi, seg(d, s), COL_HFF // D_HG + d)),
            col(COL_HI),
            pl.BlockSpec((1, 1, D_HG), lambda bi, d, s: (d, 0, 0)),
            pl.BlockSpec((1, HG_TILE, HG_TILE), lambda bi, d, s: (d, 0, 0)),
            _const_spec((HG_TILE, HG_TILE)),
            _const_spec((D_HG, D_HG)),
            _const_spec((D_HG, D_HG)),
            pl.BlockSpec((1, 1, HG_HEADS, HG_DK, HG_DV), lambda bi, d, s: (bi, d, 0, 0, 0)),
        ],
        out_specs=[
            pl.BlockSpec((1, 1, seg_rows, D_HG), lambda bi, d, s: (d, bi, seg(d, s), 0)),
            pl.BlockSpec((1, 1, HG_HEADS, HG_DK, HG_DV), lambda bi, d, s: (bi, d, 0, 0, 0)),
        ],
        out_shape=[jax.ShapeDtypeStruct((2, b, seq, D_HG), F32),
                   jax.ShapeDtypeStruct((b, 2, HG_HEADS, HG_DK, HG_DV), F32)],
        scratch_shapes=[pltpu.VMEM((D_HG, D_HG), F32)]
        + [pltpu.VMEM((HG_BLK * HG_BLK, D_HG), F32) for _ in range(2)]
        + [pltpu.VMEM((D_HG, D_HG), F32) for _ in range(2)] + [scr() for _ in range(7)],
        compiler_params=_cparams(("parallel", "parallel", "arbitrary")),
        name="hgrn2_scan",
    )(p, p, p, lb.reshape(2, 1, D_HG), tri, ones, headb, headf, s0)


HY_COLS = 2 * HY_ORDER * D_HY
HY_FEAT_PAD = 128
HY_GEN_ROWS = 256


def _dot_hi(a, b):
    ah = _bf(a)
    al = _bf(a - ah.astype(F32))
    bh = _bf(b)
    bl = _bf(b - bh.astype(F32))
    return _dot(ah, bh) + _dot(ah, bl) + _dot(al, bh)


HY_HALF_COLS = HY_ORDER * D_HY


def _filter_gen_kernel(feat_ref, w1_ref, b1_ref, w2_ref, b2_ref, w3_ref, fr_ref, dl_ref, h_ref, ssq_ref, *, seq):
    i = pl.program_id(0)
    tm = feat_ref.shape[0]
    feat = feat_ref[...]
    fr = fr_ref[...]
    h = jnp.sin(fr * (_dot_hi(feat, w1_ref[...]) + b1_ref[...]))
    h = jnp.sin(fr * (_dot_hi(h, w2_ref[...]) + b2_ref[...]))
    h = _dot_hi(h, w3_ref[...])
    decay = jnp.exp(-feat[:, 0:1] * dl_ref[...])
    decay = jnp.concatenate([decay] * HY_ORDER, axis=1)
    row = i * tm + lax.broadcasted_iota(jnp.int32, (tm, HY_HALF_COLS), 0)
    taps = h * decay
    h_ref[...] = jnp.where(row == seq, 0.0, taps)

    @pl.when(i == 0)
    def _():
        ssq_ref[...] = jnp.zeros_like(ssq_ref)

    ssq_ref[...] += jnp.sum(taps * taps, axis=0, keepdims=True)


def _hyena_features(seq):
    order = np.concatenate([np.arange(seq), [0], np.arange(seq - 1, 0, -1)])
    pos = order.astype(np.float32)
    t = pos / np.float32(max(seq - 1, 1))
    bands = np.linspace(1e-4, HY_BANDS - 1, HY_BANDS, dtype=np.float32)
    ang = np.float32(2.0 * math.pi / seq) * pos[:, None] * bands[None, :]
    feat = np.concatenate([t[:, None], np.cos(ang), np.sin(ang)], axis=-1).astype(np.float32)
    out = np.zeros((2 * seq, HY_FEAT_PAD), np.float32)
    out[:, :HY_EMB] = feat
    return jnp.asarray(out)


def hyena_filter_gen(seq, w1, b1, w2, b2, w3, freq):
    feat = _hyena_features(seq)
    w1p = jnp.zeros((HY_FEAT_PAD, HY_FFN), F32).at[:HY_EMB].set(w1.astype(F32))
    deltas = jnp.asarray(np.linspace(HY_DECAY_SLOW, HY_DECAY_FAST, D_HY, dtype=np.float32)).reshape(1, D_HY)
    tm = HY_GEN_ROWS
    return pl.pallas_call(
        functools.partial(_filter_gen_kernel, seq=seq),
        grid=(2 * seq // tm,),
        in_specs=[
            pl.BlockSpec((tm, HY_FEAT_PAD), lambda i: (i, 0)),
            _const_spec((HY_FEAT_PAD, HY_FFN)), _const_spec((1, HY_FFN)),
            _const_spec((HY_FFN, HY_FFN)), _const_spec((1, HY_FFN)),
            pl.BlockSpec((HY_FFN, HY_HALF_COLS), lambda i: (0, (i * tm) // seq)),
            _const_spec((1, HY_FFN)), _const_spec((1, D_HY)),
        ],
        out_specs=[pl.BlockSpec((tm, HY_HALF_COLS), lambda i: (i, 0)), _const_spec((1, HY_HALF_COLS))],
        out_shape=[jax.ShapeDtypeStruct((2 * seq, HY_HALF_COLS), F32),
                   jax.ShapeDtypeStruct((1, HY_HALF_COLS), F32)],
        compiler_params=_cparams(("arbitrary",)),
        name="hyena_filter_gen",
    )(feat, w1p, b1.reshape(1, HY_FFN), w2, b2.reshape(1, HY_FFN), w3, freq.reshape(1, HY_FFN), deltas)


def _filter_norm(ssq_ref, n):
    return lax.rsqrt(ssq_ref[:, n * D_HY:(n + 1) * D_HY])


def _conv3(pad_ref, w_ref, b_ref, r0, rows, pad):
    acc = b_ref[...]
    for k in range(HY_SHORT):
        lo = r0 + pad - HY_SHORT // 2 + k
        acc = acc + w_ref[k:k + 1, :] * pad_ref[lo:lo + rows, :]
    return acc


def _dft_constants(seq):
    n = 2 * seq
    k = np.arange(n)
    ang = 2.0 * np.pi * np.outer(k, k) / n
    fwd = np.concatenate([np.cos(ang), -np.sin(ang)], axis=0)
    inv = np.concatenate([np.cos(ang[:seq]), -np.sin(ang[:seq])], axis=1) / n
    return jnp.asarray(fwd, BF16), jnp.asarray(inv, BF16)


def _ctx_spectrum_kernel(filt_ref, ssq_ref, fwd_ref, h_ref):
    n = pl.program_id(0)
    norm = jnp.where(n == 0, _filter_norm(ssq_ref, 0), _filter_norm(ssq_ref, 1))
    x = filt_ref[...] * norm
    xh = _bf(x)
    xl = _bf(x - xh.astype(F32))
    h_ref[0] = _dot(fwd_ref[...], xh) + _dot(fwd_ref[...], xl)


def hyena_ctx_spectrum(filt, ssq):
    n = filt.shape[0]
    fwd, _ = _dft_constants(n // 2)
    return pl.pallas_call(
        _ctx_spectrum_kernel,
        grid=(HY_ORDER,),
        in_specs=[pl.BlockSpec((n, D_HY), lambda o: (0, o)), _const_spec((1, HY_HALF_COLS)),
                  _const_spec((2 * n, n))],
        out_specs=pl.BlockSpec((1, 2 * n, D_HY), lambda o: (o, 0, 0)),
        out_shape=jax.ShapeDtypeStruct((HY_ORDER, 2 * n, D_HY), F32),
        compiler_params=_cparams(("parallel",)),
        name="hyena_ctx_spectrum",
    )(filt, ssq, fwd)


HY_PAD = 8


def _hyena_ctx_kernel(v_ref, x1_ref, x2_ref, cw_ref, cb_ref, bias_ref, spec_ref, fwd_ref, inv_ref, o_ref, pad_ref):
    seq = v_ref.shape[1]
    n = 2 * seq
    zeros = jnp.zeros((HY_PAD, D_HY), F32)
    pad_ref[0:HY_PAD, :] = zeros
    pad_ref[HY_PAD + seq:2 * HY_PAD + seq, :] = zeros

    def short_conv(src_ref, part):
        pad_ref[HY_PAD:HY_PAD + seq, :] = src_ref[0]
        return _conv3(pad_ref, cw_ref.at[part], cb_ref.at[part], 0, seq, HY_PAD)

    z = short_conv(v_ref, 0)
    for order, gate_ref in enumerate((x1_ref, x2_ref)):
        spec = _dot(fwd_ref[:, 0:seq], _bf(z))
        xr, xi = spec[:n], spec[n:]
        hr, hi = spec_ref[order, 0:n, :], spec_ref[order, n:2 * n, :]
        prod = jnp.concatenate([xr * hr - xi * hi, xr * hi + xi * hr], axis=0)
        conv = _dot(inv_ref[...], _bf(prod))
        z = short_conv(gate_ref, order + 1) * (conv + bias_ref[order:order + 1, :] * z)
    o_ref[0] = z


def hyena_ctx(p, conv_w, conv_b, bias, spec):
    b, seq, _ = p.shape
    fwd, inv = _dft_constants(seq)
    col = lambda c: pl.BlockSpec((1, seq, D_HY), lambda i: (i, 0, c))
    c0 = COL_HY // D_HY
    return pl.pallas_call(
        _hyena_ctx_kernel,
        grid=(b,),
        in_specs=[
            col(c0), col(c0 + 1), col(c0 + 2),
            _const_spec((3, HY_SHORT, D_HY)), _const_spec((3, 1, D_HY)), _const_spec((HY_ORDER, D_HY)),
            _const_spec((HY_ORDER, 4 * seq, D_HY)), _const_spec((4 * seq, 2 * seq)), _const_spec((seq, 4 * seq)),
        ],
        out_specs=pl.BlockSpec((1, seq, D_HY), lambda i: (i, 0, 0)),
        out_shape=jax.ShapeDtypeStruct((b, seq, D_HY), F32),
        scratch_shapes=[pltpu.VMEM((seq + 2 * HY_PAD, D_HY), F32)],
        compiler_params=_cparams(("parallel",)),
        name="hyena_ctx",
    )(p, p, p, conv_w.reshape(HY_SHORT, 3, D_HY).transpose(1, 0, 2), conv_b.reshape(3, 1, D_HY), bias, spec,
      fwd, inv)


FFT_N1 = 64
FFT_N2 = 128
FFT_N = FFT_N1 * FFT_N2
FFT_K1 = FFT_N1 // 2 + 1
FFT_K1_PAD = 40
FFT_ROWS = FFT_K1_PAD * FFT_N2
HY_LANES = 128
FFT_UNROLL1 = 16
FFT_UNROLL2 = 11


def _fft_constants():
    n1, n2, n, kp = FFT_N1, FFT_N2, FFT_N, FFT_K1_PAD
    k1 = np.arange(kp)
    w1 = np.exp(-2j * np.pi * np.outer(k1, np.arange(n1)) / n1)
    tw = np.exp(-2j * np.pi * np.outer(np.arange(n2), k1) / n)
    f1c = tw[:, :, None] * w1[None, :, :]
    f1t = np.concatenate([f1c.real, f1c.imag], axis=1).transpose(0, 2, 1)
    w2 = np.exp(-2j * np.pi * np.outer(np.arange(n2), np.arange(n2)) / n2)
    f2 = np.block([[w2.real, -w2.imag], [w2.imag, w2.real]])
    g2c = np.conj(w2)
    g2 = np.block([[g2c.real, -g2c.imag], [g2c.imag, g2c.real]])
    weight = np.where((k1 == 0) | (k1 == n1 // 2), 1.0, 2.0) * (k1 < FFT_K1)
    mc = np.conj(w1.T)[None, :n1 // 2, :] * np.conj(tw)[:, None, :] * weight / n
    g1 = np.concatenate([mc.real, -mc.imag], axis=2)
    return (jnp.asarray(f1t, BF16), jnp.asarray(f2, BF16), jnp.asarray(g2, BF16), jnp.asarray(g1, BF16))


def _dot_tn(a, b):
    return lax.dot_general(a, b, (((0,), (0,)), ((), ())), preferred_element_type=F32)


def _fft_stage1(x_ref, f1t_ref, a_ref, n1_in):
    def body(n2, carry):
        rows = x_ref[pl.ds(n2, n1_in, stride=FFT_N2), :]
        out = _dot_tn(f1t_ref[n2, 0:n1_in, :], _bf(rows))
        a_ref[0, pl.ds(n2, FFT_K1, stride=FFT_N2), :] = out[:FFT_K1]
        a_ref[1, pl.ds(n2, FFT_K1, stride=FFT_N2), :] = out[FFT_K1_PAD:FFT_K1_PAD + FFT_K1]
        return carry
    lax.fori_loop(0, FFT_N2, body, 0, unroll=FFT_UNROLL1)


def _lat_spectrum_kernel(filt_ref, ssq_ref, f1t_ref, f2_ref, h_ref, a_ref):
    n = pl.program_id(0)
    half = pl.program_id(1)
    norm_full = jnp.where(n == 0, _filter_norm(ssq_ref, 0), _filter_norm(ssq_ref, 1))
    norm = jnp.where(half == 0, norm_full[:, :HY_LANES], norm_full[:, HY_LANES:])
    _fft_stage1(filt_ref, f1t_ref, a_ref, FFT_N1)

    def stage2(k1, carry):
        r0 = pl.multiple_of(k1 * FFT_N2, FFT_N2)
        rows = pl.ds(r0, FFT_N2)
        a = jnp.concatenate([a_ref[0, rows, :], a_ref[1, rows, :]], axis=0)
        b = _dot(f2_ref[...], _bf(a)) * norm
        h_ref[0, 0, rows, :] = _bf(b[:FFT_N2])
        h_ref[0, 1, rows, :] = _bf(b[FFT_N2:])
        return carry
    lax.fori_loop(0, FFT_K1, stage2, 0)


def hyena_lat_spectrum(filt, ssq):
    f1t, f2, _, _ = _fft_constants()
    rows = FFT_K1 * FFT_N2
    halves = D_HY // HY_LANES
    return pl.pallas_call(
        _lat_spectrum_kernel,
        grid=(HY_ORDER, halves),
        in_specs=[
            pl.BlockSpec((FFT_N, HY_LANES), lambda o, h: (0, o * halves + h)),
            _const_spec((1, HY_HALF_COLS)),
            _const_spec((FFT_N2, FFT_N1, 2 * FFT_K1_PAD)),
            _const_spec((2 * FFT_N2, 2 * FFT_N2)),
        ],
        out_specs=pl.BlockSpec((1, 2, rows, HY_LANES), lambda o, h: (o, 0, 0, h)),
        out_shape=jax.ShapeDtypeStruct((HY_ORDER, 2, rows, D_HY), BF16),
        scratch_shapes=[pltpu.VMEM((2, FFT_ROWS, HY_LANES), F32)],
        compiler_params=_cparams(("parallel", "parallel")),
        name="hyena_lat_spectrum",
    )(filt, ssq, f1t, f2)


HY_ROWS = 512


def _hyena_lat_kernel(v_ref, x1_ref, x2_ref, cw_ref, cb_ref, bias_ref, spec_ref, f1t_ref, f2_ref, g2_ref, g1_ref,
                      o_ref, pad_ref, z_ref, a_ref):
    seq = v_ref.shape[1]
    zeros = jnp.zeros((HY_PAD, HY_LANES), F32)
    pad_ref[0:HY_PAD, :] = zeros
    pad_ref[HY_PAD + seq:2 * HY_PAD + seq, :] = zeros
    pad_ref[HY_PAD:HY_PAD + seq, :] = v_ref[0]
    for ti in range(seq // HY_ROWS):
        r0 = ti * HY_ROWS
        z_ref[r0:r0 + HY_ROWS, :] = _conv3(pad_ref, cw_ref.at[0], cb_ref.at[0], r0, HY_ROWS, HY_PAD)
    tail = jnp.zeros((FFT_ROWS - FFT_K1 * FFT_N2, HY_LANES), F32)
    a_ref[0, FFT_K1 * FFT_N2:FFT_ROWS, :] = tail
    a_ref[1, FFT_K1 * FFT_N2:FFT_ROWS, :] = tail

    for order, gate_ref in enumerate((x1_ref, x2_ref)):
        _fft_stage1(z_ref, f1t_ref, a_ref, FFT_N1 // 2)

        def stage2(k1, carry):
            r0 = pl.multiple_of(k1 * FFT_N2, FFT_N2)
            rows = pl.ds(r0, FFT_N2)
            a = jnp.concatenate([a_ref[0, rows, :], a_ref[1, rows, :]], axis=0)
            b = _dot(f2_ref[...], _bf(a))
            br, bi = b[:FFT_N2], b[FFT_N2:]
            hr = spec_ref[order, 0, rows, :].astype(F32)
            hi = spec_ref[order, 1, rows, :].astype(F32)
            y = jnp.concatenate([br * hr - bi * hi, br * hi + bi * hr], axis=0)
            c = _dot(g2_ref[...], _bf(y))
            a_ref[0, rows, :] = c[:FFT_N2]
            a_ref[1, rows, :] = c[FFT_N2:]
            return carry
        lax.fori_loop(0, FFT_K1, stage2, 0, unroll=FFT_UNROLL2)

        def stage1_inv(n2, carry):
            c = jnp.concatenate([a_ref[0, pl.ds(n2, FFT_K1_PAD, stride=FFT_N2), :],
                                 a_ref[1, pl.ds(n2, FFT_K1_PAD, stride=FFT_N2), :]], axis=0)
            o_ref[0, pl.ds(n2, FFT_N1 // 2, stride=FFT_N2), :] = _dot(g1_ref[n2], _bf(c))
            return carry
        lax.fori_loop(0, FFT_N2, stage1_inv, 0, unroll=FFT_UNROLL1)

        pad_ref[HY_PAD:HY_PAD + seq, :] = gate_ref[0]
        for ti in range(seq // HY_ROWS):
            r0 = ti * HY_ROWS
            rows = slice(r0, r0 + HY_ROWS)
            gate = _conv3(pad_ref, cw_ref.at[order + 1], cb_ref.at[order + 1], r0, HY_ROWS, HY_PAD)
            z = gate * (o_ref[0, rows, :] + bias_ref[order:order + 1, :] * z_ref[rows, :])
            if order + 1 < HY_ORDER:
                z_ref[rows, :] = z
            else:
                o_ref[0, rows, :] = z


def hyena_lat(p, conv_w, conv_b, bias, spec):
    b, seq, _ = p.shape
    assert 2 * seq == FFT_N
    f1t, f2, g2, g1 = _fft_constants()
    hw = HY_LANES
    col = lambda c: pl.BlockSpec((1, seq, hw), lambda i, h: (i, 0, (COL_HY + c * D_HY) // hw + h))
    return pl.pallas_call(
        _hyena_lat_kernel,
        grid=(b, D_HY // hw),
        in_specs=[
            col(0), col(1), col(2),
            pl.BlockSpec((3, HY_SHORT, hw), lambda i, h: (0, 0, h)),
            pl.BlockSpec((3, 1, hw), lambda i, h: (0, 0, h)),
            pl.BlockSpec((HY_ORDER, hw), lambda i, h: (0, h)),
            pl.BlockSpec((HY_ORDER, 2, FFT_K1 * FFT_N2, hw), lambda i, h: (0, 0, 0, h)),
            _const_spec((FFT_N2, FFT_N1, 2 * FFT_K1_PAD)),
            _const_spec((2 * FFT_N2, 2 * FFT_N2)),
            _const_spec((2 * FFT_N2, 2 * FFT_N2)),
            _const_spec((FFT_N2, FFT_N1 // 2, 2 * FFT_K1_PAD)),
        ],
        out_specs=pl.BlockSpec((1, seq, hw), lambda i, h: (i, 0, h)),
        out_shape=jax.ShapeDtypeStruct((b, seq, D_HY), F32),
        scratch_shapes=[
            pltpu.VMEM((seq + 2 * HY_PAD, hw), F32),
            pltpu.VMEM((seq, hw), F32),
            pltpu.VMEM((2, FFT_ROWS, hw), F32),
        ],
        compiler_params=_cparams(("parallel", "parallel")),
        name="hyena_lat",
    )(p, p, p, conv_w.reshape(HY_SHORT, 3, D_HY).transpose(1, 0, 2), conv_b.reshape(3, 1, D_HY), bias, spec,
      f1t, f2, g2, g1)


HG_SEG_ROWS = 512
COND_ROWS = 16


def _trunk_layer(x, mods, lp, lb, batch, seq, ctx, final):
    x = ffn_half_step(x, mods, lp['ln_ffn1'], lp['w_ffn1_in'], lp['w_ffn1_out'], lp['final_norm'],
                      mod_base=0, final=False, seq_len=seq)
    p2d = mixer_in_proj(x, mods, lp['ln_mix'], lp['w_in'], seq_len=seq)
    p = p2d.reshape(batch, seq, N_IN)

    filt, ssq = hyena_filter_gen(seq, lp['hy_w1'], lp['hy_b1'], lp['hy_w2'], lp['hy_b2'], lp['hy_w3'],
                                 lp['hy_freq'])
    if ctx is None:
        att = context_attention(p, lp['attn_sink'])
        lru_h0 = jnp.zeros((batch, 2, D_LRU), F32)
        hg_s0 = jnp.zeros((batch, 2, HG_HEADS, HG_DK, HG_DV), F32)
        hy = hyena_ctx(p, lp['hy_conv_w'], lp['hy_conv_b'], lp['hy_bias'], hyena_ctx_spectrum(filt, ssq))
    else:
        k_ctx, v_ctx, lru_h0, hg_state = ctx
        att = latent_attention(p, k_ctx, v_ctx, lp['attn_sink'])
        hg_s0 = hg_state.astype(F32)
        hy = hyena_lat(p, lp['hy_conv_w'], lp['hy_conv_b'], lp['hy_bias'], hyena_lat_spectrum(filt, ssq))
    lru, lru_state = rglru_mixer(p, lp['lru_conv_w'], lp['lru_conv_b'], lp['lru_w_r'], lp['lru_b_r'],
                                 lp['lru_w_i'], lp['lru_b_i'], lp['lru_lambda'], lru_h0.astype(F32))
    o_dirs, hg_state_out = hgrn2_scan(p, lb, hg_s0, min(seq, HG_SEG_ROWS))
    t = batch * seq

    x = gated_merge(x, mods, lp['ln_mix'], att.reshape(t, D_ATTN), lru.reshape(t, D_LRU),
                    o_dirs.reshape(2, t, D_HG), p2d, lp['hg_norm'], hy.reshape(t, D_HY), lp['w_gate'],
                    lp['w_bo_attn'], lp['w_bo_lru'], lp['w_bo_hg'], lp['w_bo_hy'], lp['w_out'], seq_len=seq)
    x = ffn_half_step(x, mods, lp['ln_ffn2'], lp['w_ffn2_in'], lp['w_ffn2_out'], lp['final_norm'],
                      mod_base=6, final=final, seq_len=seq)
    state = None
    if ctx is None:
        k = p[:, :, COL_K:COL_K + D_KV].reshape(batch, seq, N_KV_HEADS, HEAD_DIM)
        v = p[:, :, COL_V:COL_V + D_KV].reshape(batch, seq, N_KV_HEADS, HEAD_DIM)
        state = (k, v, lru_state, hg_state_out)
    return x, state


def kernel(x_prompt, x_sample, cache_k, cache_v, state_lru, state_hgrn, c, c_ctx, ln_ffn1, ln_mix, ln_ffn2, w_ada, b_ada, w_ffn1_in, w_ffn1_out, w_ffn2_in, w_ffn2_out, w_in, attn_sink, lru_conv_w, lru_conv_b, lru_w_r, lru_b_r, lru_w_i, lru_b_i, lru_lambda, hg_lb_logits, hg_norm, hy_conv_w, hy_conv_b, hy_w1, hy_b1, hy_w2, hy_b2, hy_w3, hy_freq, hy_bias, w_bo_attn, w_bo_lru, w_bo_hg, w_bo_hy, w_gate, w_out, final_norm):
    batch, seq, _ = x_prompt.shape
    dec_batch, dec_seq, _ = x_sample.shape
    lb_soft = jax.nn.softmax(hg_lb_logits.astype(F32), axis=0)
    lb_all = jnp.cumsum(lb_soft, axis=0) - lb_soft[0]

    stacked = dict(ln_ffn1=ln_ffn1, ln_mix=ln_mix, ln_ffn2=ln_ffn2, w_ffn1_in=w_ffn1_in, w_ffn1_out=w_ffn1_out,
                   w_ffn2_in=w_ffn2_in, w_ffn2_out=w_ffn2_out, w_in=w_in, attn_sink=attn_sink,
                   lru_conv_w=lru_conv_w, lru_conv_b=lru_conv_b, lru_w_r=lru_w_r, lru_b_r=lru_b_r,
                   lru_w_i=lru_w_i, lru_b_i=lru_b_i, lru_lambda=lru_lambda, hg_norm=hg_norm,
                   hy_conv_w=hy_conv_w, hy_conv_b=hy_conv_b, hy_w1=hy_w1, hy_b1=hy_b1, hy_w2=hy_w2, hy_b2=hy_b2,
                   hy_w3=hy_w3, hy_freq=hy_freq, hy_bias=hy_bias, w_bo_attn=w_bo_attn, w_bo_lru=w_bo_lru,
                   w_bo_hg=w_bo_hg, w_bo_hy=w_bo_hy, w_gate=w_gate, w_out=w_out)

    cond = jnp.zeros((COND_ROWS, D_MODEL), F32).at[0].set(c_ctx).at[1:1 + dec_batch].set(c)
    mods = ada_mods(cond, w_ada, b_ada).reshape(DEPTH, COND_ROWS, N_MOD, D_MODEL)

    h = x_prompt.reshape(batch * seq, D_MODEL)
    z = x_sample.reshape(dec_batch * dec_seq, D_MODEL)
    ks, vs, lrus, hgs = [], [], [], []
    for l in range(DEPTH):
        lp = {name: w[l] for name, w in stacked.items()}
        lp['final_norm'] = final_norm
        final = l == DEPTH - 1
        h, (k_l, v_l, lru_l, hg_l) = _trunk_layer(h, mods[l, 0:1], lp, lb_all[l], batch, seq, None, final)
        ks.append(k_l)
        vs.append(v_l)
        lrus.append(lru_l)
        hgs.append(hg_l)
        ctx = (cache_k[:, l].reshape(dec_batch, -1, D_KV), cache_v[:, l].reshape(dec_batch, -1, D_KV),
               state_lru[:, l], state_hgrn[:, l])
        z, _ = _trunk_layer(z, mods[l, 1:1 + dec_batch], lp, lb_all[l], dec_batch, dec_seq, ctx, final)

    y_prompt = h.reshape(batch, seq, D_MODEL)
    y_sample = z.reshape(dec_batch, dec_seq, D_MODEL)
    return (y_prompt, y_sample, jnp.stack(ks, axis=1), jnp.stack(vs, axis=1), jnp.stack(lrus, axis=1),
            jnp.stack(hgs, axis=1))
```

```python
import functools
import math

import numpy as np
import jax
import jax.numpy as jnp
from jax import lax
from jax.experimental import pallas as pl
from jax.experimental.pallas import tpu as pltpu

F32 = jnp.float32
BF16 = jnp.bfloat16

D_MODEL = 1024
DEPTH = 2
GRID_W = 64
EPS = 1e-6
N_MOD = 9
N_BRANCH = 4
D_FF = 2816
N_HEADS = 8
N_KV_HEADS = 2
HEAD_DIM = 64
GQA_GROUP = N_HEADS // N_KV_HEADS
D_ATTN = N_HEADS * HEAD_DIM
D_KV = N_KV_HEADS * HEAD_DIM
WINDOW = 128
ATTN_BLOCK = 128
ATTN_SCALE = HEAD_DIM ** -0.5
ROPE_BASE = 10000.0
D_LRU = D_MODEL // 4
LRU_HEADS = 4
LRU_BLOCK = D_LRU // LRU_HEADS
LRU_CONV = 4
LRU_C = 8.0
HG_HEADS = 4
HG_DK = 64
HG_DV = 64
D_HG = HG_HEADS * HG_DK
HG_SCALE = HG_DK ** -0.5
HG_BLK = 16
D_HY = D_MODEL // 4
HY_ORDER = 2
HY_SHORT = 3
HY_BANDS = 8
HY_EMB = 2 * HY_BANDS + 1
HY_FFN = 64
HY_TARGET = 1e-2
HY_DECAY_SLOW = -math.log(HY_TARGET) / 1.5
HY_DECAY_FAST = -math.log(HY_TARGET) / 0.3
N_IN = D_ATTN + 2 * D_KV + 2 * D_LRU + 5 * D_HG + 3 * D_HY

COL_Q = 0
COL_K = D_ATTN
COL_V = D_ATTN + D_KV
COL_RX = D_ATTN + 2 * D_KV
COL_RG = COL_RX + D_LRU
COL_HQ = COL_RG + D_LRU
COL_HFF = COL_HQ + D_HG
COL_HFB = COL_HFF + D_HG
COL_HI = COL_HFB + D_HG
COL_HGATE = COL_HI + D_HG
COL_HY = COL_HGATE + D_HG

FF_CHUNK = 256
N_FF_CHUNK = D_FF // FF_CHUNK
ROW_TILE = 512
VMEM_LIMIT = 56 * 1024 * 1024


def _bf(x):
    return x.astype(BF16)


def _dot(a, b):
    return jnp.dot(a, b, preferred_element_type=F32)


def _split3(x):
    hi = _bf(x)
    r1 = x - hi.astype(F32)
    mid = _bf(r1)
    lo = _bf(r1 - mid.astype(F32))
    return hi, mid, lo


def _dot_exact_rhs(mat_bf, x):
    hi, mid, lo = _split3(x)
    return _dot(mat_bf, hi) + _dot(mat_bf, mid) + _dot(mat_bf, lo)


def _sigmoid(x):
    return jax.nn.sigmoid(x)


def _silu(x):
    return x * _sigmoid(x)


def _norm_mod(x, lnw, shift, scale):
    ms = jnp.mean(x * x, axis=-1, keepdims=True)
    y = x * lax.rsqrt(ms + EPS) * lnw
    return y * (1.0 + scale) + shift


def _cparams(sem):
    return pltpu.CompilerParams(dimension_semantics=sem, vmem_limit_bytes=VMEM_LIMIT)


def _const_spec(shape):
    nd = len(shape)
    return pl.BlockSpec(shape, lambda *_: (0,) * nd)


def _ada_kernel(cond_ref, w_ref, b_ref, o_ref):
    c = cond_ref[...]
    o_ref[0] = _dot(_bf(_silu(c)), _bf(w_ref[0])) + b_ref[0]


def ada_mods(cond, w_ada, b_ada):
    r = cond.shape[0]
    cb = 1024
    ncol = (N_MOD * D_MODEL) // cb
    return pl.pallas_call(
        _ada_kernel,
        grid=(DEPTH, ncol),
        in_specs=[
            pl.BlockSpec((r, D_MODEL), lambda l, j: (0, 0)),
            pl.BlockSpec((1, D_MODEL, cb), lambda l, j: (l, 0, j)),
            pl.BlockSpec((1, 1, cb), lambda l, j: (l, 0, j)),
        ],
        out_specs=pl.BlockSpec((1, r, cb), lambda l, j: (l, 0, j)),
        out_shape=jax.ShapeDtypeStruct((DEPTH, r, N_MOD * D_MODEL), F32),
        compiler_params=_cparams(("parallel", "parallel")),
        name="ada_mods",
    )(cond, w_ada, b_ada.reshape(DEPTH, 1, N_MOD * D_MODEL))


def _seq_mod_index(seq_len, n_mod, tm):
    def idx(i):
        return jnp.minimum((i * tm) // seq_len, n_mod - 1)
    return idx


def _ffn_kernel(x_ref, mod_ref, ln_ref, wg_ref, wu_ref, wo_ref, fn_ref, o_ref, *, mod_base, final):
    x = x_ref[...]
    m = mod_ref[0]
    sh = m[mod_base:mod_base + 1]
    sc = m[mod_base + 1:mod_base + 2]
    g = m[mod_base + 2:mod_base + 3]
    h = _bf(_norm_mod(x, ln_ref[...], sh, sc))
    acc = None
    for c in range(N_FF_CHUNK):
        gate = _dot(h, wg_ref[c])
        up = _dot(h, wu_ref[c])
        part = _dot(_bf(_silu(gate) * up), wo_ref[c])
        acc = part if acc is None else acc + part
    y = x + 0.5 * g * acc
    if final:
        ms = jnp.mean(y * y, axis=-1, keepdims=True)
        y = y * lax.rsqrt(ms + EPS) * fn_ref[...]
    o_ref[...] = y


def ffn_half_step(x, mods, ln_w, w_in, w_out, final_w, *, mod_base, final, seq_len):
    t = x.shape[0]
    tm = ROW_TILE
    wg = _bf(w_in[:, :D_FF]).reshape(D_MODEL, N_FF_CHUNK, FF_CHUNK).transpose(1, 0, 2)
    wu = _bf(w_in[:, D_FF:]).reshape(D_MODEL, N_FF_CHUNK, FF_CHUNK).transpose(1, 0, 2)
    wo = _bf(w_out).reshape(N_FF_CHUNK, FF_CHUNK, D_MODEL)
    midx = _seq_mod_index(seq_len, mods.shape[0], tm)
    return pl.pallas_call(
        functools.partial(_ffn_kernel, mod_base=mod_base, final=final),
        grid=(t // tm,),
        in_specs=[
            pl.BlockSpec((tm, D_MODEL), lambda i: (i, 0)),
            pl.BlockSpec((1, N_MOD, D_MODEL), lambda i: (midx(i), 0, 0)),
            _const_spec((1, D_MODEL)),
            _const_spec((N_FF_CHUNK, D_MODEL, FF_CHUNK)),
            _const_spec((N_FF_CHUNK, D_MODEL, FF_CHUNK)),
            _const_spec((N_FF_CHUNK, FF_CHUNK, D_MODEL)),
            _const_spec((1, D_MODEL)),
        ],
        out_specs=pl.BlockSpec((tm, D_MODEL), lambda i: (i, 0)),
        out_shape=jax.ShapeDtypeStruct((t, D_MODEL), F32),
        compiler_params=_cparams(("parallel",)),
        name="ffn_half_step",
    )(x, mods, ln_w.reshape(1, D_MODEL), wg, wu, wo, final_w.reshape(1, D_MODEL))


PROJ_CHUNK = 256


def _proj_kernel(x_ref, mod_ref, ln_ref, w_ref, o_ref):
    x = x_ref[...]
    m = mod_ref[0]
    u = _bf(_norm_mod(x, ln_ref[...], m[3:4], m[4:5]))
    for c in range(N_IN // PROJ_CHUNK):
        sl = slice(c * PROJ_CHUNK, (c + 1) * PROJ_CHUNK)
        o_ref[:, sl] = _dot(u, w_ref[:, sl])


def mixer_in_proj(x, mods, ln_w, w_in, *, seq_len):
    t = x.shape[0]
    tm = ROW_TILE
    midx = _seq_mod_index(seq_len, mods.shape[0], tm)
    return pl.pallas_call(
        _proj_kernel,
        grid=(t // tm,),
        in_specs=[
            pl.BlockSpec((tm, D_MODEL), lambda i: (i, 0)),
            pl.BlockSpec((1, N_MOD, D_MODEL), lambda i: (midx(i), 0, 0)),
            _const_spec((1, D_MODEL)),
            _const_spec((D_MODEL, N_IN)),
        ],
        out_specs=pl.BlockSpec((tm, N_IN), lambda i: (i, 0)),
        out_shape=jax.ShapeDtypeStruct((t, N_IN), F32),
        compiler_params=_cparams(("parallel",)),
        name="mixer_in_proj",
    )(x, mods, ln_w.reshape(1, D_MODEL), _bf(w_in))


MERGE_CHUNK = 256


def _merge_kernel(x_ref, mod_ref, ln_ref, att_ref, lru_ref, hgf_ref, hgb_ref, hgg_ref, hgn_ref, headb_ref, hy_ref,
                  wg_ref, wa_ref, wl_ref, wh_ref, wy_ref, wo_ref, o_ref, mix_ref):
    x = x_ref[...]
    m = mod_ref[0]
    u = _bf(_norm_mod(x, ln_ref[...], m[3:4], m[4:5]))
    o = hgf_ref[0] + hgb_ref[0]
    sq = o * o
    sq_hi = _bf(sq)
    sq_lo = _bf(sq - sq_hi.astype(F32))
    ms = (_dot(sq_hi, headb_ref[...]) + _dot(sq_lo, headb_ref[...])) * (1.0 / HG_DV)
    hg = o * lax.rsqrt(ms + EPS) * hgn_ref[...] * _silu(hgg_ref[...])
    branches = (_bf(att_ref[...]), _bf(lru_ref[...]), _bf(hg), _bf(hy_ref[...]))
    w_bo = (wa_ref, wl_ref, wh_ref, wy_ref)
    for c in range(D_MODEL // MERGE_CHUNK):
        sl = slice(c * MERGE_CHUNK, (c + 1) * MERGE_CHUNK)
        mixed = None
        for n in range(N_BRANCH):
            term = _sigmoid(_dot(u, wg_ref[n, :, sl])) * _dot(branches[n], w_bo[n][:, sl])
            mixed = term if mixed is None else mixed + term
        mix_ref[:, sl] = _bf(mixed)
    o_ref[...] = x + m[5:6] * _dot(mix_ref[...], wo_ref[...])


def gated_merge(x, mods, ln_w, att, lru, hg_dirs, p2d, hg_norm, hy, w_gate, w_bo_attn, w_bo_lru, w_bo_hg, w_bo_hy,
                w_out, *, seq_len):
    t = x.shape[0]
    tm = ROW_TILE
    midx = _seq_mod_index(seq_len, mods.shape[0], tm)
    row = lambda w: pl.BlockSpec((tm, w), lambda i: (i, 0))
    _, _, headb = _hg_constants()
    return pl.pallas_call(
        _merge_kernel,
        grid=(t // tm,),
        in_specs=[
            row(D_MODEL),
            pl.BlockSpec((1, N_MOD, D_MODEL), lambda i: (midx(i), 0, 0)),
            _const_spec((1, D_MODEL)),
            row(D_ATTN), row(D_LRU),
            pl.BlockSpec((1, tm, D_HG), lambda i: (0, i, 0)),
            pl.BlockSpec((1, tm, D_HG), lambda i: (1, i, 0)),
            pl.BlockSpec((tm, D_HG), lambda i: (i, COL_HGATE // D_HG)),
            _const_spec((1, D_HG)),
            _const_spec((D_HG, D_HG)),
            row(D_HY),
            _const_spec((N_BRANCH, D_MODEL, D_MODEL)),
            _const_spec((D_ATTN, D_MODEL)),
            _const_spec((D_LRU, D_MODEL)),
            _const_spec((D_HG, D_MODEL)),
            _const_spec((D_HY, D_MODEL)),
            _const_spec((D_MODEL, D_MODEL)),
        ],
        out_specs=row(D_MODEL),
        out_shape=jax.ShapeDtypeStruct((t, D_MODEL), F32),
        scratch_shapes=[pltpu.VMEM((tm, D_MODEL), BF16)],
        compiler_params=_cparams(("parallel",)),
        name="gated_merge",
    )(x, mods, ln_w.reshape(1, D_MODEL), att, lru, hg_dirs, hg_dirs, p2d, hg_norm.reshape(1, D_HG), headb, hy,
      _bf(w_gate), _bf(w_bo_attn), _bf(w_bo_lru), _bf(w_bo_hg), _bf(w_bo_hy), _bf(w_out))


def _dot_nt(a, b):
    return lax.dot_general(a, b, (((1,), (1,)), ((), ())), preferred_element_type=F32)


def _ctx_attn_kernel(q_ref, k_ref, v_ref, sink_ref, o_ref):
    rows = q_ref.shape[1]
    blk = ATTN_BLOCK
    keys = _dup_heads(k_ref[0])
    vals_t = _values_t(v_ref[0])
    low = lax.broadcasted_iota(jnp.int32, (blk, 2 * HEAD_DIM), 1) < HEAD_DIM
    units = [(pair, qb) for qb in range(rows // blk) for pair in range(N_HEADS // 2)]

    def where(pair, qb):
        return slice(qb * blk, (qb + 1) * blk), slice(pair * 2 * HEAD_DIM, (pair + 1) * 2 * HEAD_DIM)

    scores = []
    for pair, qb in units:
        q_rows, lanes = where(pair, qb)
        q = q_ref[0, q_rows, lanes] * ATTN_SCALE
        q_heads = _bf(jnp.concatenate([jnp.where(low, q, 0.0), jnp.where(low, 0.0, q)], axis=0))
        scores.append(_dot_nt(keys[(2 * pair) // GQA_GROUP], q_heads))
    probs = []
    for (pair, qb), s in zip(units, scores):
        sink = jnp.concatenate([sink_ref[2 * pair:2 * pair + 1, :], sink_ref[2 * pair + 1:2 * pair + 2, :]], axis=1)
        m = jnp.maximum(jnp.max(s, axis=0, keepdims=True), sink)
        probs.append((_bf(jnp.exp(s - m)), jnp.exp(sink - m)))
    for (pair, qb), (p, sink_term) in zip(units, probs):
        q_rows, lanes = where(pair, qb)
        acc = _dot(vals_t[(2 * pair) // GQA_GROUP], p)
        out_t = acc[:HEAD_DIM, :] / (acc[HEAD_DIM:HEAD_DIM + 1, :] + sink_term)
        o_ref[0, q_rows, lanes] = jnp.transpose(jnp.concatenate([out_t[:, :blk], out_t[:, blk:]], axis=0))


def _sink_table(sink):
    return jnp.broadcast_to(sink.astype(F32)[:, None], (N_HEADS, 128))


def context_attention(p_ctx, sink):
    b, seq, _ = p_ctx.shape
    return pl.pallas_call(
        _ctx_attn_kernel,
        grid=(b,),
        in_specs=[
            pl.BlockSpec((1, seq, D_ATTN), lambda i: (i, 0, 0)),
            pl.BlockSpec((1, seq, D_KV), lambda i: (i, 0, COL_K // D_KV)),
            pl.BlockSpec((1, seq, D_KV), lambda i: (i, 0, COL_V // D_KV)),
            _const_spec((N_HEADS, 128)),
        ],
        out_specs=pl.BlockSpec((1, seq, D_ATTN), lambda i: (i, 0, 0)),
        out_shape=jax.ShapeDtypeStruct((b, seq, D_ATTN), F32),
        compiler_params=_cparams(("parallel",)),
        name="context_attention",
    )(p_ctx, p_ctx, p_ctx, _sink_table(sink))


def _rope(x, cos, sin_signed):
    n = x.shape[-1]
    lane = lax.broadcasted_iota(jnp.int32, x.shape, x.ndim - 1)
    first = (lane % (HEAD_DIM // 2)) < (HEAD_DIM // 4)
    partner = jnp.where(first, pltpu.roll(x, n - HEAD_DIM // 4, axis=x.ndim - 1),
                        pltpu.roll(x, HEAD_DIM // 4, axis=x.ndim - 1))
    return x * cos + partner * sin_signed


def _rope_tables(seq):
    rows = seq // GRID_W
    row = np.repeat(np.arange(rows, dtype=np.float32), GRID_W)
    col = np.tile(np.arange(GRID_W, dtype=np.float32), rows)
    n = HEAD_DIM // 4
    inv = (ROPE_BASE ** (-jnp.arange(n, dtype=F32) / n))
    ang_r = jnp.asarray(row)[:, None] * inv
    ang_c = jnp.asarray(col)[:, None] * inv
    cos = jnp.concatenate([jnp.cos(ang_r), jnp.cos(ang_r), jnp.cos(ang_c), jnp.cos(ang_c)], axis=-1)
    sin = jnp.concatenate([-jnp.sin(ang_r), jnp.sin(ang_r), -jnp.sin(ang_c), jnp.sin(ang_c)], axis=-1)
    return jnp.tile(cos, (1, 2)), jnp.tile(sin, (1, 2))


ATTN_PREP_ROWS = 512
ATTN_QBLOCKS = 4


def _dup_heads(x):
    lane = lax.broadcasted_iota(jnp.int32, x.shape, 1)
    swapped = pltpu.roll(x, HEAD_DIM, axis=1)
    low = lane < HEAD_DIM
    return _bf(jnp.where(low, x, swapped)), _bf(jnp.where(low, swapped, x))


def _values_t(x):
    xt = jnp.transpose(x)
    ones = jnp.ones((HEAD_DIM, x.shape[0]), F32)
    return (_bf(jnp.concatenate([xt[:HEAD_DIM], ones], axis=0)),
            _bf(jnp.concatenate([xt[HEAD_DIM:], ones], axis=0)))


def _lat_attn_kernel(q_ref, k_ref, v_ref, kc_ref, vc_ref, cos_ref, sin_ref, sink_ref, o_ref,
                     kd_ref, vt_ref, kcd_ref, vct_ref):
    seq = k_ref.shape[1]
    blk = ATTN_BLOCK
    band = 3 * blk
    past = kc_ref.shape[1]
    i = pl.program_id(1)

    @pl.when(i == 0)
    def _():
        for ti in range(seq // ATTN_PREP_ROWS):
            rows = slice(ti * ATTN_PREP_ROWS, (ti + 1) * ATTN_PREP_ROWS)
            k0, k1 = _dup_heads(_rope(k_ref[0, rows, :], cos_ref[rows, :], sin_ref[rows, :]))
            kd_ref[0, rows, :] = k0
            kd_ref[1, rows, :] = k1
            v0, v1 = _values_t(v_ref[0, rows, :])
            vt_ref[0, :, rows] = v0
            vt_ref[1, :, rows] = v1
        kc0, kc1 = _dup_heads(kc_ref[0])
        kcd_ref[0] = kc0
        kcd_ref[1] = kc1
        vc0, vc1 = _values_t(vc_ref[0])
        vct_ref[0] = vc0
        vct_ref[1] = vc1

    key = lax.broadcasted_iota(jnp.int32, (band + past, 2 * blk), 0)
    qry = lax.broadcasted_iota(jnp.int32, (band + past, 2 * blk), 1) % blk
    low = lax.broadcasted_iota(jnp.int32, (blk, 2 * HEAD_DIM), 1) < HEAD_DIM
    subs = []
    for sub in range(ATTN_QBLOCKS):
        ib = ATTN_QBLOCKS * i + sub
        start = pl.multiple_of(jnp.clip((ib - 1) * blk, 0, seq - band), blk)
        q0 = pl.multiple_of(ib * blk, blk)
        subs.append(dict(
            rows=slice(sub * blk, (sub + 1) * blk),
            cos=cos_ref[pl.ds(q0, blk), :], sin=sin_ref[pl.ds(q0, blk), :],
            mask=(jnp.abs((q0 + qry) - (start + key)) <= WINDOW) | (key >= band),
            keys=[jnp.concatenate([kd_ref[kvh, pl.ds(start, band), :], kcd_ref[kvh]], axis=0)
                  for kvh in range(N_KV_HEADS)],
            vals_t=[jnp.concatenate([vt_ref[kvh, :, pl.ds(start, band)], vct_ref[kvh]], axis=1)
                    for kvh in range(N_KV_HEADS)]))
    units = [(sub, pair) for sub in range(ATTN_QBLOCKS) for pair in range(N_HEADS // 2)]
    scores = []
    for sub, pair in units:
        u = subs[sub]
        lanes = slice(pair * 2 * HEAD_DIM, (pair + 1) * 2 * HEAD_DIM)
        q = _rope(q_ref[0, u['rows'], lanes], u['cos'], u['sin']) * ATTN_SCALE
        q_heads = _bf(jnp.concatenate([jnp.where(low, q, 0.0), jnp.where(low, 0.0, q)], axis=0))
        scores.append(_dot_nt(u['keys'][(2 * pair) // GQA_GROUP], q_heads))
    probs = []
    for (sub, pair), raw in zip(units, scores):
        s = jnp.where(subs[sub]['mask'], raw, -jnp.inf)
        sink = jnp.concatenate([sink_ref[2 * pair:2 * pair + 1, :], sink_ref[2 * pair + 1:2 * pair + 2, :]], axis=1)
        m = jnp.maximum(jnp.max(s, axis=0, keepdims=True), sink)
        probs.append((_bf(jnp.exp(s - m)), jnp.exp(sink - m)))
    for (sub, pair), (p, sink_term) in zip(units, probs):
        u = subs[sub]
        lanes = slice(pair * 2 * HEAD_DIM, (pair + 1) * 2 * HEAD_DIM)
        acc = _dot(u['vals_t'][(2 * pair) // GQA_GROUP], p)
        out_t = acc[:HEAD_DIM, :] / (acc[HEAD_DIM:HEAD_DIM + 1, :] + sink_term)
        o_ref[0, u['rows'], lanes] = jnp.transpose(jnp.concatenate([out_t[:, :blk], out_t[:, blk:]], axis=0))


def latent_attention(p_lat, k_ctx, v_ctx, sink):
    b, seq, _ = p_lat.shape
    past = k_ctx.shape[1]
    cos, sin = _rope_tables(seq)
    return pl.pallas_call(
        _lat_attn_kernel,
        grid=(b, seq // (ATTN_QBLOCKS * ATTN_BLOCK)),
        in_specs=[
            pl.BlockSpec((1, ATTN_QBLOCKS * ATTN_BLOCK, D_ATTN), lambda bi, i: (bi, i, 0)),
            pl.BlockSpec((1, seq, D_KV), lambda bi, i: (bi, 0, COL_K // D_KV)),
            pl.BlockSpec((1, seq, D_KV), lambda bi, i: (bi, 0, COL_V // D_KV)),
            pl.BlockSpec((1, past, D_KV), lambda bi, i: (bi, 0, 0)),
            pl.BlockSpec((1, past, D_KV), lambda bi, i: (bi, 0, 0)),
            _const_spec((seq, 128)),
            _const_spec((seq, 128)),
            _const_spec((N_HEADS, 128)),
        ],
        out_specs=pl.BlockSpec((1, ATTN_QBLOCKS * ATTN_BLOCK, D_ATTN), lambda bi, i: (bi, i, 0)),
        out_shape=jax.ShapeDtypeStruct((b, seq, D_ATTN), F32),
        scratch_shapes=[
            pltpu.VMEM((N_KV_HEADS, seq, D_KV), BF16),
            pltpu.VMEM((N_KV_HEADS, D_KV, seq), BF16),
            pltpu.VMEM((N_KV_HEADS, past, D_KV), BF16),
            pltpu.VMEM((N_KV_HEADS, D_KV, past), BF16),
        ],
        compiler_params=_cparams(("parallel", "arbitrary")),
        name="latent_attention",
    )(p_lat, p_lat, p_lat, k_ctx, v_ctx, cos, sin, _sink_table(sink))


LRU_HALF = D_LRU // 2
LRU_ROWS = 256
LRU_PAD = 8


def _neg_expm1(y, exp_y):
    return jnp.tanh(-0.5 * y) * (exp_y + 1.0)


def _softplus(x):
    return jnp.maximum(x, 0.0) + jnp.log1p(jnp.exp(-jnp.abs(x)))


def _gelu_tanh(x):
    return 0.5 * x * (1.0 + jnp.tanh(math.sqrt(2.0 / math.pi) * (x + 0.044715 * (x * x * x))))


LRU_SCAN = 8


def _affine_scan8(a, b, *, reverse):
    row = lax.broadcasted_iota(jnp.int32, a.shape, 0)
    k = 1
    while k < LRU_SCAN:
        if reverse:
            valid = row < LRU_SCAN - k
            shift = LRU_SCAN - k
        else:
            valid = row >= k
            shift = k
        a_prev = jnp.where(valid, pltpu.roll(a, shift, axis=0), 1.0)
        b_prev = jnp.where(valid, pltpu.roll(b, shift, axis=0), 0.0)
        b = b + a * b_prev
        a = a * a_prev
        k *= 2
    return a, b


def _lru_kernel(x_ref, g_ref, cw_ref, cb_ref, wr_ref, br_ref, wi_ref, bi_ref, lam_ref, h0_ref,
                o_ref, st_ref, pad_ref, a_ref, bx_ref, hb_ref):
    seq = x_ref.shape[1]
    zeros = jnp.zeros((LRU_PAD, LRU_HALF), F32)
    pad_ref[0:LRU_PAD, :] = zeros
    pad_ref[LRU_PAD + seq:2 * LRU_PAD + seq, :] = zeros
    pad_ref[LRU_PAD:LRU_PAD + seq, :] = x_ref[0]
    left = LRU_CONV // 2
    for ti in range(seq // LRU_ROWS):
        r0 = ti * LRU_ROWS
        xc = cb_ref[...]
        for k in range(LRU_CONV):
            xc = xc + cw_ref[k:k + 1, :] * pad_ref[r0 + LRU_PAD - left + k:r0 + LRU_PAD - left + k + LRU_ROWS, :]
        xcb = _bf(xc)
        for d in range(2):
            r = _sigmoid(_dot(xcb, wr_ref[d]) + br_ref[d:d + 1, :])
            i = _sigmoid(_dot(xcb, wi_ref[d]) + bi_ref[d:d + 1, :])
            log_a = (-LRU_C) * r * _softplus(-lam_ref[d:d + 1, :])
            a = jnp.exp(log_a)
            a_ref[d, r0:r0 + LRU_ROWS, :] = a
            bx_ref[d, r0:r0 + LRU_ROWS, :] = jnp.sqrt(_neg_expm1(2.0 * log_a, a * a)) * (i * xc)

    nt = seq // LRU_SCAN

    def step(blk, carry):
        hf, hb = carry
        rf = pl.ds(pl.multiple_of(blk * LRU_SCAN, LRU_SCAN), LRU_SCAN)
        rb = pl.ds(pl.multiple_of((nt - 1 - blk) * LRU_SCAN, LRU_SCAN), LRU_SCAN)
        af, bf_ = _affine_scan8(a_ref[0, rf, :], bx_ref[0, rf, :], reverse=False)
        ab, bb = _affine_scan8(a_ref[1, rb, :], bx_ref[1, rb, :], reverse=True)
        tile_f = af * hf + bf_
        tile_b = ab * hb + bb
        o_ref[0, rf, :] = tile_f
        hb_ref[rb, :] = tile_b
        return tile_f[LRU_SCAN - 1:LRU_SCAN, :], tile_b[0:1, :]

    hf, hb = lax.fori_loop(0, nt, step, (h0_ref[0, 0:1, :], h0_ref[0, 1:2, :]), unroll=2)
    st_ref[0, 0:1, :] = hf
    st_ref[0, 1:2, :] = hb
    for ti in range(seq // LRU_ROWS):
        rows = slice(ti * LRU_ROWS, (ti + 1) * LRU_ROWS)
        o_ref[0, rows, :] = (o_ref[0, rows, :] + hb_ref[rows, :]) * _gelu_tanh(g_ref[0, rows, :])


def _block_diag(w):
    n, blk, _ = w.shape
    eye = jnp.eye(n, dtype=w.dtype)
    return (eye[:, None, :, None] * w[:, :, None, :]).reshape(n * blk, n * blk)


def rglru_mixer(p, conv_w, conv_b, w_r, b_r, w_i, b_i, lam, h0):
    b, seq, _ = p.shape
    wr = _bf(jnp.stack([_block_diag(w_r[d]) for d in range(2)]))
    wi = _bf(jnp.stack([_block_diag(w_i[d]) for d in range(2)]))
    hw = LRU_HALF
    vec = lambda rows: pl.BlockSpec((rows, hw), lambda bi, h: (0, h))
    return pl.pallas_call(
        _lru_kernel,
        grid=(b, 2),
        in_specs=[
            pl.BlockSpec((1, seq, hw), lambda bi, h: (bi, 0, COL_RX // hw + h)),
            pl.BlockSpec((1, seq, hw), lambda bi, h: (bi, 0, COL_RG // hw + h)),
            vec(LRU_CONV), vec(1),
            pl.BlockSpec((2, hw, hw), lambda bi, h: (0, h, h)), vec(2),
            pl.BlockSpec((2, hw, hw), lambda bi, h: (0, h, h)), vec(2),
            vec(2),
            pl.BlockSpec((1, 2, hw), lambda bi, h: (bi, 0, h)),
        ],
        out_specs=[
            pl.BlockSpec((1, seq, hw), lambda bi, h: (bi, 0, h)),
            pl.BlockSpec((1, 2, hw), lambda bi, h: (bi, 0, h)),
        ],
        out_shape=[jax.ShapeDtypeStruct((b, seq, D_LRU), F32), jax.ShapeDtypeStruct((b, 2, D_LRU), F32)],
        scratch_shapes=[
            pltpu.VMEM((seq + 2 * LRU_PAD, hw), F32),
            pltpu.VMEM((2, seq, hw), F32),
            pltpu.VMEM((2, seq, hw), F32),
            pltpu.VMEM((seq, hw), F32),
        ],
        compiler_params=_cparams(("parallel", "parallel")),
        name="rglru_mixer",
    )(p, p, conv_w, conv_b.reshape(1, D_LRU), wr, b_r, wi, b_i, lam, h0)


HG_TILE = 256
HG_GROUP = 8
HG_AHEAD = 1
HG_NBUF = 2
LOG2_E = math.log2(math.e)


def _hg_constants():
    idx = np.arange(HG_TILE)
    same_blk = (idx[:, None] // HG_BLK) == (idx[None, :] // HG_BLK)
    tri_f = same_blk & (idx[None, :] <= idx[:, None])
    tri_b = same_blk & (idx[None, :] >= idx[:, None])
    head = (idx[:, None] // HG_DK) == (idx[None, :] // HG_DK)
    return jnp.asarray(np.stack([tri_f, tri_b]), BF16), jnp.asarray(same_blk, BF16), jnp.asarray(head, BF16)


def _hgrn_kernel(q_ref, z_ref, v_ref, lb_ref, tri_ref, ones_ref, headb_ref, s0_ref,
                 o_ref, st_ref, state, *scratch):
    att_bufs = scratch[0:HG_NBUF]
    upd_bufs = scratch[HG_NBUF:2 * HG_NBUF]
    qst_bufs = scratch[2 * HG_NBUF:3 * HG_NBUF]
    qs, srcs, vs, cums, qds, kns, tots = scratch[3 * HG_NBUF:]
    seg_rows = q_ref.shape[1]
    d = pl.program_id(1)
    s = pl.program_id(2)
    nseg = pl.num_programs(2)

    @pl.when(s == 0)
    def _():
        state[...] = jnp.concatenate([jnp.transpose(s0_ref[0, 0, h]) for h in range(HG_HEADS)], axis=1)

    lb = lb_ref[0]
    for ti in range(seg_rows // HG_TILE):
        rows = slice(ti * HG_TILE, (ti + 1) * HG_TILE)
        f = lb + (1.0 - lb) * _sigmoid(z_ref[0, rows, :])
        g = jnp.log(f)
        kk = 1.0 - f
        q = _silu(q_ref[0, rows, :]) * HG_SCALE
        cum = _dot_exact_rhs(tri_ref[0], g)
        tot = _dot_exact_rhs(ones_ref[...], g)
        qs[rows, :] = q
        srcs[rows, :] = (cum - jnp.log(kk)) * LOG2_E
        vs[rows, :] = v_ref[0, rows, :]
        cums[rows, :] = cum * LOG2_E
        tots[rows, :] = tot
        qds[rows, :] = q * jnp.exp(cum)
        kns[rows, :] = kk * jnp.exp(tot - cum)

    nblk = seg_rows // HG_BLK
    sgn = jnp.where(d == 0, 1, -1)
    t_signed = [(lax.broadcasted_iota(jnp.int32, (8, D_HG), 0) + 8 * h) * sgn for h in range(HG_BLK // 8)]
    lane_head_blk = lax.broadcasted_iota(jnp.int32, (HG_BLK, D_HG), 1) // HG_DK
    lane_head_dv = lax.broadcasted_iota(jnp.int32, (HG_DV, D_HG), 1) // HG_DK

    n_half = HG_BLK // 8

    def block_rows(j):
        je = jnp.where(d == 0, j, nblk - 1 - j)
        return pl.multiple_of(je * HG_BLK, HG_BLK)

    def prepare(j, slot):
        r0 = block_rows(j)
        rows = pl.ds(r0, HG_BLK)
        cum_h = [cums[pl.ds(r0 + 8 * h, 8), :] for h in range(n_half)]
        q_h = [qs[pl.ds(r0 + 8 * h, 8), :] for h in range(n_half)]
        tiles = []
        for src in range(HG_BLK):
            src_row = srcs[pl.ds(r0 + src, 1), :]
            for h in range(n_half):
                pair = jnp.exp2(cum_h[h] - src_row) * q_h[h]
                tiles.append(jnp.where(t_signed[h] >= src * sgn, pair, 0.0))
        att_bufs[slot][...] = _dot(_bf(jnp.concatenate(tiles, axis=0)), headb_ref[...])
        upd = lax.dot_general(_bf(vs[rows, :]), _bf(kns[rows, :]), (((0,), (0,)), ((), ())),
                              preferred_element_type=F32)
        upd_fold = jnp.where(lane_head_dv == 0, upd[0:HG_DV], 0.0)
        for h in range(1, HG_HEADS):
            upd_fold = upd_fold + jnp.where(lane_head_dv == h, upd[h * HG_DV:(h + 1) * HG_DV], 0.0)
        upd_bufs[slot][...] = upd_fold
        qd = qds[rows, :]
        qst_bufs[slot][...] = _bf(jnp.concatenate(
            [jnp.where(lane_head_blk == h, qd, 0.0) for h in range(HG_HEADS)], axis=0))

    def finish(j, slot, st):
        r0 = block_rows(j)
        rows = pl.ds(r0, HG_BLK)
        o_halves = [None] * n_half
        for src in range(HG_BLK):
            v_row = vs[pl.ds(r0 + src, 1), :]
            for h in range(n_half):
                term = att_bufs[slot][src * HG_BLK + h * 8:src * HG_BLK + (h + 1) * 8, :] * v_row
                o_halves[h] = term if o_halves[h] is None else o_halves[h] + term
        by_head = _dot_nt(qst_bufs[slot][...], _bf(st))
        o_state = jnp.concatenate([by_head[h * HG_BLK:(h + 1) * HG_BLK] for h in range(HG_HEADS)], axis=1)
        o_ref[0, 0, rows, :] = jnp.concatenate(o_halves, axis=0) + o_state
        return jnp.exp(tots[pl.ds(r0, 1), :]) * st + upd_bufs[slot][...]

    for k in range(HG_AHEAD):
        prepare(k, k)

    def block_group(i, st):
        j = HG_GROUP * i
        for k in range(HG_GROUP):
            prepare(jnp.minimum(j + k + HG_AHEAD, nblk - 1), (k + HG_AHEAD) % HG_NBUF)
            st = finish(j + k, k % HG_NBUF, st)
        return st

    state[...] = lax.fori_loop(0, nblk // HG_GROUP, block_group, state[...])

    @pl.when(s == nseg - 1)
    def _():
        for h in range(HG_HEADS):
            st_ref[0, 0, h] = jnp.transpose(state[:, h * HG_DK:(h + 1) * HG_DK])


def hgrn2_scan(p, lb, s0, seg_rows):
    b, seq, _ = p.shape
    nseg = seq // seg_rows
    tri, ones, headb = _hg_constants()
    seg = lambda d, s: jnp.where(d == 0, s, nseg - 1 - s)
    col = lambda c: pl.BlockSpec((1, seg_rows, D_HG), lambda bi, d, s: (bi, seg(d, s), c // D_HG))
    scr = lambda: pltpu.VMEM((seg_rows, D_HG), F32)
    return pl.pallas_call(
        _hgrn_kernel,
        grid=(b, 2, nseg),
        in_specs=[
            col(COL_HQ),
            pl.BlockSpec((1, seg_rows, D_HG), lambda bi, d, s: (bi, seg(d, s), COL_HFF // D_HG + d)),
            col(COL_HI),
            pl.BlockSpec((1, 1, D_HG), lambda bi, d, s: (d, 0, 0)),
            pl.BlockSpec((1, HG_TILE, HG_TILE), lambda bi, d, s: (d, 0, 0)),
            _const_spec((HG_TILE, HG_TILE)),
            _const_spec((D_HG, D_HG)),
            pl.BlockSpec((1, 1, HG_HEADS, HG_DK, HG_DV), lambda bi, d, s: (bi, d, 0, 0, 0)),
        ],
        out_specs=[
            pl.BlockSpec((1, 1, seg_rows, D_HG), lambda bi, d, s: (d, bi, seg(d, s), 0)),
            pl.BlockSpec((1, 1, HG_HEADS, HG_DK, HG_DV), lambda bi, d, s: (bi, d, 0, 0, 0)),
        ],
        out_shape=[jax.ShapeDtypeStruct((2, b, seq, D_HG), F32),
                   jax.ShapeDtypeStruct((b, 2, HG_HEADS, HG_DK, HG_DV), F32)],
        scratch_shapes=[pltpu.VMEM((HG_DV, D_HG), F32)]
        + [pltpu.VMEM((HG_BLK * HG_BLK, D_HG), F32) for _ in range(HG_NBUF)]
        + [pltpu.VMEM((HG_DV, D_HG), F32) for _ in range(HG_NBUF)]
        + [pltpu.VMEM((HG_HEADS * HG_BLK, D_HG), BF16) for _ in range(HG_NBUF)] + [scr() for _ in range(7)],
        compiler_params=_cparams(("parallel", "parallel", "arbitrary")),
        name="hgrn2_scan",
    )(p, p, p, lb.reshape(2, 1, D_HG), tri, ones, headb, s0)


HY_COLS = 2 * HY_ORDER * D_HY
HY_FEAT_PAD = 128
HY_GEN_ROWS = 256


def _dot_hi(a, b):
    ah = _bf(a)
    al = _bf(a - ah.astype(F32))
    bh = _bf(b)
    bl = _bf(b - bh.astype(F32))
    return _dot(ah, bh) + _dot(ah, bl) + _dot(al, bh)


HY_HALF_COLS = HY_ORDER * D_HY


def _filter_gen_kernel(feat_ref, w1_ref, b1_ref, w2_ref, b2_ref, w3_ref, fr_ref, dl_ref, h_ref, ssq_ref, *, seq):
    i = pl.program_id(0)
    tm = feat_ref.shape[0]
    feat = feat_ref[...]
    fr = fr_ref[...]
    h = jnp.sin(fr * (_dot_hi(feat, w1_ref[...]) + b1_ref[...]))
    h = jnp.sin(fr * (_dot_hi(h, w2_ref[...]) + b2_ref[...]))
    h = _dot_hi(h, w3_ref[...])
    decay = jnp.exp(-feat[:, 0:1] * dl_ref[...])
    decay = jnp.concatenate([decay] * HY_ORDER, axis=1)
    row = i * tm + lax.broadcasted_iota(jnp.int32, (tm, HY_HALF_COLS), 0)
    taps = h * decay
    h_ref[...] = jnp.where(row == seq, 0.0, taps)

    @pl.when(i == 0)
    def _():
        ssq_ref[...] = jnp.zeros_like(ssq_ref)

    ssq_ref[...] += jnp.sum(taps * taps, axis=0, keepdims=True)


def _hyena_features(seq):
    order = np.concatenate([np.arange(seq), [0], np.arange(seq - 1, 0, -1)])
    pos = order.astype(np.float32)
    t = pos / np.float32(max(seq - 1, 1))
    bands = np.linspace(1e-4, HY_BANDS - 1, HY_BANDS, dtype=np.float32)
    ang = np.float32(2.0 * math.pi / seq) * pos[:, None] * bands[None, :]
    feat = np.concatenate([t[:, None], np.cos(ang), np.sin(ang)], axis=-1).astype(np.float32)
    out = np.zeros((2 * seq, HY_FEAT_PAD), np.float32)
    out[:, :HY_EMB] = feat
    return jnp.asarray(out)


def hyena_filter_gen(seq, w1, b1, w2, b2, w3, freq):
    feat = _hyena_features(seq)
    w1p = jnp.zeros((HY_FEAT_PAD, HY_FFN), F32).at[:HY_EMB].set(w1.astype(F32))
    deltas = jnp.asarray(np.linspace(HY_DECAY_SLOW, HY_DECAY_FAST, D_HY, dtype=np.float32)).reshape(1, D_HY)
    tm = HY_GEN_ROWS
    return pl.pallas_call(
        functools.partial(_filter_gen_kernel, seq=seq),
        grid=(2 * seq // tm,),
        in_specs=[
            pl.BlockSpec((tm, HY_FEAT_PAD), lambda i: (i, 0)),
            _const_spec((HY_FEAT_PAD, HY_FFN)), _const_spec((1, HY_FFN)),
            _const_spec((HY_FFN, HY_FFN)), _const_spec((1, HY_FFN)),
            pl.BlockSpec((HY_FFN, HY_HALF_COLS), lambda i: (0, (i * tm) // seq)),
            _const_spec((1, HY_FFN)), _const_spec((1, D_HY)),
        ],
        out_specs=[pl.BlockSpec((tm, HY_HALF_COLS), lambda i: (i, 0)), _const_spec((1, HY_HALF_COLS))],
        out_shape=[jax.ShapeDtypeStruct((2 * seq, HY_HALF_COLS), F32),
                   jax.ShapeDtypeStruct((1, HY_HALF_COLS), F32)],
        compiler_params=_cparams(("arbitrary",)),
        name="hyena_filter_gen",
    )(feat, w1p, b1.reshape(1, HY_FFN), w2, b2.reshape(1, HY_FFN), w3, freq.reshape(1, HY_FFN), deltas)


def _filter_norm(ssq_ref, n):
    return lax.rsqrt(ssq_ref[:, n * D_HY:(n + 1) * D_HY])


def _conv3(pad_ref, w_ref, b_ref, r0, rows, pad):
    acc = b_ref[...]
    for k in range(HY_SHORT):
        lo = r0 + pad - HY_SHORT // 2 + k
        acc = acc + w_ref[k:k + 1, :] * pad_ref[lo:lo + rows, :]
    return acc


def _dft_constants(seq):
    n = 2 * seq
    k = np.arange(n)
    ang = 2.0 * np.pi * np.outer(k, k) / n
    fwd = np.concatenate([np.cos(ang), -np.sin(ang)], axis=0)
    inv = np.concatenate([np.cos(ang[:seq]), -np.sin(ang[:seq])], axis=1) / n
    return jnp.asarray(fwd, BF16), jnp.asarray(inv, BF16)


def _ctx_spectrum_kernel(filt_ref, ssq_ref, fwd_ref, h_ref):
    n = pl.program_id(0)
    norm = jnp.where(n == 0, _filter_norm(ssq_ref, 0), _filter_norm(ssq_ref, 1))
    x = filt_ref[...] * norm
    xh = _bf(x)
    xl = _bf(x - xh.astype(F32))
    h_ref[0] = _dot(fwd_ref[...], xh) + _dot(fwd_ref[...], xl)


def hyena_ctx_spectrum(filt, ssq):
    n = filt.shape[0]
    fwd, _ = _dft_constants(n // 2)
    return pl.pallas_call(
        _ctx_spectrum_kernel,
        grid=(HY_ORDER,),
        in_specs=[pl.BlockSpec((n, D_HY), lambda o: (0, o)), _const_spec((1, HY_HALF_COLS)),
                  _const_spec((2 * n, n))],
        out_specs=pl.BlockSpec((1, 2 * n, D_HY), lambda o: (o, 0, 0)),
        out_shape=jax.ShapeDtypeStruct((HY_ORDER, 2 * n, D_HY), F32),
        compiler_params=_cparams(("parallel",)),
        name="hyena_ctx_spectrum",
    )(filt, ssq, fwd)


HY_PAD = 8


def _hyena_ctx_kernel(v_ref, x1_ref, x2_ref, cw_ref, cb_ref, bias_ref, spec_ref, fwd_ref, inv_ref, o_ref, pad_ref):
    seq = v_ref.shape[1]
    n = 2 * seq
    zeros = jnp.zeros((HY_PAD, D_HY), F32)
    pad_ref[0:HY_PAD, :] = zeros
    pad_ref[HY_PAD + seq:2 * HY_PAD + seq, :] = zeros

    def short_conv(src_ref, part):
        pad_ref[HY_PAD:HY_PAD + seq, :] = src_ref[0]
        return _conv3(pad_ref, cw_ref.at[part], cb_ref.at[part], 0, seq, HY_PAD)

    z = short_conv(v_ref, 0)
    for order, gate_ref in enumerate((x1_ref, x2_ref)):
        spec = _dot(fwd_ref[:, 0:seq], _bf(z))
        xr, xi = spec[:n], spec[n:]
        hr, hi = spec_ref[order, 0:n, :], spec_ref[order, n:2 * n, :]
        prod = jnp.concatenate([xr * hr - xi * hi, xr * hi + xi * hr], axis=0)
        conv = _dot(inv_ref[...], _bf(prod))
        z = short_conv(gate_ref, order + 1) * (conv + bias_ref[order:order + 1, :] * z)
    o_ref[0] = z


def hyena_ctx(p, conv_w, conv_b, bias, spec):
    b, seq, _ = p.shape
    fwd, inv = _dft_constants(seq)
    col = lambda c: pl.BlockSpec((1, seq, D_HY), lambda i: (i, 0, c))
    c0 = COL_HY // D_HY
    return pl.pallas_call(
        _hyena_ctx_kernel,
        grid=(b,),
        in_specs=[
            col(c0), col(c0 + 1), col(c0 + 2),
            _const_spec((3, HY_SHORT, D_HY)), _const_spec((3, 1, D_HY)), _const_spec((HY_ORDER, D_HY)),
            _const_spec((HY_ORDER, 4 * seq, D_HY)), _const_spec((4 * seq, 2 * seq)), _const_spec((seq, 4 * seq)),
        ],
        out_specs=pl.BlockSpec((1, seq, D_HY), lambda i: (i, 0, 0)),
        out_shape=jax.ShapeDtypeStruct((b, seq, D_HY), F32),
        scratch_shapes=[pltpu.VMEM((seq + 2 * HY_PAD, D_HY), F32)],
        compiler_params=_cparams(("parallel",)),
        name="hyena_ctx",
    )(p, p, p, conv_w.reshape(HY_SHORT, 3, D_HY).transpose(1, 0, 2), conv_b.reshape(3, 1, D_HY), bias, spec,
      fwd, inv)


FFT_N1 = 64
FFT_N2 = 128
FFT_N = FFT_N1 * FFT_N2
FFT_K1 = FFT_N1 // 2 + 1
FFT_K1_PAD = 40
FFT_ROWS = FFT_K1_PAD * FFT_N2
HY_LANES = 128
FFT_UNROLL1 = 16
FFT_UNROLL2 = 11


def _fft_constants():
    n1, n2, n, kp = FFT_N1, FFT_N2, FFT_N, FFT_K1_PAD
    k1 = np.arange(kp)
    w1 = np.exp(-2j * np.pi * np.outer(k1, np.arange(n1)) / n1)
    tw = np.exp(-2j * np.pi * np.outer(np.arange(n2), k1) / n)
    f1c = tw[:, :, None] * w1[None, :, :]
    f1t = np.concatenate([f1c.real, f1c.imag], axis=1).transpose(0, 2, 1)
    w2 = np.exp(-2j * np.pi * np.outer(np.arange(n2), np.arange(n2)) / n2)
    f2 = np.block([[w2.real, -w2.imag], [w2.imag, w2.real]])
    g2c = np.conj(w2)
    g2 = np.block([[g2c.real, -g2c.imag], [g2c.imag, g2c.real]])
    weight = np.where((k1 == 0) | (k1 == n1 // 2), 1.0, 2.0) * (k1 < FFT_K1)
    mc = np.conj(w1.T)[None, :n1 // 2, :] * np.conj(tw)[:, None, :] * weight / n
    g1 = np.concatenate([mc.real, -mc.imag], axis=2)
    return (jnp.asarray(f1t, BF16), jnp.asarray(f2, BF16), jnp.asarray(g2, BF16), jnp.asarray(g1, BF16))


def _dot_tn(a, b):
    return lax.dot_general(a, b, (((0,), (0,)), ((), ())), preferred_element_type=F32)


def _fft_stage1(x_ref, f1t_ref, a_ref, n1_in):
    def body(n2, carry):
        rows = x_ref[pl.ds(n2, n1_in, stride=FFT_N2), :]
        out = _dot_tn(f1t_ref[n2, 0:n1_in, :], _bf(rows))
        a_ref[0, pl.ds(n2, FFT_K1, stride=FFT_N2), :] = out[:FFT_K1]
        a_ref[1, pl.ds(n2, FFT_K1, stride=FFT_N2), :] = out[FFT_K1_PAD:FFT_K1_PAD + FFT_K1]
        return carry
    lax.fori_loop(0, FFT_N2, body, 0, unroll=FFT_UNROLL1)


def _lat_spectrum_kernel(filt_ref, ssq_ref, f1t_ref, f2_ref, h_ref, a_ref):
    n = pl.program_id(0)
    half = pl.program_id(1)
    norm_full = jnp.where(n == 0, _filter_norm(ssq_ref, 0), _filter_norm(ssq_ref, 1))
    norm = jnp.where(half == 0, norm_full[:, :HY_LANES], norm_full[:, HY_LANES:])
    _fft_stage1(filt_ref, f1t_ref, a_ref, FFT_N1)

    def stage2(k1, carry):
        r0 = pl.multiple_of(k1 * FFT_N2, FFT_N2)
        rows = pl.ds(r0, FFT_N2)
        a = jnp.concatenate([a_ref[0, rows, :], a_ref[1, rows, :]], axis=0)
        b = _dot(f2_ref[...], _bf(a)) * norm
        h_ref[0, 0, rows, :] = _bf(b[:FFT_N2])
        h_ref[0, 1, rows, :] = _bf(b[FFT_N2:])
        return carry
    lax.fori_loop(0, FFT_K1, stage2, 0)


def hyena_lat_spectrum(filt, ssq):
    f1t, f2, _, _ = _fft_constants()
    rows = FFT_K1 * FFT_N2
    halves = D_HY // HY_LANES
    return pl.pallas_call(
        _lat_spectrum_kernel,
        grid=(HY_ORDER, halves),
        in_specs=[
            pl.BlockSpec((FFT_N, HY_LANES), lambda o, h: (0, o * halves + h)),
            _const_spec((1, HY_HALF_COLS)),
            _const_spec((FFT_N2, FFT_N1, 2 * FFT_K1_PAD)),
            _const_spec((2 * FFT_N2, 2 * FFT_N2)),
        ],
        out_specs=pl.BlockSpec((1, 2, rows, HY_LANES), lambda o, h: (o, 0, 0, h)),
        out_shape=jax.ShapeDtypeStruct((HY_ORDER, 2, rows, D_HY), BF16),
        scratch_shapes=[pltpu.VMEM((2, FFT_ROWS, HY_LANES), F32)],
        compiler_params=_cparams(("parallel", "parallel")),
        name="hyena_lat_spectrum",
    )(filt, ssq, f1t, f2)


HY_ROWS = 512


def _hyena_lat_kernel(v_ref, x1_ref, x2_ref, cw_ref, cb_ref, bias_ref, spec_ref, f1t_ref, f2_ref, g2_ref, g1_ref,
                      o_ref, pad_ref, z_ref, a_ref):
    seq = v_ref.shape[1]
    zeros = jnp.zeros((HY_PAD, HY_LANES), F32)
    pad_ref[0:HY_PAD, :] = zeros
    pad_ref[HY_PAD + seq:2 * HY_PAD + seq, :] = zeros
    pad_ref[HY_PAD:HY_PAD + seq, :] = v_ref[0]
    for ti in range(seq // HY_ROWS):
        r0 = ti * HY_ROWS
        z_ref[r0:r0 + HY_ROWS, :] = _conv3(pad_ref, cw_ref.at[0], cb_ref.at[0], r0, HY_ROWS, HY_PAD)
    tail = jnp.zeros((FFT_ROWS - FFT_K1 * FFT_N2, HY_LANES), F32)
    a_ref[0, FFT_K1 * FFT_N2:FFT_ROWS, :] = tail
    a_ref[1, FFT_K1 * FFT_N2:FFT_ROWS, :] = tail

    for order, gate_ref in enumerate((x1_ref, x2_ref)):
        _fft_stage1(z_ref, f1t_ref, a_ref, FFT_N1 // 2)

        def stage2(k1, carry):
            r0 = pl.multiple_of(k1 * FFT_N2, FFT_N2)
            rows = pl.ds(r0, FFT_N2)
            a = jnp.concatenate([a_ref[0, rows, :], a_ref[1, rows, :]], axis=0)
            b = _dot(f2_ref[...], _bf(a))
            br, bi = b[:FFT_N2], b[FFT_N2:]
            hr = spec_ref[order, 0, rows, :].astype(F32)
            hi = spec_ref[order, 1, rows, :].astype(F32)
            y = jnp.concatenate([br * hr - bi * hi, br * hi + bi * hr], axis=0)
            c = _dot(g2_ref[...], _bf(y))
            a_ref[0, rows, :] = c[:FFT_N2]
            a_ref[1, rows, :] = c[FFT_N2:]
            return carry
        lax.fori_loop(0, FFT_K1, stage2, 0, unroll=FFT_UNROLL2)

        def stage1_inv(n2, carry):
            c = jnp.concatenate([a_ref[0, pl.ds(n2, FFT_K1_PAD, stride=FFT_N2), :],
                                 a_ref[1, pl.ds(n2, FFT_K1_PAD, stride=FFT_N2), :]], axis=0)
            o_ref[0, pl.ds(n2, FFT_N1 // 2, stride=FFT_N2), :] = _dot(g1_ref[n2], _bf(c))
            return carry
        lax.fori_loop(0, FFT_N2, stage1_inv, 0, unroll=FFT_UNROLL1)

        pad_ref[HY_PAD:HY_PAD + seq, :] = gate_ref[0]
        for ti in range(seq // HY_ROWS):
            r0 = ti * HY_ROWS
            rows = slice(r0, r0 + HY_ROWS)
            gate = _conv3(pad_ref, cw_ref.at[order + 1], cb_ref.at[order + 1], r0, HY_ROWS, HY_PAD)
            z = gate * (o_ref[0, rows, :] + bias_ref[order:order + 1, :] * z_ref[rows, :])
            if order + 1 < HY_ORDER:
                z_ref[rows, :] = z
            else:
                o_ref[0, rows, :] = z


def hyena_lat(p, conv_w, conv_b, bias, spec):
    b, seq, _ = p.shape
    assert 2 * seq == FFT_N
    f1t, f2, g2, g1 = _fft_constants()
    hw = HY_LANES
    col = lambda c: pl.BlockSpec((1, seq, hw), lambda i, h: (i, 0, (COL_HY + c * D_HY) // hw + h))
    return pl.pallas_call(
        _hyena_lat_kernel,
        grid=(b, D_HY // hw),
        in_specs=[
            col(0), col(1), col(2),
            pl.BlockSpec((3, HY_SHORT, hw), lambda i, h: (0, 0, h)),
            pl.BlockSpec((3, 1, hw), lambda i, h: (0, 0, h)),
            pl.BlockSpec((HY_ORDER, hw), lambda i, h: (0, h)),
            pl.BlockSpec((HY_ORDER, 2, FFT_K1 * FFT_N2, hw), lambda i, h: (0, 0, 0, h)),
            _const_spec((FFT_N2, FFT_N1, 2 * FFT_K1_PAD)),
            _const_spec((2 * FFT_N2, 2 * FFT_N2)),
            _const_spec((2 * FFT_N2, 2 * FFT_N2)),
            _const_spec((FFT_N2, FFT_N1 // 2, 2 * FFT_K1_PAD)),
        ],
        out_specs=pl.BlockSpec((1, seq, hw), lambda i, h: (i, 0, h)),
        out_shape=jax.ShapeDtypeStruct((b, seq, D_HY), F32),
        scratch_shapes=[
            pltpu.VMEM((seq + 2 * HY_PAD, hw), F32),
            pltpu.VMEM((seq, hw), F32),
            pltpu.VMEM((2, FFT_ROWS, hw), F32),
        ],
        compiler_params=_cparams(("parallel", "parallel")),
        name="hyena_lat",
    )(p, p, p, conv_w.reshape(HY_SHORT, 3, D_HY).transpose(1, 0, 2), conv_b.reshape(3, 1, D_HY), bias, spec,
      f1t, f2, g2, g1)


HG_SEG_ROWS = 512
COND_ROWS = 16


def _trunk_layer(x, mods, lp, lb, batch, seq, ctx, final):
    x = ffn_half_step(x, mods, lp['ln_ffn1'], lp['w_ffn1_in'], lp['w_ffn1_out'], lp['final_norm'],
                      mod_base=0, final=False, seq_len=seq)
    p2d = mixer_in_proj(x, mods, lp['ln_mix'], lp['w_in'], seq_len=seq)
    p = p2d.reshape(batch, seq, N_IN)

    filt, ssq = hyena_filter_gen(seq, lp['hy_w1'], lp['hy_b1'], lp['hy_w2'], lp['hy_b2'], lp['hy_w3'],
                                 lp['hy_freq'])
    if ctx is None:
        att = context_attention(p, lp['attn_sink'])
        lru_h0 = jnp.zeros((batch, 2, D_LRU), F32)
        hg_s0 = jnp.zeros((batch, 2, HG_HEADS, HG_DK, HG_DV), F32)
        hy = hyena_ctx(p, lp['hy_conv_w'], lp['hy_conv_b'], lp['hy_bias'], hyena_ctx_spectrum(filt, ssq))
    else:
        k_ctx, v_ctx, lru_h0, hg_state = ctx
        att = latent_attention(p, k_ctx, v_ctx, lp['attn_sink'])
        hg_s0 = hg_state.astype(F32)
        hy = hyena_lat(p, lp['hy_conv_w'], lp['hy_conv_b'], lp['hy_bias'], hyena_lat_spectrum(filt, ssq))
    lru, lru_state = rglru_mixer(p, lp['lru_conv_w'], lp['lru_conv_b'], lp['lru_w_r'], lp['lru_b_r'],
                                 lp['lru_w_i'], lp['lru_b_i'], lp['lru_lambda'], lru_h0.astype(F32))
    o_dirs, hg_state_out = hgrn2_scan(p, lb, hg_s0, min(seq, HG_SEG_ROWS))
    t = batch * seq

    x = gated_merge(x, mods, lp['ln_mix'], att.reshape(t, D_ATTN), lru.reshape(t, D_LRU),
                    o_dirs.reshape(2, t, D_HG), p2d, lp['hg_norm'], hy.reshape(t, D_HY), lp['w_gate'],
                    lp['w_bo_attn'], lp['w_bo_lru'], lp['w_bo_hg'], lp['w_bo_hy'], lp['w_out'], seq_len=seq)
    x = ffn_half_step(x, mods, lp['ln_ffn2'], lp['w_ffn2_in'], lp['w_ffn2_out'], lp['final_norm'],
                      mod_base=6, final=final, seq_len=seq)
    state = None
    if ctx is None:
        k = p[:, :, COL_K:COL_K + D_KV].reshape(batch, seq, N_KV_HEADS, HEAD_DIM)
        v = p[:, :, COL_V:COL_V + D_KV].reshape(batch, seq, N_KV_HEADS, HEAD_DIM)
        state = (k, v, lru_state, hg_state_out)
    return x, state


def kernel(x_prompt, x_sample, cache_k, cache_v, state_lru, state_hgrn, c, c_ctx, ln_ffn1, ln_mix, ln_ffn2, w_ada, b_ada, w_ffn1_in, w_ffn1_out, w_ffn2_in, w_ffn2_out, w_in, attn_sink, lru_conv_w, lru_conv_b, lru_w_r, lru_b_r, lru_w_i, lru_b_i, lru_lambda, hg_lb_logits, hg_norm, hy_conv_w, hy_conv_b, hy_w1, hy_b1, hy_w2, hy_b2, hy_w3, hy_freq, hy_bias, w_bo_attn, w_bo_lru, w_bo_hg, w_bo_hy, w_gate, w_out, final_norm):
    batch, seq, _ = x_prompt.shape
    dec_batch, dec_seq, _ = x_sample.shape
    lb_soft = jax.nn.softmax(hg_lb_logits.astype(F32), axis=0)
    lb_all = jnp.cumsum(lb_soft, axis=0) - lb_soft[0]

    stacked = dict(ln_ffn1=ln_ffn1, ln_mix=ln_mix, ln_ffn2=ln_ffn2, w_ffn1_in=w_ffn1_in, w_ffn1_out=w_ffn1_out,
                   w_ffn2_in=w_ffn2_in, w_ffn2_out=w_ffn2_out, w_in=w_in, attn_sink=attn_sink,
                   lru_conv_w=lru_conv_w, lru_conv_b=lru_conv_b, lru_w_r=lru_w_r, lru_b_r=lru_b_r,
                   lru_w_i=lru_w_i, lru_b_i=lru_b_i, lru_lambda=lru_lambda, hg_norm=hg_norm,
                   hy_conv_w=hy_conv_w, hy_conv_b=hy_conv_b, hy_w1=hy_w1, hy_b1=hy_b1, hy_w2=hy_w2, hy_b2=hy_b2,
                   hy_w3=hy_w3, hy_freq=hy_freq, hy_bias=hy_bias, w_bo_attn=w_bo_attn, w_bo_lru=w_bo_lru,
                   w_bo_hg=w_bo_hg, w_bo_hy=w_bo_hy, w_gate=w_gate, w_out=w_out)

    cond = jnp.zeros((COND_ROWS, D_MODEL), F32).at[0].set(c_ctx).at[1:1 + dec_batch].set(c)
    mods = ada_mods(cond, w_ada, b_ada).reshape(DEPTH, COND_ROWS, N_MOD, D_MODEL)

    h = x_prompt.reshape(batch * seq, D_MODEL)
    z = x_sample.reshape(dec_batch * dec_seq, D_MODEL)
    ks, vs, lrus, hgs = [], [], [], []
    for l in range(DEPTH):
        lp = {name: w[l] for name, w in stacked.items()}
        lp['final_norm'] = final_norm
        final = l == DEPTH - 1
        h, (k_l, v_l, lru_l, hg_l) = _trunk_layer(h, mods[l, 0:1], lp, lb_all[l], batch, seq, None, final)
        ks.append(k_l)
        vs.append(v_l)
        lrus.append(lru_l)
        hgs.append(hg_l)
        ctx = (cache_k[:, l].reshape(dec_batch, -1, D_KV), cache_v[:, l].reshape(dec_batch, -1, D_KV),
               state_lru[:, l], state_hgrn[:, l])
        z, _ = _trunk_layer(z, mods[l, 1:1 + dec_batch], lp, lb_all[l], dec_batch, dec_seq, ctx, final)

    y_prompt = h.reshape(batch, seq, D_MODEL)
    y_sample = z.reshape(dec_batch, dec_seq, D_MODEL)
    return (y_prompt, y_sample, jnp.stack(ks, axis=1), jnp.stack(vs, axis=1), jnp.stack(lrus, axis=1),
            jnp.stack(hgs, axis=1))
```

```python
import functools
import math

import numpy as np
import jax
import jax.numpy as jnp
from jax import lax
from jax.experimental import pallas as pl
from jax.experimental.pallas import tpu as pltpu

F32 = jnp.float32
BF16 = jnp.bfloat16

D_MODEL = 1024
DEPTH = 2
GRID_W = 64
EPS = 1e-6
N_MOD = 9
N_BRANCH = 4
D_FF = 2816
N_HEADS = 8
N_KV_HEADS = 2
HEAD_DIM = 64
GQA_GROUP = N_HEADS // N_KV_HEADS
D_ATTN = N_HEADS * HEAD_DIM
D_KV = N_KV_HEADS * HEAD_DIM
WINDOW = 128
ATTN_BLOCK = 128
ATTN_SCALE = HEAD_DIM ** -0.5
ROPE_BASE = 10000.0
D_LRU = D_MODEL // 4
LRU_HEADS = 4
LRU_BLOCK = D_LRU // LRU_HEADS
LRU_CONV = 4
LRU_C = 8.0
HG_HEADS = 4
HG_DK = 64
HG_DV = 64
D_HG = HG_HEADS * HG_DK
HG_SCALE = HG_DK ** -0.5
HG_BLK = 16
D_HY = D_MODEL // 4
HY_ORDER = 2
HY_SHORT = 3
HY_BANDS = 8
HY_EMB = 2 * HY_BANDS + 1
HY_FFN = 64
HY_TARGET = 1e-2
HY_DECAY_SLOW = -math.log(HY_TARGET) / 1.5
HY_DECAY_FAST = -math.log(HY_TARGET) / 0.3
N_IN = D_ATTN + 2 * D_KV + 2 * D_LRU + 5 * D_HG + 3 * D_HY

COL_Q = 0
COL_K = D_ATTN
COL_V = D_ATTN + D_KV
COL_RX = D_ATTN + 2 * D_KV
COL_RG = COL_RX + D_LRU
COL_HQ = COL_RG + D_LRU
COL_HFF = COL_HQ + D_HG
COL_HFB = COL_HFF + D_HG
COL_HI = COL_HFB + D_HG
COL_HGATE = COL_HI + D_HG
COL_HY = COL_HGATE + D_HG

FF_CHUNK = 256
N_FF_CHUNK = D_FF // FF_CHUNK
ROW_TILE = 512
VMEM_LIMIT = 56 * 1024 * 1024


def _bf(x):
    return x.astype(BF16)


def _dot(a, b):
    return jnp.dot(a, b, preferred_element_type=F32)


def _split3(x):
    hi = _bf(x)
    r1 = x - hi.astype(F32)
    mid = _bf(r1)
    lo = _bf(r1 - mid.astype(F32))
    return hi, mid, lo


def _dot_exact_rhs(mat_bf, x):
    hi, mid, lo = _split3(x)
    return _dot(mat_bf, hi) + _dot(mat_bf, mid) + _dot(mat_bf, lo)


def _sigmoid(x):
    return jax.nn.sigmoid(x)


def _silu(x):
    return x * _sigmoid(x)


def _norm_mod(x, lnw, shift, scale):
    ms = jnp.mean(x * x, axis=-1, keepdims=True)
    y = x * lax.rsqrt(ms + EPS) * lnw
    return y * (1.0 + scale) + shift


def _cparams(sem):
    return pltpu.CompilerParams(dimension_semantics=sem, vmem_limit_bytes=VMEM_LIMIT)


def _const_spec(shape):
    nd = len(shape)
    return pl.BlockSpec(shape, lambda *_: (0,) * nd)


def _ada_kernel(cond_ref, w_ref, b_ref, o_ref):
    c = cond_ref[...]
    o_ref[0] = _dot(_bf(_silu(c)), _bf(w_ref[0])) + b_ref[0]


def ada_mods(cond, w_ada, b_ada):
    r = cond.shape[0]
    cb = 1024
    ncol = (N_MOD * D_MODEL) // cb
    return pl.pallas_call(
        _ada_kernel,
        grid=(DEPTH, ncol),
        in_specs=[
            pl.BlockSpec((r, D_MODEL), lambda l, j: (0, 0)),
            pl.BlockSpec((1, D_MODEL, cb), lambda l, j: (l, 0, j)),
            pl.BlockSpec((1, 1, cb), lambda l, j: (l, 0, j)),
        ],
        out_specs=pl.BlockSpec((1, r, cb), lambda l, j: (l, 0, j)),
        out_shape=jax.ShapeDtypeStruct((DEPTH, r, N_MOD * D_MODEL), F32),
        compiler_params=_cparams(("parallel", "parallel")),
        name="ada_mods",
    )(cond, w_ada, b_ada.reshape(DEPTH, 1, N_MOD * D_MODEL))


def _seq_mod_index(seq_len, n_mod, tm):
    def idx(i):
        return jnp.minimum((i * tm) // seq_len, n_mod - 1)
    return idx


def _ffn_kernel(x_ref, mod_ref, ln_ref, wg_ref, wu_ref, wo_ref, fn_ref, o_ref, *, mod_base, final):
    x = x_ref[...]
    m = mod_ref[0]
    sh = m[mod_base:mod_base + 1]
    sc = m[mod_base + 1:mod_base + 2]
    g = m[mod_base + 2:mod_base + 3]
    h = _bf(_norm_mod(x, ln_ref[...], sh, sc))
    acc = None
    for c in range(N_FF_CHUNK):
        gate = _dot(h, wg_ref[c])
        up = _dot(h, wu_ref[c])
        part = _dot(_bf(_silu(gate) * up), wo_ref[c])
        acc = part if acc is None else acc + part
    y = x + 0.5 * g * acc
    if final:
        ms = jnp.mean(y * y, axis=-1, keepdims=True)
        y = y * lax.rsqrt(ms + EPS) * fn_ref[...]
    o_ref[...] = y


def ffn_half_step(x, mods, ln_w, w_in, w_out, final_w, *, mod_base, final, seq_len):
    t = x.shape[0]
    tm = ROW_TILE
    wg = _bf(w_in[:, :D_FF]).reshape(D_MODEL, N_FF_CHUNK, FF_CHUNK).transpose(1, 0, 2)
    wu = _bf(w_in[:, D_FF:]).reshape(D_MODEL, N_FF_CHUNK, FF_CHUNK).transpose(1, 0, 2)
    wo = _bf(w_out).reshape(N_FF_CHUNK, FF_CHUNK, D_MODEL)
    midx = _seq_mod_index(seq_len, mods.shape[0], tm)
    return pl.pallas_call(
        functools.partial(_ffn_kernel, mod_base=mod_base, final=final),
        grid=(t // tm,),
        in_specs=[
            pl.BlockSpec((tm, D_MODEL), lambda i: (i, 0)),
            pl.BlockSpec((1, N_MOD, D_MODEL), lambda i: (midx(i), 0, 0)),
            _const_spec((1, D_MODEL)),
            _const_spec((N_FF_CHUNK, D_MODEL, FF_CHUNK)),
            _const_spec((N_FF_CHUNK, D_MODEL, FF_CHUNK)),
            _const_spec((N_FF_CHUNK, FF_CHUNK, D_MODEL)),
            _const_spec((1, D_MODEL)),
        ],
        out_specs=pl.BlockSpec((tm, D_MODEL), lambda i: (i, 0)),
        out_shape=jax.ShapeDtypeStruct((t, D_MODEL), F32),
        compiler_params=_cparams(("parallel",)),
        name="ffn_half_step",
    )(x, mods, ln_w.reshape(1, D_MODEL), wg, wu, wo, final_w.reshape(1, D_MODEL))


PROJ_CHUNK = 256


def _proj_kernel(x_ref, mod_ref, ln_ref, w_ref, o_ref):
    x = x_ref[...]
    m = mod_ref[0]
    u = _bf(_norm_mod(x, ln_ref[...], m[3:4], m[4:5]))
    for c in range(N_IN // PROJ_CHUNK):
        sl = slice(c * PROJ_CHUNK, (c + 1) * PROJ_CHUNK)
        o_ref[:, sl] = _dot(u, w_ref[:, sl])


def mixer_in_proj(x, mods, ln_w, w_in, *, seq_len):
    t = x.shape[0]
    tm = ROW_TILE
    midx = _seq_mod_index(seq_len, mods.shape[0], tm)
    return pl.pallas_call(
        _proj_kernel,
        grid=(t // tm,),
        in_specs=[
            pl.BlockSpec((tm, D_MODEL), lambda i: (i, 0)),
            pl.BlockSpec((1, N_MOD, D_MODEL), lambda i: (midx(i), 0, 0)),
            _const_spec((1, D_MODEL)),
            _const_spec((D_MODEL, N_IN)),
        ],
        out_specs=pl.BlockSpec((tm, N_IN), lambda i: (i, 0)),
        out_shape=jax.ShapeDtypeStruct((t, N_IN), F32),
        compiler_params=_cparams(("parallel",)),
        name="mixer_in_proj",
    )(x, mods, ln_w.reshape(1, D_MODEL), _bf(w_in))


MERGE_CHUNK = 256


def _merge_kernel(x_ref, mod_ref, ln_ref, att_ref, lru_ref, hgf_ref, hgb_ref, hgg_ref, hgn_ref, headb_ref, hy_ref,
                  wg_ref, wa_ref, wl_ref, wh_ref, wy_ref, wo_ref, o_ref, mix_ref):
    x = x_ref[...]
    m = mod_ref[0]
    u = _bf(_norm_mod(x, ln_ref[...], m[3:4], m[4:5]))
    o = hgf_ref[0] + hgb_ref[0]
    sq = o * o
    sq_hi = _bf(sq)
    sq_lo = _bf(sq - sq_hi.astype(F32))
    ms = (_dot(sq_hi, headb_ref[...]) + _dot(sq_lo, headb_ref[...])) * (1.0 / HG_DV)
    hg = o * lax.rsqrt(ms + EPS) * hgn_ref[...] * _silu(hgg_ref[...])
    branches = (_bf(att_ref[...]), _bf(lru_ref[...]), _bf(hg), _bf(hy_ref[...]))
    w_bo = (wa_ref, wl_ref, wh_ref, wy_ref)
    for c in range(D_MODEL // MERGE_CHUNK):
        sl = slice(c * MERGE_CHUNK, (c + 1) * MERGE_CHUNK)
        mixed = None
        for n in range(N_BRANCH):
            term = _sigmoid(_dot(u, wg_ref[n, :, sl])) * _dot(branches[n], w_bo[n][:, sl])
            mixed = term if mixed is None else mixed + term
        mix_ref[:, sl] = _bf(mixed)
    o_ref[...] = x + m[5:6] * _dot(mix_ref[...], wo_ref[...])


def gated_merge(x, mods, ln_w, att, lru, hg_dirs, p2d, hg_norm, hy, w_gate, w_bo_attn, w_bo_lru, w_bo_hg, w_bo_hy,
                w_out, *, seq_len):
    t = x.shape[0]
    tm = ROW_TILE
    midx = _seq_mod_index(seq_len, mods.shape[0], tm)
    row = lambda w: pl.BlockSpec((tm, w), lambda i: (i, 0))
    _, _, headb = _hg_constants()
    return pl.pallas_call(
        _merge_kernel,
        grid=(t // tm,),
        in_specs=[
            row(D_MODEL),
            pl.BlockSpec((1, N_MOD, D_MODEL), lambda i: (midx(i), 0, 0)),
            _const_spec((1, D_MODEL)),
            row(D_ATTN), row(D_LRU),
            pl.BlockSpec((1, tm, D_HG), lambda i: (0, i, 0)),
            pl.BlockSpec((1, tm, D_HG), lambda i: (1, i, 0)),
            pl.BlockSpec((tm, D_HG), lambda i: (i, COL_HGATE // D_HG)),
            _const_spec((1, D_HG)),
            _const_spec((D_HG, D_HG)),
            row(D_HY),
            _const_spec((N_BRANCH, D_MODEL, D_MODEL)),
            _const_spec((D_ATTN, D_MODEL)),
            _const_spec((D_LRU, D_MODEL)),
            _const_spec((D_HG, D_MODEL)),
            _const_spec((D_HY, D_MODEL)),
            _const_spec((D_MODEL, D_MODEL)),
        ],
        out_specs=row(D_MODEL),
        out_shape=jax.ShapeDtypeStruct((t, D_MODEL), F32),
        scratch_shapes=[pltpu.VMEM((tm, D_MODEL), BF16)],
        compiler_params=_cparams(("parallel",)),
        name="gated_merge",
    )(x, mods, ln_w.reshape(1, D_MODEL), att, lru, hg_dirs, hg_dirs, p2d, hg_norm.reshape(1, D_HG), headb, hy,
      _bf(w_gate), _bf(w_bo_attn), _bf(w_bo_lru), _bf(w_bo_hg), _bf(w_bo_hy), _bf(w_out))


def _dot_nt(a, b):
    return lax.dot_general(a, b, (((1,), (1,)), ((), ())), preferred_element_type=F32)


def _ctx_attn_kernel(q_ref, k_ref, v_ref, sink_ref, o_ref):
    rows = q_ref.shape[1]
    blk = ATTN_BLOCK
    keys = _dup_heads(k_ref[0])
    vals_t = _values_t(v_ref[0])
    low = lax.broadcasted_iota(jnp.int32, (blk, 2 * HEAD_DIM), 1) < HEAD_DIM
    units = [(pair, qb) for qb in range(rows // blk) for pair in range(N_HEADS // 2)]

    def where(pair, qb):
        return slice(qb * blk, (qb + 1) * blk), slice(pair * 2 * HEAD_DIM, (pair + 1) * 2 * HEAD_DIM)

    scores = []
    for pair, qb in units:
        q_rows, lanes = where(pair, qb)
        q = q_ref[0, q_rows, lanes] * ATTN_SCALE
        q_heads = _bf(jnp.concatenate([jnp.where(low, q, 0.0), jnp.where(low, 0.0, q)], axis=0))
        scores.append(_dot_nt(keys[(2 * pair) // GQA_GROUP], q_heads))
    probs = []
    for (pair, qb), s in zip(units, scores):
        sink = jnp.concatenate([sink_ref[2 * pair:2 * pair + 1, :], sink_ref[2 * pair + 1:2 * pair + 2, :]], axis=1)
        m = jnp.maximum(jnp.max(s, axis=0, keepdims=True), sink)
        probs.append((_bf(jnp.exp(s - m)), jnp.exp(sink - m)))
    for (pair, qb), (p, sink_term) in zip(units, probs):
        q_rows, lanes = where(pair, qb)
        acc = _dot(vals_t[(2 * pair) // GQA_GROUP], p)
        out_t = acc[:HEAD_DIM, :] / (acc[HEAD_DIM:HEAD_DIM + 1, :] + sink_term)
        o_ref[0, q_rows, lanes] = jnp.transpose(jnp.concatenate([out_t[:, :blk], out_t[:, blk:]], axis=0))


def _sink_table(sink):
    return jnp.broadcast_to(sink.astype(F32)[:, None], (N_HEADS, 128))


def context_attention(p_ctx, sink):
    b, seq, _ = p_ctx.shape
    return pl.pallas_call(
        _ctx_attn_kernel,
        grid=(b,),
        in_specs=[
            pl.BlockSpec((1, seq, D_ATTN), lambda i: (i, 0, 0)),
            pl.BlockSpec((1, seq, D_KV), lambda i: (i, 0, COL_K // D_KV)),
            pl.BlockSpec((1, seq, D_KV), lambda i: (i, 0, COL_V // D_KV)),
            _const_spec((N_HEADS, 128)),
        ],
        out_specs=pl.BlockSpec((1, seq, D_ATTN), lambda i: (i, 0, 0)),
        out_shape=jax.ShapeDtypeStruct((b, seq, D_ATTN), F32),
        compiler_params=_cparams(("parallel",)),
        name="context_attention",
    )(p_ctx, p_ctx, p_ctx, _sink_table(sink))


def _rope(x, cos, sin_signed):
    n = x.shape[-1]
    lane = lax.broadcasted_iota(jnp.int32, x.shape, x.ndim - 1)
    first = (lane % (HEAD_DIM // 2)) < (HEAD_DIM // 4)
    partner = jnp.where(first, pltpu.roll(x, n - HEAD_DIM // 4, axis=x.ndim - 1),
                        pltpu.roll(x, HEAD_DIM // 4, axis=x.ndim - 1))
    return x * cos + partner * sin_signed


def _rope_tables(seq):
    rows = seq // GRID_W
    row = np.repeat(np.arange(rows, dtype=np.float32), GRID_W)
    col = np.tile(np.arange(GRID_W, dtype=np.float32), rows)
    n = HEAD_DIM // 4
    inv = (ROPE_BASE ** (-jnp.arange(n, dtype=F32) / n))
    ang_r = jnp.asarray(row)[:, None] * inv
    ang_c = jnp.asarray(col)[:, None] * inv
    cos = jnp.concatenate([jnp.cos(ang_r), jnp.cos(ang_r), jnp.cos(ang_c), jnp.cos(ang_c)], axis=-1)
    sin = jnp.concatenate([-jnp.sin(ang_r), jnp.sin(ang_r), -jnp.sin(ang_c), jnp.sin(ang_c)], axis=-1)
    return jnp.tile(cos, (1, 2)), jnp.tile(sin, (1, 2))


ATTN_PREP_ROWS = 512
ATTN_QBLOCKS = 4


def _dup_heads(x):
    lane = lax.broadcasted_iota(jnp.int32, x.shape, 1)
    swapped = pltpu.roll(x, HEAD_DIM, axis=1)
    low = lane < HEAD_DIM
    return _bf(jnp.where(low, x, swapped)), _bf(jnp.where(low, swapped, x))


def _values_t(x):
    xt = jnp.transpose(x)
    ones = jnp.ones((HEAD_DIM, x.shape[0]), F32)
    return (_bf(jnp.concatenate([xt[:HEAD_DIM], ones], axis=0)),
            _bf(jnp.concatenate([xt[HEAD_DIM:], ones], axis=0)))


def _lat_attn_kernel(q_ref, k_ref, v_ref, kc_ref, vc_ref, cos_ref, sin_ref, sink_ref, o_ref,
                     kd_ref, vt_ref, kcd_ref, vct_ref):
    seq = k_ref.shape[1]
    blk = ATTN_BLOCK
    band = 3 * blk
    past = kc_ref.shape[1]
    i = pl.program_id(1)

    @pl.when(i == 0)
    def _():
        for ti in range(seq // ATTN_PREP_ROWS):
            rows = slice(ti * ATTN_PREP_ROWS, (ti + 1) * ATTN_PREP_ROWS)
            k0, k1 = _dup_heads(_rope(k_ref[0, rows, :], cos_ref[rows, :], sin_ref[rows, :]))
            kd_ref[0, rows, :] = k0
            kd_ref[1, rows, :] = k1
            v0, v1 = _values_t(v_ref[0, rows, :])
            vt_ref[0, :, rows] = v0
            vt_ref[1, :, rows] = v1
        kc0, kc1 = _dup_heads(kc_ref[0])
        kcd_ref[0] = kc0
        kcd_ref[1] = kc1
        vc0, vc1 = _values_t(vc_ref[0])
        vct_ref[0] = vc0
        vct_ref[1] = vc1

    key = lax.broadcasted_iota(jnp.int32, (band + past, 2 * blk), 0)
    qry = lax.broadcasted_iota(jnp.int32, (band + past, 2 * blk), 1) % blk
    low = lax.broadcasted_iota(jnp.int32, (blk, 2 * HEAD_DIM), 1) < HEAD_DIM
    subs = []
    for sub in range(ATTN_QBLOCKS):
        ib = ATTN_QBLOCKS * i + sub
        start = pl.multiple_of(jnp.clip((ib - 1) * blk, 0, seq - band), blk)
        q0 = pl.multiple_of(ib * blk, blk)
        subs.append(dict(
            rows=slice(sub * blk, (sub + 1) * blk),
            cos=cos_ref[pl.ds(q0, blk), :], sin=sin_ref[pl.ds(q0, blk), :],
            mask=(jnp.abs((q0 + qry) - (start + key)) <= WINDOW) | (key >= band),
            keys=[jnp.concatenate([kd_ref[kvh, pl.ds(start, band), :], kcd_ref[kvh]], axis=0)
                  for kvh in range(N_KV_HEADS)],
            vals_t=[jnp.concatenate([vt_ref[kvh, :, pl.ds(start, band)], vct_ref[kvh]], axis=1)
                    for kvh in range(N_KV_HEADS)]))
    units = [(sub, pair) for sub in range(ATTN_QBLOCKS) for pair in range(N_HEADS // 2)]
    scores = []
    for sub, pair in units:
        u = subs[sub]
        lanes = slice(pair * 2 * HEAD_DIM, (pair + 1) * 2 * HEAD_DIM)
        q = _rope(q_ref[0, u['rows'], lanes], u['cos'], u['sin']) * ATTN_SCALE
        q_heads = _bf(jnp.concatenate([jnp.where(low, q, 0.0), jnp.where(low, 0.0, q)], axis=0))
        scores.append(_dot_nt(u['keys'][(2 * pair) // GQA_GROUP], q_heads))
    probs = []
    for (sub, pair), raw in zip(units, scores):
        s = jnp.where(subs[sub]['mask'], raw, -jnp.inf)
        sink = jnp.concatenate([sink_ref[2 * pair:2 * pair + 1, :], sink_ref[2 * pair + 1:2 * pair + 2, :]], axis=1)
        m = jnp.maximum(jnp.max(s, axis=0, keepdims=True), sink)
        probs.append((_bf(jnp.exp(s - m)), jnp.exp(sink - m)))
    for (sub, pair), (p, sink_term) in zip(units, probs):
        u = subs[sub]
        lanes = slice(pair * 2 * HEAD_DIM, (pair + 1) * 2 * HEAD_DIM)
        acc = _dot(u['vals_t'][(2 * pair) // GQA_GROUP], p)
        out_t = acc[:HEAD_DIM, :] / (acc[HEAD_DIM:HEAD_DIM + 1, :] + sink_term)
        o_ref[0, u['rows'], lanes] = jnp.transpose(jnp.concatenate([out_t[:, :blk], out_t[:, blk:]], axis=0))


def latent_attention(p_lat, k_ctx, v_ctx, sink):
    b, seq, _ = p_lat.shape
    past = k_ctx.shape[1]
    cos, sin = _rope_tables(seq)
    return pl.pallas_call(
        _lat_attn_kernel,
        grid=(b, seq // (ATTN_QBLOCKS * ATTN_BLOCK)),
        in_specs=[
            pl.BlockSpec((1, ATTN_QBLOCKS * ATTN_BLOCK, D_ATTN), lambda bi, i: (bi, i, 0)),
            pl.BlockSpec((1, seq, D_KV), lambda bi, i: (bi, 0, COL_K // D_KV)),
            pl.BlockSpec((1, seq, D_KV), lambda bi, i: (bi, 0, COL_V // D_KV)),
            pl.BlockSpec((1, past, D_KV), lambda bi, i: (bi, 0, 0)),
            pl.BlockSpec((1, past, D_KV), lambda bi, i: (bi, 0, 0)),
            _const_spec((seq, 128)),
            _const_spec((seq, 128)),
            _const_spec((N_HEADS, 128)),
        ],
        out_specs=pl.BlockSpec((1, ATTN_QBLOCKS * ATTN_BLOCK, D_ATTN), lambda bi, i: (bi, i, 0)),
        out_shape=jax.ShapeDtypeStruct((b, seq, D_ATTN), F32),
        scratch_shapes=[
            pltpu.VMEM((N_KV_HEADS, seq, D_KV), BF16),
            pltpu.VMEM((N_KV_HEADS, D_KV, seq), BF16),
            pltpu.VMEM((N_KV_HEADS, past, D_KV), BF16),
            pltpu.VMEM((N_KV_HEADS, D_KV, past), BF16),
        ],
        compiler_params=_cparams(("parallel", "arbitrary")),
        name="latent_attention",
    )(p_lat, p_lat, p_lat, k_ctx, v_ctx, cos, sin, _sink_table(sink))


LRU_HALF = D_LRU // 2
LRU_ROWS = 256
LRU_PAD = 8


def _neg_expm1(y, exp_y):
    return jnp.tanh(-0.5 * y) * (exp_y + 1.0)


def _softplus(x):
    return jnp.maximum(x, 0.0) + jnp.log1p(jnp.exp(-jnp.abs(x)))


def _gelu_tanh(x):
    return 0.5 * x * (1.0 + jnp.tanh(math.sqrt(2.0 / math.pi) * (x + 0.044715 * (x * x * x))))


LRU_SCAN = 8


def _affine_scan8(a, b, *, reverse):
    row = lax.broadcasted_iota(jnp.int32, a.shape, 0)
    k = 1
    while k < LRU_SCAN:
        if reverse:
            valid = row < LRU_SCAN - k
            shift = LRU_SCAN - k
        else:
            valid = row >= k
            shift = k
        a_prev = jnp.where(valid, pltpu.roll(a, shift, axis=0), 1.0)
        b_prev = jnp.where(valid, pltpu.roll(b, shift, axis=0), 0.0)
        b = b + a * b_prev
        a = a * a_prev
        k *= 2
    return a, b


def _lru_kernel(x_ref, g_ref, cw_ref, cb_ref, wr_ref, br_ref, wi_ref, bi_ref, lam_ref, h0_ref,
                o_ref, st_ref, pad_ref, a_ref, bx_ref, hb_ref):
    seq = x_ref.shape[1]
    zeros = jnp.zeros((LRU_PAD, LRU_HALF), F32)
    pad_ref[0:LRU_PAD, :] = zeros
    pad_ref[LRU_PAD + seq:2 * LRU_PAD + seq, :] = zeros
    pad_ref[LRU_PAD:LRU_PAD + seq, :] = x_ref[0]
    left = LRU_CONV // 2
    for ti in range(seq // LRU_ROWS):
        r0 = ti * LRU_ROWS
        xc = cb_ref[...]
        for k in range(LRU_CONV):
            xc = xc + cw_ref[k:k + 1, :] * pad_ref[r0 + LRU_PAD - left + k:r0 + LRU_PAD - left + k + LRU_ROWS, :]
        xcb = _bf(xc)
        for d in range(2):
            r = _sigmoid(_dot(xcb, wr_ref[d]) + br_ref[d:d + 1, :])
            i = _sigmoid(_dot(xcb, wi_ref[d]) + bi_ref[d:d + 1, :])
            log_a = (-LRU_C) * r * _softplus(-lam_ref[d:d + 1, :])
            a = jnp.exp(log_a)
            a_ref[d, r0:r0 + LRU_ROWS, :] = a
            bx_ref[d, r0:r0 + LRU_ROWS, :] = jnp.sqrt(_neg_expm1(2.0 * log_a, a * a)) * (i * xc)

    nt = seq // LRU_SCAN

    def step(blk, carry):
        hf, hb = carry
        rf = pl.ds(pl.multiple_of(blk * LRU_SCAN, LRU_SCAN), LRU_SCAN)
        rb = pl.ds(pl.multiple_of((nt - 1 - blk) * LRU_SCAN, LRU_SCAN), LRU_SCAN)
        af, bf_ = _affine_scan8(a_ref[0, rf, :], bx_ref[0, rf, :], reverse=False)
        ab, bb = _affine_scan8(a_ref[1, rb, :], bx_ref[1, rb, :], reverse=True)
        tile_f = af * hf + bf_
        tile_b = ab * hb + bb
        o_ref[0, rf, :] = tile_f
        hb_ref[rb, :] = tile_b
        return tile_f[LRU_SCAN - 1:LRU_SCAN, :], tile_b[0:1, :]

    hf, hb = lax.fori_loop(0, nt, step, (h0_ref[0, 0:1, :], h0_ref[0, 1:2, :]), unroll=2)
    st_ref[0, 0:1, :] = hf
    st_ref[0, 1:2, :] = hb
    for ti in range(seq // LRU_ROWS):
        rows = slice(ti * LRU_ROWS, (ti + 1) * LRU_ROWS)
        o_ref[0, rows, :] = (o_ref[0, rows, :] + hb_ref[rows, :]) * _gelu_tanh(g_ref[0, rows, :])


def _block_diag(w):
    n, blk, _ = w.shape
    eye = jnp.eye(n, dtype=w.dtype)
    return (eye[:, None, :, None] * w[:, :, None, :]).reshape(n * blk, n * blk)


def rglru_mixer(p, conv_w, conv_b, w_r, b_r, w_i, b_i, lam, h0):
    b, seq, _ = p.shape
    wr = _bf(jnp.stack([_block_diag(w_r[d]) for d in range(2)]))
    wi = _bf(jnp.stack([_block_diag(w_i[d]) for d in range(2)]))
    hw = LRU_HALF
    vec = lambda rows: pl.BlockSpec((rows, hw), lambda bi, h: (0, h))
    return pl.pallas_call(
        _lru_kernel,
        grid=(b, 2),
        in_specs=[
            pl.BlockSpec((1, seq, hw), lambda bi, h: (bi, 0, COL_RX // hw + h)),
            pl.BlockSpec((1, seq, hw), lambda bi, h: (bi, 0, COL_RG // hw + h)),
            vec(LRU_CONV), vec(1),
            pl.BlockSpec((2, hw, hw), lambda bi, h: (0, h, h)), vec(2),
            pl.BlockSpec((2, hw, hw), lambda bi, h: (0, h, h)), vec(2),
            vec(2),
            pl.BlockSpec((1, 2, hw), lambda bi, h: (bi, 0, h)),
        ],
        out_specs=[
            pl.BlockSpec((1, seq, hw), lambda bi, h: (bi, 0, h)),
            pl.BlockSpec((1, 2, hw), lambda bi, h: (bi, 0, h)),
        ],
        out_shape=[jax.ShapeDtypeStruct((b, seq, D_LRU), F32), jax.ShapeDtypeStruct((b, 2, D_LRU), F32)],
        scratch_shapes=[
            pltpu.VMEM((seq + 2 * LRU_PAD, hw), F32),
            pltpu.VMEM((2, seq, hw), F32),
            pltpu.VMEM((2, seq, hw), F32),
            pltpu.VMEM((seq, hw), F32),
        ],
        compiler_params=_cparams(("parallel", "parallel")),
        name="rglru_mixer",
    )(p, p, conv_w, conv_b.reshape(1, D_LRU), wr, b_r, wi, b_i, lam, h0)


HG_TILE = 256
HG_GROUP = 16
HG_AHEAD = 1
HG_NBUF = 2
LOG2_E = math.log2(math.e)


def _hg_constants():
    idx = np.arange(HG_TILE)
    same_blk = (idx[:, None] // HG_BLK) == (idx[None, :] // HG_BLK)
    tri_f = same_blk & (idx[None, :] <= idx[:, None])
    tri_b = same_blk & (idx[None, :] >= idx[:, None])
    head = (idx[:, None] // HG_DK) == (idx[None, :] // HG_DK)
    return jnp.asarray(np.stack([tri_f, tri_b]), BF16), jnp.asarray(same_blk, BF16), jnp.asarray(head, BF16)


def _hgrn_kernel(q_ref, z_ref, v_ref, lb_ref, tri_ref, ones_ref, headb_ref, s0_ref,
                 o_ref, st_ref, state, *scratch):
    att_bufs = scratch[0:HG_NBUF]
    upd_bufs = scratch[HG_NBUF:2 * HG_NBUF]
    qst_bufs = scratch[2 * HG_NBUF:3 * HG_NBUF]
    qs, srcs, vs, cums, qds, kns, tots = scratch[3 * HG_NBUF:]
    seg_rows = q_ref.shape[1]
    d = pl.program_id(1)
    s = pl.program_id(2)
    nseg = pl.num_programs(2)

    @pl.when(s == 0)
    def _():
        state[...] = jnp.concatenate([jnp.transpose(s0_ref[0, 0, h]) for h in range(HG_HEADS)], axis=1)

    lb = lb_ref[0]
    for ti in range(seg_rows // HG_TILE):
        rows = slice(ti * HG_TILE, (ti + 1) * HG_TILE)
        f = lb + (1.0 - lb) * _sigmoid(z_ref[0, rows, :])
        g = jnp.log(f)
        kk = 1.0 - f
        q = _silu(q_ref[0, rows, :]) * HG_SCALE
        cum = _dot_exact_rhs(tri_ref[0], g)
        tot = _dot_exact_rhs(ones_ref[...], g)
        qs[rows, :] = q
        srcs[rows, :] = (cum - jnp.log(kk)) * LOG2_E
        vs[rows, :] = v_ref[0, rows, :]
        cums[rows, :] = cum * LOG2_E
        tots[rows, :] = tot
        qds[rows, :] = q * jnp.exp(cum)
        kns[rows, :] = kk * jnp.exp(tot - cum)

    nblk = seg_rows // HG_BLK
    sgn = jnp.where(d == 0, 1, -1)
    t_signed = [(lax.broadcasted_iota(jnp.int32, (8, D_HG), 0) + 8 * h) * sgn for h in range(HG_BLK // 8)]
    lane_head_blk = lax.broadcasted_iota(jnp.int32, (HG_BLK, D_HG), 1) // HG_DK
    lane_head_dv = lax.broadcasted_iota(jnp.int32, (HG_DV, D_HG), 1) // HG_DK

    n_half = HG_BLK // 8

    def block_rows(j):
        je = jnp.where(d == 0, j, nblk - 1 - j)
        return pl.multiple_of(je * HG_BLK, HG_BLK)

    def prepare(j, slot):
        r0 = block_rows(j)
        rows = pl.ds(r0, HG_BLK)
        cum_h = [cums[pl.ds(r0 + 8 * h, 8), :] for h in range(n_half)]
        q_h = [qs[pl.ds(r0 + 8 * h, 8), :] for h in range(n_half)]
        tiles = []
        for src in range(HG_BLK):
            src_row = srcs[pl.ds(r0 + src, 1), :]
            for h in range(n_half):
                pair = jnp.exp2(cum_h[h] - src_row) * q_h[h]
                tiles.append(jnp.where(t_signed[h] >= src * sgn, pair, 0.0))
        att_bufs[slot][...] = _dot(_bf(jnp.concatenate(tiles, axis=0)), headb_ref[...])
        upd = lax.dot_general(_bf(vs[rows, :]), _bf(kns[rows, :]), (((0,), (0,)), ((), ())),
                              preferred_element_type=F32)
        upd_fold = jnp.where(lane_head_dv == 0, upd[0:HG_DV], 0.0)
        for h in range(1, HG_HEADS):
            upd_fold = upd_fold + jnp.where(lane_head_dv == h, upd[h * HG_DV:(h + 1) * HG_DV], 0.0)
        upd_bufs[slot][...] = upd_fold
        qd = qds[rows, :]
        qst_bufs[slot][...] = _bf(jnp.concatenate(
            [jnp.where(lane_head_blk == h, qd, 0.0) for h in range(HG_HEADS)], axis=0))

    def finish(j, slot, st):
        r0 = block_rows(j)
        rows = pl.ds(r0, HG_BLK)
        o_halves = [None] * n_half
        for src in range(HG_BLK):
            v_row = vs[pl.ds(r0 + src, 1), :]
            for h in range(n_half):
                term = att_bufs[slot][src * HG_BLK + h * 8:src * HG_BLK + (h + 1) * 8, :] * v_row
                o_halves[h] = term if o_halves[h] is None else o_halves[h] + term
        by_head = _dot_nt(qst_bufs[slot][...], _bf(st))
        o_state = jnp.concatenate([by_head[h * HG_BLK:(h + 1) * HG_BLK] for h in range(HG_HEADS)], axis=1)
        o_ref[0, 0, rows, :] = jnp.concatenate(o_halves, axis=0) + o_state
        return jnp.exp(tots[pl.ds(r0, 1), :]) * st + upd_bufs[slot][...]

    for k in range(HG_AHEAD):
        prepare(k, k)

    def block_group(i, st):
        j = HG_GROUP * i
        for k in range(HG_GROUP):
            prepare(jnp.minimum(j + k + HG_AHEAD, nblk - 1), (k + HG_AHEAD) % HG_NBUF)
            st = finish(j + k, k % HG_NBUF, st)
        return st

    state[...] = lax.fori_loop(0, nblk // HG_GROUP, block_group, state[...])

    @pl.when(s == nseg - 1)
    def _():
        for h in range(HG_HEADS):
            st_ref[0, 0, h] = jnp.transpose(state[:, h * HG_DK:(h + 1) * HG_DK])


def hgrn2_scan(p, lb, s0, seg_rows):
    b, seq, _ = p.shape
    nseg = seq // seg_rows
    tri, ones, headb = _hg_constants()
    seg = lambda d, s: jnp.where(d == 0, s, nseg - 1 - s)
    col = lambda c: pl.BlockSpec((1, seg_rows, D_HG), lambda bi, d, s: (bi, seg(d, s), c // D_HG))
    scr = lambda: pltpu.VMEM((seg_rows, D_HG), F32)
    return pl.pallas_call(
        _hgrn_kernel,
        grid=(b, 2, nseg),
        in_specs=[
            col(COL_HQ),
            pl.BlockSpec((1, seg_rows, D_HG), lambda bi, d, s: (bi, seg(d, s), COL_HFF // D_HG + d)),
            col(COL_HI),
            pl.BlockSpec((1, 1, D_HG), lambda bi, d, s: (d, 0, 0)),
            pl.BlockSpec((1, HG_TILE, HG_TILE), lambda bi, d, s: (d, 0, 0)),
            _const_spec((HG_TILE, HG_TILE)),
            _const_spec((D_HG, D_HG)),
            pl.BlockSpec((1, 1, HG_HEADS, HG_DK, HG_DV), lambda bi, d, s: (bi, d, 0, 0, 0)),
        ],
        out_specs=[
            pl.BlockSpec((1, 1, seg_rows, D_HG), lambda bi, d, s: (d, bi, seg(d, s), 0)),
            pl.BlockSpec((1, 1, HG_HEADS, HG_DK, HG_DV), lambda bi, d, s: (bi, d, 0, 0, 0)),
        ],
        out_shape=[jax.ShapeDtypeStruct((2, b, seq, D_HG), F32),
                   jax.ShapeDtypeStruct((b, 2, HG_HEADS, HG_DK, HG_DV), F32)],
        scratch_shapes=[pltpu.VMEM((HG_DV, D_HG), F32)]
        + [pltpu.VMEM((HG_BLK * HG_BLK, D_HG), F32) for _ in range(HG_NBUF)]
        + [pltpu.VMEM((HG_DV, D_HG), F32) for _ in range(HG_NBUF)]
        + [pltpu.VMEM((HG_HEADS * HG_BLK, D_HG), BF16) for _ in range(HG_NBUF)] + [scr() for _ in range(7)],
        compiler_params=_cparams(("parallel", "parallel", "arbitrary")),
        name="hgrn2_scan",
    )(p, p, p, lb.reshape(2, 1, D_HG), tri, ones, headb, s0)


HY_COLS = 2 * HY_ORDER * D_HY
HY_FEAT_PAD = 128
HY_GEN_ROWS = 256


def _dot_hi(a, b):
    ah = _bf(a)
    al = _bf(a - ah.astype(F32))
    bh = _bf(b)
    bl = _bf(b - bh.astype(F32))
    return _dot(ah, bh) + _dot(ah, bl) + _dot(al, bh)


HY_HALF_COLS = HY_ORDER * D_HY


def _filter_gen_kernel(feat_ref, w1_ref, b1_ref, w2_ref, b2_ref, w3_ref, fr_ref, dl_ref, h_ref, ssq_ref, *, seq):
    i = pl.program_id(0)
    tm = feat_ref.shape[0]
    feat = feat_ref[...]
    fr = fr_ref[...]
    h = jnp.sin(fr * (_dot_hi(feat, w1_ref[...]) + b1_ref[...]))
    h = jnp.sin(fr * (_dot_hi(h, w2_ref[...]) + b2_ref[...]))
    h = _dot_hi(h, w3_ref[...])
    decay = jnp.exp(-feat[:, 0:1] * dl_ref[...])
    decay = jnp.concatenate([decay] * HY_ORDER, axis=1)
    row = i * tm + lax.broadcasted_iota(jnp.int32, (tm, HY_HALF_COLS), 0)
    taps = h * decay
    h_ref[...] = jnp.where(row == seq, 0.0, taps)

    @pl.when(i == 0)
    def _():
        ssq_ref[...] = jnp.zeros_like(ssq_ref)

    ssq_ref[...] += jnp.sum(taps * taps, axis=0, keepdims=True)


def _hyena_features(seq):
    order = np.concatenate([np.arange(seq), [0], np.arange(seq - 1, 0, -1)])
    pos = order.astype(np.float32)
    t = pos / np.float32(max(seq - 1, 1))
    bands = np.linspace(1e-4, HY_BANDS - 1, HY_BANDS, dtype=np.float32)
    ang = np.float32(2.0 * math.pi / seq) * pos[:, None] * bands[None, :]
    feat = np.concatenate([t[:, None], np.cos(ang), np.sin(ang)], axis=-1).astype(np.float32)
    out = np.zeros((2 * seq, HY_FEAT_PAD), np.float32)
    out[:, :HY_EMB] = feat
    return jnp.asarray(out)


def hyena_filter_gen(seq, w1, b1, w2, b2, w3, freq):
    feat = _hyena_features(seq)
    w1p = jnp.zeros((HY_FEAT_PAD, HY_FFN), F32).at[:HY_EMB].set(w1.astype(F32))
    deltas = jnp.asarray(np.linspace(HY_DECAY_SLOW, HY_DECAY_FAST, D_HY, dtype=np.float32)).reshape(1, D_HY)
    tm = HY_GEN_ROWS
    return pl.pallas_call(
        functools.partial(_filter_gen_kernel, seq=seq),
        grid=(2 * seq // tm,),
        in_specs=[
            pl.BlockSpec((tm, HY_FEAT_PAD), lambda i: (i, 0)),
            _const_spec((HY_FEAT_PAD, HY_FFN)), _const_spec((1, HY_FFN)),
            _const_spec((HY_FFN, HY_FFN)), _const_spec((1, HY_FFN)),
            pl.BlockSpec((HY_FFN, HY_HALF_COLS), lambda i: (0, (i * tm) // seq)),
            _const_spec((1, HY_FFN)), _const_spec((1, D_HY)),
        ],
        out_specs=[pl.BlockSpec((tm, HY_HALF_COLS), lambda i: (i, 0)), _const_spec((1, HY_HALF_COLS))],
        out_shape=[jax.ShapeDtypeStruct((2 * seq, HY_HALF_COLS), F32),
                   jax.ShapeDtypeStruct((1, HY_HALF_COLS), F32)],
        compiler_params=_cparams(("arbitrary",)),
        name="hyena_filter_gen",
    )(feat, w1p, b1.reshape(1, HY_FFN), w2, b2.reshape(1, HY_FFN), w3, freq.reshape(1, HY_FFN), deltas)


def _filter_norm(ssq_ref, n):
    return lax.rsqrt(ssq_ref[:, n * D_HY:(n + 1) * D_HY])


def _conv3(pad_ref, w_ref, b_ref, r0, rows, pad):
    acc = b_ref[...]
    for k in range(HY_SHORT):
        lo = r0 + pad - HY_SHORT // 2 + k
        acc = acc + w_ref[k:k + 1, :] * pad_ref[lo:lo + rows, :]
    return acc


def _dft_constants(seq):
    n = 2 * seq
    k = np.arange(n)
    ang = 2.0 * np.pi * np.outer(k, k) / n
    fwd = np.concatenate([np.cos(ang), -np.sin(ang)], axis=0)
    inv = np.concatenate([np.cos(ang[:seq]), -np.sin(ang[:seq])], axis=1) / n
    return jnp.asarray(fwd, BF16), jnp.asarray(inv, BF16)


def _ctx_spectrum_kernel(filt_ref, ssq_ref, fwd_ref, h_ref):
    n = pl.program_id(0)
    norm = jnp.where(n == 0, _filter_norm(ssq_ref, 0), _filter_norm(ssq_ref, 1))
    x = filt_ref[...] * norm
    xh = _bf(x)
    xl = _bf(x - xh.astype(F32))
    h_ref[0] = _dot(fwd_ref[...], xh) + _dot(fwd_ref[...], xl)


def hyena_ctx_spectrum(filt, ssq):
    n = filt.shape[0]
    fwd, _ = _dft_constants(n // 2)
    return pl.pallas_call(
        _ctx_spectrum_kernel,
        grid=(HY_ORDER,),
        in_specs=[pl.BlockSpec((n, D_HY), lambda o: (0, o)), _const_spec((1, HY_HALF_COLS)),
                  _const_spec((2 * n, n))],
        out_specs=pl.BlockSpec((1, 2 * n, D_HY), lambda o: (o, 0, 0)),
        out_shape=jax.ShapeDtypeStruct((HY_ORDER, 2 * n, D_HY), F32),
        compiler_params=_cparams(("parallel",)),
        name="hyena_ctx_spectrum",
    )(filt, ssq, fwd)


HY_PAD = 8


def _hyena_ctx_kernel(v_ref, x1_ref, x2_ref, cw_ref, cb_ref, bias_ref, spec_ref, fwd_ref, inv_ref, o_ref, pad_ref):
    seq = v_ref.shape[1]
    n = 2 * seq
    zeros = jnp.zeros((HY_PAD, D_HY), F32)
    pad_ref[0:HY_PAD, :] = zeros
    pad_ref[HY_PAD + seq:2 * HY_PAD + seq, :] = zeros

    def short_conv(src_ref, part):
        pad_ref[HY_PAD:HY_PAD + seq, :] = src_ref[0]
        return _conv3(pad_ref, cw_ref.at[part], cb_ref.at[part], 0, seq, HY_PAD)

    z = short_conv(v_ref, 0)
    for order, gate_ref in enumerate((x1_ref, x2_ref)):
        spec = _dot(fwd_ref[:, 0:seq], _bf(z))
        xr, xi = spec[:n], spec[n:]
        hr, hi = spec_ref[order, 0:n, :], spec_ref[order, n:2 * n, :]
        prod = jnp.concatenate([xr * hr - xi * hi, xr * hi + xi * hr], axis=0)
        conv = _dot(inv_ref[...], _bf(prod))
        z = short_conv(gate_ref, order + 1) * (conv + bias_ref[order:order + 1, :] * z)
    o_ref[0] = z


def hyena_ctx(p, conv_w, conv_b, bias, spec):
    b, seq, _ = p.shape
    fwd, inv = _dft_constants(seq)
    col = lambda c: pl.BlockSpec((1, seq, D_HY), lambda i: (i, 0, c))
    c0 = COL_HY // D_HY
    return pl.pallas_call(
        _hyena_ctx_kernel,
        grid=(b,),
        in_specs=[
            col(c0), col(c0 + 1), col(c0 + 2),
            _const_spec((3, HY_SHORT, D_HY)), _const_spec((3, 1, D_HY)), _const_spec((HY_ORDER, D_HY)),
            _const_spec((HY_ORDER, 4 * seq, D_HY)), _const_spec((4 * seq, 2 * seq)), _const_spec((seq, 4 * seq)),
        ],
        out_specs=pl.BlockSpec((1, seq, D_HY), lambda i: (i, 0, 0)),
        out_shape=jax.ShapeDtypeStruct((b, seq, D_HY), F32),
        scratch_shapes=[pltpu.VMEM((seq + 2 * HY_PAD, D_HY), F32)],
        compiler_params=_cparams(("parallel",)),
        name="hyena_ctx",
    )(p, p, p, conv_w.reshape(HY_SHORT, 3, D_HY).transpose(1, 0, 2), conv_b.reshape(3, 1, D_HY), bias, spec,
      fwd, inv)


FFT_N1 = 64
FFT_N2 = 128
FFT_N = FFT_N1 * FFT_N2
FFT_K1 = FFT_N1 // 2 + 1
FFT_K1_PAD = 40
FFT_ROWS = FFT_K1_PAD * FFT_N2
HY_LANES = 128
FFT_UNROLL1 = 16
FFT_UNROLL2 = 11


def _fft_constants():
    n1, n2, n, kp = FFT_N1, FFT_N2, FFT_N, FFT_K1_PAD
    k1 = np.arange(kp)
    w1 = np.exp(-2j * np.pi * np.outer(k1, np.arange(n1)) / n1)
    tw = np.exp(-2j * np.pi * np.outer(np.arange(n2), k1) / n)
    f1c = tw[:, :, None] * w1[None, :, :]
    f1t = np.concatenate([f1c.real, f1c.imag], axis=1).transpose(0, 2, 1)
    w2 = np.exp(-2j * np.pi * np.outer(np.arange(n2), np.arange(n2)) / n2)
    f2 = np.block([[w2.real, -w2.imag], [w2.imag, w2.real]])
    g2c = np.conj(w2)
    g2 = np.block([[g2c.real, -g2c.imag], [g2c.imag, g2c.real]])
    weight = np.where((k1 == 0) | (k1 == n1 // 2), 1.0, 2.0) * (k1 < FFT_K1)
    mc = np.conj(w1.T)[None, :n1 // 2, :] * np.conj(tw)[:, None, :] * weight / n
    g1 = np.concatenate([mc.real, -mc.imag], axis=2)
    return (jnp.asarray(f1t, BF16), jnp.asarray(f2, BF16), jnp.asarray(g2, BF16), jnp.asarray(g1, BF16))


def _dot_tn(a, b):
    return lax.dot_general(a, b, (((0,), (0,)), ((), ())), preferred_element_type=F32)


def _fft_stage1(x_ref, f1t_ref, a_ref, n1_in):
    def body(n2, carry):
        rows = x_ref[pl.ds(n2, n1_in, stride=FFT_N2), :]
        out = _dot_tn(f1t_ref[n2, 0:n1_in, :], _bf(rows))
        a_ref[0, pl.ds(n2, FFT_K1, stride=FFT_N2), :] = out[:FFT_K1]
        a_ref[1, pl.ds(n2, FFT_K1, stride=FFT_N2), :] = out[FFT_K1_PAD:FFT_K1_PAD + FFT_K1]
        return carry
    lax.fori_loop(0, FFT_N2, body, 0, unroll=FFT_UNROLL1)


def _lat_spectrum_kernel(filt_ref, ssq_ref, f1t_ref, f2_ref, h_ref, a_ref):
    n = pl.program_id(0)
    half = pl.program_id(1)
    norm_full = jnp.where(n == 0, _filter_norm(ssq_ref, 0), _filter_norm(ssq_ref, 1))
    norm = jnp.where(half == 0, norm_full[:, :HY_LANES], norm_full[:, HY_LANES:])
    _fft_stage1(filt_ref, f1t_ref, a_ref, FFT_N1)

    def stage2(k1, carry):
        r0 = pl.multiple_of(k1 * FFT_N2, FFT_N2)
        rows = pl.ds(r0, FFT_N2)
        a = jnp.concatenate([a_ref[0, rows, :], a_ref[1, rows, :]], axis=0)
        b = _dot(f2_ref[...], _bf(a)) * norm
        h_ref[0, 0, rows, :] = _bf(b[:FFT_N2])
        h_ref[0, 1, rows, :] = _bf(b[FFT_N2:])
        return carry
    lax.fori_loop(0, FFT_K1, stage2, 0)


def hyena_lat_spectrum(filt, ssq):
    f1t, f2, _, _ = _fft_constants()
    rows = FFT_K1 * FFT_N2
    halves = D_HY // HY_LANES
    return pl.pallas_call(
        _lat_spectrum_kernel,
        grid=(HY_ORDER, halves),
        in_specs=[
            pl.BlockSpec((FFT_N, HY_LANES), lambda o, h: (0, o * halves + h)),
            _const_spec((1, HY_HALF_COLS)),
            _const_spec((FFT_N2, FFT_N1, 2 * FFT_K1_PAD)),
            _const_spec((2 * FFT_N2, 2 * FFT_N2)),
        ],
        out_specs=pl.BlockSpec((1, 2, rows, HY_LANES), lambda o, h: (o, 0, 0, h)),
        out_shape=jax.ShapeDtypeStruct((HY_ORDER, 2, rows, D_HY), BF16),
        scratch_shapes=[pltpu.VMEM((2, FFT_ROWS, HY_LANES), F32)],
        compiler_params=_cparams(("parallel", "parallel")),
        name="hyena_lat_spectrum",
    )(filt, ssq, f1t, f2)


HY_ROWS = 512


def _hyena_lat_kernel(v_ref, x1_ref, x2_ref, cw_ref, cb_ref, bias_ref, spec_ref, f1t_ref, f2_ref, g2_ref, g1_ref,
                      o_ref, pad_ref, z_ref, a_ref):
    seq = v_ref.shape[1]
    zeros = jnp.zeros((HY_PAD, HY_LANES), F32)
    pad_ref[0:HY_PAD, :] = zeros
    pad_ref[HY_PAD + seq:2 * HY_PAD + seq, :] = zeros
    pad_ref[HY_PAD:HY_PAD + seq, :] = v_ref[0]
    for ti in range(seq // HY_ROWS):
        r0 = ti * HY_ROWS
        z_ref[r0:r0 + HY_ROWS, :] = _conv3(pad_ref, cw_ref.at[0], cb_ref.at[0], r0, HY_ROWS, HY_PAD)
    tail = jnp.zeros((FFT_ROWS - FFT_K1 * FFT_N2, HY_LANES), F32)
    a_ref[0, FFT_K1 * FFT_N2:FFT_ROWS, :] = tail
    a_ref[1, FFT_K1 * FFT_N2:FFT_ROWS, :] = tail

    for order, gate_ref in enumerate((x1_ref, x2_ref)):
        _fft_stage1(z_ref, f1t_ref, a_ref, FFT_N1 // 2)

        def stage2(k1, carry):
            r0 = pl.multiple_of(k1 * FFT_N2, FFT_N2)
            rows = pl.ds(r0, FFT_N2)
            a = jnp.concatenate([a_ref[0, rows, :], a_ref[1, rows, :]], axis=0)
            b = _dot(f2_ref[...], _bf(a))
            br, bi = b[:FFT_N2], b[FFT_N2:]
            hr = spec_ref[order, 0, rows, :].astype(F32)
            hi = spec_ref[order, 1, rows, :].astype(F32)
            y = jnp.concatenate([br * hr - bi * hi, br * hi + bi * hr], axis=0)
            c = _dot(g2_ref[...], _bf(y))
            a_ref[0, rows, :] = c[:FFT_N2]
            a_ref[1, rows, :] = c[FFT_N2:]
            return carry
        lax.fori_loop(0, FFT_K1, stage2, 0, unroll=FFT_UNROLL2)

        def stage1_inv(n2, carry):
            c = jnp.concatenate([a_ref[0, pl.ds(n2, FFT_K1_PAD, stride=FFT_N2), :],
                                 a_ref[1, pl.ds(n2, FFT_K1_PAD, stride=FFT_N2), :]], axis=0)
            o_ref[0, pl.ds(n2, FFT_N1 // 2, stride=FFT_N2), :] = _dot(g1_ref[n2], _bf(c))
            return carry
        lax.fori_loop(0, FFT_N2, stage1_inv, 0, unroll=FFT_UNROLL1)

        pad_ref[HY_PAD:HY_PAD + seq, :] = gate_ref[0]
        for ti in range(seq // HY_ROWS):
            r0 = ti * HY_ROWS
            rows = slice(r0, r0 + HY_ROWS)
            gate = _conv3(pad_ref, cw_ref.at[order + 1], cb_ref.at[order + 1], r0, HY_ROWS, HY_PAD)
            z = gate * (o_ref[0, rows, :] + bias_ref[order:order + 1, :] * z_ref[rows, :])
            if order + 1 < HY_ORDER:
                z_ref[rows, :] = z
            else:
                o_ref[0, rows, :] = z


def hyena_lat(p, conv_w, conv_b, bias, spec):
    b, seq, _ = p.shape
    assert 2 * seq == FFT_N
    f1t, f2, g2, g1 = _fft_constants()
    hw = HY_LANES
    col = lambda c: pl.BlockSpec((1, seq, hw), lambda i, h: (i, 0, (COL_HY + c * D_HY) // hw + h))
    return pl.pallas_call(
        _hyena_lat_kernel,
        grid=(b, D_HY // hw),
        in_specs=[
            col(0), col(1), col(2),
            pl.BlockSpec((3, HY_SHORT, hw), lambda i, h: (0, 0, h)),
            pl.BlockSpec((3, 1, hw), lambda i, h: (0, 0, h)),
            pl.BlockSpec((HY_ORDER, hw), lambda i, h: (0, h)),
            pl.BlockSpec((HY_ORDER, 2, FFT_K1 * FFT_N2, hw), lambda i, h: (0, 0, 0, h)),
            _const_spec((FFT_N2, FFT_N1, 2 * FFT_K1_PAD)),
            _const_spec((2 * FFT_N2, 2 * FFT_N2)),
            _const_spec((2 * FFT_N2, 2 * FFT_N2)),
            _const_spec((FFT_N2, FFT_N1 // 2, 2 * FFT_K1_PAD)),
        ],
        out_specs=pl.BlockSpec((1, seq, hw), lambda i, h: (i, 0, h)),
        out_shape=jax.ShapeDtypeStruct((b, seq, D_HY), F32),
        scratch_shapes=[
            pltpu.VMEM((seq + 2 * HY_PAD, hw), F32),
            pltpu.VMEM((seq, hw), F32),
            pltpu.VMEM((2, FFT_ROWS, hw), F32),
        ],
        compiler_params=_cparams(("parallel", "parallel")),
        name="hyena_lat",
    )(p, p, p, conv_w.reshape(HY_SHORT, 3, D_HY).transpose(1, 0, 2), conv_b.reshape(3, 1, D_HY), bias, spec,
      f1t, f2, g2, g1)


HG_SEG_ROWS = 512
COND_ROWS = 16


def _trunk_layer(x, mods, lp, lb, batch, seq, ctx, final):
    x = ffn_half_step(x, mods, lp['ln_ffn1'], lp['w_ffn1_in'], lp['w_ffn1_out'], lp['final_norm'],
                      mod_base=0, final=False, seq_len=seq)
    p2d = mixer_in_proj(x, mods, lp['ln_mix'], lp['w_in'], seq_len=seq)
    p = p2d.reshape(batch, seq, N_IN)

    filt, ssq = hyena_filter_gen(seq, lp['hy_w1'], lp['hy_b1'], lp['hy_w2'], lp['hy_b2'], lp['hy_w3'],
                                 lp['hy_freq'])
    if ctx is None:
        att = context_attention(p, lp['attn_sink'])
        lru_h0 = jnp.zeros((batch, 2, D_LRU), F32)
        hg_s0 = jnp.zeros((batch, 2, HG_HEADS, HG_DK, HG_DV), F32)
        hy = hyena_ctx(p, lp['hy_conv_w'], lp['hy_conv_b'], lp['hy_bias'], hyena_ctx_spectrum(filt, ssq))
    else:
        k_ctx, v_ctx, lru_h0, hg_state = ctx
        att = latent_attention(p, k_ctx, v_ctx, lp['attn_sink'])
        hg_s0 = hg_state.astype(F32)
        hy = hyena_lat(p, lp['hy_conv_w'], lp['hy_conv_b'], lp['hy_bias'], hyena_lat_spectrum(filt, ssq))
    lru, lru_state = rglru_mixer(p, lp['lru_conv_w'], lp['lru_conv_b'], lp['lru_w_r'], lp['lru_b_r'],
                                 lp['lru_w_i'], lp['lru_b_i'], lp['lru_lambda'], lru_h0.astype(F32))
    o_dirs, hg_state_out = hgrn2_scan(p, lb, hg_s0, min(seq, HG_SEG_ROWS))
    t = batch * seq

    x = gated_merge(x, mods, lp['ln_mix'], att.reshape(t, D_ATTN), lru.reshape(t, D_LRU),
                    o_dirs.reshape(2, t, D_HG), p2d, lp['hg_norm'], hy.reshape(t, D_HY), lp['w_gate'],
                    lp['w_bo_attn'], lp['w_bo_lru'], lp['w_bo_hg'], lp['w_bo_hy'], lp['w_out'], seq_len=seq)
    x = ffn_half_step(x, mods, lp['ln_ffn2'], lp['w_ffn2_in'], lp['w_ffn2_out'], lp['final_norm'],
                      mod_base=6, final=final, seq_len=seq)
    state = None
    if ctx is None:
        k = p[:, :, COL_K:COL_K + D_KV].reshape(batch, seq, N_KV_HEADS, HEAD_DIM)
        v = p[:, :, COL_V:COL_V + D_KV].reshape(batch, seq, N_KV_HEADS, HEAD_DIM)
        state = (k, v, lru_state, hg_state_out)
    return x, state


def kernel(x_prompt, x_sample, cache_k, cache_v, state_lru, state_hgrn, c, c_ctx, ln_ffn1, ln_mix, ln_ffn2, w_ada, b_ada, w_ffn1_in, w_ffn1_out, w_ffn2_in, w_ffn2_out, w_in, attn_sink, lru_conv_w, lru_conv_b, lru_w_r, lru_b_r, lru_w_i, lru_b_i, lru_lambda, hg_lb_logits, hg_norm, hy_conv_w, hy_conv_b, hy_w1, hy_b1, hy_w2, hy_b2, hy_w3, hy_freq, hy_bias, w_bo_attn, w_bo_lru, w_bo_hg, w_bo_hy, w_gate, w_out, final_norm):
    batch, seq, _ = x_prompt.shape
    dec_batch, dec_seq, _ = x_sample.shape
    lb_soft = jax.nn.softmax(hg_lb_logits.astype(F32), axis=0)
    lb_all = jnp.cumsum(lb_soft, axis=0) - lb_soft[0]

    stacked = dict(ln_ffn1=ln_ffn1, ln_mix=ln_mix, ln_ffn2=ln_ffn2, w_ffn1_in=w_ffn1_in, w_ffn1_out=w_ffn1_out,
                   w_ffn2_in=w_ffn2_in, w_ffn2_out=w_ffn2_out, w_in=w_in, attn_sink=attn_sink,
                   lru_conv_w=lru_conv_w, lru_conv_b=lru_conv_b, lru_w_r=lru_w_r, lru_b_r=lru_b_r,
                   lru_w_i=lru_w_i, lru_b_i=lru_b_i, lru_lambda=lru_lambda, hg_norm=hg_norm,
                   hy_conv_w=hy_conv_w, hy_conv_b=hy_conv_b, hy_w1=hy_w1, hy_b1=hy_b1, hy_w2=hy_w2, hy_b2=hy_b2,
                   hy_w3=hy_w3, hy_freq=hy_freq, hy_bias=hy_bias, w_bo_attn=w_bo_attn, w_bo_lru=w_bo_lru,
                   w_bo_hg=w_bo_hg, w_bo_hy=w_bo_hy, w_gate=w_gate, w_out=w_out)

    cond = jnp.zeros((COND_ROWS, D_MODEL), F32).at[0].set(c_ctx).at[1:1 + dec_batch].set(c)
    mods = ada_mods(cond, w_ada, b_ada).reshape(DEPTH, COND_ROWS, N_MOD, D_MODEL)

    h = x_prompt.reshape(batch * seq, D_MODEL)
    z = x_sample.reshape(dec_batch * dec_seq, D_MODEL)
    ks, vs, lrus, hgs = [], [], [], []
    for l in range(DEPTH):
        lp = {name: w[l] for name, w in stacked.items()}
        lp['final_norm'] = final_norm
        final = l == DEPTH - 1
        h, (k_l, v_l, lru_l, hg_l) = _trunk_layer(h, mods[l, 0:1], lp, lb_all[l], batch, seq, None, final)
        ks.append(k_l)
        vs.append(v_l)
        lrus.append(lru_l)
        hgs.append(hg_l)
        ctx = (cache_k[:, l].reshape(dec_batch, -1, D_KV), cache_v[:, l].reshape(dec_batch, -1, D_KV),
               state_lru[:, l], state_hgrn[:, l])
        z, _ = _trunk_layer(z, mods[l, 1:1 + dec_batch], lp, lb_all[l], dec_batch, dec_seq, ctx, final)

    y_prompt = h.reshape(batch, seq, D_MODEL)
    y_sample = z.reshape(dec_batch, dec_seq, D_MODEL)
    return (y_prompt, y_sample, jnp.stack(ks, axis=1), jnp.stack(vs, axis=1), jnp.stack(lrus, axis=1),
            jnp.stack(hgs, axis=1))
```

```python
import functools
import math

import numpy as np
import jax
import jax.numpy as jnp
from jax import lax
from jax.experimental import pallas as pl
from jax.experimental.pallas import tpu as pltpu

F32 = jnp.float32
BF16 = jnp.bfloat16

D_MODEL = 1024
DEPTH = 2
GRID_W = 64
EPS = 1e-6
N_MOD = 9
N_BRANCH = 4
D_FF = 2816
N_HEADS = 8
N_KV_HEADS = 2
HEAD_DIM = 64
GQA_GROUP = N_HEADS // N_KV_HEADS
D_ATTN = N_HEADS * HEAD_DIM
D_KV = N_KV_HEADS * HEAD_DIM
WINDOW = 128
ATTN_BLOCK = 128
ATTN_SCALE = HEAD_DIM ** -0.5
ROPE_BASE = 10000.0
D_LRU = D_MODEL // 4
LRU_HEADS = 4
LRU_BLOCK = D_LRU // LRU_HEADS
LRU_CONV = 4
LRU_C = 8.0
HG_HEADS = 4
HG_DK = 64
HG_DV = 64
D_HG = HG_HEADS * HG_DK
HG_SCALE = HG_DK ** -0.5
HG_BLK = 16
D_HY = D_MODEL // 4
HY_ORDER = 2
HY_SHORT = 3
HY_BANDS = 8
HY_EMB = 2 * HY_BANDS + 1
HY_FFN = 64
HY_TARGET = 1e-2
HY_DECAY_SLOW = -math.log(HY_TARGET) / 1.5
HY_DECAY_FAST = -math.log(HY_TARGET) / 0.3
N_IN = D_ATTN + 2 * D_KV + 2 * D_LRU + 5 * D_HG + 3 * D_HY

COL_Q = 0
COL_K = D_ATTN
COL_V = D_ATTN + D_KV
COL_RX = D_ATTN + 2 * D_KV
COL_RG = COL_RX + D_LRU
COL_HQ = COL_RG + D_LRU
COL_HFF = COL_HQ + D_HG
COL_HFB = COL_HFF + D_HG
COL_HI = COL_HFB + D_HG
COL_HGATE = COL_HI + D_HG
COL_HY = COL_HGATE + D_HG

FF_CHUNK = 256
N_FF_CHUNK = D_FF // FF_CHUNK
ROW_TILE = 512
VMEM_LIMIT = 56 * 1024 * 1024


def _bf(x):
    return x.astype(BF16)


def _dot(a, b):
    return jnp.dot(a, b, preferred_element_type=F32)


def _split3(x):
    hi = _bf(x)
    r1 = x - hi.astype(F32)
    mid = _bf(r1)
    lo = _bf(r1 - mid.astype(F32))
    return hi, mid, lo


def _dot_exact_rhs(mat_bf, x):
    hi, mid, lo = _split3(x)
    return _dot(mat_bf, hi) + _dot(mat_bf, mid) + _dot(mat_bf, lo)


def _sigmoid(x):
    return jax.nn.sigmoid(x)


def _silu(x):
    return x * _sigmoid(x)


def _norm_mod(x, lnw, shift, scale):
    ms = jnp.mean(x * x, axis=-1, keepdims=True)
    y = x * lax.rsqrt(ms + EPS) * lnw
    return y * (1.0 + scale) + shift


def _cparams(sem):
    return pltpu.CompilerParams(dimension_semantics=sem, vmem_limit_bytes=VMEM_LIMIT)


def _const_spec(shape):
    nd = len(shape)
    return pl.BlockSpec(shape, lambda *_: (0,) * nd)


def _ada_kernel(cond_ref, w_ref, b_ref, o_ref):
    c = cond_ref[...]
    o_ref[0] = _dot(_bf(_silu(c)), _bf(w_ref[0])) + b_ref[0]


def ada_mods(cond, w_ada, b_ada):
    r = cond.shape[0]
    cb = 1024
    ncol = (N_MOD * D_MODEL) // cb
    return pl.pallas_call(
        _ada_kernel,
        grid=(DEPTH, ncol),
        in_specs=[
            pl.BlockSpec((r, D_MODEL), lambda l, j: (0, 0)),
            pl.BlockSpec((1, D_MODEL, cb), lambda l, j: (l, 0, j)),
            pl.BlockSpec((1, 1, cb), lambda l, j: (l, 0, j)),
        ],
        out_specs=pl.BlockSpec((1, r, cb), lambda l, j: (l, 0, j)),
        out_shape=jax.ShapeDtypeStruct((DEPTH, r, N_MOD * D_MODEL), F32),
        compiler_params=_cparams(("parallel", "parallel")),
        name="ada_mods",
    )(cond, w_ada, b_ada.reshape(DEPTH, 1, N_MOD * D_MODEL))


def _seq_mod_index(seq_len, n_mod, tm):
    def idx(i):
        return jnp.minimum((i * tm) // seq_len, n_mod - 1)
    return idx


def _ffn_kernel(x_ref, mod_ref, ln_ref, wg_ref, wu_ref, wo_ref, fn_ref, o_ref, *, mod_base, final):
    x = x_ref[...]
    m = mod_ref[0]
    sh = m[mod_base:mod_base + 1]
    sc = m[mod_base + 1:mod_base + 2]
    g = m[mod_base + 2:mod_base + 3]
    h = _bf(_norm_mod(x, ln_ref[...], sh, sc))
    acc = None
    for c in range(N_FF_CHUNK):
        gate = _dot(h, wg_ref[c])
        up = _dot(h, wu_ref[c])
        part = _dot(_bf(_silu(gate) * up), wo_ref[c])
        acc = part if acc is None else acc + part
    y = x + 0.5 * g * acc
    if final:
        ms = jnp.mean(y * y, axis=-1, keepdims=True)
        y = y * lax.rsqrt(ms + EPS) * fn_ref[...]
    o_ref[...] = y


def ffn_half_step(x, mods, ln_w, w_in, w_out, final_w, *, mod_base, final, seq_len):
    t = x.shape[0]
    tm = ROW_TILE
    wg = _bf(w_in[:, :D_FF]).reshape(D_MODEL, N_FF_CHUNK, FF_CHUNK).transpose(1, 0, 2)
    wu = _bf(w_in[:, D_FF:]).reshape(D_MODEL, N_FF_CHUNK, FF_CHUNK).transpose(1, 0, 2)
    wo = _bf(w_out).reshape(N_FF_CHUNK, FF_CHUNK, D_MODEL)
    midx = _seq_mod_index(seq_len, mods.shape[0], tm)
    return pl.pallas_call(
        functools.partial(_ffn_kernel, mod_base=mod_base, final=final),
        grid=(t // tm,),
        in_specs=[
            pl.BlockSpec((tm, D_MODEL), lambda i: (i, 0)),
            pl.BlockSpec((1, N_MOD, D_MODEL), lambda i: (midx(i), 0, 0)),
            _const_spec((1, D_MODEL)),
            _const_spec((N_FF_CHUNK, D_MODEL, FF_CHUNK)),
            _const_spec((N_FF_CHUNK, D_MODEL, FF_CHUNK)),
            _const_spec((N_FF_CHUNK, FF_CHUNK, D_MODEL)),
            _const_spec((1, D_MODEL)),
        ],
        out_specs=pl.BlockSpec((tm, D_MODEL), lambda i: (i, 0)),
        out_shape=jax.ShapeDtypeStruct((t, D_MODEL), F32),
        compiler_params=_cparams(("parallel",)),
        name="ffn_half_step",
    )(x, mods, ln_w.reshape(1, D_MODEL), wg, wu, wo, final_w.reshape(1, D_MODEL))


PROJ_CHUNK = 256


def _proj_kernel(x_ref, mod_ref, ln_ref, w_ref, o_ref):
    x = x_ref[...]
    m = mod_ref[0]
    u = _bf(_norm_mod(x, ln_ref[...], m[3:4], m[4:5]))
    for c in range(N_IN // PROJ_CHUNK):
        sl = slice(c * PROJ_CHUNK, (c + 1) * PROJ_CHUNK)
        o_ref[:, sl] = _dot(u, w_ref[:, sl])


def mixer_in_proj(x, mods, ln_w, w_in, *, seq_len):
    t = x.shape[0]
    tm = ROW_TILE
    midx = _seq_mod_index(seq_len, mods.shape[0], tm)
    return pl.pallas_call(
        _proj_kernel,
        grid=(t // tm,),
        in_specs=[
            pl.BlockSpec((tm, D_MODEL), lambda i: (i, 0)),
            pl.BlockSpec((1, N_MOD, D_MODEL), lambda i: (midx(i), 0, 0)),
            _const_spec((1, D_MODEL)),
            _const_spec((D_MODEL, N_IN)),
        ],
        out_specs=pl.BlockSpec((tm, N_IN), lambda i: (i, 0)),
        out_shape=jax.ShapeDtypeStruct((t, N_IN), F32),
        compiler_params=_cparams(("parallel",)),
        name="mixer_in_proj",
    )(x, mods, ln_w.reshape(1, D_MODEL), _bf(w_in))


MERGE_CHUNK = 256


def _merge_kernel(x_ref, mod_ref, ln_ref, att_ref, lru_ref, hgf_ref, hgb_ref, hgg_ref, hgn_ref, headb_ref, hy_ref,
                  wg_ref, wa_ref, wl_ref, wh_ref, wy_ref, wo_ref, o_ref, mix_ref):
    x = x_ref[...]
    m = mod_ref[0]
    u = _bf(_norm_mod(x, ln_ref[...], m[3:4], m[4:5]))
    o = hgf_ref[0] + hgb_ref[0]
    sq = o * o
    sq_hi = _bf(sq)
    sq_lo = _bf(sq - sq_hi.astype(F32))
    ms = (_dot(sq_hi, headb_ref[...]) + _dot(sq_lo, headb_ref[...])) * (1.0 / HG_DV)
    hg = o * lax.rsqrt(ms + EPS) * hgn_ref[...] * _silu(hgg_ref[...])
    branches = (_bf(att_ref[...]), _bf(lru_ref[...]), _bf(hg), _bf(hy_ref[...]))
    w_bo = (wa_ref, wl_ref, wh_ref, wy_ref)
    for c in range(D_MODEL // MERGE_CHUNK):
        sl = slice(c * MERGE_CHUNK, (c + 1) * MERGE_CHUNK)
        mixed = None
        for n in range(N_BRANCH):
            term = _sigmoid(_dot(u, wg_ref[n, :, sl])) * _dot(branches[n], w_bo[n][:, sl])
            mixed = term if mixed is None else mixed + term
        mix_ref[:, sl] = _bf(mixed)
    o_ref[...] = x + m[5:6] * _dot(mix_ref[...], wo_ref[...])


def gated_merge(x, mods, ln_w, att, lru, hg_dirs, p2d, hg_norm, hy, w_gate, w_bo_attn, w_bo_lru, w_bo_hg, w_bo_hy,
                w_out, *, seq_len):
    t = x.shape[0]
    tm = ROW_TILE
    midx = _seq_mod_index(seq_len, mods.shape[0], tm)
    row = lambda w: pl.BlockSpec((tm, w), lambda i: (i, 0))
    _, _, headb = _hg_constants()
    return pl.pallas_call(
        _merge_kernel,
        grid=(t // tm,),
        in_specs=[
            row(D_MODEL),
            pl.BlockSpec((1, N_MOD, D_MODEL), lambda i: (midx(i), 0, 0)),
            _const_spec((1, D_MODEL)),
            row(D_ATTN), row(D_LRU),
            pl.BlockSpec((1, tm, D_HG), lambda i: (0, i, 0)),
            pl.BlockSpec((1, tm, D_HG), lambda i: (1, i, 0)),
            pl.BlockSpec((tm, D_HG), lambda i: (i, COL_HGATE // D_HG)),
            _const_spec((1, D_HG)),
            _const_spec((D_HG, D_HG)),
            row(D_HY),
            _const_spec((N_BRANCH, D_MODEL, D_MODEL)),
            _const_spec((D_ATTN, D_MODEL)),
            _const_spec((D_LRU, D_MODEL)),
            _const_spec((D_HG, D_MODEL)),
            _const_spec((D_HY, D_MODEL)),
            _const_spec((D_MODEL, D_MODEL)),
        ],
        out_specs=row(D_MODEL),
        out_shape=jax.ShapeDtypeStruct((t, D_MODEL), F32),
        scratch_shapes=[pltpu.VMEM((tm, D_MODEL), BF16)],
        compiler_params=_cparams(("parallel",)),
        name="gated_merge",
    )(x, mods, ln_w.reshape(1, D_MODEL), att, lru, hg_dirs, hg_dirs, p2d, hg_norm.reshape(1, D_HG), headb, hy,
      _bf(w_gate), _bf(w_bo_attn), _bf(w_bo_lru), _bf(w_bo_hg), _bf(w_bo_hy), _bf(w_out))


def _dot_nt(a, b):
    return lax.dot_general(a, b, (((1,), (1,)), ((), ())), preferred_element_type=F32)


def _ctx_attn_kernel(q_ref, k_ref, v_ref, sink_ref, o_ref):
    rows = q_ref.shape[1]
    blk = ATTN_BLOCK
    keys = _dup_heads(k_ref[0])
    vals_t = _values_t(v_ref[0])
    low = lax.broadcasted_iota(jnp.int32, (blk, 2 * HEAD_DIM), 1) < HEAD_DIM
    units = [(pair, qb) for qb in range(rows // blk) for pair in range(N_HEADS // 2)]

    def where(pair, qb):
        return slice(qb * blk, (qb + 1) * blk), slice(pair * 2 * HEAD_DIM, (pair + 1) * 2 * HEAD_DIM)

    scores = []
    for pair, qb in units:
        q_rows, lanes = where(pair, qb)
        q = q_ref[0, q_rows, lanes] * ATTN_SCALE
        q_heads = _bf(jnp.concatenate([jnp.where(low, q, 0.0), jnp.where(low, 0.0, q)], axis=0))
        scores.append(_dot_nt(keys[(2 * pair) // GQA_GROUP], q_heads))
    probs = []
    for (pair, qb), s in zip(units, scores):
        sink = jnp.concatenate([sink_ref[2 * pair:2 * pair + 1, :], sink_ref[2 * pair + 1:2 * pair + 2, :]], axis=1)
        m = jnp.maximum(jnp.max(s, axis=0, keepdims=True), sink)
        probs.append((_bf(jnp.exp(s - m)), jnp.exp(sink - m)))
    for (pair, qb), (p, sink_term) in zip(units, probs):
        q_rows, lanes = where(pair, qb)
        acc = _dot(vals_t[(2 * pair) // GQA_GROUP], p)
        out_t = acc[:HEAD_DIM, :] / (acc[HEAD_DIM:HEAD_DIM + 1, :] + sink_term)
        o_ref[0, q_rows, lanes] = jnp.transpose(jnp.concatenate([out_t[:, :blk], out_t[:, blk:]], axis=0))


def _sink_table(sink):
    return jnp.broadcast_to(sink.astype(F32)[:, None], (N_HEADS, 128))


def context_attention(p_ctx, sink):
    b, seq, _ = p_ctx.shape
    return pl.pallas_call(
        _ctx_attn_kernel,
        grid=(b,),
        in_specs=[
            pl.BlockSpec((1, seq, D_ATTN), lambda i: (i, 0, 0)),
            pl.BlockSpec((1, seq, D_KV), lambda i: (i, 0, COL_K // D_KV)),
            pl.BlockSpec((1, seq, D_KV), lambda i: (i, 0, COL_V // D_KV)),
            _const_spec((N_HEADS, 128)),
        ],
        out_specs=pl.BlockSpec((1, seq, D_ATTN), lambda i: (i, 0, 0)),
        out_shape=jax.ShapeDtypeStruct((b, seq, D_ATTN), F32),
        compiler_params=_cparams(("parallel",)),
        name="context_attention",
    )(p_ctx, p_ctx, p_ctx, _sink_table(sink))


def _rope(x, cos, sin_signed):
    n = x.shape[-1]
    lane = lax.broadcasted_iota(jnp.int32, x.shape, x.ndim - 1)
    first = (lane % (HEAD_DIM // 2)) < (HEAD_DIM // 4)
    partner = jnp.where(first, pltpu.roll(x, n - HEAD_DIM // 4, axis=x.ndim - 1),
                        pltpu.roll(x, HEAD_DIM // 4, axis=x.ndim - 1))
    return x * cos + partner * sin_signed


def _rope_tables(seq):
    rows = seq // GRID_W
    row = np.repeat(np.arange(rows, dtype=np.float32), GRID_W)
    col = np.tile(np.arange(GRID_W, dtype=np.float32), rows)
    n = HEAD_DIM // 4
    inv = (ROPE_BASE ** (-jnp.arange(n, dtype=F32) / n))
    ang_r = jnp.asarray(row)[:, None] * inv
    ang_c = jnp.asarray(col)[:, None] * inv
    cos = jnp.concatenate([jnp.cos(ang_r), jnp.cos(ang_r), jnp.cos(ang_c), jnp.cos(ang_c)], axis=-1)
    sin = jnp.concatenate([-jnp.sin(ang_r), jnp.sin(ang_r), -jnp.sin(ang_c), jnp.sin(ang_c)], axis=-1)
    return jnp.tile(cos, (1, 2)), jnp.tile(sin, (1, 2))


ATTN_PREP_ROWS = 512
ATTN_QBLOCKS = 4


def _dup_heads(x):
    lane = lax.broadcasted_iota(jnp.int32, x.shape, 1)
    swapped = pltpu.roll(x, HEAD_DIM, axis=1)
    low = lane < HEAD_DIM
    return _bf(jnp.where(low, x, swapped)), _bf(jnp.where(low, swapped, x))


def _values_t(x):
    xt = jnp.transpose(x)
    ones = jnp.ones((HEAD_DIM, x.shape[0]), F32)
    return (_bf(jnp.concatenate([xt[:HEAD_DIM], ones], axis=0)),
            _bf(jnp.concatenate([xt[HEAD_DIM:], ones], axis=0)))


def _lat_attn_kernel(q_ref, k_ref, v_ref, kc_ref, vc_ref, cos_ref, sin_ref, sink_ref, o_ref,
                     kd_ref, vt_ref, kcd_ref, vct_ref):
    seq = k_ref.shape[1]
    blk = ATTN_BLOCK
    band = 3 * blk
    past = kc_ref.shape[1]
    i = pl.program_id(1)

    @pl.when(i == 0)
    def _():
        for ti in range(seq // ATTN_PREP_ROWS):
            rows = slice(ti * ATTN_PREP_ROWS, (ti + 1) * ATTN_PREP_ROWS)
            k0, k1 = _dup_heads(_rope(k_ref[0, rows, :], cos_ref[rows, :], sin_ref[rows, :]))
            kd_ref[0, rows, :] = k0
            kd_ref[1, rows, :] = k1
            v0, v1 = _values_t(v_ref[0, rows, :])
            vt_ref[0, :, rows] = v0
            vt_ref[1, :, rows] = v1
        kc0, kc1 = _dup_heads(kc_ref[0])
        kcd_ref[0] = kc0
        kcd_ref[1] = kc1
        vc0, vc1 = _values_t(vc_ref[0])
        vct_ref[0] = vc0
        vct_ref[1] = vc1

    key = lax.broadcasted_iota(jnp.int32, (band + past, 2 * blk), 0)
    qry = lax.broadcasted_iota(jnp.int32, (band + past, 2 * blk), 1) % blk
    low = lax.broadcasted_iota(jnp.int32, (blk, 2 * HEAD_DIM), 1) < HEAD_DIM
    subs = []
    for sub in range(ATTN_QBLOCKS):
        ib = ATTN_QBLOCKS * i + sub
        start = pl.multiple_of(jnp.clip((ib - 1) * blk, 0, seq - band), blk)
        q0 = pl.multiple_of(ib * blk, blk)
        subs.append(dict(
            rows=slice(sub * blk, (sub + 1) * blk),
            cos=cos_ref[pl.ds(q0, blk), :], sin=sin_ref[pl.ds(q0, blk), :],
            mask=(jnp.abs((q0 + qry) - (start + key)) <= WINDOW) | (key >= band),
            keys=[jnp.concatenate([kd_ref[kvh, pl.ds(start, band), :], kcd_ref[kvh]], axis=0)
                  for kvh in range(N_KV_HEADS)],
            vals_t=[jnp.concatenate([vt_ref[kvh, :, pl.ds(start, band)], vct_ref[kvh]], axis=1)
                    for kvh in range(N_KV_HEADS)]))
    units = [(sub, pair) for sub in range(ATTN_QBLOCKS) for pair in range(N_HEADS // 2)]
    scores = []
    for sub, pair in units:
        u = subs[sub]
        lanes = slice(pair * 2 * HEAD_DIM, (pair + 1) * 2 * HEAD_DIM)
        q = _rope(q_ref[0, u['rows'], lanes], u['cos'], u['sin']) * ATTN_SCALE
        q_heads = _bf(jnp.concatenate([jnp.where(low, q, 0.0), jnp.where(low, 0.0, q)], axis=0))
        scores.append(_dot_nt(u['keys'][(2 * pair) // GQA_GROUP], q_heads))
    probs = []
    for (sub, pair), raw in zip(units, scores):
        s = jnp.where(subs[sub]['mask'], raw, -jnp.inf)
        sink = jnp.concatenate([sink_ref[2 * pair:2 * pair + 1, :], sink_ref[2 * pair + 1:2 * pair + 2, :]], axis=1)
        m = jnp.maximum(jnp.max(s, axis=0, keepdims=True), sink)
        probs.append((_bf(jnp.exp(s - m)), jnp.exp(sink - m)))
    for (sub, pair), (p, sink_term) in zip(units, probs):
        u = subs[sub]
        lanes = slice(pair * 2 * HEAD_DIM, (pair + 1) * 2 * HEAD_DIM)
        acc = _dot(u['vals_t'][(2 * pair) // GQA_GROUP], p)
        out_t = acc[:HEAD_DIM, :] / (acc[HEAD_DIM:HEAD_DIM + 1, :] + sink_term)
        o_ref[0, u['rows'], lanes] = jnp.transpose(jnp.concatenate([out_t[:, :blk], out_t[:, blk:]], axis=0))


def latent_attention(p_lat, k_ctx, v_ctx, sink):
    b, seq, _ = p_lat.shape
    past = k_ctx.shape[1]
    cos, sin = _rope_tables(seq)
    return pl.pallas_call(
        _lat_attn_kernel,
        grid=(b, seq // (ATTN_QBLOCKS * ATTN_BLOCK)),
        in_specs=[
            pl.BlockSpec((1, ATTN_QBLOCKS * ATTN_BLOCK, D_ATTN), lambda bi, i: (bi, i, 0)),
            pl.BlockSpec((1, seq, D_KV), lambda bi, i: (bi, 0, COL_K // D_KV)),
            pl.BlockSpec((1, seq, D_KV), lambda bi, i: (bi, 0, COL_V // D_KV)),
            pl.BlockSpec((1, past, D_KV), lambda bi, i: (bi, 0, 0)),
            pl.BlockSpec((1, past, D_KV), lambda bi, i: (bi, 0, 0)),
            _const_spec((seq, 128)),
            _const_spec((seq, 128)),
            _const_spec((N_HEADS, 128)),
        ],
        out_specs=pl.BlockSpec((1, ATTN_QBLOCKS * ATTN_BLOCK, D_ATTN), lambda bi, i: (bi, i, 0)),
        out_shape=jax.ShapeDtypeStruct((b, seq, D_ATTN), F32),
        scratch_shapes=[
            pltpu.VMEM((N_KV_HEADS, seq, D_KV), BF16),
            pltpu.VMEM((N_KV_HEADS, D_KV, seq), BF16),
            pltpu.VMEM((N_KV_HEADS, past, D_KV), BF16),
            pltpu.VMEM((N_KV_HEADS, D_KV, past), BF16),
        ],
        compiler_params=_cparams(("parallel", "arbitrary")),
        name="latent_attention",
    )(p_lat, p_lat, p_lat, k_ctx, v_ctx, cos, sin, _sink_table(sink))


LRU_HALF = D_LRU // 2
LRU_ROWS = 256
LRU_PAD = 8


def _neg_expm1(y, exp_y):
    return jnp.tanh(-0.5 * y) * (exp_y + 1.0)


def _softplus(x):
    return jnp.maximum(x, 0.0) + jnp.log1p(jnp.exp(-jnp.abs(x)))


def _gelu_tanh(x):
    return 0.5 * x * (1.0 + jnp.tanh(math.sqrt(2.0 / math.pi) * (x + 0.044715 * (x * x * x))))


LRU_SCAN = 8


def _affine_scan8(a, b, *, reverse):
    row = lax.broadcasted_iota(jnp.int32, a.shape, 0)
    k = 1
    while k < LRU_SCAN:
        if reverse:
            valid = row < LRU_SCAN - k
            shift = LRU_SCAN - k
        else:
            valid = row >= k
            shift = k
        a_prev = jnp.where(valid, pltpu.roll(a, shift, axis=0), 1.0)
        b_prev = jnp.where(valid, pltpu.roll(b, shift, axis=0), 0.0)
        b = b + a * b_prev
        a = a * a_prev
        k *= 2
    return a, b


def _lru_kernel(x_ref, g_ref, cw_ref, cb_ref, wr_ref, br_ref, wi_ref, bi_ref, lam_ref, h0_ref,
                o_ref, st_ref, pad_ref, a_ref, bx_ref, hb_ref):
    seq = x_ref.shape[1]
    zeros = jnp.zeros((LRU_PAD, LRU_HALF), F32)
    pad_ref[0:LRU_PAD, :] = zeros
    pad_ref[LRU_PAD + seq:2 * LRU_PAD + seq, :] = zeros
    pad_ref[LRU_PAD:LRU_PAD + seq, :] = x_ref[0]
    left = LRU_CONV // 2
    for ti in range(seq // LRU_ROWS):
        r0 = ti * LRU_ROWS
        xc = cb_ref[...]
        for k in range(LRU_CONV):
            xc = xc + cw_ref[k:k + 1, :] * pad_ref[r0 + LRU_PAD - left + k:r0 + LRU_PAD - left + k + LRU_ROWS, :]
        xcb = _bf(xc)
        for d in range(2):
            r = _sigmoid(_dot(xcb, wr_ref[d]) + br_ref[d:d + 1, :])
            i = _sigmoid(_dot(xcb, wi_ref[d]) + bi_ref[d:d + 1, :])
            log_a = (-LRU_C) * r * _softplus(-lam_ref[d:d + 1, :])
            a = jnp.exp(log_a)
            a_ref[d, r0:r0 + LRU_ROWS, :] = a
            bx_ref[d, r0:r0 + LRU_ROWS, :] = jnp.sqrt(_neg_expm1(2.0 * log_a, a * a)) * (i * xc)

    nt = seq // LRU_SCAN

    def step(blk, carry):
        hf, hb = carry
        rf = pl.ds(pl.multiple_of(blk * LRU_SCAN, LRU_SCAN), LRU_SCAN)
        rb = pl.ds(pl.multiple_of((nt - 1 - blk) * LRU_SCAN, LRU_SCAN), LRU_SCAN)
        af, bf_ = _affine_scan8(a_ref[0, rf, :], bx_ref[0, rf, :], reverse=False)
        ab, bb = _affine_scan8(a_ref[1, rb, :], bx_ref[1, rb, :], reverse=True)
        tile_f = af * hf + bf_
        tile_b = ab * hb + bb
        o_ref[0, rf, :] = tile_f
        hb_ref[rb, :] = tile_b
        return tile_f[LRU_SCAN - 1:LRU_SCAN, :], tile_b[0:1, :]

    hf, hb = lax.fori_loop(0, nt, step, (h0_ref[0, 0:1, :], h0_ref[0, 1:2, :]), unroll=2)
    st_ref[0, 0:1, :] = hf
    st_ref[0, 1:2, :] = hb
    for ti in range(seq // LRU_ROWS):
        rows = slice(ti * LRU_ROWS, (ti + 1) * LRU_ROWS)
        o_ref[0, rows, :] = (o_ref[0, rows, :] + hb_ref[rows, :]) * _gelu_tanh(g_ref[0, rows, :])


def _block_diag(w):
    n, blk, _ = w.shape
    eye = jnp.eye(n, dtype=w.dtype)
    return (eye[:, None, :, None] * w[:, :, None, :]).reshape(n * blk, n * blk)


def rglru_mixer(p, conv_w, conv_b, w_r, b_r, w_i, b_i, lam, h0):
    b, seq, _ = p.shape
    wr = _bf(jnp.stack([_block_diag(w_r[d]) for d in range(2)]))
    wi = _bf(jnp.stack([_block_diag(w_i[d]) for d in range(2)]))
    hw = LRU_HALF
    vec = lambda rows: pl.BlockSpec((rows, hw), lambda bi, h: (0, h))
    return pl.pallas_call(
        _lru_kernel,
        grid=(b, 2),
        in_specs=[
            pl.BlockSpec((1, seq, hw), lambda bi, h: (bi, 0, COL_RX // hw + h)),
            pl.BlockSpec((1, seq, hw), lambda bi, h: (bi, 0, COL_RG // hw + h)),
            vec(LRU_CONV), vec(1),
            pl.BlockSpec((2, hw, hw), lambda bi, h: (0, h, h)), vec(2),
            pl.BlockSpec((2, hw, hw), lambda bi, h: (0, h, h)), vec(2),
            vec(2),
            pl.BlockSpec((1, 2, hw), lambda bi, h: (bi, 0, h)),
        ],
        out_specs=[
            pl.BlockSpec((1, seq, hw), lambda bi, h: (bi, 0, h)),
            pl.BlockSpec((1, 2, hw), lambda bi, h: (bi, 0, h)),
        ],
        out_shape=[jax.ShapeDtypeStruct((b, seq, D_LRU), F32), jax.ShapeDtypeStruct((b, 2, D_LRU), F32)],
        scratch_shapes=[
            pltpu.VMEM((seq + 2 * LRU_PAD, hw), F32),
            pltpu.VMEM((2, seq, hw), F32),
            pltpu.VMEM((2, seq, hw), F32),
            pltpu.VMEM((seq, hw), F32),
        ],
        compiler_params=_cparams(("parallel", "parallel")),
        name="rglru_mixer",
    )(p, p, conv_w, conv_b.reshape(1, D_LRU), wr, b_r, wi, b_i, lam, h0)


HG_TILE = 256
HG_GROUP = 32
HG_AHEAD = 1
HG_NBUF = 2
LOG2_E = math.log2(math.e)


def _hg_constants():
    idx = np.arange(HG_TILE)
    same_blk = (idx[:, None] // HG_BLK) == (idx[None, :] // HG_BLK)
    tri_f = same_blk & (idx[None, :] <= idx[:, None])
    tri_b = same_blk & (idx[None, :] >= idx[:, None])
    head = (idx[:, None] // HG_DK) == (idx[None, :] // HG_DK)
    return jnp.asarray(np.stack([tri_f, tri_b]), BF16), jnp.asarray(same_blk, BF16), jnp.asarray(head, BF16)


def _hgrn_kernel(q_ref, z_ref, v_ref, lb_ref, tri_ref, ones_ref, headb_ref, s0_ref,
                 o_ref, st_ref, state, *scratch):
    att_bufs = scratch[0:HG_NBUF]
    upd_bufs = scratch[HG_NBUF:2 * HG_NBUF]
    qst_bufs = scratch[2 * HG_NBUF:3 * HG_NBUF]
    qs, srcs, vs, cums, qds, kns, tots = scratch[3 * HG_NBUF:]
    seg_rows = q_ref.shape[1]
    d = pl.program_id(1)
    s = pl.program_id(2)
    nseg = pl.num_programs(2)

    @pl.when(s == 0)
    def _():
        state[...] = jnp.concatenate([jnp.transpose(s0_ref[0, 0, h]) for h in range(HG_HEADS)], axis=1)

    lb = lb_ref[0]
    for ti in range(seg_rows // HG_TILE):
        rows = slice(ti * HG_TILE, (ti + 1) * HG_TILE)
        f = lb + (1.0 - lb) * _sigmoid(z_ref[0, rows, :])
        g = jnp.log(f)
        kk = 1.0 - f
        q = _silu(q_ref[0, rows, :]) * HG_SCALE
        cum = _dot_exact_rhs(tri_ref[0], g)
        tot = _dot_exact_rhs(ones_ref[...], g)
        qs[rows, :] = q
        srcs[rows, :] = (cum - jnp.log(kk)) * LOG2_E
        vs[rows, :] = v_ref[0, rows, :]
        cums[rows, :] = cum * LOG2_E
        tots[rows, :] = tot
        qds[rows, :] = q * jnp.exp(cum)
        kns[rows, :] = kk * jnp.exp(tot - cum)

    nblk = seg_rows // HG_BLK
    sgn = jnp.where(d == 0, 1, -1)
    t_signed = [(lax.broadcasted_iota(jnp.int32, (8, D_HG), 0) + 8 * h) * sgn for h in range(HG_BLK // 8)]
    lane_head_blk = lax.broadcasted_iota(jnp.int32, (HG_BLK, D_HG), 1) // HG_DK
    lane_head_dv = lax.broadcasted_iota(jnp.int32, (HG_DV, D_HG), 1) // HG_DK

    n_half = HG_BLK // 8

    def block_rows(j):
        je = jnp.where(d == 0, j, nblk - 1 - j)
        return pl.multiple_of(je * HG_BLK, HG_BLK)

    def prepare(j, slot):
        r0 = block_rows(j)
        rows = pl.ds(r0, HG_BLK)
        cum_h = [cums[pl.ds(r0 + 8 * h, 8), :] for h in range(n_half)]
        q_h = [qs[pl.ds(r0 + 8 * h, 8), :] for h in range(n_half)]
        tiles = []
        for src in range(HG_BLK):
            src_row = srcs[pl.ds(r0 + src, 1), :]
            for h in range(n_half):
                pair = jnp.exp2(cum_h[h] - src_row) * q_h[h]
                tiles.append(jnp.where(t_signed[h] >= src * sgn, pair, 0.0))
        att_bufs[slot][...] = _dot(_bf(jnp.concatenate(tiles, axis=0)), headb_ref[...])
        upd = lax.dot_general(_bf(vs[rows, :]), _bf(kns[rows, :]), (((0,), (0,)), ((), ())),
                              preferred_element_type=F32)
        upd_fold = jnp.where(lane_head_dv == 0, upd[0:HG_DV], 0.0)
        for h in range(1, HG_HEADS):
            upd_fold = upd_fold + jnp.where(lane_head_dv == h, upd[h * HG_DV:(h + 1) * HG_DV], 0.0)
        upd_bufs[slot][...] = upd_fold
        qd = qds[rows, :]
        qst_bufs[slot][...] = _bf(jnp.concatenate(
            [jnp.where(lane_head_blk == h, qd, 0.0) for h in range(HG_HEADS)], axis=0))

    def finish(j, slot, st):
        r0 = block_rows(j)
        rows = pl.ds(r0, HG_BLK)
        o_halves = [None] * n_half
        for src in range(HG_BLK):
            v_row = vs[pl.ds(r0 + src, 1), :]
            for h in range(n_half):
                term = att_bufs[slot][src * HG_BLK + h * 8:src * HG_BLK + (h + 1) * 8, :] * v_row
                o_halves[h] = term if o_halves[h] is None else o_halves[h] + term
        by_head = _dot_nt(qst_bufs[slot][...], _bf(st))
        o_state = jnp.concatenate([by_head[h * HG_BLK:(h + 1) * HG_BLK] for h in range(HG_HEADS)], axis=1)
        o_ref[0, 0, rows, :] = jnp.concatenate(o_halves, axis=0) + o_state
        return jnp.exp(tots[pl.ds(r0, 1), :]) * st + upd_bufs[slot][...]

    for k in range(HG_AHEAD):
        prepare(k, k)

    group = min(HG_GROUP, nblk)

    def block_group(i, st):
        j = group * i
        for k in range(group):
            prepare(jnp.minimum(j + k + HG_AHEAD, nblk - 1), (k + HG_AHEAD) % HG_NBUF)
            st = finish(j + k, k % HG_NBUF, st)
        return st

    if group == nblk:
        st = state[...]
        for k in range(nblk):
            if k + HG_AHEAD < nblk:
                prepare(k + HG_AHEAD, (k + HG_AHEAD) % HG_NBUF)
            st = finish(k, k % HG_NBUF, st)
        state[...] = st
    else:
        state[...] = lax.fori_loop(0, nblk // group, block_group, state[...])

    @pl.when(s == nseg - 1)
    def _():
        for h in range(HG_HEADS):
            st_ref[0, 0, h] = jnp.transpose(state[:, h * HG_DK:(h + 1) * HG_DK])


def hgrn2_scan(p, lb, s0, seg_rows):
    b, seq, _ = p.shape
    nseg = seq // seg_rows
    tri, ones, headb = _hg_constants()
    seg = lambda d, s: jnp.where(d == 0, s, nseg - 1 - s)
    col = lambda c: pl.BlockSpec((1, seg_rows, D_HG), lambda bi, d, s: (bi, seg(d, s), c // D_HG))
    scr = lambda: pltpu.VMEM((seg_rows, D_HG), F32)
    return pl.pallas_call(
        _hgrn_kernel,
        grid=(b, 2, nseg),
        in_specs=[
            col(COL_HQ),
            pl.BlockSpec((1, seg_rows, D_HG), lambda bi, d, s: (bi, seg(d, s), COL_HFF // D_HG + d)),
            col(COL_HI),
            pl.BlockSpec((1, 1, D_HG), lambda bi, d, s: (d, 0, 0)),
            pl.BlockSpec((1, HG_TILE, HG_TILE), lambda bi, d, s: (d, 0, 0)),
            _const_spec((HG_TILE, HG_TILE)),
            _const_spec((D_HG, D_HG)),
            pl.BlockSpec((1, 1, HG_HEADS, HG_DK, HG_DV), lambda bi, d, s: (bi, d, 0, 0, 0)),
        ],
        out_specs=[
            pl.BlockSpec((1, 1, seg_rows, D_HG), lambda bi, d, s: (d, bi, seg(d, s), 0)),
            pl.BlockSpec((1, 1, HG_HEADS, HG_DK, HG_DV), lambda bi, d, s: (bi, d, 0, 0, 0)),
        ],
        out_shape=[jax.ShapeDtypeStruct((2, b, seq, D_HG), F32),
                   jax.ShapeDtypeStruct((b, 2, HG_HEADS, HG_DK, HG_DV), F32)],
        scratch_shapes=[pltpu.VMEM((HG_DV, D_HG), F32)]
        + [pltpu.VMEM((HG_BLK * HG_BLK, D_HG), F32) for _ in range(HG_NBUF)]
        + [pltpu.VMEM((HG_DV, D_HG), F32) for _ in range(HG_NBUF)]
        + [pltpu.VMEM((HG_HEADS * HG_BLK, D_HG), BF16) for _ in range(HG_NBUF)] + [scr() for _ in range(7)],
        compiler_params=_cparams(("parallel", "parallel", "arbitrary")),
        name="hgrn2_scan",
    )(p, p, p, lb.reshape(2, 1, D_HG), tri, ones, headb, s0)


HY_COLS = 2 * HY_ORDER * D_HY
HY_FEAT_PAD = 128
HY_GEN_ROWS = 256


def _dot_hi(a, b):
    ah = _bf(a)
    al = _bf(a - ah.astype(F32))
    bh = _bf(b)
    bl = _bf(b - bh.astype(F32))
    return _dot(ah, bh) + _dot(ah, bl) + _dot(al, bh)


HY_HALF_COLS = HY_ORDER * D_HY


def _filter_gen_kernel(feat_ref, w1_ref, b1_ref, w2_ref, b2_ref, w3_ref, fr_ref, dl_ref, h_ref, ssq_ref, *, seq):
    i = pl.program_id(0)
    tm = feat_ref.shape[0]
    feat = feat_ref[...]
    fr = fr_ref[...]
    h = jnp.sin(fr * (_dot_hi(feat, w1_ref[...]) + b1_ref[...]))
    h = jnp.sin(fr * (_dot_hi(h, w2_ref[...]) + b2_ref[...]))
    h = _dot_hi(h, w3_ref[...])
    decay = jnp.exp(-feat[:, 0:1] * dl_ref[...])
    decay = jnp.concatenate([decay] * HY_ORDER, axis=1)
    row = i * tm + lax.broadcasted_iota(jnp.int32, (tm, HY_HALF_COLS), 0)
    taps = h * decay
    h_ref[...] = jnp.where(row == seq, 0.0, taps)

    @pl.when(i == 0)
    def _():
        ssq_ref[...] = jnp.zeros_like(ssq_ref)

    ssq_ref[...] += jnp.sum(taps * taps, axis=0, keepdims=True)


def _hyena_features(seq):
    order = np.concatenate([np.arange(seq), [0], np.arange(seq - 1, 0, -1)])
    pos = order.astype(np.float32)
    t = pos / np.float32(max(seq - 1, 1))
    bands = np.linspace(1e-4, HY_BANDS - 1, HY_BANDS, dtype=np.float32)
    ang = np.float32(2.0 * math.pi / seq) * pos[:, None] * bands[None, :]
    feat = np.concatenate([t[:, None], np.cos(ang), np.sin(ang)], axis=-1).astype(np.float32)
    out = np.zeros((2 * seq, HY_FEAT_PAD), np.float32)
    out[:, :HY_EMB] = feat
    return jnp.asarray(out)


def hyena_filter_gen(seq, w1, b1, w2, b2, w3, freq):
    feat = _hyena_features(seq)
    w1p = jnp.zeros((HY_FEAT_PAD, HY_FFN), F32).at[:HY_EMB].set(w1.astype(F32))
    deltas = jnp.asarray(np.linspace(HY_DECAY_SLOW, HY_DECAY_FAST, D_HY, dtype=np.float32)).reshape(1, D_HY)
    tm = HY_GEN_ROWS
    return pl.pallas_call(
        functools.partial(_filter_gen_kernel, seq=seq),
        grid=(2 * seq // tm,),
        in_specs=[
            pl.BlockSpec((tm, HY_FEAT_PAD), lambda i: (i, 0)),
            _const_spec((HY_FEAT_PAD, HY_FFN)), _const_spec((1, HY_FFN)),
            _const_spec((HY_FFN, HY_FFN)), _const_spec((1, HY_FFN)),
            pl.BlockSpec((HY_FFN, HY_HALF_COLS), lambda i: (0, (i * tm) // seq)),
            _const_spec((1, HY_FFN)), _const_spec((1, D_HY)),
        ],
        out_specs=[pl.BlockSpec((tm, HY_HALF_COLS), lambda i: (i, 0)), _const_spec((1, HY_HALF_COLS))],
        out_shape=[jax.ShapeDtypeStruct((2 * seq, HY_HALF_COLS), F32),
                   jax.ShapeDtypeStruct((1, HY_HALF_COLS), F32)],
        compiler_params=_cparams(("arbitrary",)),
        name="hyena_filter_gen",
    )(feat, w1p, b1.reshape(1, HY_FFN), w2, b2.reshape(1, HY_FFN), w3, freq.reshape(1, HY_FFN), deltas)


def _filter_norm(ssq_ref, n):
    return lax.rsqrt(ssq_ref[:, n * D_HY:(n + 1) * D_HY])


def _conv3(pad_ref, w_ref, b_ref, r0, rows, pad):
    acc = b_ref[...]
    for k in range(HY_SHORT):
        lo = r0 + pad - HY_SHORT // 2 + k
        acc = acc + w_ref[k:k + 1, :] * pad_ref[lo:lo + rows, :]
    return acc


def _dft_constants(seq):
    n = 2 * seq
    k = np.arange(n)
    ang = 2.0 * np.pi * np.outer(k, k) / n
    fwd = np.concatenate([np.cos(ang), -np.sin(ang)], axis=0)
    inv = np.concatenate([np.cos(ang[:seq]), -np.sin(ang[:seq])], axis=1) / n
    return jnp.asarray(fwd, BF16), jnp.asarray(inv, BF16)


def _ctx_spectrum_kernel(filt_ref, ssq_ref, fwd_ref, h_ref):
    n = pl.program_id(0)
    norm = jnp.where(n == 0, _filter_norm(ssq_ref, 0), _filter_norm(ssq_ref, 1))
    x = filt_ref[...] * norm
    xh = _bf(x)
    xl = _bf(x - xh.astype(F32))
    h_ref[0] = _dot(fwd_ref[...], xh) + _dot(fwd_ref[...], xl)


def hyena_ctx_spectrum(filt, ssq):
    n = filt.shape[0]
    fwd, _ = _dft_constants(n // 2)
    return pl.pallas_call(
        _ctx_spectrum_kernel,
        grid=(HY_ORDER,),
        in_specs=[pl.BlockSpec((n, D_HY), lambda o: (0, o)), _const_spec((1, HY_HALF_COLS)),
                  _const_spec((2 * n, n))],
        out_specs=pl.BlockSpec((1, 2 * n, D_HY), lambda o: (o, 0, 0)),
        out_shape=jax.ShapeDtypeStruct((HY_ORDER, 2 * n, D_HY), F32),
        compiler_params=_cparams(("parallel",)),
        name="hyena_ctx_spectrum",
    )(filt, ssq, fwd)


HY_PAD = 8


def _hyena_ctx_kernel(v_ref, x1_ref, x2_ref, cw_ref, cb_ref, bias_ref, spec_ref, fwd_ref, inv_ref, o_ref, pad_ref):
    seq = v_ref.shape[1]
    n = 2 * seq
    zeros = jnp.zeros((HY_PAD, D_HY), F32)
    pad_ref[0:HY_PAD, :] = zeros
    pad_ref[HY_PAD + seq:2 * HY_PAD + seq, :] = zeros

    def short_conv(src_ref, part):
        pad_ref[HY_PAD:HY_PAD + seq, :] = src_ref[0]
        return _conv3(pad_ref, cw_ref.at[part], cb_ref.at[part], 0, seq, HY_PAD)

    z = short_conv(v_ref, 0)
    for order, gate_ref in enumerate((x1_ref, x2_ref)):
        spec = _dot(fwd_ref[:, 0:seq], _bf(z))
        xr, xi = spec[:n], spec[n:]
        hr, hi = spec_ref[order, 0:n, :], spec_ref[order, n:2 * n, :]
        prod = jnp.concatenate([xr * hr - xi * hi, xr * hi + xi * hr], axis=0)
        conv = _dot(inv_ref[...], _bf(prod))
        z = short_conv(gate_ref, order + 1) * (conv + bias_ref[order:order + 1, :] * z)
    o_ref[0] = z


def hyena_ctx(p, conv_w, conv_b, bias, spec):
    b, seq, _ = p.shape
    fwd, inv = _dft_constants(seq)
    col = lambda c: pl.BlockSpec((1, seq, D_HY), lambda i: (i, 0, c))
    c0 = COL_HY // D_HY
    return pl.pallas_call(
        _hyena_ctx_kernel,
        grid=(b,),
        in_specs=[
            col(c0), col(c0 + 1), col(c0 + 2),
            _const_spec((3, HY_SHORT, D_HY)), _const_spec((3, 1, D_HY)), _const_spec((HY_ORDER, D_HY)),
            _const_spec((HY_ORDER, 4 * seq, D_HY)), _const_spec((4 * seq, 2 * seq)), _const_spec((seq, 4 * seq)),
        ],
        out_specs=pl.BlockSpec((1, seq, D_HY), lambda i: (i, 0, 0)),
        out_shape=jax.ShapeDtypeStruct((b, seq, D_HY), F32),
        scratch_shapes=[pltpu.VMEM((seq + 2 * HY_PAD, D_HY), F32)],
        compiler_params=_cparams(("parallel",)),
        name="hyena_ctx",
    )(p, p, p, conv_w.reshape(HY_SHORT, 3, D_HY).transpose(1, 0, 2), conv_b.reshape(3, 1, D_HY), bias, spec,
      fwd, inv)


FFT_N1 = 64
FFT_N2 = 128
FFT_N = FFT_N1 * FFT_N2
FFT_K1 = FFT_N1 // 2 + 1
FFT_K1_PAD = 40
FFT_ROWS = FFT_K1_PAD * FFT_N2
HY_LANES = 128
FFT_UNROLL1 = 16
FFT_UNROLL2 = 11


def _fft_constants():
    n1, n2, n, kp = FFT_N1, FFT_N2, FFT_N, FFT_K1_PAD
    k1 = np.arange(kp)
    w1 = np.exp(-2j * np.pi * np.outer(k1, np.arange(n1)) / n1)
    tw = np.exp(-2j * np.pi * np.outer(np.arange(n2), k1) / n)
    f1c = tw[:, :, None] * w1[None, :, :]
    f1t = np.concatenate([f1c.real, f1c.imag], axis=1).transpose(0, 2, 1)
    w2 = np.exp(-2j * np.pi * np.outer(np.arange(n2), np.arange(n2)) / n2)
    f2 = np.block([[w2.real, -w2.imag], [w2.imag, w2.real]])
    g2c = np.conj(w2)
    g2 = np.block([[g2c.real, -g2c.imag], [g2c.imag, g2c.real]])
    weight = np.where((k1 == 0) | (k1 == n1 // 2), 1.0, 2.0) * (k1 < FFT_K1)
    mc = np.conj(w1.T)[None, :n1 // 2, :] * np.conj(tw)[:, None, :] * weight / n
    g1 = np.concatenate([mc.real, -mc.imag], axis=2)
    return (jnp.asarray(f1t, BF16), jnp.asarray(f2, BF16), jnp.asarray(g2, BF16), jnp.asarray(g1, BF16))


def _dot_tn(a, b):
    return lax.dot_general(a, b, (((0,), (0,)), ((), ())), preferred_element_type=F32)


def _fft_stage1(x_ref, f1t_ref, a_ref, n1_in):
    def body(n2, carry):
        rows = x_ref[pl.ds(n2, n1_in, stride=FFT_N2), :]
        out = _dot_tn(f1t_ref[n2, 0:n1_in, :], _bf(rows))
        a_ref[0, pl.ds(n2, FFT_K1, stride=FFT_N2), :] = out[:FFT_K1]
        a_ref[1, pl.ds(n2, FFT_K1, stride=FFT_N2), :] = out[FFT_K1_PAD:FFT_K1_PAD + FFT_K1]
        return carry
    lax.fori_loop(0, FFT_N2, body, 0, unroll=FFT_UNROLL1)


def _lat_spectrum_kernel(filt_ref, ssq_ref, f1t_ref, f2_ref, h_ref, a_ref):
    n = pl.program_id(0)
    half = pl.program_id(1)
    norm_full = jnp.where(n == 0, _filter_norm(ssq_ref, 0), _filter_norm(ssq_ref, 1))
    norm = jnp.where(half == 0, norm_full[:, :HY_LANES], norm_full[:, HY_LANES:])
    _fft_stage1(filt_ref, f1t_ref, a_ref, FFT_N1)

    def stage2(k1, carry):
        r0 = pl.multiple_of(k1 * FFT_N2, FFT_N2)
        rows = pl.ds(r0, FFT_N2)
        a = jnp.concatenate([a_ref[0, rows, :], a_ref[1, rows, :]], axis=0)
        b = _dot(f2_ref[...], _bf(a)) * norm
        h_ref[0, 0, rows, :] = _bf(b[:FFT_N2])
        h_ref[0, 1, rows, :] = _bf(b[FFT_N2:])
        return carry
    lax.fori_loop(0, FFT_K1, stage2, 0)


def hyena_lat_spectrum(filt, ssq):
    f1t, f2, _, _ = _fft_constants()
    rows = FFT_K1 * FFT_N2
    halves = D_HY // HY_LANES
    return pl.pallas_call(
        _lat_spectrum_kernel,
        grid=(HY_ORDER, halves),
        in_specs=[
            pl.BlockSpec((FFT_N, HY_LANES), lambda o, h: (0, o * halves + h)),
            _const_spec((1, HY_HALF_COLS)),
            _const_spec((FFT_N2, FFT_N1, 2 * FFT_K1_PAD)),
            _const_spec((2 * FFT_N2, 2 * FFT_N2)),
        ],
        out_specs=pl.BlockSpec((1, 2, rows, HY_LANES), lambda o, h: (o, 0, 0, h)),
        out_shape=jax.ShapeDtypeStruct((HY_ORDER, 2, rows, D_HY), BF16),
        scratch_shapes=[pltpu.VMEM((2, FFT_ROWS, HY_LANES), F32)],
        compiler_params=_cparams(("parallel", "parallel")),
        name="hyena_lat_spectrum",
    )(filt, ssq, f1t, f2)


HY_ROWS = 512


def _hyena_lat_kernel(v_ref, x1_ref, x2_ref, cw_ref, cb_ref, bias_ref, spec_ref, f1t_ref, f2_ref, g2_ref, g1_ref,
                      o_ref, pad_ref, z_ref, a_ref):
    seq = v_ref.shape[1]
    zeros = jnp.zeros((HY_PAD, HY_LANES), F32)
    pad_ref[0:HY_PAD, :] = zeros
    pad_ref[HY_PAD + seq:2 * HY_PAD + seq, :] = zeros
    pad_ref[HY_PAD:HY_PAD + seq, :] = v_ref[0]
    for ti in range(seq // HY_ROWS):
        r0 = ti * HY_ROWS
        z_ref[r0:r0 + HY_ROWS, :] = _conv3(pad_ref, cw_ref.at[0], cb_ref.at[0], r0, HY_ROWS, HY_PAD)
    tail = jnp.zeros((FFT_ROWS - FFT_K1 * FFT_N2, HY_LANES), F32)
    a_ref[0, FFT_K1 * FFT_N2:FFT_ROWS, :] = tail
    a_ref[1, FFT_K1 * FFT_N2:FFT_ROWS, :] = tail

    for order, gate_ref in enumerate((x1_ref, x2_ref)):
        _fft_stage1(z_ref, f1t_ref, a_ref, FFT_N1 // 2)

        def stage2(k1, carry):
            r0 = pl.multiple_of(k1 * FFT_N2, FFT_N2)
            rows = pl.ds(r0, FFT_N2)
            a = jnp.concatenate([a_ref[0, rows, :], a_ref[1, rows, :]], axis=0)
            b = _dot(f2_ref[...], _bf(a))
            br, bi = b[:FFT_N2], b[FFT_N2:]
            hr = spec_ref[order, 0, rows, :].astype(F32)
            hi = spec_ref[order, 1, rows, :].astype(F32)
            y = jnp.concatenate([br * hr - bi * hi, br * hi + bi * hr], axis=0)
            c = _dot(g2_ref[...], _bf(y))
            a_ref[0, rows, :] = c[:FFT_N2]
            a_ref[1, rows, :] = c[FFT_N2:]
            return carry
        lax.fori_loop(0, FFT_K1, stage2, 0, unroll=FFT_UNROLL2)

        def stage1_inv(n2, carry):
            c = jnp.concatenate([a_ref[0, pl.ds(n2, FFT_K1_PAD, stride=FFT_N2), :],
                                 a_ref[1, pl.ds(n2, FFT_K1_PAD, stride=FFT_N2), :]], axis=0)
            o_ref[0, pl.ds(n2, FFT_N1 // 2, stride=FFT_N2), :] = _dot(g1_ref[n2], _bf(c))
            return carry
        lax.fori_loop(0, FFT_N2, stage1_inv, 0, unroll=FFT_UNROLL1)

        pad_ref[HY_PAD:HY_PAD + seq, :] = gate_ref[0]
        for ti in range(seq // HY_ROWS):
            r0 = ti * HY_ROWS
            rows = slice(r0, r0 + HY_ROWS)
            gate = _conv3(pad_ref, cw_ref.at[order + 1], cb_ref.at[order + 1], r0, HY_ROWS, HY_PAD)
            z = gate * (o_ref[0, rows, :] + bias_ref[order:order + 1, :] * z_ref[rows, :])
            if order + 1 < HY_ORDER:
                z_ref[rows, :] = z
            else:
                o_ref[0, rows, :] = z


def hyena_lat(p, conv_w, conv_b, bias, spec):
    b, seq, _ = p.shape
    assert 2 * seq == FFT_N
    f1t, f2, g2, g1 = _fft_constants()
    hw = HY_LANES
    col = lambda c: pl.BlockSpec((1, seq, hw), lambda i, h: (i, 0, (COL_HY + c * D_HY) // hw + h))
    return pl.pallas_call(
        _hyena_lat_kernel,
        grid=(b, D_HY // hw),
        in_specs=[
            col(0), col(1), col(2),
            pl.BlockSpec((3, HY_SHORT, hw), lambda i, h: (0, 0, h)),
            pl.BlockSpec((3, 1, hw), lambda i, h: (0, 0, h)),
            pl.BlockSpec((HY_ORDER, hw), lambda i, h: (0, h)),
            pl.BlockSpec((HY_ORDER, 2, FFT_K1 * FFT_N2, hw), lambda i, h: (0, 0, 0, h)),
            _const_spec((FFT_N2, FFT_N1, 2 * FFT_K1_PAD)),
            _const_spec((2 * FFT_N2, 2 * FFT_N2)),
            _const_spec((2 * FFT_N2, 2 * FFT_N2)),
            _const_spec((FFT_N2, FFT_N1 // 2, 2 * FFT_K1_PAD)),
        ],
        out_specs=pl.BlockSpec((1, seq, hw), lambda i, h: (i, 0, h)),
        out_shape=jax.ShapeDtypeStruct((b, seq, D_HY), F32),
        scratch_shapes=[
            pltpu.VMEM((seq + 2 * HY_PAD, hw), F32),
            pltpu.VMEM((seq, hw), F32),
            pltpu.VMEM((2, FFT_ROWS, hw), F32),
        ],
        compiler_params=_cparams(("parallel", "parallel")),
        name="hyena_lat",
    )(p, p, p, conv_w.reshape(HY_SHORT, 3, D_HY).transpose(1, 0, 2), conv_b.reshape(3, 1, D_HY), bias, spec,
      f1t, f2, g2, g1)


HG_SEG_ROWS = 512
COND_ROWS = 16


def _trunk_layer(x, mods, lp, lb, batch, seq, ctx, final):
    x = ffn_half_step(x, mods, lp['ln_ffn1'], lp['w_ffn1_in'], lp['w_ffn1_out'], lp['final_norm'],
                      mod_base=0, final=False, seq_len=seq)
    p2d = mixer_in_proj(x, mods, lp['ln_mix'], lp['w_in'], seq_len=seq)
    p = p2d.reshape(batch, seq, N_IN)

    filt, ssq = hyena_filter_gen(seq, lp['hy_w1'], lp['hy_b1'], lp['hy_w2'], lp['hy_b2'], lp['hy_w3'],
                                 lp['hy_freq'])
    if ctx is None:
        att = context_attention(p, lp['attn_sink'])
        lru_h0 = jnp.zeros((batch, 2, D_LRU), F32)
        hg_s0 = jnp.zeros((batch, 2, HG_HEADS, HG_DK, HG_DV), F32)
        hy = hyena_ctx(p, lp['hy_conv_w'], lp['hy_conv_b'], lp['hy_bias'], hyena_ctx_spectrum(filt, ssq))
    else:
        k_ctx, v_ctx, lru_h0, hg_state = ctx
        att = latent_attention(p, k_ctx, v_ctx, lp['attn_sink'])
        hg_s0 = hg_state.astype(F32)
        hy = hyena_lat(p, lp['hy_conv_w'], lp['hy_conv_b'], lp['hy_bias'], hyena_lat_spectrum(filt, ssq))
    lru, lru_state = rglru_mixer(p, lp['lru_conv_w'], lp['lru_conv_b'], lp['lru_w_r'], lp['lru_b_r'],
                                 lp['lru_w_i'], lp['lru_b_i'], lp['lru_lambda'], lru_h0.astype(F32))
    o_dirs, hg_state_out = hgrn2_scan(p, lb, hg_s0, min(seq, HG_SEG_ROWS))
    t = batch * seq

    x = gated_merge(x, mods, lp['ln_mix'], att.reshape(t, D_ATTN), lru.reshape(t, D_LRU),
                    o_dirs.reshape(2, t, D_HG), p2d, lp['hg_norm'], hy.reshape(t, D_HY), lp['w_gate'],
                    lp['w_bo_attn'], lp['w_bo_lru'], lp['w_bo_hg'], lp['w_bo_hy'], lp['w_out'], seq_len=seq)
    x = ffn_half_step(x, mods, lp['ln_ffn2'], lp['w_ffn2_in'], lp['w_ffn2_out'], lp['final_norm'],
                      mod_base=6, final=final, seq_len=seq)
    state = None
    if ctx is None:
        k = p[:, :, COL_K:COL_K + D_KV].reshape(batch, seq, N_KV_HEADS, HEAD_DIM)
        v = p[:, :, COL_V:COL_V + D_KV].reshape(batch, seq, N_KV_HEADS, HEAD_DIM)
        state = (k, v, lru_state, hg_state_out)
    return x, state


def kernel(x_prompt, x_sample, cache_k, cache_v, state_lru, state_hgrn, c, c_ctx, ln_ffn1, ln_mix, ln_ffn2, w_ada, b_ada, w_ffn1_in, w_ffn1_out, w_ffn2_in, w_ffn2_out, w_in, attn_sink, lru_conv_w, lru_conv_b, lru_w_r, lru_b_r, lru_w_i, lru_b_i, lru_lambda, hg_lb_logits, hg_norm, hy_conv_w, hy_conv_b, hy_w1, hy_b1, hy_w2, hy_b2, hy_w3, hy_freq, hy_bias, w_bo_attn, w_bo_lru, w_bo_hg, w_bo_hy, w_gate, w_out, final_norm):
    batch, seq, _ = x_prompt.shape
    dec_batch, dec_seq, _ = x_sample.shape
    lb_soft = jax.nn.softmax(hg_lb_logits.astype(F32), axis=0)
    lb_all = jnp.cumsum(lb_soft, axis=0) - lb_soft[0]

    stacked = dict(ln_ffn1=ln_ffn1, ln_mix=ln_mix, ln_ffn2=ln_ffn2, w_ffn1_in=w_ffn1_in, w_ffn1_out=w_ffn1_out,
                   w_ffn2_in=w_ffn2_in, w_ffn2_out=w_ffn2_out, w_in=w_in, attn_sink=attn_sink,
                   lru_conv_w=lru_conv_w, lru_conv_b=lru_conv_b, lru_w_r=lru_w_r, lru_b_r=lru_b_r,
                   lru_w_i=lru_w_i, lru_b_i=lru_b_i, lru_lambda=lru_lambda, hg_norm=hg_norm,
                   hy_conv_w=hy_conv_w, hy_conv_b=hy_conv_b, hy_w1=hy_w1, hy_b1=hy_b1, hy_w2=hy_w2, hy_b2=hy_b2,
                   hy_w3=hy_w3, hy_freq=hy_freq, hy_bias=hy_bias, w_bo_attn=w_bo_attn, w_bo_lru=w_bo_lru,
                   w_bo_hg=w_bo_hg, w_bo_hy=w_bo_hy, w_gate=w_gate, w_out=w_out)

    cond = jnp.zeros((COND_ROWS, D_MODEL), F32).at[0].set(c_ctx).at[1:1 + dec_batch].set(c)
    mods = ada_mods(cond, w_ada, b_ada).reshape(DEPTH, COND_ROWS, N_MOD, D_MODEL)

    h = x_prompt.reshape(batch * seq, D_MODEL)
    z = x_sample.reshape(dec_batch * dec_seq, D_MODEL)
    ks, vs, lrus, hgs = [], [], [], []
    for l in range(DEPTH):
        lp = {name: w[l] for name, w in stacked.items()}
        lp['final_norm'] = final_norm
        final = l == DEPTH - 1
        h, (k_l, v_l, lru_l, hg_l) = _trunk_layer(h, mods[l, 0:1], lp, lb_all[l], batch, seq, None, final)
        ks.append(k_l)
        vs.append(v_l)
        lrus.append(lru_l)
        hgs.append(hg_l)
        ctx = (cache_k[:, l].reshape(dec_batch, -1, D_KV), cache_v[:, l].reshape(dec_batch, -1, D_KV),
               state_lru[:, l], state_hgrn[:, l])
        z, _ = _trunk_layer(z, mods[l, 1:1 + dec_batch], lp, lb_all[l], dec_batch, dec_seq, ctx, final)

    y_prompt = h.reshape(batch, seq, D_MODEL)
    y_sample = z.reshape(dec_batch, dec_seq, D_MODEL)
    return (y_prompt, y_sample, jnp.stack(ks, axis=1), jnp.stack(vs, axis=1), jnp.stack(lrus, axis=1),
            jnp.stack(hgs, axis=1))
```

```python
import functools
import math

import numpy as np
import jax
import jax.numpy as jnp
from jax import lax
from jax.experimental import pallas as pl
from jax.experimental.pallas import tpu as pltpu

F32 = jnp.float32
BF16 = jnp.bfloat16

D_MODEL = 1024
DEPTH = 2
GRID_W = 64
EPS = 1e-6
N_MOD = 9
N_BRANCH = 4
D_FF = 2816
N_HEADS = 8
N_KV_HEADS = 2
HEAD_DIM = 64
GQA_GROUP = N_HEADS // N_KV_HEADS
D_ATTN = N_HEADS * HEAD_DIM
D_KV = N_KV_HEADS * HEAD_DIM
WINDOW = 128
ATTN_BLOCK = 128
ATTN_SCALE = HEAD_DIM ** -0.5
ROPE_BASE = 10000.0
D_LRU = D_MODEL // 4
LRU_CONV = 4
LRU_C = 8.0
HG_HEADS = 4
HG_DK = 64
HG_DV = 64
D_HG = HG_HEADS * HG_DK
HG_SCALE = HG_DK ** -0.5
HG_BLK = 16
D_HY = D_MODEL // 4
HY_ORDER = 2
HY_SHORT = 3
HY_BANDS = 8
HY_EMB = 2 * HY_BANDS + 1
HY_FFN = 64
HY_TARGET = 1e-2
HY_DECAY_SLOW = -math.log(HY_TARGET) / 1.5
HY_DECAY_FAST = -math.log(HY_TARGET) / 0.3
N_IN = D_ATTN + 2 * D_KV + 2 * D_LRU + 5 * D_HG + 3 * D_HY

COL_K = D_ATTN
COL_V = D_ATTN + D_KV
COL_RX = D_ATTN + 2 * D_KV
COL_RG = COL_RX + D_LRU
COL_HQ = COL_RG + D_LRU
COL_HFF = COL_HQ + D_HG
COL_HFB = COL_HFF + D_HG
COL_HI = COL_HFB + D_HG
COL_HGATE = COL_HI + D_HG
COL_HY = COL_HGATE + D_HG

FF_CHUNK = 256
N_FF_CHUNK = D_FF // FF_CHUNK
ROW_TILE = 512
VMEM_LIMIT = 56 * 1024 * 1024


def _bf(x):
    return x.astype(BF16)


def _dot(a, b):
    return jnp.dot(a, b, preferred_element_type=F32)


def _split3(x):
    hi = _bf(x)
    r1 = x - hi.astype(F32)
    mid = _bf(r1)
    lo = _bf(r1 - mid.astype(F32))
    return hi, mid, lo


def _dot_exact_rhs(mat_bf, x):
    hi, mid, lo = _split3(x)
    return _dot(mat_bf, hi) + _dot(mat_bf, mid) + _dot(mat_bf, lo)


def _sigmoid(x):
    return jax.nn.sigmoid(x)


def _silu(x):
    return x * _sigmoid(x)


def _norm_mod(x, lnw, shift, scale):
    ms = jnp.mean(x * x, axis=-1, keepdims=True)
    y = x * lax.rsqrt(ms + EPS) * lnw
    return y * (1.0 + scale) + shift


def _cparams(sem):
    return pltpu.CompilerParams(dimension_semantics=sem, vmem_limit_bytes=VMEM_LIMIT)


def _const_spec(shape):
    nd = len(shape)
    return pl.BlockSpec(shape, lambda *_: (0,) * nd)


def _ada_kernel(cond_ref, w_ref, b_ref, o_ref):
    c = cond_ref[...]
    o_ref[0] = _dot(_bf(_silu(c)), _bf(w_ref[0])) + b_ref[0]


def ada_mods(cond, w_ada, b_ada):
    r = cond.shape[0]
    cb = 1024
    ncol = (N_MOD * D_MODEL) // cb
    return pl.pallas_call(
        _ada_kernel,
        grid=(DEPTH, ncol),
        in_specs=[
            pl.BlockSpec((r, D_MODEL), lambda l, j: (0, 0)),
            pl.BlockSpec((1, D_MODEL, cb), lambda l, j: (l, 0, j)),
            pl.BlockSpec((1, 1, cb), lambda l, j: (l, 0, j)),
        ],
        out_specs=pl.BlockSpec((1, r, cb), lambda l, j: (l, 0, j)),
        out_shape=jax.ShapeDtypeStruct((DEPTH, r, N_MOD * D_MODEL), F32),
        compiler_params=_cparams(("parallel", "parallel")),
        name="ada_mods",
    )(cond, w_ada, b_ada.reshape(DEPTH, 1, N_MOD * D_MODEL))


def _seq_mod_index(seq_len, n_mod, tm):
    def idx(i):
        return jnp.minimum((i * tm) // seq_len, n_mod - 1)
    return idx


def _ffn_kernel(x_ref, mod_ref, ln_ref, wg_ref, wu_ref, wo_ref, fn_ref, o_ref, *, mod_base, final):
    x = x_ref[...]
    m = mod_ref[0]
    sh = m[mod_base:mod_base + 1]
    sc = m[mod_base + 1:mod_base + 2]
    g = m[mod_base + 2:mod_base + 3]
    h = _bf(_norm_mod(x, ln_ref[...], sh, sc))
    acc = None
    for c in range(N_FF_CHUNK):
        gate = _dot(h, wg_ref[c])
        up = _dot(h, wu_ref[c])
        part = _dot(_bf(_silu(gate) * up), wo_ref[c])
        acc = part if acc is None else acc + part
    y = x + 0.5 * g * acc
    if final:
        ms = jnp.mean(y * y, axis=-1, keepdims=True)
        y = y * lax.rsqrt(ms + EPS) * fn_ref[...]
    o_ref[...] = y


def ffn_half_step(x, mods, ln_w, w_in, w_out, final_w, *, mod_base, final, seq_len):
    t = x.shape[0]
    tm = ROW_TILE
    wg = _bf(w_in[:, :D_FF]).reshape(D_MODEL, N_FF_CHUNK, FF_CHUNK).transpose(1, 0, 2)
    wu = _bf(w_in[:, D_FF:]).reshape(D_MODEL, N_FF_CHUNK, FF_CHUNK).transpose(1, 0, 2)
    wo = _bf(w_out).reshape(N_FF_CHUNK, FF_CHUNK, D_MODEL)
    midx = _seq_mod_index(seq_len, mods.shape[0], tm)
    return pl.pallas_call(
        functools.partial(_ffn_kernel, mod_base=mod_base, final=final),
        grid=(t // tm,),
        in_specs=[
            pl.BlockSpec((tm, D_MODEL), lambda i: (i, 0)),
            pl.BlockSpec((1, N_MOD, D_MODEL), lambda i: (midx(i), 0, 0)),
            _const_spec((1, D_MODEL)),
            _const_spec((N_FF_CHUNK, D_MODEL, FF_CHUNK)),
            _const_spec((N_FF_CHUNK, D_MODEL, FF_CHUNK)),
            _const_spec((N_FF_CHUNK, FF_CHUNK, D_MODEL)),
            _const_spec((1, D_MODEL)),
        ],
        out_specs=pl.BlockSpec((tm, D_MODEL), lambda i: (i, 0)),
        out_shape=jax.ShapeDtypeStruct((t, D_MODEL), F32),
        compiler_params=_cparams(("parallel",)),
        name="ffn_half_step",
    )(x, mods, ln_w.reshape(1, D_MODEL), wg, wu, wo, final_w.reshape(1, D_MODEL))


PROJ_CHUNK = 256


def _proj_kernel(x_ref, mod_ref, ln_ref, w_ref, o_ref):
    x = x_ref[...]
    m = mod_ref[0]
    u = _bf(_norm_mod(x, ln_ref[...], m[3:4], m[4:5]))
    for c in range(N_IN // PROJ_CHUNK):
        sl = slice(c * PROJ_CHUNK, (c + 1) * PROJ_CHUNK)
        o_ref[:, sl] = _dot(u, w_ref[:, sl])


def mixer_in_proj(x, mods, ln_w, w_in, *, seq_len):
    t = x.shape[0]
    tm = ROW_TILE
    midx = _seq_mod_index(seq_len, mods.shape[0], tm)
    return pl.pallas_call(
        _proj_kernel,
        grid=(t // tm,),
        in_specs=[
            pl.BlockSpec((tm, D_MODEL), lambda i: (i, 0)),
            pl.BlockSpec((1, N_MOD, D_MODEL), lambda i: (midx(i), 0, 0)),
            _const_spec((1, D_MODEL)),
            _const_spec((D_MODEL, N_IN)),
        ],
        out_specs=pl.BlockSpec((tm, N_IN), lambda i: (i, 0)),
        out_shape=jax.ShapeDtypeStruct((t, N_IN), F32),
        compiler_params=_cparams(("parallel",)),
        name="mixer_in_proj",
    )(x, mods, ln_w.reshape(1, D_MODEL), _bf(w_in))


MERGE_CHUNK = 256


def _merge_kernel(x_ref, mod_ref, ln_ref, att_ref, lru_ref, hgf_ref, hgb_ref, hgg_ref, hgn_ref, headb_ref, hy_ref,
                  wg_ref, wa_ref, wl_ref, wh_ref, wy_ref, wo_ref, o_ref, mix_ref):
    x = x_ref[...]
    m = mod_ref[0]
    u = _bf(_norm_mod(x, ln_ref[...], m[3:4], m[4:5]))
    o = hgf_ref[0] + hgb_ref[0]
    sq = o * o
    sq_hi = _bf(sq)
    sq_lo = _bf(sq - sq_hi.astype(F32))
    ms = (_dot(sq_hi, headb_ref[...]) + _dot(sq_lo, headb_ref[...])) * (1.0 / HG_DV)
    hg = o * lax.rsqrt(ms + EPS) * hgn_ref[...] * _silu(hgg_ref[...])
    branches = (_bf(att_ref[...]), _bf(lru_ref[...]), _bf(hg), _bf(hy_ref[...]))
    w_bo = (wa_ref, wl_ref, wh_ref, wy_ref)
    for c in range(D_MODEL // MERGE_CHUNK):
        sl = slice(c * MERGE_CHUNK, (c + 1) * MERGE_CHUNK)
        mixed = None
        for n in range(N_BRANCH):
            term = _sigmoid(_dot(u, wg_ref[n, :, sl])) * _dot(branches[n], w_bo[n][:, sl])
            mixed = term if mixed is None else mixed + term
        mix_ref[:, sl] = _bf(mixed)
    o_ref[...] = x + m[5:6] * _dot(mix_ref[...], wo_ref[...])


def gated_merge(x, mods, ln_w, att, lru, hg_dirs, p2d, hg_norm, hy, w_gate, w_bo_attn, w_bo_lru, w_bo_hg, w_bo_hy,
                w_out, *, seq_len):
    t = x.shape[0]
    tm = ROW_TILE
    midx = _seq_mod_index(seq_len, mods.shape[0], tm)
    row = lambda w: pl.BlockSpec((tm, w), lambda i: (i, 0))
    _, _, headb = _hg_constants()
    return pl.pallas_call(
        _merge_kernel,
        grid=(t // tm,),
        in_specs=[
            row(D_MODEL),
            pl.BlockSpec((1, N_MOD, D_MODEL), lambda i: (midx(i), 0, 0)),
            _const_spec((1, D_MODEL)),
            row(D_ATTN), row(D_LRU),
            pl.BlockSpec((1, tm, D_HG), lambda i: (0, i, 0)),
            pl.BlockSpec((1, tm, D_HG), lambda i: (1, i, 0)),
            pl.BlockSpec((tm, D_HG), lambda i: (i, COL_HGATE // D_HG)),
            _const_spec((1, D_HG)),
            _const_spec((D_HG, D_HG)),
            row(D_HY),
            _const_spec((N_BRANCH, D_MODEL, D_MODEL)),
            _const_spec((D_ATTN, D_MODEL)),
            _const_spec((D_LRU, D_MODEL)),
            _const_spec((D_HG, D_MODEL)),
            _const_spec((D_HY, D_MODEL)),
            _const_spec((D_MODEL, D_MODEL)),
        ],
        out_specs=row(D_MODEL),
        out_shape=jax.ShapeDtypeStruct((t, D_MODEL), F32),
        scratch_shapes=[pltpu.VMEM((tm, D_MODEL), BF16)],
        compiler_params=_cparams(("parallel",)),
        name="gated_merge",
    )(x, mods, ln_w.reshape(1, D_MODEL), att, lru, hg_dirs, hg_dirs, p2d, hg_norm.reshape(1, D_HG), headb, hy,
      _bf(w_gate), _bf(w_bo_attn), _bf(w_bo_lru), _bf(w_bo_hg), _bf(w_bo_hy), _bf(w_out))


def _dot_nt(a, b):
    return lax.dot_general(a, b, (((1,), (1,)), ((), ())), preferred_element_type=F32)


def _ctx_attn_kernel(q_ref, k_ref, v_ref, sink_ref, o_ref):
    rows = q_ref.shape[1]
    blk = ATTN_BLOCK
    keys = _dup_heads(k_ref[0])
    vals_t = _values_t(v_ref[0])
    low = lax.broadcasted_iota(jnp.int32, (blk, 2 * HEAD_DIM), 1) < HEAD_DIM
    units = [(pair, qb) for qb in range(rows // blk) for pair in range(N_HEADS // 2)]

    def where(pair, qb):
        return slice(qb * blk, (qb + 1) * blk), slice(pair * 2 * HEAD_DIM, (pair + 1) * 2 * HEAD_DIM)

    scores = []
    for pair, qb in units:
        q_rows, lanes = where(pair, qb)
        q = q_ref[0, q_rows, lanes] * ATTN_SCALE
        q_heads = _bf(jnp.concatenate([jnp.where(low, q, 0.0), jnp.where(low, 0.0, q)], axis=0))
        scores.append(_dot_nt(keys[(2 * pair) // GQA_GROUP], q_heads))
    probs = []
    for (pair, qb), s in zip(units, scores):
        sink = jnp.concatenate([sink_ref[2 * pair:2 * pair + 1, :], sink_ref[2 * pair + 1:2 * pair + 2, :]], axis=1)
        m = jnp.maximum(jnp.max(s, axis=0, keepdims=True), sink)
        probs.append((_bf(jnp.exp(s - m)), jnp.exp(sink - m)))
    for (pair, qb), (p, sink_term) in zip(units, probs):
        q_rows, lanes = where(pair, qb)
        acc = _dot(vals_t[(2 * pair) // GQA_GROUP], p)
        out_t = acc[:HEAD_DIM, :] / (acc[HEAD_DIM:HEAD_DIM + 1, :] + sink_term)
        o_ref[0, q_rows, lanes] = jnp.transpose(jnp.concatenate([out_t[:, :blk], out_t[:, blk:]], axis=0))


def _sink_table(sink):
    return jnp.broadcast_to(sink.astype(F32)[:, None], (N_HEADS, 128))


def context_attention(p_ctx, sink):
    b, seq, _ = p_ctx.shape
    return pl.pallas_call(
        _ctx_attn_kernel,
        grid=(b,),
        in_specs=[
            pl.BlockSpec((1, seq, D_ATTN), lambda i: (i, 0, 0)),
            pl.BlockSpec((1, seq, D_KV), lambda i: (i, 0, COL_K // D_KV)),
            pl.BlockSpec((1, seq, D_KV), lambda i: (i, 0, COL_V // D_KV)),
            _const_spec((N_HEADS, 128)),
        ],
        out_specs=pl.BlockSpec((1, seq, D_ATTN), lambda i: (i, 0, 0)),
        out_shape=jax.ShapeDtypeStruct((b, seq, D_ATTN), F32),
        compiler_params=_cparams(("parallel",)),
        name="context_attention",
    )(p_ctx, p_ctx, p_ctx, _sink_table(sink))


def _rope(x, cos, sin_signed):
    n = x.shape[-1]
    lane = lax.broadcasted_iota(jnp.int32, x.shape, x.ndim - 1)
    first = (lane % (HEAD_DIM // 2)) < (HEAD_DIM // 4)
    partner = jnp.where(first, pltpu.roll(x, n - HEAD_DIM // 4, axis=x.ndim - 1),
                        pltpu.roll(x, HEAD_DIM // 4, axis=x.ndim - 1))
    return x * cos + partner * sin_signed


def _rope_tables(seq):
    rows = seq // GRID_W
    row = np.repeat(np.arange(rows, dtype=np.float32), GRID_W)
    col = np.tile(np.arange(GRID_W, dtype=np.float32), rows)
    n = HEAD_DIM // 4
    inv = (ROPE_BASE ** (-jnp.arange(n, dtype=F32) / n))
    ang_r = jnp.asarray(row)[:, None] * inv
    ang_c = jnp.asarray(col)[:, None] * inv
    cos = jnp.concatenate([jnp.cos(ang_r), jnp.cos(ang_r), jnp.cos(ang_c), jnp.cos(ang_c)], axis=-1)
    sin = jnp.concatenate([-jnp.sin(ang_r), jnp.sin(ang_r), -jnp.sin(ang_c), jnp.sin(ang_c)], axis=-1)
    return jnp.tile(cos, (1, 2)), jnp.tile(sin, (1, 2))


ATTN_PREP_ROWS = 512
ATTN_QBLOCKS = 8


def _dup_heads(x):
    lane = lax.broadcasted_iota(jnp.int32, x.shape, 1)
    swapped = pltpu.roll(x, HEAD_DIM, axis=1)
    low = lane < HEAD_DIM
    return _bf(jnp.where(low, x, swapped)), _bf(jnp.where(low, swapped, x))


def _values_t(x):
    xt = jnp.transpose(x)
    ones = jnp.ones((HEAD_DIM, x.shape[0]), F32)
    return (_bf(jnp.concatenate([xt[:HEAD_DIM], ones], axis=0)),
            _bf(jnp.concatenate([xt[HEAD_DIM:], ones], axis=0)))


def _lat_attn_kernel(q_ref, k_ref, v_ref, kc_ref, vc_ref, cos_ref, sin_ref, sink_ref, o_ref,
                     kd_ref, vt_ref, kcd_ref, vct_ref):
    seq = k_ref.shape[1]
    blk = ATTN_BLOCK
    band = 3 * blk
    past = kc_ref.shape[1]
    i = pl.program_id(1)

    @pl.when(i == 0)
    def _():
        for ti in range(seq // ATTN_PREP_ROWS):
            rows = slice(ti * ATTN_PREP_ROWS, (ti + 1) * ATTN_PREP_ROWS)
            k0, k1 = _dup_heads(_rope(k_ref[0, rows, :], cos_ref[rows, :], sin_ref[rows, :]))
            kd_ref[0, rows, :] = k0
            kd_ref[1, rows, :] = k1
            v0, v1 = _values_t(v_ref[0, rows, :])
            vt_ref[0, :, rows] = v0
            vt_ref[1, :, rows] = v1
        kc0, kc1 = _dup_heads(kc_ref[0])
        kcd_ref[0] = kc0
        kcd_ref[1] = kc1
        vc0, vc1 = _values_t(vc_ref[0])
        vct_ref[0] = vc0
        vct_ref[1] = vc1

    key = lax.broadcasted_iota(jnp.int32, (band + past, 2 * blk), 0)
    qry = lax.broadcasted_iota(jnp.int32, (band + past, 2 * blk), 1) % blk
    low = lax.broadcasted_iota(jnp.int32, (blk, 2 * HEAD_DIM), 1) < HEAD_DIM
    subs = []
    for sub in range(ATTN_QBLOCKS):
        ib = ATTN_QBLOCKS * i + sub
        start = pl.multiple_of(jnp.clip((ib - 1) * blk, 0, seq - band), blk)
        q0 = pl.multiple_of(ib * blk, blk)
        subs.append(dict(
            rows=slice(sub * blk, (sub + 1) * blk),
            cos=cos_ref[pl.ds(q0, blk), :], sin=sin_ref[pl.ds(q0, blk), :],
            mask=(jnp.abs((q0 + qry) - (start + key)) <= WINDOW) | (key >= band),
            keys=[jnp.concatenate([kd_ref[kvh, pl.ds(start, band), :], kcd_ref[kvh]], axis=0)
                  for kvh in range(N_KV_HEADS)],
            vals_t=[jnp.concatenate([vt_ref[kvh, :, pl.ds(start, band)], vct_ref[kvh]], axis=1)
                    for kvh in range(N_KV_HEADS)]))
    units = [(sub, pair) for sub in range(ATTN_QBLOCKS) for pair in range(N_HEADS // 2)]
    scores = []
    for sub, pair in units:
        u = subs[sub]
        lanes = slice(pair * 2 * HEAD_DIM, (pair + 1) * 2 * HEAD_DIM)
        q = _rope(q_ref[0, u['rows'], lanes], u['cos'], u['sin']) * ATTN_SCALE
        q_heads = _bf(jnp.concatenate([jnp.where(low, q, 0.0), jnp.where(low, 0.0, q)], axis=0))
        scores.append(_dot_nt(u['keys'][(2 * pair) // GQA_GROUP], q_heads))
    probs = []
    for (sub, pair), raw in zip(units, scores):
        s = jnp.where(subs[sub]['mask'], raw, -jnp.inf)
        sink = jnp.concatenate([sink_ref[2 * pair:2 * pair + 1, :], sink_ref[2 * pair + 1:2 * pair + 2, :]], axis=1)
        m = jnp.maximum(jnp.max(s, axis=0, keepdims=True), sink)
        probs.append((_bf(jnp.exp(s - m)), jnp.exp(sink - m)))
    for (sub, pair), (p, sink_term) in zip(units, probs):
        u = subs[sub]
        lanes = slice(pair * 2 * HEAD_DIM, (pair + 1) * 2 * HEAD_DIM)
        acc = _dot(u['vals_t'][(2 * pair) // GQA_GROUP], p)
        out_t = acc[:HEAD_DIM, :] / (acc[HEAD_DIM:HEAD_DIM + 1, :] + sink_term)
        o_ref[0, u['rows'], lanes] = jnp.transpose(jnp.concatenate([out_t[:, :blk], out_t[:, blk:]], axis=0))


def latent_attention(p_lat, k_ctx, v_ctx, sink):
    b, seq, _ = p_lat.shape
    past = k_ctx.shape[1]
    cos, sin = _rope_tables(seq)
    return pl.pallas_call(
        _lat_attn_kernel,
        grid=(b, seq // (ATTN_QBLOCKS * ATTN_BLOCK)),
        in_specs=[
            pl.BlockSpec((1, ATTN_QBLOCKS * ATTN_BLOCK, D_ATTN), lambda bi, i: (bi, i, 0)),
            pl.BlockSpec((1, seq, D_KV), lambda bi, i: (bi, 0, COL_K // D_KV)),
            pl.BlockSpec((1, seq, D_KV), lambda bi, i: (bi, 0, COL_V // D_KV)),
            pl.BlockSpec((1, past, D_KV), lambda bi, i: (bi, 0, 0)),
            pl.BlockSpec((1, past, D_KV), lambda bi, i: (bi, 0, 0)),
            _const_spec((seq, 128)),
            _const_spec((seq, 128)),
            _const_spec((N_HEADS, 128)),
        ],
        out_specs=pl.BlockSpec((1, ATTN_QBLOCKS * ATTN_BLOCK, D_ATTN), lambda bi, i: (bi, i, 0)),
        out_shape=jax.ShapeDtypeStruct((b, seq, D_ATTN), F32),
        scratch_shapes=[
            pltpu.VMEM((N_KV_HEADS, seq, D_KV), BF16),
            pltpu.VMEM((N_KV_HEADS, D_KV, seq), BF16),
            pltpu.VMEM((N_KV_HEADS, past, D_KV), BF16),
            pltpu.VMEM((N_KV_HEADS, D_KV, past), BF16),
        ],
        compiler_params=_cparams(("parallel", "arbitrary")),
        name="latent_attention",
    )(p_lat, p_lat, p_lat, k_ctx, v_ctx, cos, sin, _sink_table(sink))


LRU_HALF = D_LRU // 2
LRU_ROWS = 256
LRU_PAD = 8


def _neg_expm1(y, exp_y):
    return jnp.tanh(-0.5 * y) * (exp_y + 1.0)


def _softplus(x):
    return jnp.maximum(x, 0.0) + jnp.log1p(jnp.exp(-jnp.abs(x)))


def _gelu_tanh(x):
    return 0.5 * x * (1.0 + jnp.tanh(math.sqrt(2.0 / math.pi) * (x + 0.044715 * (x * x * x))))


LRU_SCAN = 8


def _affine_scan8(a, b, *, reverse):
    row = lax.broadcasted_iota(jnp.int32, a.shape, 0)
    k = 1
    while k < LRU_SCAN:
        if reverse:
            valid = row < LRU_SCAN - k
            shift = LRU_SCAN - k
        else:
            valid = row >= k
            shift = k
        a_prev = jnp.where(valid, pltpu.roll(a, shift, axis=0), 1.0)
        b_prev = jnp.where(valid, pltpu.roll(b, shift, axis=0), 0.0)
        b = b + a * b_prev
        a = a * a_prev
        k *= 2
    return a, b


def _lru_kernel(x_ref, g_ref, cw_ref, cb_ref, wr_ref, br_ref, wi_ref, bi_ref, lam_ref, h0_ref,
                o_ref, st_ref, pad_ref, a_ref, bx_ref, hb_ref):
    seq = x_ref.shape[1]
    zeros = jnp.zeros((LRU_PAD, LRU_HALF), F32)
    pad_ref[0:LRU_PAD, :] = zeros
    pad_ref[LRU_PAD + seq:2 * LRU_PAD + seq, :] = zeros
    pad_ref[LRU_PAD:LRU_PAD + seq, :] = x_ref[0]
    left = LRU_CONV // 2
    for ti in range(seq // LRU_ROWS):
        r0 = ti * LRU_ROWS
        xc = cb_ref[...]
        for k in range(LRU_CONV):
            xc = xc + cw_ref[k:k + 1, :] * pad_ref[r0 + LRU_PAD - left + k:r0 + LRU_PAD - left + k + LRU_ROWS, :]
        xcb = _bf(xc)
        for d in range(2):
            r = _sigmoid(_dot(xcb, wr_ref[d]) + br_ref[d:d + 1, :])
            i = _sigmoid(_dot(xcb, wi_ref[d]) + bi_ref[d:d + 1, :])
            log_a = (-LRU_C) * r * _softplus(-lam_ref[d:d + 1, :])
            a = jnp.exp(log_a)
            a_ref[d, r0:r0 + LRU_ROWS, :] = a
            bx_ref[d, r0:r0 + LRU_ROWS, :] = jnp.sqrt(_neg_expm1(2.0 * log_a, a * a)) * (i * xc)

    nt = seq // LRU_SCAN

    def step(blk, carry):
        hf, hb = carry
        rf = pl.ds(pl.multiple_of(blk * LRU_SCAN, LRU_SCAN), LRU_SCAN)
        rb = pl.ds(pl.multiple_of((nt - 1 - blk) * LRU_SCAN, LRU_SCAN), LRU_SCAN)
        af, bf_ = _affine_scan8(a_ref[0, rf, :], bx_ref[0, rf, :], reverse=False)
        ab, bb = _affine_scan8(a_ref[1, rb, :], bx_ref[1, rb, :], reverse=True)
        tile_f = af * hf + bf_
        tile_b = ab * hb + bb
        o_ref[0, rf, :] = tile_f
        hb_ref[rb, :] = tile_b
        return tile_f[LRU_SCAN - 1:LRU_SCAN, :], tile_b[0:1, :]

    hf, hb = lax.fori_loop(0, nt, step, (h0_ref[0, 0:1, :], h0_ref[0, 1:2, :]), unroll=2)
    st_ref[0, 0:1, :] = hf
    st_ref[0, 1:2, :] = hb
    for ti in range(seq // LRU_ROWS):
        rows = slice(ti * LRU_ROWS, (ti + 1) * LRU_ROWS)
        o_ref[0, rows, :] = (o_ref[0, rows, :] + hb_ref[rows, :]) * _gelu_tanh(g_ref[0, rows, :])


def _block_diag(w):
    n, blk, _ = w.shape
    eye = jnp.eye(n, dtype=w.dtype)
    return (eye[:, None, :, None] * w[:, :, None, :]).reshape(n * blk, n * blk)


def rglru_mixer(p, conv_w, conv_b, w_r, b_r, w_i, b_i, lam, h0):
    b, seq, _ = p.shape
    wr = _bf(jnp.stack([_block_diag(w_r[d]) for d in range(2)]))
    wi = _bf(jnp.stack([_block_diag(w_i[d]) for d in range(2)]))
    hw = LRU_HALF
    vec = lambda rows: pl.BlockSpec((rows, hw), lambda bi, h: (0, h))
    return pl.pallas_call(
        _lru_kernel,
        grid=(b, 2),
        in_specs=[
            pl.BlockSpec((1, seq, hw), lambda bi, h: (bi, 0, COL_RX // hw + h)),
            pl.BlockSpec((1, seq, hw), lambda bi, h: (bi, 0, COL_RG // hw + h)),
            vec(LRU_CONV), vec(1),
            pl.BlockSpec((2, hw, hw), lambda bi, h: (0, h, h)), vec(2),
            pl.BlockSpec((2, hw, hw), lambda bi, h: (0, h, h)), vec(2),
            vec(2),
            pl.BlockSpec((1, 2, hw), lambda bi, h: (bi, 0, h)),
        ],
        out_specs=[
            pl.BlockSpec((1, seq, hw), lambda bi, h: (bi, 0, h)),
            pl.BlockSpec((1, 2, hw), lambda bi, h: (bi, 0, h)),
        ],
        out_shape=[jax.ShapeDtypeStruct((b, seq, D_LRU), F32), jax.ShapeDtypeStruct((b, 2, D_LRU), F32)],
        scratch_shapes=[
            pltpu.VMEM((seq + 2 * LRU_PAD, hw), F32),
            pltpu.VMEM((2, seq, hw), F32),
            pltpu.VMEM((2, seq, hw), F32),
            pltpu.VMEM((seq, hw), F32),
        ],
        compiler_params=_cparams(("parallel", "parallel")),
        name="rglru_mixer",
    )(p, p, conv_w, conv_b.reshape(1, D_LRU), wr, b_r, wi, b_i, lam, h0)


HG_TILE = 256
HG_GROUP = 32
HG_AHEAD = 1
HG_NBUF = 2
LOG2_E = math.log2(math.e)


def _hg_constants():
    idx = np.arange(HG_TILE)
    same_blk = (idx[:, None] // HG_BLK) == (idx[None, :] // HG_BLK)
    tri_f = same_blk & (idx[None, :] <= idx[:, None])
    tri_b = same_blk & (idx[None, :] >= idx[:, None])
    head = (idx[:, None] // HG_DK) == (idx[None, :] // HG_DK)
    return jnp.asarray(np.stack([tri_f, tri_b]), BF16), jnp.asarray(same_blk, BF16), jnp.asarray(head, BF16)


def _hgrn_kernel(q_ref, z_ref, v_ref, lb_ref, tri_ref, ones_ref, headb_ref, s0_ref,
                 o_ref, st_ref, state, *scratch):
    att_bufs = scratch[0:HG_NBUF]
    upd_bufs = scratch[HG_NBUF:2 * HG_NBUF]
    qst_bufs = scratch[2 * HG_NBUF:3 * HG_NBUF]
    qs, srcs, vs, cums, qds, kns, tots = scratch[3 * HG_NBUF:]
    seg_rows = q_ref.shape[1]
    d = pl.program_id(1)
    s = pl.program_id(2)
    nseg = pl.num_programs(2)

    @pl.when(s == 0)
    def _():
        state[...] = jnp.concatenate([jnp.transpose(s0_ref[0, 0, h]) for h in range(HG_HEADS)], axis=1)

    lb = lb_ref[0]
    for ti in range(seg_rows // HG_TILE):
        rows = slice(ti * HG_TILE, (ti + 1) * HG_TILE)
        f = lb + (1.0 - lb) * _sigmoid(z_ref[0, rows, :])
        g = jnp.log(f)
        kk = 1.0 - f
        q = _silu(q_ref[0, rows, :]) * HG_SCALE
        cum = _dot_exact_rhs(tri_ref[0], g)
        tot = _dot_exact_rhs(ones_ref[...], g)
        qs[rows, :] = q
        srcs[rows, :] = (cum - jnp.log(kk)) * LOG2_E
        vs[rows, :] = v_ref[0, rows, :]
        cums[rows, :] = cum * LOG2_E
        tots[rows, :] = tot
        qds[rows, :] = q * jnp.exp(cum)
        kns[rows, :] = kk * jnp.exp(tot - cum)

    nblk = seg_rows // HG_BLK
    sgn = jnp.where(d == 0, 1, -1)
    t_signed = [(lax.broadcasted_iota(jnp.int32, (8, D_HG), 0) + 8 * h) * sgn for h in range(HG_BLK // 8)]
    lane_head_blk = lax.broadcasted_iota(jnp.int32, (HG_BLK, D_HG), 1) // HG_DK
    lane_head_dv = lax.broadcasted_iota(jnp.int32, (HG_DV, D_HG), 1) // HG_DK

    n_half = HG_BLK // 8

    def block_rows(j):
        je = jnp.where(d == 0, j, nblk - 1 - j)
        return pl.multiple_of(je * HG_BLK, HG_BLK)

    def prepare(j, slot):
        r0 = block_rows(j)
        rows = pl.ds(r0, HG_BLK)
        cum_h = [cums[pl.ds(r0 + 8 * h, 8), :] for h in range(n_half)]
        q_h = [qs[pl.ds(r0 + 8 * h, 8), :] for h in range(n_half)]
        tiles = []
        for src in range(HG_BLK):
            src_row = srcs[pl.ds(r0 + src, 1), :]
            for h in range(n_half):
                pair = jnp.exp2(cum_h[h] - src_row) * q_h[h]
                tiles.append(jnp.where(t_signed[h] >= src * sgn, pair, 0.0))
        att_bufs[slot][...] = _dot(_bf(jnp.concatenate(tiles, axis=0)), headb_ref[...])
        upd = lax.dot_general(_bf(vs[rows, :]), _bf(kns[rows, :]), (((0,), (0,)), ((), ())),
                              preferred_element_type=F32)
        upd_fold = jnp.where(lane_head_dv == 0, upd[0:HG_DV], 0.0)
        for h in range(1, HG_HEADS):
            upd_fold = upd_fold + jnp.where(lane_head_dv == h, upd[h * HG_DV:(h + 1) * HG_DV], 0.0)
        upd_bufs[slot][...] = upd_fold
        qd = qds[rows, :]
        qst_bufs[slot][...] = _bf(jnp.concatenate(
            [jnp.where(lane_head_blk == h, qd, 0.0) for h in range(HG_HEADS)], axis=0))

    def finish(j, slot, st):
        r0 = block_rows(j)
        rows = pl.ds(r0, HG_BLK)
        o_halves = [None] * n_half
        for src in range(HG_BLK):
            v_row = vs[pl.ds(r0 + src, 1), :]
            for h in range(n_half):
                term = att_bufs[slot][src * HG_BLK + h * 8:src * HG_BLK + (h + 1) * 8, :] * v_row
                o_halves[h] = term if o_halves[h] is None else o_halves[h] + term
        by_head = _dot_nt(qst_bufs[slot][...], _bf(st))
        o_state = jnp.concatenate([by_head[h * HG_BLK:(h + 1) * HG_BLK] for h in range(HG_HEADS)], axis=1)
        o_ref[0, 0, rows, :] = jnp.concatenate(o_halves, axis=0) + o_state
        return jnp.exp(tots[pl.ds(r0, 1), :]) * st + upd_bufs[slot][...]

    for k in range(HG_AHEAD):
        prepare(k, k)

    group = min(HG_GROUP, nblk)

    def block_group(i, st):
        j = group * i
        for k in range(group):
            prepare(jnp.minimum(j + k + HG_AHEAD, nblk - 1), (k + HG_AHEAD) % HG_NBUF)
            st = finish(j + k, k % HG_NBUF, st)
        return st

    if group == nblk:
        st = state[...]
        for k in range(nblk):
            if k + HG_AHEAD < nblk:
                prepare(k + HG_AHEAD, (k + HG_AHEAD) % HG_NBUF)
            st = finish(k, k % HG_NBUF, st)
        state[...] = st
    else:
        state[...] = lax.fori_loop(0, nblk // group, block_group, state[...])

    @pl.when(s == nseg - 1)
    def _():
        for h in range(HG_HEADS):
            st_ref[0, 0, h] = jnp.transpose(state[:, h * HG_DK:(h + 1) * HG_DK])


def hgrn2_scan(p, lb, s0, seg_rows):
    b, seq, _ = p.shape
    nseg = seq // seg_rows
    tri, ones, headb = _hg_constants()
    seg = lambda d, s: jnp.where(d == 0, s, nseg - 1 - s)
    col = lambda c: pl.BlockSpec((1, seg_rows, D_HG), lambda bi, d, s: (bi, seg(d, s), c // D_HG))
    scr = lambda: pltpu.VMEM((seg_rows, D_HG), F32)
    return pl.pallas_call(
        _hgrn_kernel,
        grid=(b, 2, nseg),
        in_specs=[
            col(COL_HQ),
            pl.BlockSpec((1, seg_rows, D_HG), lambda bi, d, s: (bi, seg(d, s), COL_HFF // D_HG + d)),
            col(COL_HI),
            pl.BlockSpec((1, 1, D_HG), lambda bi, d, s: (d, 0, 0)),
            pl.BlockSpec((1, HG_TILE, HG_TILE), lambda bi, d, s: (d, 0, 0)),
            _const_spec((HG_TILE, HG_TILE)),
            _const_spec((D_HG, D_HG)),
            pl.BlockSpec((1, 1, HG_HEADS, HG_DK, HG_DV), lambda bi, d, s: (bi, d, 0, 0, 0)),
        ],
        out_specs=[
            pl.BlockSpec((1, 1, seg_rows, D_HG), lambda bi, d, s: (d, bi, seg(d, s), 0)),
            pl.BlockSpec((1, 1, HG_HEADS, HG_DK, HG_DV), lambda bi, d, s: (bi, d, 0, 0, 0)),
        ],
        out_shape=[jax.ShapeDtypeStruct((2, b, seq, D_HG), F32),
                   jax.ShapeDtypeStruct((b, 2, HG_HEADS, HG_DK, HG_DV), F32)],
        scratch_shapes=[pltpu.VMEM((HG_DV, D_HG), F32)]
        + [pltpu.VMEM((HG_BLK * HG_BLK, D_HG), F32) for _ in range(HG_NBUF)]
        + [pltpu.VMEM((HG_DV, D_HG), F32) for _ in range(HG_NBUF)]
        + [pltpu.VMEM((HG_HEADS * HG_BLK, D_HG), BF16) for _ in range(HG_NBUF)] + [scr() for _ in range(7)],
        compiler_params=_cparams(("parallel", "parallel", "arbitrary")),
        name="hgrn2_scan",
    )(p, p, p, lb.reshape(2, 1, D_HG), tri, ones, headb, s0)


HY_FEAT_PAD = 128
HY_GEN_ROWS = 256


def _dot_hi(a, b):
    ah = _bf(a)
    al = _bf(a - ah.astype(F32))
    bh = _bf(b)
    bl = _bf(b - bh.astype(F32))
    return _dot(ah, bh) + _dot(ah, bl) + _dot(al, bh)


HY_HALF_COLS = HY_ORDER * D_HY


def _filter_gen_kernel(feat_ref, w1_ref, b1_ref, w2_ref, b2_ref, w3_ref, fr_ref, dl_ref, h_ref, ssq_ref, *, seq):
    i = pl.program_id(0)
    tm = feat_ref.shape[0]
    feat = feat_ref[...]
    fr = fr_ref[...]
    h = jnp.sin(fr * (_dot_hi(feat, w1_ref[...]) + b1_ref[...]))
    h = jnp.sin(fr * (_dot_hi(h, w2_ref[...]) + b2_ref[...]))
    h = _dot_hi(h, w3_ref[...])
    decay = jnp.exp(-feat[:, 0:1] * dl_ref[...])
    decay = jnp.concatenate([decay] * HY_ORDER, axis=1)
    row = i * tm + lax.broadcasted_iota(jnp.int32, (tm, HY_HALF_COLS), 0)
    taps = h * decay
    h_ref[...] = jnp.where(row == seq, 0.0, taps)

    @pl.when(i == 0)
    def _():
        ssq_ref[...] = jnp.zeros_like(ssq_ref)

    ssq_ref[...] += jnp.sum(taps * taps, axis=0, keepdims=True)


def _hyena_features(seq):
    order = np.concatenate([np.arange(seq), [0], np.arange(seq - 1, 0, -1)])
    pos = order.astype(np.float32)
    t = pos / np.float32(max(seq - 1, 1))
    bands = np.linspace(1e-4, HY_BANDS - 1, HY_BANDS, dtype=np.float32)
    ang = np.float32(2.0 * math.pi / seq) * pos[:, None] * bands[None, :]
    feat = np.concatenate([t[:, None], np.cos(ang), np.sin(ang)], axis=-1).astype(np.float32)
    out = np.zeros((2 * seq, HY_FEAT_PAD), np.float32)
    out[:, :HY_EMB] = feat
    return jnp.asarray(out)


def hyena_filter_gen(seq, w1, b1, w2, b2, w3, freq):
    feat = _hyena_features(seq)
    w1p = jnp.zeros((HY_FEAT_PAD, HY_FFN), F32).at[:HY_EMB].set(w1.astype(F32))
    deltas = jnp.asarray(np.linspace(HY_DECAY_SLOW, HY_DECAY_FAST, D_HY, dtype=np.float32)).reshape(1, D_HY)
    tm = HY_GEN_ROWS
    return pl.pallas_call(
        functools.partial(_filter_gen_kernel, seq=seq),
        grid=(2 * seq // tm,),
        in_specs=[
            pl.BlockSpec((tm, HY_FEAT_PAD), lambda i: (i, 0)),
            _const_spec((HY_FEAT_PAD, HY_FFN)), _const_spec((1, HY_FFN)),
            _const_spec((HY_FFN, HY_FFN)), _const_spec((1, HY_FFN)),
            pl.BlockSpec((HY_FFN, HY_HALF_COLS), lambda i: (0, (i * tm) // seq)),
            _const_spec((1, HY_FFN)), _const_spec((1, D_HY)),
        ],
        out_specs=[pl.BlockSpec((tm, HY_HALF_COLS), lambda i: (i, 0)), _const_spec((1, HY_HALF_COLS))],
        out_shape=[jax.ShapeDtypeStruct((2 * seq, HY_HALF_COLS), F32),
                   jax.ShapeDtypeStruct((1, HY_HALF_COLS), F32)],
        compiler_params=_cparams(("arbitrary",)),
        name="hyena_filter_gen",
    )(feat, w1p, b1.reshape(1, HY_FFN), w2, b2.reshape(1, HY_FFN), w3, freq.reshape(1, HY_FFN), deltas)


def _filter_norm(ssq_ref, n):
    return lax.rsqrt(ssq_ref[:, n * D_HY:(n + 1) * D_HY])


def _conv3(pad_ref, w_ref, b_ref, r0, rows, pad):
    acc = b_ref[...]
    for k in range(HY_SHORT):
        lo = r0 + pad - HY_SHORT // 2 + k
        acc = acc + w_ref[k:k + 1, :] * pad_ref[lo:lo + rows, :]
    return acc


def _dft_constants(seq):
    n = 2 * seq
    k = np.arange(n)
    ang = 2.0 * np.pi * np.outer(k, k) / n
    fwd = np.concatenate([np.cos(ang), -np.sin(ang)], axis=0)
    inv = np.concatenate([np.cos(ang[:seq]), -np.sin(ang[:seq])], axis=1) / n
    return jnp.asarray(fwd, BF16), jnp.asarray(inv, BF16)


def _ctx_spectrum_kernel(filt_ref, ssq_ref, fwd_ref, h_ref):
    n = pl.program_id(0)
    norm = jnp.where(n == 0, _filter_norm(ssq_ref, 0), _filter_norm(ssq_ref, 1))
    x = filt_ref[...] * norm
    xh = _bf(x)
    xl = _bf(x - xh.astype(F32))
    h_ref[0] = _dot(fwd_ref[...], xh) + _dot(fwd_ref[...], xl)


def hyena_ctx_spectrum(filt, ssq):
    n = filt.shape[0]
    fwd, _ = _dft_constants(n // 2)
    return pl.pallas_call(
        _ctx_spectrum_kernel,
        grid=(HY_ORDER,),
        in_specs=[pl.BlockSpec((n, D_HY), lambda o: (0, o)), _const_spec((1, HY_HALF_COLS)),
                  _const_spec((2 * n, n))],
        out_specs=pl.BlockSpec((1, 2 * n, D_HY), lambda o: (o, 0, 0)),
        out_shape=jax.ShapeDtypeStruct((HY_ORDER, 2 * n, D_HY), F32),
        compiler_params=_cparams(("parallel",)),
        name="hyena_ctx_spectrum",
    )(filt, ssq, fwd)


HY_PAD = 8


def _hyena_ctx_kernel(v_ref, x1_ref, x2_ref, cw_ref, cb_ref, bias_ref, spec_ref, fwd_ref, inv_ref, o_ref, pad_ref):
    seq = v_ref.shape[1]
    n = 2 * seq
    zeros = jnp.zeros((HY_PAD, D_HY), F32)
    pad_ref[0:HY_PAD, :] = zeros
    pad_ref[HY_PAD + seq:2 * HY_PAD + seq, :] = zeros

    def short_conv(src_ref, part):
        pad_ref[HY_PAD:HY_PAD + seq, :] = src_ref[0]
        return _conv3(pad_ref, cw_ref.at[part], cb_ref.at[part], 0, seq, HY_PAD)

    z = short_conv(v_ref, 0)
    for order, gate_ref in enumerate((x1_ref, x2_ref)):
        spec = _dot(fwd_ref[:, 0:seq], _bf(z))
        xr, xi = spec[:n], spec[n:]
        hr, hi = spec_ref[order, 0:n, :], spec_ref[order, n:2 * n, :]
        prod = jnp.concatenate([xr * hr - xi * hi, xr * hi + xi * hr], axis=0)
        conv = _dot(inv_ref[...], _bf(prod))
        z = short_conv(gate_ref, order + 1) * (conv + bias_ref[order:order + 1, :] * z)
    o_ref[0] = z


def hyena_ctx(p, conv_w, conv_b, bias, spec):
    b, seq, _ = p.shape
    fwd, inv = _dft_constants(seq)
    col = lambda c: pl.BlockSpec((1, seq, D_HY), lambda i: (i, 0, c))
    c0 = COL_HY // D_HY
    return pl.pallas_call(
        _hyena_ctx_kernel,
        grid=(b,),
        in_specs=[
            col(c0), col(c0 + 1), col(c0 + 2),
            _const_spec((3, HY_SHORT, D_HY)), _const_spec((3, 1, D_HY)), _const_spec((HY_ORDER, D_HY)),
            _const_spec((HY_ORDER, 4 * seq, D_HY)), _const_spec((4 * seq, 2 * seq)), _const_spec((seq, 4 * seq)),
        ],
        out_specs=pl.BlockSpec((1, seq, D_HY), lambda i: (i, 0, 0)),
        out_shape=jax.ShapeDtypeStruct((b, seq, D_HY), F32),
        scratch_shapes=[pltpu.VMEM((seq + 2 * HY_PAD, D_HY), F32)],
        compiler_params=_cparams(("parallel",)),
        name="hyena_ctx",
    )(p, p, p, conv_w.reshape(HY_SHORT, 3, D_HY).transpose(1, 0, 2), conv_b.reshape(3, 1, D_HY), bias, spec,
      fwd, inv)


FFT_N1 = 64
FFT_N2 = 128
FFT_N = FFT_N1 * FFT_N2
FFT_K1 = FFT_N1 // 2 + 1
FFT_K1_PAD = 40
FFT_ROWS = FFT_K1_PAD * FFT_N2
HY_LANES = 128
FFT_UNROLL1 = 16
FFT_UNROLL2 = 11


def _fft_constants():
    n1, n2, n, kp = FFT_N1, FFT_N2, FFT_N, FFT_K1_PAD
    k1 = np.arange(kp)
    w1 = np.exp(-2j * np.pi * np.outer(k1, np.arange(n1)) / n1)
    tw = np.exp(-2j * np.pi * np.outer(np.arange(n2), k1) / n)
    f1c = tw[:, :, None] * w1[None, :, :]
    f1t = np.concatenate([f1c.real, f1c.imag], axis=1).transpose(0, 2, 1)
    w2 = np.exp(-2j * np.pi * np.outer(np.arange(n2), np.arange(n2)) / n2)
    f2 = np.block([[w2.real, -w2.imag], [w2.imag, w2.real]])
    g2c = np.conj(w2)
    g2 = np.block([[g2c.real, -g2c.imag], [g2c.imag, g2c.real]])
    weight = np.where((k1 == 0) | (k1 == n1 // 2), 1.0, 2.0) * (k1 < FFT_K1)
    mc = np.conj(w1.T)[None, :n1 // 2, :] * np.conj(tw)[:, None, :] * weight / n
    g1 = np.concatenate([mc.real, -mc.imag], axis=2)
    return (jnp.asarray(f1t, BF16), jnp.asarray(f2, BF16), jnp.asarray(g2, BF16), jnp.asarray(g1, BF16))


def _dot_tn(a, b):
    return lax.dot_general(a, b, (((0,), (0,)), ((), ())), preferred_element_type=F32)


def _fft_stage1(x_ref, f1t_ref, a_ref, n1_in):
    def body(n2, carry):
        rows = x_ref[pl.ds(n2, n1_in, stride=FFT_N2), :]
        out = _dot_tn(f1t_ref[n2, 0:n1_in, :], _bf(rows))
        a_ref[0, pl.ds(n2, FFT_K1, stride=FFT_N2), :] = out[:FFT_K1]
        a_ref[1, pl.ds(n2, FFT_K1, stride=FFT_N2), :] = out[FFT_K1_PAD:FFT_K1_PAD + FFT_K1]
        return carry
    lax.fori_loop(0, FFT_N2, body, 0, unroll=FFT_UNROLL1)


def _lat_spectrum_kernel(filt_ref, ssq_ref, f1t_ref, f2_ref, h_ref, a_ref):
    n = pl.program_id(0)
    half = pl.program_id(1)
    norm_full = jnp.where(n == 0, _filter_norm(ssq_ref, 0), _filter_norm(ssq_ref, 1))
    norm = jnp.where(half == 0, norm_full[:, :HY_LANES], norm_full[:, HY_LANES:])
    _fft_stage1(filt_ref, f1t_ref, a_ref, FFT_N1)

    def stage2(k1, carry):
        r0 = pl.multiple_of(k1 * FFT_N2, FFT_N2)
        rows = pl.ds(r0, FFT_N2)
        a = jnp.concatenate([a_ref[0, rows, :], a_ref[1, rows, :]], axis=0)
        b = _dot(f2_ref[...], _bf(a)) * norm
        h_ref[0, 0, rows, :] = _bf(b[:FFT_N2])
        h_ref[0, 1, rows, :] = _bf(b[FFT_N2:])
        return carry
    lax.fori_loop(0, FFT_K1, stage2, 0)


def hyena_lat_spectrum(filt, ssq):
    f1t, f2, _, _ = _fft_constants()
    rows = FFT_K1 * FFT_N2
    halves = D_HY // HY_LANES
    return pl.pallas_call(
        _lat_spectrum_kernel,
        grid=(HY_ORDER, halves),
        in_specs=[
            pl.BlockSpec((FFT_N, HY_LANES), lambda o, h: (0, o * halves + h)),
            _const_spec((1, HY_HALF_COLS)),
            _const_spec((FFT_N2, FFT_N1, 2 * FFT_K1_PAD)),
            _const_spec((2 * FFT_N2, 2 * FFT_N2)),
        ],
        out_specs=pl.BlockSpec((1, 2, rows, HY_LANES), lambda o, h: (o, 0, 0, h)),
        out_shape=jax.ShapeDtypeStruct((HY_ORDER, 2, rows, D_HY), BF16),
        scratch_shapes=[pltpu.VMEM((2, FFT_ROWS, HY_LANES), F32)],
        compiler_params=_cparams(("parallel", "parallel")),
        name="hyena_lat_spectrum",
    )(filt, ssq, f1t, f2)


HY_ROWS = 512


def _hyena_lat_kernel(v_ref, x1_ref, x2_ref, cw_ref, cb_ref, bias_ref, spec_ref, f1t_ref, f2_ref, g2_ref, g1_ref,
                      o_ref, pad_ref, z_ref, a_ref):
    seq = v_ref.shape[1]
    zeros = jnp.zeros((HY_PAD, HY_LANES), F32)
    pad_ref[0:HY_PAD, :] = zeros
    pad_ref[HY_PAD + seq:2 * HY_PAD + seq, :] = zeros
    pad_ref[HY_PAD:HY_PAD + seq, :] = v_ref[0]
    for ti in range(seq // HY_ROWS):
        r0 = ti * HY_ROWS
        z_ref[r0:r0 + HY_ROWS, :] = _conv3(pad_ref, cw_ref.at[0], cb_ref.at[0], r0, HY_ROWS, HY_PAD)
    tail = jnp.zeros((FFT_ROWS - FFT_K1 * FFT_N2, HY_LANES), F32)
    a_ref[0, FFT_K1 * FFT_N2:FFT_ROWS, :] = tail
    a_ref[1, FFT_K1 * FFT_N2:FFT_ROWS, :] = tail

    for order, gate_ref in enumerate((x1_ref, x2_ref)):
        _fft_stage1(z_ref, f1t_ref, a_ref, FFT_N1 // 2)

        def stage2(k1, carry):
            r0 = pl.multiple_of(k1 * FFT_N2, FFT_N2)
            rows = pl.ds(r0, FFT_N2)
            a = jnp.concatenate([a_ref[0, rows, :], a_ref[1, rows, :]], axis=0)
            b = _dot(f2_ref[...], _bf(a))
            br, bi = b[:FFT_N2], b[FFT_N2:]
            hr = spec_ref[order, 0, rows, :].astype(F32)
            hi = spec_ref[order, 1, rows, :].astype(F32)
            y = jnp.concatenate([br * hr - bi * hi, br * hi + bi * hr], axis=0)
            c = _dot(g2_ref[...], _bf(y))
            a_ref[0, rows, :] = c[:FFT_N2]
            a_ref[1, rows, :] = c[FFT_N2:]
            return carry
        lax.fori_loop(0, FFT_K1, stage2, 0, unroll=FFT_UNROLL2)

        def stage1_inv(n2, carry):
            c = jnp.concatenate([a_ref[0, pl.ds(n2, FFT_K1_PAD, stride=FFT_N2), :],
                                 a_ref[1, pl.ds(n2, FFT_K1_PAD, stride=FFT_N2), :]], axis=0)
            o_ref[0, pl.ds(n2, FFT_N1 // 2, stride=FFT_N2), :] = _dot(g1_ref[n2], _bf(c))
            return carry
        lax.fori_loop(0, FFT_N2, stage1_inv, 0, unroll=FFT_UNROLL1)

        pad_ref[HY_PAD:HY_PAD + seq, :] = gate_ref[0]
        for ti in range(seq // HY_ROWS):
            r0 = ti * HY_ROWS
            rows = slice(r0, r0 + HY_ROWS)
            gate = _conv3(pad_ref, cw_ref.at[order + 1], cb_ref.at[order + 1], r0, HY_ROWS, HY_PAD)
            z = gate * (o_ref[0, rows, :] + bias_ref[order:order + 1, :] * z_ref[rows, :])
            if order + 1 < HY_ORDER:
                z_ref[rows, :] = z
            else:
                o_ref[0, rows, :] = z


def hyena_lat(p, conv_w, conv_b, bias, spec):
    b, seq, _ = p.shape
    assert 2 * seq == FFT_N
    f1t, f2, g2, g1 = _fft_constants()
    hw = HY_LANES
    col = lambda c: pl.BlockSpec((1, seq, hw), lambda i, h: (i, 0, (COL_HY + c * D_HY) // hw + h))
    return pl.pallas_call(
        _hyena_lat_kernel,
        grid=(b, D_HY // hw),
        in_specs=[
            col(0), col(1), col(2),
            pl.BlockSpec((3, HY_SHORT, hw), lambda i, h: (0, 0, h)),
            pl.BlockSpec((3, 1, hw), lambda i, h: (0, 0, h)),
            pl.BlockSpec((HY_ORDER, hw), lambda i, h: (0, h)),
            pl.BlockSpec((HY_ORDER, 2, FFT_K1 * FFT_N2, hw), lambda i, h: (0, 0, 0, h)),
            _const_spec((FFT_N2, FFT_N1, 2 * FFT_K1_PAD)),
            _const_spec((2 * FFT_N2, 2 * FFT_N2)),
            _const_spec((2 * FFT_N2, 2 * FFT_N2)),
            _const_spec((FFT_N2, FFT_N1 // 2, 2 * FFT_K1_PAD)),
        ],
        out_specs=pl.BlockSpec((1, seq, hw), lambda i, h: (i, 0, h)),
        out_shape=jax.ShapeDtypeStruct((b, seq, D_HY), F32),
        scratch_shapes=[
            pltpu.VMEM((seq + 2 * HY_PAD, hw), F32),
            pltpu.VMEM((seq, hw), F32),
            pltpu.VMEM((2, FFT_ROWS, hw), F32),
        ],
        compiler_params=_cparams(("parallel", "parallel")),
        name="hyena_lat",
    )(p, p, p, conv_w.reshape(HY_SHORT, 3, D_HY).transpose(1, 0, 2), conv_b.reshape(3, 1, D_HY), bias, spec,
      f1t, f2, g2, g1)


HG_SEG_ROWS = 512
COND_ROWS = 16


def _trunk_layer(x, mods, lp, lb, batch, seq, ctx, final):
    x = ffn_half_step(x, mods, lp['ln_ffn1'], lp['w_ffn1_in'], lp['w_ffn1_out'], lp['final_norm'],
                      mod_base=0, final=False, seq_len=seq)
    p2d = mixer_in_proj(x, mods, lp['ln_mix'], lp['w_in'], seq_len=seq)
    p = p2d.reshape(batch, seq, N_IN)

    filt, ssq = hyena_filter_gen(seq, lp['hy_w1'], lp['hy_b1'], lp['hy_w2'], lp['hy_b2'], lp['hy_w3'],
                                 lp['hy_freq'])
    if ctx is None:
        att = context_attention(p, lp['attn_sink'])
        lru_h0 = jnp.zeros((batch, 2, D_LRU), F32)
        hg_s0 = jnp.zeros((batch, 2, HG_HEADS, HG_DK, HG_DV), F32)
        hy = hyena_ctx(p, lp['hy_conv_w'], lp['hy_conv_b'], lp['hy_bias'], hyena_ctx_spectrum(filt, ssq))
    else:
        k_ctx, v_ctx, lru_h0, hg_state = ctx
        att = latent_attention(p, k_ctx, v_ctx, lp['attn_sink'])
        hg_s0 = hg_state.astype(F32)
        hy = hyena_lat(p, lp['hy_conv_w'], lp['hy_conv_b'], lp['hy_bias'], hyena_lat_spectrum(filt, ssq))
    lru, lru_state = rglru_mixer(p, lp['lru_conv_w'], lp['lru_conv_b'], lp['lru_w_r'], lp['lru_b_r'],
                                 lp['lru_w_i'], lp['lru_b_i'], lp['lru_lambda'], lru_h0.astype(F32))
    o_dirs, hg_state_out = hgrn2_scan(p, lb, hg_s0, min(seq, HG_SEG_ROWS))
    t = batch * seq

    x = gated_merge(x, mods, lp['ln_mix'], att.reshape(t, D_ATTN), lru.reshape(t, D_LRU),
                    o_dirs.reshape(2, t, D_HG), p2d, lp['hg_norm'], hy.reshape(t, D_HY), lp['w_gate'],
                    lp['w_bo_attn'], lp['w_bo_lru'], lp['w_bo_hg'], lp['w_bo_hy'], lp['w_out'], seq_len=seq)
    x = ffn_half_step(x, mods, lp['ln_ffn2'], lp['w_ffn2_in'], lp['w_ffn2_out'], lp['final_norm'],
                      mod_base=6, final=final, seq_len=seq)
    state = None
    if ctx is None:
        k = p[:, :, COL_K:COL_K + D_KV].reshape(batch, seq, N_KV_HEADS, HEAD_DIM)
        v = p[:, :, COL_V:COL_V + D_KV].reshape(batch, seq, N_KV_HEADS, HEAD_DIM)
        state = (k, v, lru_state, hg_state_out)
    return x, state


def kernel(x_prompt, x_sample, cache_k, cache_v, state_lru, state_hgrn, c, c_ctx, ln_ffn1, ln_mix, ln_ffn2, w_ada, b_ada, w_ffn1_in, w_ffn1_out, w_ffn2_in, w_ffn2_out, w_in, attn_sink, lru_conv_w, lru_conv_b, lru_w_r, lru_b_r, lru_w_i, lru_b_i, lru_lambda, hg_lb_logits, hg_norm, hy_conv_w, hy_conv_b, hy_w1, hy_b1, hy_w2, hy_b2, hy_w3, hy_freq, hy_bias, w_bo_attn, w_bo_lru, w_bo_hg, w_bo_hy, w_gate, w_out, final_norm):
    batch, seq, _ = x_prompt.shape
    dec_batch, dec_seq, _ = x_sample.shape
    lb_soft = jax.nn.softmax(hg_lb_logits.astype(F32), axis=0)
    lb_all = jnp.cumsum(lb_soft, axis=0) - lb_soft[0]

    stacked = dict(ln_ffn1=ln_ffn1, ln_mix=ln_mix, ln_ffn2=ln_ffn2, w_ffn1_in=w_ffn1_in, w_ffn1_out=w_ffn1_out,
                   w_ffn2_in=w_ffn2_in, w_ffn2_out=w_ffn2_out, w_in=w_in, attn_sink=attn_sink,
                   lru_conv_w=lru_conv_w, lru_conv_b=lru_conv_b, lru_w_r=lru_w_r, lru_b_r=lru_b_r,
                   lru_w_i=lru_w_i, lru_b_i=lru_b_i, lru_lambda=lru_lambda, hg_norm=hg_norm,
                   hy_conv_w=hy_conv_w, hy_conv_b=hy_conv_b, hy_w1=hy_w1, hy_b1=hy_b1, hy_w2=hy_w2, hy_b2=hy_b2,
                   hy_w3=hy_w3, hy_freq=hy_freq, hy_bias=hy_bias, w_bo_attn=w_bo_attn, w_bo_lru=w_bo_lru,
                   w_bo_hg=w_bo_hg, w_bo_hy=w_bo_hy, w_gate=w_gate, w_out=w_out)

    cond = jnp.zeros((COND_ROWS, D_MODEL), F32).at[0].set(c_ctx).at[1:1 + dec_batch].set(c)
    mods = ada_mods(cond, w_ada, b_ada).reshape(DEPTH, COND_ROWS, N_MOD, D_MODEL)

    h = x_prompt.reshape(batch * seq, D_MODEL)
    z = x_sample.reshape(dec_batch * dec_seq, D_MODEL)
    ks, vs, lrus, hgs = [], [], [], []
    for l in range(DEPTH):
        lp = {name: w[l] for name, w in stacked.items()}
        lp['final_norm'] = final_norm
        final = l == DEPTH - 1
        h, (k_l, v_l, lru_l, hg_l) = _trunk_layer(h, mods[l, 0:1], lp, lb_all[l], batch, seq, None, final)
        ks.append(k_l)
        vs.append(v_l)
        lrus.append(lru_l)
        hgs.append(hg_l)
        ctx = (cache_k[:, l].reshape(dec_batch, -1, D_KV), cache_v[:, l].reshape(dec_batch, -1, D_KV),
               state_lru[:, l], state_hgrn[:, l])
        z, _ = _trunk_layer(z, mods[l, 1:1 + dec_batch], lp, lb_all[l], dec_batch, dec_seq, ctx, final)

    y_prompt = h.reshape(batch, seq, D_MODEL)
    y_sample = z.reshape(dec_batch, dec_seq, D_MODEL)
    return (y_prompt, y_sample, jnp.stack(ks, axis=1), jnp.stack(vs, axis=1), jnp.stack(lrus, axis=1),
            jnp.stack(hgs, axis=1))
```

```python
import functools
import math

import numpy as np
import jax
import jax.numpy as jnp
from jax import lax
from jax.experimental import pallas as pl
from jax.experimental.pallas import tpu as pltpu

F32 = jnp.float32
BF16 = jnp.bfloat16

D_MODEL = 1024
DEPTH = 2
GRID_W = 64
EPS = 1e-6
N_MOD = 9
N_BRANCH = 4
D_FF = 2816
N_HEADS = 8
N_KV_HEADS = 2
HEAD_DIM = 64
GQA_GROUP = N_HEADS // N_KV_HEADS
D_ATTN = N_HEADS * HEAD_DIM
D_KV = N_KV_HEADS * HEAD_DIM
WINDOW = 128
ATTN_BLOCK = 128
ATTN_SCALE = HEAD_DIM ** -0.5
ROPE_BASE = 10000.0
D_LRU = D_MODEL // 4
LRU_CONV = 4
LRU_C = 8.0
HG_HEADS = 4
HG_DK = 64
HG_DV = 64
D_HG = HG_HEADS * HG_DK
HG_SCALE = HG_DK ** -0.5
HG_BLK = 16
D_HY = D_MODEL // 4
HY_ORDER = 2
HY_SHORT = 3
HY_BANDS = 8
HY_EMB = 2 * HY_BANDS + 1
HY_FFN = 64
HY_TARGET = 1e-2
HY_DECAY_SLOW = -math.log(HY_TARGET) / 1.5
HY_DECAY_FAST = -math.log(HY_TARGET) / 0.3
N_IN = D_ATTN + 2 * D_KV + 2 * D_LRU + 5 * D_HG + 3 * D_HY

COL_K = D_ATTN
COL_V = D_ATTN + D_KV
COL_RX = D_ATTN + 2 * D_KV
COL_RG = COL_RX + D_LRU
COL_HQ = COL_RG + D_LRU
COL_HFF = COL_HQ + D_HG
COL_HFB = COL_HFF + D_HG
COL_HI = COL_HFB + D_HG
COL_HGATE = COL_HI + D_HG
COL_HY = COL_HGATE + D_HG

FF_CHUNK = 256
N_FF_CHUNK = D_FF // FF_CHUNK
ROW_TILE = 512
VMEM_LIMIT = 56 * 1024 * 1024


def _bf(x):
    return x.astype(BF16)


def _dot(a, b):
    return jnp.dot(a, b, preferred_element_type=F32)


def _split3(x):
    hi = _bf(x)
    r1 = x - hi.astype(F32)
    mid = _bf(r1)
    lo = _bf(r1 - mid.astype(F32))
    return hi, mid, lo


def _dot_exact_rhs(mat_bf, x):
    hi, mid, lo = _split3(x)
    return _dot(mat_bf, hi) + _dot(mat_bf, mid) + _dot(mat_bf, lo)


def _sigmoid(x):
    return jax.nn.sigmoid(x)


def _silu(x):
    return x * _sigmoid(x)


def _norm_mod(x, lnw, shift, scale):
    ms = jnp.mean(x * x, axis=-1, keepdims=True)
    y = x * lax.rsqrt(ms + EPS) * lnw
    return y * (1.0 + scale) + shift


def _cparams(sem):
    return pltpu.CompilerParams(dimension_semantics=sem, vmem_limit_bytes=VMEM_LIMIT)


def _const_spec(shape):
    nd = len(shape)
    return pl.BlockSpec(shape, lambda *_: (0,) * nd)


def _ada_kernel(cond_ref, w_ref, b_ref, o_ref):
    c = cond_ref[...]
    o_ref[0] = _dot(_bf(_silu(c)), _bf(w_ref[0])) + b_ref[0]


def ada_mods(cond, w_ada, b_ada):
    r = cond.shape[0]
    cb = 1024
    ncol = (N_MOD * D_MODEL) // cb
    return pl.pallas_call(
        _ada_kernel,
        grid=(DEPTH, ncol),
        in_specs=[
            pl.BlockSpec((r, D_MODEL), lambda l, j: (0, 0)),
            pl.BlockSpec((1, D_MODEL, cb), lambda l, j: (l, 0, j)),
            pl.BlockSpec((1, 1, cb), lambda l, j: (l, 0, j)),
        ],
        out_specs=pl.BlockSpec((1, r, cb), lambda l, j: (l, 0, j)),
        out_shape=jax.ShapeDtypeStruct((DEPTH, r, N_MOD * D_MODEL), F32),
        compiler_params=_cparams(("parallel", "parallel")),
        name="ada_mods",
    )(cond, w_ada, b_ada.reshape(DEPTH, 1, N_MOD * D_MODEL))


def _seq_mod_index(seq_len, n_mod, tm):
    def idx(i):
        return jnp.minimum((i * tm) // seq_len, n_mod - 1)
    return idx


def _ffn_kernel(x_ref, mod_ref, ln_ref, wg_ref, wu_ref, wo_ref, fn_ref, o_ref, *, mod_base, final):
    x = x_ref[...]
    m = mod_ref[0]
    sh = m[mod_base:mod_base + 1]
    sc = m[mod_base + 1:mod_base + 2]
    g = m[mod_base + 2:mod_base + 3]
    h = _bf(_norm_mod(x, ln_ref[...], sh, sc))
    acc = None
    for c in range(N_FF_CHUNK):
        gate = _dot(h, wg_ref[c])
        up = _dot(h, wu_ref[c])
        part = _dot(_bf(_silu(gate) * up), wo_ref[c])
        acc = part if acc is None else acc + part
    y = x + 0.5 * g * acc
    if final:
        ms = jnp.mean(y * y, axis=-1, keepdims=True)
        y = y * lax.rsqrt(ms + EPS) * fn_ref[...]
    o_ref[...] = y


def ffn_half_step(x, mods, ln_w, w_in, w_out, final_w, *, mod_base, final, seq_len):
    t = x.shape[0]
    tm = ROW_TILE
    wg = _bf(w_in[:, :D_FF]).reshape(D_MODEL, N_FF_CHUNK, FF_CHUNK).transpose(1, 0, 2)
    wu = _bf(w_in[:, D_FF:]).reshape(D_MODEL, N_FF_CHUNK, FF_CHUNK).transpose(1, 0, 2)
    wo = _bf(w_out).reshape(N_FF_CHUNK, FF_CHUNK, D_MODEL)
    midx = _seq_mod_index(seq_len, mods.shape[0], tm)
    return pl.pallas_call(
        functools.partial(_ffn_kernel, mod_base=mod_base, final=final),
        grid=(t // tm,),
        in_specs=[
            pl.BlockSpec((tm, D_MODEL), lambda i: (i, 0)),
            pl.BlockSpec((1, N_MOD, D_MODEL), lambda i: (midx(i), 0, 0)),
            _const_spec((1, D_MODEL)),
            _const_spec((N_FF_CHUNK, D_MODEL, FF_CHUNK)),
            _const_spec((N_FF_CHUNK, D_MODEL, FF_CHUNK)),
            _const_spec((N_FF_CHUNK, FF_CHUNK, D_MODEL)),
            _const_spec((1, D_MODEL)),
        ],
        out_specs=pl.BlockSpec((tm, D_MODEL), lambda i: (i, 0)),
        out_shape=jax.ShapeDtypeStruct((t, D_MODEL), F32),
        compiler_params=_cparams(("parallel",)),
        name="ffn_half_step",
    )(x, mods, ln_w.reshape(1, D_MODEL), wg, wu, wo, final_w.reshape(1, D_MODEL))


PROJ_CHUNK = 256


def _proj_kernel(x_ref, mod_ref, ln_ref, w_ref, o_ref):
    x = x_ref[...]
    m = mod_ref[0]
    u = _bf(_norm_mod(x, ln_ref[...], m[3:4], m[4:5]))
    for c in range(N_IN // PROJ_CHUNK):
        sl = slice(c * PROJ_CHUNK, (c + 1) * PROJ_CHUNK)
        o_ref[:, sl] = _dot(u, w_ref[:, sl])


def mixer_in_proj(x, mods, ln_w, w_in, *, seq_len):
    t = x.shape[0]
    tm = ROW_TILE
    midx = _seq_mod_index(seq_len, mods.shape[0], tm)
    return pl.pallas_call(
        _proj_kernel,
        grid=(t // tm,),
        in_specs=[
            pl.BlockSpec((tm, D_MODEL), lambda i: (i, 0)),
            pl.BlockSpec((1, N_MOD, D_MODEL), lambda i: (midx(i), 0, 0)),
            _const_spec((1, D_MODEL)),
            _const_spec((D_MODEL, N_IN)),
        ],
        out_specs=pl.BlockSpec((tm, N_IN), lambda i: (i, 0)),
        out_shape=jax.ShapeDtypeStruct((t, N_IN), F32),
        compiler_params=_cparams(("parallel",)),
        name="mixer_in_proj",
    )(x, mods, ln_w.reshape(1, D_MODEL), _bf(w_in))


MERGE_CHUNK = 256


def _merge_kernel(x_ref, mod_ref, ln_ref, att_ref, lru_ref, hgf_ref, hgb_ref, hgg_ref, hgn_ref, headb_ref, hy_ref,
                  wg_ref, wa_ref, wl_ref, wh_ref, wy_ref, wo_ref, o_ref, mix_ref):
    x = x_ref[...]
    m = mod_ref[0]
    u = _bf(_norm_mod(x, ln_ref[...], m[3:4], m[4:5]))
    o = hgf_ref[0] + hgb_ref[0]
    sq = o * o
    sq_hi = _bf(sq)
    sq_lo = _bf(sq - sq_hi.astype(F32))
    ms = (_dot(sq_hi, headb_ref[...]) + _dot(sq_lo, headb_ref[...])) * (1.0 / HG_DV)
    hg = o * lax.rsqrt(ms + EPS) * hgn_ref[...] * _silu(hgg_ref[...])
    branches = (_bf(att_ref[...]), _bf(lru_ref[...]), _bf(hg), _bf(hy_ref[...]))
    w_bo = (wa_ref, wl_ref, wh_ref, wy_ref)
    for c in range(D_MODEL // MERGE_CHUNK):
        sl = slice(c * MERGE_CHUNK, (c + 1) * MERGE_CHUNK)
        mixed = None
        for n in range(N_BRANCH):
            term = _sigmoid(_dot(u, wg_ref[n, :, sl])) * _dot(branches[n], w_bo[n][:, sl])
            mixed = term if mixed is None else mixed + term
        mix_ref[:, sl] = _bf(mixed)
    o_ref[...] = x + m[5:6] * _dot(mix_ref[...], wo_ref[...])


def gated_merge(x, mods, ln_w, att, lru, hg_dirs, p2d, hg_norm, hy, w_gate, w_bo_attn, w_bo_lru, w_bo_hg, w_bo_hy,
                w_out, *, seq_len):
    t = x.shape[0]
    tm = ROW_TILE
    midx = _seq_mod_index(seq_len, mods.shape[0], tm)
    row = lambda w: pl.BlockSpec((tm, w), lambda i: (i, 0))
    _, _, headb = _hg_constants()
    return pl.pallas_call(
        _merge_kernel,
        grid=(t // tm,),
        in_specs=[
            row(D_MODEL),
            pl.BlockSpec((1, N_MOD, D_MODEL), lambda i: (midx(i), 0, 0)),
            _const_spec((1, D_MODEL)),
            row(D_ATTN), row(D_LRU),
            pl.BlockSpec((1, tm, D_HG), lambda i: (0, i, 0)),
            pl.BlockSpec((1, tm, D_HG), lambda i: (1, i, 0)),
            pl.BlockSpec((tm, D_HG), lambda i: (i, COL_HGATE // D_HG)),
            _const_spec((1, D_HG)),
            _const_spec((D_HG, D_HG)),
            row(D_HY),
            _const_spec((N_BRANCH, D_MODEL, D_MODEL)),
            _const_spec((D_ATTN, D_MODEL)),
            _const_spec((D_LRU, D_MODEL)),
            _const_spec((D_HG, D_MODEL)),
            _const_spec((D_HY, D_MODEL)),
            _const_spec((D_MODEL, D_MODEL)),
        ],
        out_specs=row(D_MODEL),
        out_shape=jax.ShapeDtypeStruct((t, D_MODEL), F32),
        scratch_shapes=[pltpu.VMEM((tm, D_MODEL), BF16)],
        compiler_params=_cparams(("parallel",)),
        name="gated_merge",
    )(x, mods, ln_w.reshape(1, D_MODEL), att, lru, hg_dirs, hg_dirs, p2d, hg_norm.reshape(1, D_HG), headb, hy,
      _bf(w_gate), _bf(w_bo_attn), _bf(w_bo_lru), _bf(w_bo_hg), _bf(w_bo_hy), _bf(w_out))


def _dot_nt(a, b):
    return lax.dot_general(a, b, (((1,), (1,)), ((), ())), preferred_element_type=F32)


def _ctx_attn_kernel(q_ref, k_ref, v_ref, sink_ref, o_ref):
    rows = q_ref.shape[1]
    blk = ATTN_BLOCK
    keys = _dup_heads(k_ref[0])
    vals_t = _values_t(v_ref[0])
    low = lax.broadcasted_iota(jnp.int32, (blk, 2 * HEAD_DIM), 1) < HEAD_DIM
    units = [(pair, qb) for qb in range(rows // blk) for pair in range(N_HEADS // 2)]

    def where(pair, qb):
        return slice(qb * blk, (qb + 1) * blk), slice(pair * 2 * HEAD_DIM, (pair + 1) * 2 * HEAD_DIM)

    scores = []
    for pair, qb in units:
        q_rows, lanes = where(pair, qb)
        q = q_ref[0, q_rows, lanes] * ATTN_SCALE
        q_heads = _bf(jnp.concatenate([jnp.where(low, q, 0.0), jnp.where(low, 0.0, q)], axis=0))
        scores.append(_dot_nt(keys[(2 * pair) // GQA_GROUP], q_heads))
    probs = []
    for (pair, qb), s in zip(units, scores):
        sink = jnp.concatenate([sink_ref[2 * pair:2 * pair + 1, :], sink_ref[2 * pair + 1:2 * pair + 2, :]], axis=1)
        m = jnp.maximum(jnp.max(s, axis=0, keepdims=True), sink)
        probs.append((_bf(jnp.exp(s - m)), jnp.exp(sink - m)))
    for (pair, qb), (p, sink_term) in zip(units, probs):
        q_rows, lanes = where(pair, qb)
        acc = _dot(vals_t[(2 * pair) // GQA_GROUP], p)
        out_t = acc[:HEAD_DIM, :] / (acc[HEAD_DIM:HEAD_DIM + 1, :] + sink_term)
        o_ref[0, q_rows, lanes] = jnp.transpose(jnp.concatenate([out_t[:, :blk], out_t[:, blk:]], axis=0))


def _sink_table(sink):
    return jnp.broadcast_to(sink.astype(F32)[:, None], (N_HEADS, 128))


def context_attention(p_ctx, sink):
    b, seq, _ = p_ctx.shape
    return pl.pallas_call(
        _ctx_attn_kernel,
        grid=(b,),
        in_specs=[
            pl.BlockSpec((1, seq, D_ATTN), lambda i: (i, 0, 0)),
            pl.BlockSpec((1, seq, D_KV), lambda i: (i, 0, COL_K // D_KV)),
            pl.BlockSpec((1, seq, D_KV), lambda i: (i, 0, COL_V // D_KV)),
            _const_spec((N_HEADS, 128)),
        ],
        out_specs=pl.BlockSpec((1, seq, D_ATTN), lambda i: (i, 0, 0)),
        out_shape=jax.ShapeDtypeStruct((b, seq, D_ATTN), F32),
        compiler_params=_cparams(("parallel",)),
        name="context_attention",
    )(p_ctx, p_ctx, p_ctx, _sink_table(sink))


def _rope(x, cos, sin_signed):
    n = x.shape[-1]
    lane = lax.broadcasted_iota(jnp.int32, x.shape, x.ndim - 1)
    first = (lane % (HEAD_DIM // 2)) < (HEAD_DIM // 4)
    partner = jnp.where(first, pltpu.roll(x, n - HEAD_DIM // 4, axis=x.ndim - 1),
                        pltpu.roll(x, HEAD_DIM // 4, axis=x.ndim - 1))
    return x * cos + partner * sin_signed


def _rope_tables(seq):
    rows = seq // GRID_W
    row = np.repeat(np.arange(rows, dtype=np.float32), GRID_W)
    col = np.tile(np.arange(GRID_W, dtype=np.float32), rows)
    n = HEAD_DIM // 4
    inv = (ROPE_BASE ** (-jnp.arange(n, dtype=F32) / n))
    ang_r = jnp.asarray(row)[:, None] * inv
    ang_c = jnp.asarray(col)[:, None] * inv
    cos = jnp.concatenate([jnp.cos(ang_r), jnp.cos(ang_r), jnp.cos(ang_c), jnp.cos(ang_c)], axis=-1)
    sin = jnp.concatenate([-jnp.sin(ang_r), jnp.sin(ang_r), -jnp.sin(ang_c), jnp.sin(ang_c)], axis=-1)
    return jnp.tile(cos, (1, 2)), jnp.tile(sin, (1, 2))


ATTN_PREP_ROWS = 512
ATTN_QBLOCKS = 8


def _dup_heads(x):
    lane = lax.broadcasted_iota(jnp.int32, x.shape, 1)
    swapped = pltpu.roll(x, HEAD_DIM, axis=1)
    low = lane < HEAD_DIM
    return _bf(jnp.where(low, x, swapped)), _bf(jnp.where(low, swapped, x))


def _values_t(x):
    xt = jnp.transpose(x)
    ones = jnp.ones((HEAD_DIM, x.shape[0]), F32)
    return (_bf(jnp.concatenate([xt[:HEAD_DIM], ones], axis=0)),
            _bf(jnp.concatenate([xt[HEAD_DIM:], ones], axis=0)))


def _lat_attn_kernel(q_ref, k_ref, v_ref, kc_ref, vc_ref, cos_ref, sin_ref, sink_ref, o_ref,
                     kd_ref, vt_ref, kcd_ref, vct_ref):
    seq = k_ref.shape[1]
    blk = ATTN_BLOCK
    band = 3 * blk
    past = kc_ref.shape[1]
    i = pl.program_id(1)

    @pl.when(i == 0)
    def _():
        for ti in range(seq // ATTN_PREP_ROWS):
            rows = slice(ti * ATTN_PREP_ROWS, (ti + 1) * ATTN_PREP_ROWS)
            k0, k1 = _dup_heads(_rope(k_ref[0, rows, :], cos_ref[rows, :], sin_ref[rows, :]))
            kd_ref[0, rows, :] = k0
            kd_ref[1, rows, :] = k1
            v0, v1 = _values_t(v_ref[0, rows, :])
            vt_ref[0, :, rows] = v0
            vt_ref[1, :, rows] = v1
        kc0, kc1 = _dup_heads(kc_ref[0])
        kcd_ref[0] = kc0
        kcd_ref[1] = kc1
        vc0, vc1 = _values_t(vc_ref[0])
        vct_ref[0] = vc0
        vct_ref[1] = vc1

    key = lax.broadcasted_iota(jnp.int32, (band + past, 2 * blk), 0)
    qry = lax.broadcasted_iota(jnp.int32, (band + past, 2 * blk), 1) % blk
    low = lax.broadcasted_iota(jnp.int32, (blk, 2 * HEAD_DIM), 1) < HEAD_DIM
    subs = []
    for sub in range(ATTN_QBLOCKS):
        ib = ATTN_QBLOCKS * i + sub
        start = pl.multiple_of(jnp.clip((ib - 1) * blk, 0, seq - band), blk)
        q0 = pl.multiple_of(ib * blk, blk)
        subs.append(dict(
            rows=slice(sub * blk, (sub + 1) * blk),
            cos=cos_ref[pl.ds(q0, blk), :], sin=sin_ref[pl.ds(q0, blk), :],
            mask=(jnp.abs((q0 + qry) - (start + key)) <= WINDOW) | (key >= band),
            keys=[jnp.concatenate([kd_ref[kvh, pl.ds(start, band), :], kcd_ref[kvh]], axis=0)
                  for kvh in range(N_KV_HEADS)],
            vals_t=[jnp.concatenate([vt_ref[kvh, :, pl.ds(start, band)], vct_ref[kvh]], axis=1)
                    for kvh in range(N_KV_HEADS)]))
    units = [(sub, pair) for sub in range(ATTN_QBLOCKS) for pair in range(N_HEADS // 2)]
    scores = []
    for sub, pair in units:
        u = subs[sub]
        lanes = slice(pair * 2 * HEAD_DIM, (pair + 1) * 2 * HEAD_DIM)
        q = _rope(q_ref[0, u['rows'], lanes], u['cos'], u['sin']) * ATTN_SCALE
        q_heads = _bf(jnp.concatenate([jnp.where(low, q, 0.0), jnp.where(low, 0.0, q)], axis=0))
        scores.append(_dot_nt(u['keys'][(2 * pair) // GQA_GROUP], q_heads))
    probs = []
    for (sub, pair), raw in zip(units, scores):
        s = jnp.where(subs[sub]['mask'], raw, -jnp.inf)
        sink = jnp.concatenate([sink_ref[2 * pair:2 * pair + 1, :], sink_ref[2 * pair + 1:2 * pair + 2, :]], axis=1)
        m = jnp.maximum(jnp.max(s, axis=0, keepdims=True), sink)
        probs.append((_bf(jnp.exp(s - m)), jnp.exp(sink - m)))
    for (sub, pair), (p, sink_term) in zip(units, probs):
        u = subs[sub]
        lanes = slice(pair * 2 * HEAD_DIM, (pair + 1) * 2 * HEAD_DIM)
        acc = _dot(u['vals_t'][(2 * pair) // GQA_GROUP], p)
        out_t = acc[:HEAD_DIM, :] / (acc[HEAD_DIM:HEAD_DIM + 1, :] + sink_term)
        o_ref[0, u['rows'], lanes] = jnp.transpose(jnp.concatenate([out_t[:, :blk], out_t[:, blk:]], axis=0))


def latent_attention(p_lat, k_ctx, v_ctx, sink):
    b, seq, _ = p_lat.shape
    past = k_ctx.shape[1]
    cos, sin = _rope_tables(seq)
    return pl.pallas_call(
        _lat_attn_kernel,
        grid=(b, seq // (ATTN_QBLOCKS * ATTN_BLOCK)),
        in_specs=[
            pl.BlockSpec((1, ATTN_QBLOCKS * ATTN_BLOCK, D_ATTN), lambda bi, i: (bi, i, 0)),
            pl.BlockSpec((1, seq, D_KV), lambda bi, i: (bi, 0, COL_K // D_KV)),
            pl.BlockSpec((1, seq, D_KV), lambda bi, i: (bi, 0, COL_V // D_KV)),
            pl.BlockSpec((1, past, D_KV), lambda bi, i: (bi, 0, 0)),
            pl.BlockSpec((1, past, D_KV), lambda bi, i: (bi, 0, 0)),
            _const_spec((seq, 128)),
            _const_spec((seq, 128)),
            _const_spec((N_HEADS, 128)),
        ],
        out_specs=pl.BlockSpec((1, ATTN_QBLOCKS * ATTN_BLOCK, D_ATTN), lambda bi, i: (bi, i, 0)),
        out_shape=jax.ShapeDtypeStruct((b, seq, D_ATTN), F32),
        scratch_shapes=[
            pltpu.VMEM((N_KV_HEADS, seq, D_KV), BF16),
            pltpu.VMEM((N_KV_HEADS, D_KV, seq), BF16),
            pltpu.VMEM((N_KV_HEADS, past, D_KV), BF16),
            pltpu.VMEM((N_KV_HEADS, D_KV, past), BF16),
        ],
        compiler_params=_cparams(("parallel", "arbitrary")),
        name="latent_attention",
    )(p_lat, p_lat, p_lat, k_ctx, v_ctx, cos, sin, _sink_table(sink))


LRU_HALF = D_LRU // 2
LRU_ROWS = 256
LRU_PAD = 8


def _neg_expm1(y, exp_y):
    return jnp.tanh(-0.5 * y) * (exp_y + 1.0)


def _softplus(x):
    return jnp.maximum(x, 0.0) + jnp.log1p(jnp.exp(-jnp.abs(x)))


def _gelu_tanh(x):
    return 0.5 * x * (1.0 + jnp.tanh(math.sqrt(2.0 / math.pi) * (x + 0.044715 * (x * x * x))))


LRU_SCAN = 8


def _affine_scan8(a, b, *, reverse):
    row = lax.broadcasted_iota(jnp.int32, a.shape, 0)
    k = 1
    while k < LRU_SCAN:
        if reverse:
            valid = row < LRU_SCAN - k
            shift = LRU_SCAN - k
        else:
            valid = row >= k
            shift = k
        a_prev = jnp.where(valid, pltpu.roll(a, shift, axis=0), 1.0)
        b_prev = jnp.where(valid, pltpu.roll(b, shift, axis=0), 0.0)
        b = b + a * b_prev
        a = a * a_prev
        k *= 2
    return a, b


def _lru_kernel(x_ref, g_ref, cw_ref, cb_ref, wr_ref, br_ref, wi_ref, bi_ref, lam_ref, h0_ref,
                o_ref, st_ref, pad_ref, a_ref, bx_ref, hb_ref):
    seq = x_ref.shape[1]
    zeros = jnp.zeros((LRU_PAD, LRU_HALF), F32)
    pad_ref[0:LRU_PAD, :] = zeros
    pad_ref[LRU_PAD + seq:2 * LRU_PAD + seq, :] = zeros
    pad_ref[LRU_PAD:LRU_PAD + seq, :] = x_ref[0]
    left = LRU_CONV // 2
    for ti in range(seq // LRU_ROWS):
        r0 = ti * LRU_ROWS
        xc = cb_ref[...]
        for k in range(LRU_CONV):
            xc = xc + cw_ref[k:k + 1, :] * pad_ref[r0 + LRU_PAD - left + k:r0 + LRU_PAD - left + k + LRU_ROWS, :]
        xcb = _bf(xc)
        for d in range(2):
            r = _sigmoid(_dot(xcb, wr_ref[d]) + br_ref[d:d + 1, :])
            i = _sigmoid(_dot(xcb, wi_ref[d]) + bi_ref[d:d + 1, :])
            log_a = (-LRU_C) * r * _softplus(-lam_ref[d:d + 1, :])
            a = jnp.exp(log_a)
            a_ref[d, r0:r0 + LRU_ROWS, :] = a
            bx_ref[d, r0:r0 + LRU_ROWS, :] = jnp.sqrt(_neg_expm1(2.0 * log_a, a * a)) * (i * xc)

    nt = seq // LRU_SCAN

    def step(blk, carry):
        hf, hb = carry
        rf = pl.ds(pl.multiple_of(blk * LRU_SCAN, LRU_SCAN), LRU_SCAN)
        rb = pl.ds(pl.multiple_of((nt - 1 - blk) * LRU_SCAN, LRU_SCAN), LRU_SCAN)
        af, bf_ = _affine_scan8(a_ref[0, rf, :], bx_ref[0, rf, :], reverse=False)
        ab, bb = _affine_scan8(a_ref[1, rb, :], bx_ref[1, rb, :], reverse=True)
        tile_f = af * hf + bf_
        tile_b = ab * hb + bb
        o_ref[0, rf, :] = tile_f
        hb_ref[rb, :] = tile_b
        return tile_f[LRU_SCAN - 1:LRU_SCAN, :], tile_b[0:1, :]

    hf, hb = lax.fori_loop(0, nt, step, (h0_ref[0, 0:1, :], h0_ref[0, 1:2, :]), unroll=2)
    st_ref[0, 0:1, :] = hf
    st_ref[0, 1:2, :] = hb
    for ti in range(seq // LRU_ROWS):
        rows = slice(ti * LRU_ROWS, (ti + 1) * LRU_ROWS)
        o_ref[0, rows, :] = (o_ref[0, rows, :] + hb_ref[rows, :]) * _gelu_tanh(g_ref[0, rows, :])


def _block_diag(w):
    n, blk, _ = w.shape
    eye = jnp.eye(n, dtype=w.dtype)
    return (eye[:, None, :, None] * w[:, :, None, :]).reshape(n * blk, n * blk)


def rglru_mixer(p, conv_w, conv_b, w_r, b_r, w_i, b_i, lam, h0):
    b, seq, _ = p.shape
    wr = _bf(jnp.stack([_block_diag(w_r[d]) for d in range(2)]))
    wi = _bf(jnp.stack([_block_diag(w_i[d]) for d in range(2)]))
    hw = LRU_HALF
    vec = lambda rows: pl.BlockSpec((rows, hw), lambda bi, h: (0, h))
    return pl.pallas_call(
        _lru_kernel,
        grid=(b, 2),
        in_specs=[
            pl.BlockSpec((1, seq, hw), lambda bi, h: (bi, 0, COL_RX // hw + h)),
            pl.BlockSpec((1, seq, hw), lambda bi, h: (bi, 0, COL_RG // hw + h)),
            vec(LRU_CONV), vec(1),
            pl.BlockSpec((2, hw, hw), lambda bi, h: (0, h, h)), vec(2),
            pl.BlockSpec((2, hw, hw), lambda bi, h: (0, h, h)), vec(2),
            vec(2),
            pl.BlockSpec((1, 2, hw), lambda bi, h: (bi, 0, h)),
        ],
        out_specs=[
            pl.BlockSpec((1, seq, hw), lambda bi, h: (bi, 0, h)),
            pl.BlockSpec((1, 2, hw), lambda bi, h: (bi, 0, h)),
        ],
        out_shape=[jax.ShapeDtypeStruct((b, seq, D_LRU), F32), jax.ShapeDtypeStruct((b, 2, D_LRU), F32)],
        scratch_shapes=[
            pltpu.VMEM((seq + 2 * LRU_PAD, hw), F32),
            pltpu.VMEM((2, seq, hw), F32),
            pltpu.VMEM((2, seq, hw), F32),
            pltpu.VMEM((seq, hw), F32),
        ],
        compiler_params=_cparams(("parallel", "parallel")),
        name="rglru_mixer",
    )(p, p, conv_w, conv_b.reshape(1, D_LRU), wr, b_r, wi, b_i, lam, h0)


HG_TILE = 256
HG_GROUP = 32
HG_AHEAD = 1
HG_NBUF = 2
LOG2_E = math.log2(math.e)


def _hg_constants():
    idx = np.arange(HG_TILE)
    same_blk = (idx[:, None] // HG_BLK) == (idx[None, :] // HG_BLK)
    tri_f = same_blk & (idx[None, :] <= idx[:, None])
    tri_b = same_blk & (idx[None, :] >= idx[:, None])
    head = (idx[:, None] // HG_DK) == (idx[None, :] // HG_DK)
    return jnp.asarray(np.stack([tri_f, tri_b]), BF16), jnp.asarray(same_blk, BF16), jnp.asarray(head, BF16)


def _hgrn_kernel(q_ref, z_ref, v_ref, lb_ref, tri_ref, ones_ref, headb_ref, s0_ref,
                 o_ref, st_ref, state, *scratch):
    att_bufs = scratch[0:HG_NBUF]
    upd_bufs = scratch[HG_NBUF:2 * HG_NBUF]
    qst_bufs = scratch[2 * HG_NBUF:3 * HG_NBUF]
    qs, srcs, vs, cums, qds, kns, tots = scratch[3 * HG_NBUF:]
    seg_rows = q_ref.shape[1]
    d = pl.program_id(1)
    s = pl.program_id(2)
    nseg = pl.num_programs(2)

    @pl.when(s == 0)
    def _():
        state[...] = jnp.concatenate([jnp.transpose(s0_ref[0, 0, h]) for h in range(HG_HEADS)], axis=1)

    lb = lb_ref[0]
    for ti in range(seg_rows // HG_TILE):
        rows = slice(ti * HG_TILE, (ti + 1) * HG_TILE)
        f = lb + (1.0 - lb) * _sigmoid(z_ref[0, rows, :])
        g = jnp.log(f)
        kk = 1.0 - f
        q = _silu(q_ref[0, rows, :]) * HG_SCALE
        cum = _dot_exact_rhs(tri_ref[0], g)
        tot = _dot_exact_rhs(ones_ref[...], g)
        qs[rows, :] = q
        srcs[rows, :] = (cum - jnp.log(kk)) * LOG2_E
        vs[rows, :] = v_ref[0, rows, :]
        cums[rows, :] = cum * LOG2_E
        tots[rows, :] = tot
        qds[rows, :] = q * jnp.exp(cum)
        kns[rows, :] = kk * jnp.exp(tot - cum)

    nblk = seg_rows // HG_BLK
    sgn = jnp.where(d == 0, 1, -1)
    t_signed = [(lax.broadcasted_iota(jnp.int32, (8, D_HG), 0) + 8 * h) * sgn for h in range(HG_BLK // 8)]
    lane_head_blk = lax.broadcasted_iota(jnp.int32, (HG_BLK, D_HG), 1) // HG_DK
    lane_head_dv = lax.broadcasted_iota(jnp.int32, (HG_DV, D_HG), 1) // HG_DK

    n_half = HG_BLK // 8

    def block_rows(j):
        je = jnp.where(d == 0, j, nblk - 1 - j)
        return pl.multiple_of(je * HG_BLK, HG_BLK)

    def prepare(j, slot):
        r0 = block_rows(j)
        rows = pl.ds(r0, HG_BLK)
        cum_h = [cums[pl.ds(r0 + 8 * h, 8), :] for h in range(n_half)]
        q_h = [qs[pl.ds(r0 + 8 * h, 8), :] for h in range(n_half)]
        tiles = []
        for src in range(HG_BLK):
            src_row = srcs[pl.ds(r0 + src, 1), :]
            for h in range(n_half):
                pair = jnp.exp2(cum_h[h] - src_row) * q_h[h]
                tiles.append(jnp.where(t_signed[h] >= src * sgn, pair, 0.0))
        att_bufs[slot][...] = _dot(_bf(jnp.concatenate(tiles, axis=0)), headb_ref[...])
        upd = lax.dot_general(_bf(vs[rows, :]), _bf(kns[rows, :]), (((0,), (0,)), ((), ())),
                              preferred_element_type=F32)
        upd_fold = jnp.where(lane_head_dv == 0, upd[0:HG_DV], 0.0)
        for h in range(1, HG_HEADS):
            upd_fold = upd_fold + jnp.where(lane_head_dv == h, upd[h * HG_DV:(h + 1) * HG_DV], 0.0)
        upd_bufs[slot][...] = upd_fold
        qd = qds[rows, :]
        qst_bufs[slot][...] = _bf(jnp.concatenate(
            [jnp.where(lane_head_blk == h, qd, 0.0) for h in range(HG_HEADS)], axis=0))

    def finish(j, slot, st):
        r0 = block_rows(j)
        rows = pl.ds(r0, HG_BLK)
        o_halves = [None] * n_half
        for src in range(HG_BLK):
            v_row = vs[pl.ds(r0 + src, 1), :]
            for h in range(n_half):
                term = att_bufs[slot][src * HG_BLK + h * 8:src * HG_BLK + (h + 1) * 8, :] * v_row
                o_halves[h] = term if o_halves[h] is None else o_halves[h] + term
        by_head = _dot_nt(qst_bufs[slot][...], _bf(st))
        o_state = jnp.concatenate([by_head[h * HG_BLK:(h + 1) * HG_BLK] for h in range(HG_HEADS)], axis=1)
        o_ref[0, 0, rows, :] = jnp.concatenate(o_halves, axis=0) + o_state
        return jnp.exp(tots[pl.ds(r0, 1), :]) * st + upd_bufs[slot][...]

    for k in range(HG_AHEAD):
        prepare(k, k)

    group = min(HG_GROUP, nblk)

    def block_group(i, st):
        j = group * i
        for k in range(group):
            prepare(jnp.minimum(j + k + HG_AHEAD, nblk - 1), (k + HG_AHEAD) % HG_NBUF)
            st = finish(j + k, k % HG_NBUF, st)
        return st

    if group == nblk:
        st = state[...]
        for k in range(nblk):
            if k + HG_AHEAD < nblk:
                prepare(k + HG_AHEAD, (k + HG_AHEAD) % HG_NBUF)
            st = finish(k, k % HG_NBUF, st)
        state[...] = st
    else:
        state[...] = lax.fori_loop(0, nblk // group, block_group, state[...])

    @pl.when(s == nseg - 1)
    def _():
        for h in range(HG_HEADS):
            st_ref[0, 0, h] = jnp.transpose(state[:, h * HG_DK:(h + 1) * HG_DK])


def hgrn2_scan(p, lb, s0, seg_rows):
    b, seq, _ = p.shape
    nseg = seq // seg_rows
    tri, ones, headb = _hg_constants()
    seg = lambda d, s: jnp.where(d == 0, s, nseg - 1 - s)
    col = lambda c: pl.BlockSpec((1, seg_rows, D_HG), lambda bi, d, s: (bi, seg(d, s), c // D_HG))
    scr = lambda: pltpu.VMEM((seg_rows, D_HG), F32)
    return pl.pallas_call(
        _hgrn_kernel,
        grid=(b, 2, nseg),
        in_specs=[
            col(COL_HQ),
            pl.BlockSpec((1, seg_rows, D_HG), lambda bi, d, s: (bi, seg(d, s), COL_HFF // D_HG + d)),
            col(COL_HI),
            pl.BlockSpec((1, 1, D_HG), lambda bi, d, s: (d, 0, 0)),
            pl.BlockSpec((1, HG_TILE, HG_TILE), lambda bi, d, s: (d, 0, 0)),
            _const_spec((HG_TILE, HG_TILE)),
            _const_spec((D_HG, D_HG)),
            pl.BlockSpec((1, 1, HG_HEADS, HG_DK, HG_DV), lambda bi, d, s: (bi, d, 0, 0, 0)),
        ],
        out_specs=[
            pl.BlockSpec((1, 1, seg_rows, D_HG), lambda bi, d, s: (d, bi, seg(d, s), 0)),
            pl.BlockSpec((1, 1, HG_HEADS, HG_DK, HG_DV), lambda bi, d, s: (bi, d, 0, 0, 0)),
        ],
        out_shape=[jax.ShapeDtypeStruct((2, b, seq, D_HG), F32),
                   jax.ShapeDtypeStruct((b, 2, HG_HEADS, HG_DK, HG_DV), F32)],
        scratch_shapes=[pltpu.VMEM((HG_DV, D_HG), F32)]
        + [pltpu.VMEM((HG_BLK * HG_BLK, D_HG), F32) for _ in range(HG_NBUF)]
        + [pltpu.VMEM((HG_DV, D_HG), F32) for _ in range(HG_NBUF)]
        + [pltpu.VMEM((HG_HEADS * HG_BLK, D_HG), BF16) for _ in range(HG_NBUF)] + [scr() for _ in range(7)],
        compiler_params=_cparams(("parallel", "parallel", "arbitrary")),
        name="hgrn2_scan",
    )(p, p, p, lb.reshape(2, 1, D_HG), tri, ones, headb, s0)


HY_FEAT_PAD = 128
HY_GEN_ROWS = 256


def _dot_hi(a, b):
    ah = _bf(a)
    al = _bf(a - ah.astype(F32))
    bh = _bf(b)
    bl = _bf(b - bh.astype(F32))
    return _dot(ah, bh) + _dot(ah, bl) + _dot(al, bh)


HY_HALF_COLS = HY_ORDER * D_HY


def _filter_gen_kernel(feat_ref, w1_ref, b1_ref, w2_ref, b2_ref, w3_ref, fr_ref, dl_ref, h_ref, ssq_ref, *, seq):
    i = pl.program_id(0)
    tm = feat_ref.shape[0]
    feat = feat_ref[...]
    fr = fr_ref[...]
    h = jnp.sin(fr * (_dot_hi(feat, w1_ref[...]) + b1_ref[...]))
    h = jnp.sin(fr * (_dot_hi(h, w2_ref[...]) + b2_ref[...]))
    h = _dot_hi(h, w3_ref[...])
    decay = jnp.exp(-feat[:, 0:1] * dl_ref[...])
    decay = jnp.concatenate([decay] * HY_ORDER, axis=1)
    row = i * tm + lax.broadcasted_iota(jnp.int32, (tm, HY_HALF_COLS), 0)
    taps = h * decay
    h_ref[...] = jnp.where(row == seq, 0.0, taps)

    @pl.when(i == 0)
    def _():
        ssq_ref[...] = jnp.zeros_like(ssq_ref)

    ssq_ref[...] += jnp.sum(taps * taps, axis=0, keepdims=True)


def _hyena_features(seq):
    order = np.concatenate([np.arange(seq), [0], np.arange(seq - 1, 0, -1)])
    pos = order.astype(np.float32)
    t = pos / np.float32(max(seq - 1, 1))
    bands = np.linspace(1e-4, HY_BANDS - 1, HY_BANDS, dtype=np.float32)
    ang = np.float32(2.0 * math.pi / seq) * pos[:, None] * bands[None, :]
    feat = np.concatenate([t[:, None], np.cos(ang), np.sin(ang)], axis=-1).astype(np.float32)
    out = np.zeros((2 * seq, HY_FEAT_PAD), np.float32)
    out[:, :HY_EMB] = feat
    return jnp.asarray(out)


def hyena_filter_gen(seq, w1, b1, w2, b2, w3, freq):
    feat = _hyena_features(seq)
    w1p = jnp.zeros((HY_FEAT_PAD, HY_FFN), F32).at[:HY_EMB].set(w1.astype(F32))
    deltas = jnp.asarray(np.linspace(HY_DECAY_SLOW, HY_DECAY_FAST, D_HY, dtype=np.float32)).reshape(1, D_HY)
    tm = HY_GEN_ROWS
    return pl.pallas_call(
        functools.partial(_filter_gen_kernel, seq=seq),
        grid=(2 * seq // tm,),
        in_specs=[
            pl.BlockSpec((tm, HY_FEAT_PAD), lambda i: (i, 0)),
            _const_spec((HY_FEAT_PAD, HY_FFN)), _const_spec((1, HY_FFN)),
            _const_spec((HY_FFN, HY_FFN)), _const_spec((1, HY_FFN)),
            pl.BlockSpec((HY_FFN, HY_HALF_COLS), lambda i: (0, (i * tm) // seq)),
            _const_spec((1, HY_FFN)), _const_spec((1, D_HY)),
        ],
        out_specs=[pl.BlockSpec((tm, HY_HALF_COLS), lambda i: (i, 0)), _const_spec((1, HY_HALF_COLS))],
        out_shape=[jax.ShapeDtypeStruct((2 * seq, HY_HALF_COLS), F32),
                   jax.ShapeDtypeStruct((1, HY_HALF_COLS), F32)],
        compiler_params=_cparams(("arbitrary",)),
        name="hyena_filter_gen",
    )(feat, w1p, b1.reshape(1, HY_FFN), w2, b2.reshape(1, HY_FFN), w3, freq.reshape(1, HY_FFN), deltas)


def _filter_norm(ssq_ref, n):
    return lax.rsqrt(ssq_ref[:, n * D_HY:(n + 1) * D_HY])


def _conv3(pad_ref, w_ref, b_ref, r0, rows, pad):
    acc = b_ref[...]
    for k in range(HY_SHORT):
        lo = r0 + pad - HY_SHORT // 2 + k
        acc = acc + w_ref[k:k + 1, :] * pad_ref[lo:lo + rows, :]
    return acc


def _dft_constants(seq):
    n = 2 * seq
    k = np.arange(n)
    ang = 2.0 * np.pi * np.outer(k, k) / n
    fwd = np.concatenate([np.cos(ang), -np.sin(ang)], axis=0)
    inv = np.concatenate([np.cos(ang[:seq]), -np.sin(ang[:seq])], axis=1) / n
    return jnp.asarray(fwd, BF16), jnp.asarray(inv, BF16)


def _ctx_spectrum_kernel(filt_ref, ssq_ref, fwd_ref, h_ref):
    n = pl.program_id(0)
    norm = jnp.where(n == 0, _filter_norm(ssq_ref, 0), _filter_norm(ssq_ref, 1))
    x = filt_ref[...] * norm
    xh = _bf(x)
    xl = _bf(x - xh.astype(F32))
    h_ref[0] = _dot(fwd_ref[...], xh) + _dot(fwd_ref[...], xl)


def hyena_ctx_spectrum(filt, ssq):
    n = filt.shape[0]
    fwd, _ = _dft_constants(n // 2)
    return pl.pallas_call(
        _ctx_spectrum_kernel,
        grid=(HY_ORDER,),
        in_specs=[pl.BlockSpec((n, D_HY), lambda o: (0, o)), _const_spec((1, HY_HALF_COLS)),
                  _const_spec((2 * n, n))],
        out_specs=pl.BlockSpec((1, 2 * n, D_HY), lambda o: (o, 0, 0)),
        out_shape=jax.ShapeDtypeStruct((HY_ORDER, 2 * n, D_HY), F32),
        compiler_params=_cparams(("parallel",)),
        name="hyena_ctx_spectrum",
    )(filt, ssq, fwd)


HY_PAD = 8


def _hyena_ctx_kernel(v_ref, x1_ref, x2_ref, cw_ref, cb_ref, bias_ref, spec_ref, fwd_ref, inv_ref, o_ref, pad_ref):
    seq = v_ref.shape[1]
    n = 2 * seq
    zeros = jnp.zeros((HY_PAD, D_HY), F32)
    pad_ref[0:HY_PAD, :] = zeros
    pad_ref[HY_PAD + seq:2 * HY_PAD + seq, :] = zeros

    def short_conv(src_ref, part):
        pad_ref[HY_PAD:HY_PAD + seq, :] = src_ref[0]
        return _conv3(pad_ref, cw_ref.at[part], cb_ref.at[part], 0, seq, HY_PAD)

    z = short_conv(v_ref, 0)
    for order, gate_ref in enumerate((x1_ref, x2_ref)):
        spec = _dot(fwd_ref[:, 0:seq], _bf(z))
        xr, xi = spec[:n], spec[n:]
        hr, hi = spec_ref[order, 0:n, :], spec_ref[order, n:2 * n, :]
        prod = jnp.concatenate([xr * hr - xi * hi, xr * hi + xi * hr], axis=0)
        conv = _dot(inv_ref[...], _bf(prod))
        z = short_conv(gate_ref, order + 1) * (conv + bias_ref[order:order + 1, :] * z)
    o_ref[0] = z


def hyena_ctx(p, conv_w, conv_b, bias, spec):
    b, seq, _ = p.shape
    fwd, inv = _dft_constants(seq)
    col = lambda c: pl.BlockSpec((1, seq, D_HY), lambda i: (i, 0, c))
    c0 = COL_HY // D_HY
    return pl.pallas_call(
        _hyena_ctx_kernel,
        grid=(b,),
        in_specs=[
            col(c0), col(c0 + 1), col(c0 + 2),
            _const_spec((3, HY_SHORT, D_HY)), _const_spec((3, 1, D_HY)), _const_spec((HY_ORDER, D_HY)),
            _const_spec((HY_ORDER, 4 * seq, D_HY)), _const_spec((4 * seq, 2 * seq)), _const_spec((seq, 4 * seq)),
        ],
        out_specs=pl.BlockSpec((1, seq, D_HY), lambda i: (i, 0, 0)),
        out_shape=jax.ShapeDtypeStruct((b, seq, D_HY), F32),
        scratch_shapes=[pltpu.VMEM((seq + 2 * HY_PAD, D_HY), F32)],
        compiler_params=_cparams(("parallel",)),
        name="hyena_ctx",
    )(p, p, p, conv_w.reshape(HY_SHORT, 3, D_HY).transpose(1, 0, 2), conv_b.reshape(3, 1, D_HY), bias, spec,
      fwd, inv)


FFT_N1 = 64
FFT_N2 = 128
FFT_N = FFT_N1 * FFT_N2
FFT_K1 = FFT_N1 // 2 + 1
FFT_K1_PAD = 40
FFT_ROWS = FFT_K1_PAD * FFT_N2
HY_LANES = 128
FFT_UNROLL1 = 16
FFT_UNROLL2 = 11


def _fft_constants():
    n1, n2, n, kp = FFT_N1, FFT_N2, FFT_N, FFT_K1_PAD
    k1 = np.arange(kp)
    w1 = np.exp(-2j * np.pi * np.outer(k1, np.arange(n1)) / n1)
    tw = np.exp(-2j * np.pi * np.outer(np.arange(n2), k1) / n)
    f1c = tw[:, :, None] * w1[None, :, :]
    f1t = np.concatenate([f1c.real, f1c.imag], axis=1).transpose(0, 2, 1)
    w2 = np.exp(-2j * np.pi * np.outer(np.arange(n2), np.arange(n2)) / n2)
    f2 = np.block([[w2.real, -w2.imag], [w2.imag, w2.real]])
    g2c = np.conj(w2)
    g2 = np.block([[g2c.real, -g2c.imag], [g2c.imag, g2c.real]])
    weight = np.where((k1 == 0) | (k1 == n1 // 2), 1.0, 2.0) * (k1 < FFT_K1)
    mc = np.conj(w1.T)[None, :n1 // 2, :] * np.conj(tw)[:, None, :] * weight / n
    g1 = np.concatenate([mc.real, -mc.imag], axis=2)
    return (jnp.asarray(f1t, BF16), jnp.asarray(f2, BF16), jnp.asarray(g2, BF16), jnp.asarray(g1, BF16))


def _dot_tn(a, b):
    return lax.dot_general(a, b, (((0,), (0,)), ((), ())), preferred_element_type=F32)


def _fft_stage1(x_ref, f1t_ref, a_ref, n1_in):
    def body(n2, carry):
        rows = x_ref[pl.ds(n2, n1_in, stride=FFT_N2), :]
        out = _dot_tn(f1t_ref[n2, 0:n1_in, :], _bf(rows))
        a_ref[0, pl.ds(n2, FFT_K1, stride=FFT_N2), :] = out[:FFT_K1]
        a_ref[1, pl.ds(n2, FFT_K1, stride=FFT_N2), :] = out[FFT_K1_PAD:FFT_K1_PAD + FFT_K1]
        return carry
    lax.fori_loop(0, FFT_N2, body, 0, unroll=FFT_UNROLL1)


def _lat_spectrum_kernel(filt_ref, ssq_ref, f1t_ref, f2_ref, h_ref, a_ref):
    n = pl.program_id(0)
    half = pl.program_id(1)
    norm_full = jnp.where(n == 0, _filter_norm(ssq_ref, 0), _filter_norm(ssq_ref, 1))
    norm = jnp.where(half == 0, norm_full[:, :HY_LANES], norm_full[:, HY_LANES:])
    _fft_stage1(filt_ref, f1t_ref, a_ref, FFT_N1)

    def stage2(k1, carry):
        r0 = pl.multiple_of(k1 * FFT_N2, FFT_N2)
        rows = pl.ds(r0, FFT_N2)
        a = jnp.concatenate([a_ref[0, rows, :], a_ref[1, rows, :]], axis=0)
        b = _dot(f2_ref[...], _bf(a)) * norm
        h_ref[0, 0, rows, :] = _bf(b[:FFT_N2])
        h_ref[0, 1, rows, :] = _bf(b[FFT_N2:])
        return carry
    lax.fori_loop(0, FFT_K1, stage2, 0)


def hyena_lat_spectrum(filt, ssq):
    f1t, f2, _, _ = _fft_constants()
    rows = FFT_K1 * FFT_N2
    halves = D_HY // HY_LANES
    return pl.pallas_call(
        _lat_spectrum_kernel,
        grid=(HY_ORDER, halves),
        in_specs=[
            pl.BlockSpec((FFT_N, HY_LANES), lambda o, h: (0, o * halves + h)),
            _const_spec((1, HY_HALF_COLS)),
            _const_spec((FFT_N2, FFT_N1, 2 * FFT_K1_PAD)),
            _const_spec((2 * FFT_N2, 2 * FFT_N2)),
        ],
        out_specs=pl.BlockSpec((1, 2, rows, HY_LANES), lambda o, h: (o, 0, 0, h)),
        out_shape=jax.ShapeDtypeStruct((HY_ORDER, 2, rows, D_HY), BF16),
        scratch_shapes=[pltpu.VMEM((2, FFT_ROWS, HY_LANES), F32)],
        compiler_params=_cparams(("parallel", "parallel")),
        name="hyena_lat_spectrum",
    )(filt, ssq, f1t, f2)


HY_ROWS = 512


def _hyena_lat_kernel(v_ref, x1_ref, x2_ref, cw_ref, cb_ref, bias_ref, spec_ref, f1t_ref, f2_ref, g2_ref, g1_ref,
                      o_ref, pad_ref, z_ref, a_ref):
    seq = v_ref.shape[1]
    zeros = jnp.zeros((HY_PAD, HY_LANES), F32)
    pad_ref[0:HY_PAD, :] = zeros
    pad_ref[HY_PAD + seq:2 * HY_PAD + seq, :] = zeros
    pad_ref[HY_PAD:HY_PAD + seq, :] = v_ref[0]
    for ti in range(seq // HY_ROWS):
        r0 = ti * HY_ROWS
        z_ref[r0:r0 + HY_ROWS, :] = _conv3(pad_ref, cw_ref.at[0], cb_ref.at[0], r0, HY_ROWS, HY_PAD)
    tail = jnp.zeros((FFT_ROWS - FFT_K1 * FFT_N2, HY_LANES), F32)
    a_ref[0, FFT_K1 * FFT_N2:FFT_ROWS, :] = tail
    a_ref[1, FFT_K1 * FFT_N2:FFT_ROWS, :] = tail

    for order, gate_ref in enumerate((x1_ref, x2_ref)):
        _fft_stage1(z_ref, f1t_ref, a_ref, FFT_N1 // 2)

        def stage2(k1, carry):
            r0 = pl.multiple_of(k1 * FFT_N2, FFT_N2)
            rows = pl.ds(r0, FFT_N2)
            a = jnp.concatenate([a_ref[0, rows, :], a_ref[1, rows, :]], axis=0)
            b = _dot(f2_ref[...], _bf(a))
            br, bi = b[:FFT_N2], b[FFT_N2:]
            hr = spec_ref[order, 0, rows, :].astype(F32)
            hi = spec_ref[order, 1, rows, :].astype(F32)
            y = jnp.concatenate([br * hr - bi * hi, br * hi + bi * hr], axis=0)
            c = _dot(g2_ref[...], _bf(y))
            a_ref[0, rows, :] = c[:FFT_N2]
            a_ref[1, rows, :] = c[FFT_N2:]
            return carry
        lax.fori_loop(0, FFT_K1, stage2, 0, unroll=FFT_UNROLL2)

        def stage1_inv(n2, carry):
            c = jnp.concatenate([a_ref[0, pl.ds(n2, FFT_K1_PAD, stride=FFT_N2), :],
                                 a_ref[1, pl.ds(n2, FFT_K1_PAD, stride=FFT_N2), :]], axis=0)
            o_ref[0, pl.ds(n2, FFT_N1 // 2, stride=FFT_N2), :] = _dot(g1_ref[n2], _bf(c))
            return carry
        lax.fori_loop(0, FFT_N2, stage1_inv, 0, unroll=FFT_UNROLL1)

        pad_ref[HY_PAD:HY_PAD + seq, :] = gate_ref[0]
        for ti in range(seq // HY_ROWS):
            r0 = ti * HY_ROWS
            rows = slice(r0, r0 + HY_ROWS)
            gate = _conv3(pad_ref, cw_ref.at[order + 1], cb_ref.at[order + 1], r0, HY_ROWS, HY_PAD)
            z = gate * (o_ref[0, rows, :] + bias_ref[order:order + 1, :] * z_ref[rows, :])
            if order + 1 < HY_ORDER:
                z_ref[rows, :] = z
            else:
                o_ref[0, rows, :] = z


def hyena_lat(p, conv_w, conv_b, bias, spec):
    b, seq, _ = p.shape
    assert 2 * seq == FFT_N
    f1t, f2, g2, g1 = _fft_constants()
    hw = HY_LANES
    col = lambda c: pl.BlockSpec((1, seq, hw), lambda i, h: (i, 0, (COL_HY + c * D_HY) // hw + h))
    return pl.pallas_call(
        _hyena_lat_kernel,
        grid=(b, D_HY // hw),
        in_specs=[
            col(0), col(1), col(2),
            pl.BlockSpec((3, HY_SHORT, hw), lambda i, h: (0, 0, h)),
            pl.BlockSpec((3, 1, hw), lambda i, h: (0, 0, h)),
            pl.BlockSpec((HY_ORDER, hw), lambda i, h: (0, h)),
            pl.BlockSpec((HY_ORDER, 2, FFT_K1 * FFT_N2, hw), lambda i, h: (0, 0, 0, h)),
            _const_spec((FFT_N2, FFT_N1, 2 * FFT_K1_PAD)),
            _const_spec((2 * FFT_N2, 2 * FFT_N2)),
            _const_spec((2 * FFT_N2, 2 * FFT_N2)),
            _const_spec((FFT_N2, FFT_N1 // 2, 2 * FFT_K1_PAD)),
        ],
        out_specs=pl.BlockSpec((1, seq, hw), lambda i, h: (i, 0, h)),
        out_shape=jax.ShapeDtypeStruct((b, seq, D_HY), F32),
        scratch_shapes=[
            pltpu.VMEM((seq + 2 * HY_PAD, hw), F32),
            pltpu.VMEM((seq, hw), F32),
            pltpu.VMEM((2, FFT_ROWS, hw), F32),
        ],
        compiler_params=_cparams(("parallel", "parallel")),
        name="hyena_lat",
    )(p, p, p, conv_w.reshape(HY_SHORT, 3, D_HY).transpose(1, 0, 2), conv_b.reshape(3, 1, D_HY), bias, spec,
      f1t, f2, g2, g1)


HG_SEG_ROWS = 1024
COND_ROWS = 16


def _trunk_layer(x, mods, lp, lb, batch, seq, ctx, final):
    x = ffn_half_step(x, mods, lp['ln_ffn1'], lp['w_ffn1_in'], lp['w_ffn1_out'], lp['final_norm'],
                      mod_base=0, final=False, seq_len=seq)
    p2d = mixer_in_proj(x, mods, lp['ln_mix'], lp['w_in'], seq_len=seq)
    p = p2d.reshape(batch, seq, N_IN)

    filt, ssq = hyena_filter_gen(seq, lp['hy_w1'], lp['hy_b1'], lp['hy_w2'], lp['hy_b2'], lp['hy_w3'],
                                 lp['hy_freq'])
    if ctx is None:
        att = context_attention(p, lp['attn_sink'])
        lru_h0 = jnp.zeros((batch, 2, D_LRU), F32)
        hg_s0 = jnp.zeros((batch, 2, HG_HEADS, HG_DK, HG_DV), F32)
        hy = hyena_ctx(p, lp['hy_conv_w'], lp['hy_conv_b'], lp['hy_bias'], hyena_ctx_spectrum(filt, ssq))
    else:
        k_ctx, v_ctx, lru_h0, hg_state = ctx
        att = latent_attention(p, k_ctx, v_ctx, lp['attn_sink'])
        hg_s0 = hg_state.astype(F32)
        hy = hyena_lat(p, lp['hy_conv_w'], lp['hy_conv_b'], lp['hy_bias'], hyena_lat_spectrum(filt, ssq))
    lru, lru_state = rglru_mixer(p, lp['lru_conv_w'], lp['lru_conv_b'], lp['lru_w_r'], lp['lru_b_r'],
                                 lp['lru_w_i'], lp['lru_b_i'], lp['lru_lambda'], lru_h0.astype(F32))
    o_dirs, hg_state_out = hgrn2_scan(p, lb, hg_s0, min(seq, HG_SEG_ROWS))
    t = batch * seq

    x = gated_merge(x, mods, lp['ln_mix'], att.reshape(t, D_ATTN), lru.reshape(t, D_LRU),
                    o_dirs.reshape(2, t, D_HG), p2d, lp['hg_norm'], hy.reshape(t, D_HY), lp['w_gate'],
                    lp['w_bo_attn'], lp['w_bo_lru'], lp['w_bo_hg'], lp['w_bo_hy'], lp['w_out'], seq_len=seq)
    x = ffn_half_step(x, mods, lp['ln_ffn2'], lp['w_ffn2_in'], lp['w_ffn2_out'], lp['final_norm'],
                      mod_base=6, final=final, seq_len=seq)
    state = None
    if ctx is None:
        k = p[:, :, COL_K:COL_K + D_KV].reshape(batch, seq, N_KV_HEADS, HEAD_DIM)
        v = p[:, :, COL_V:COL_V + D_KV].reshape(batch, seq, N_KV_HEADS, HEAD_DIM)
        state = (k, v, lru_state, hg_state_out)
    return x, state


def kernel(x_prompt, x_sample, cache_k, cache_v, state_lru, state_hgrn, c, c_ctx, ln_ffn1, ln_mix, ln_ffn2, w_ada, b_ada, w_ffn1_in, w_ffn1_out, w_ffn2_in, w_ffn2_out, w_in, attn_sink, lru_conv_w, lru_conv_b, lru_w_r, lru_b_r, lru_w_i, lru_b_i, lru_lambda, hg_lb_logits, hg_norm, hy_conv_w, hy_conv_b, hy_w1, hy_b1, hy_w2, hy_b2, hy_w3, hy_freq, hy_bias, w_bo_attn, w_bo_lru, w_bo_hg, w_bo_hy, w_gate, w_out, final_norm):
    batch, seq, _ = x_prompt.shape
    dec_batch, dec_seq, _ = x_sample.shape
    lb_soft = jax.nn.softmax(hg_lb_logits.astype(F32), axis=0)
    lb_all = jnp.cumsum(lb_soft, axis=0) - lb_soft[0]

    stacked = dict(ln_ffn1=ln_ffn1, ln_mix=ln_mix, ln_ffn2=ln_ffn2, w_ffn1_in=w_ffn1_in, w_ffn1_out=w_ffn1_out,
                   w_ffn2_in=w_ffn2_in, w_ffn2_out=w_ffn2_out, w_in=w_in, attn_sink=attn_sink,
                   lru_conv_w=lru_conv_w, lru_conv_b=lru_conv_b, lru_w_r=lru_w_r, lru_b_r=lru_b_r,
                   lru_w_i=lru_w_i, lru_b_i=lru_b_i, lru_lambda=lru_lambda, hg_norm=hg_norm,
                   hy_conv_w=hy_conv_w, hy_conv_b=hy_conv_b, hy_w1=hy_w1, hy_b1=hy_b1, hy_w2=hy_w2, hy_b2=hy_b2,
                   hy_w3=hy_w3, hy_freq=hy_freq, hy_bias=hy_bias, w_bo_attn=w_bo_attn, w_bo_lru=w_bo_lru,
                   w_bo_hg=w_bo_hg, w_bo_hy=w_bo_hy, w_gate=w_gate, w_out=w_out)

    cond = jnp.zeros((COND_ROWS, D_MODEL), F32).at[0].set(c_ctx).at[1:1 + dec_batch].set(c)
    mods = ada_mods(cond, w_ada, b_ada).reshape(DEPTH, COND_ROWS, N_MOD, D_MODEL)

    h = x_prompt.reshape(batch * seq, D_MODEL)
    z = x_sample.reshape(dec_batch * dec_seq, D_MODEL)
    ks, vs, lrus, hgs = [], [], [], []
    for l in range(DEPTH):
        lp = {name: w[l] for name, w in stacked.items()}
        lp['final_norm'] = final_norm
        final = l == DEPTH - 1
        h, (k_l, v_l, lru_l, hg_l) = _trunk_layer(h, mods[l, 0:1], lp, lb_all[l], batch, seq, None, final)
        ks.append(k_l)
        vs.append(v_l)
        lrus.append(lru_l)
        hgs.append(hg_l)
        ctx = (cache_k[:, l].reshape(dec_batch, -1, D_KV), cache_v[:, l].reshape(dec_batch, -1, D_KV),
               state_lru[:, l], state_hgrn[:, l])
        z, _ = _trunk_layer(z, mods[l, 1:1 + dec_batch], lp, lb_all[l], dec_batch, dec_seq, ctx, final)

    y_prompt = h.reshape(batch, seq, D_MODEL)
    y_sample = z.reshape(dec_batch, dec_seq, D_MODEL)
    return (y_prompt, y_sample, jnp.stack(ks, axis=1), jnp.stack(vs, axis=1), jnp.stack(lrus, axis=1),
            jnp.stack(hgs, axis=1))
```

```python
import functools
import math

import numpy as np
import jax
import jax.numpy as jnp
from jax import lax
from jax.experimental import pallas as pl
from jax.experimental.pallas import tpu as pltpu

F32 = jnp.float32
BF16 = jnp.bfloat16

D_MODEL = 1024
DEPTH = 2
GRID_W = 64
EPS = 1e-6
N_MOD = 9
N_BRANCH = 4
D_FF = 2816
N_HEADS = 8
N_KV_HEADS = 2
HEAD_DIM = 64
GQA_GROUP = N_HEADS // N_KV_HEADS
D_ATTN = N_HEADS * HEAD_DIM
D_KV = N_KV_HEADS * HEAD_DIM
WINDOW = 128
ATTN_BLOCK = 128
ATTN_SCALE = HEAD_DIM ** -0.5
ROPE_BASE = 10000.0
D_LRU = D_MODEL // 4
LRU_CONV = 4
LRU_C = 8.0
HG_HEADS = 4
HG_DK = 64
HG_DV = 64
D_HG = HG_HEADS * HG_DK
HG_SCALE = HG_DK ** -0.5
HG_BLK = 16
D_HY = D_MODEL // 4
HY_ORDER = 2
HY_SHORT = 3
HY_BANDS = 8
HY_EMB = 2 * HY_BANDS + 1
HY_FFN = 64
HY_TARGET = 1e-2
HY_DECAY_SLOW = -math.log(HY_TARGET) / 1.5
HY_DECAY_FAST = -math.log(HY_TARGET) / 0.3
N_IN = D_ATTN + 2 * D_KV + 2 * D_LRU + 5 * D_HG + 3 * D_HY

COL_K = D_ATTN
COL_V = D_ATTN + D_KV
COL_RX = D_ATTN + 2 * D_KV
COL_RG = COL_RX + D_LRU
COL_HQ = COL_RG + D_LRU
COL_HFF = COL_HQ + D_HG
COL_HFB = COL_HFF + D_HG
COL_HI = COL_HFB + D_HG
COL_HGATE = COL_HI + D_HG
COL_HY = COL_HGATE + D_HG

FF_CHUNK = 256
N_FF_CHUNK = D_FF // FF_CHUNK
ROW_TILE = 512
VMEM_LIMIT = 56 * 1024 * 1024


def _bf(x):
    return x.astype(BF16)


def _dot(a, b):
    return jnp.dot(a, b, preferred_element_type=F32)


def _split3(x):
    hi = _bf(x)
    r1 = x - hi.astype(F32)
    mid = _bf(r1)
    lo = _bf(r1 - mid.astype(F32))
    return hi, mid, lo


def _dot_exact_rhs(mat_bf, x):
    hi, mid, lo = _split3(x)
    return _dot(mat_bf, hi) + _dot(mat_bf, mid) + _dot(mat_bf, lo)


def _sigmoid(x):
    return jax.nn.sigmoid(x)


def _silu(x):
    return x * _sigmoid(x)


def _norm_mod(x, lnw, shift, scale):
    ms = jnp.mean(x * x, axis=-1, keepdims=True)
    y = x * lax.rsqrt(ms + EPS) * lnw
    return y * (1.0 + scale) + shift


def _cparams(sem):
    return pltpu.CompilerParams(dimension_semantics=sem, vmem_limit_bytes=VMEM_LIMIT)


def _const_spec(shape):
    nd = len(shape)
    return pl.BlockSpec(shape, lambda *_: (0,) * nd, pipeline_mode=pl.Buffered(1))


def _ada_kernel(cond_ref, w_ref, b_ref, o_ref):
    c = cond_ref[...]
    o_ref[0] = _dot(_bf(_silu(c)), _bf(w_ref[0])) + b_ref[0]


def ada_mods(cond, w_ada, b_ada):
    r = cond.shape[0]
    cb = 1024
    ncol = (N_MOD * D_MODEL) // cb
    return pl.pallas_call(
        _ada_kernel,
        grid=(DEPTH, ncol),
        in_specs=[
            pl.BlockSpec((r, D_MODEL), lambda l, j: (0, 0)),
            pl.BlockSpec((1, D_MODEL, cb), lambda l, j: (l, 0, j)),
            pl.BlockSpec((1, 1, cb), lambda l, j: (l, 0, j)),
        ],
        out_specs=pl.BlockSpec((1, r, cb), lambda l, j: (l, 0, j)),
        out_shape=jax.ShapeDtypeStruct((DEPTH, r, N_MOD * D_MODEL), F32),
        compiler_params=_cparams(("parallel", "parallel")),
        name="ada_mods",
    )(cond, w_ada, b_ada.reshape(DEPTH, 1, N_MOD * D_MODEL))


def _seq_mod_index(seq_len, n_mod, tm):
    def idx(i):
        return jnp.minimum((i * tm) // seq_len, n_mod - 1)
    return idx


def _ffn_kernel(x_ref, mod_ref, ln_ref, wg_ref, wu_ref, wo_ref, fn_ref, o_ref, *, mod_base, final):
    x = x_ref[...]
    m = mod_ref[0]
    sh = m[mod_base:mod_base + 1]
    sc = m[mod_base + 1:mod_base + 2]
    g = m[mod_base + 2:mod_base + 3]
    h = _bf(_norm_mod(x, ln_ref[...], sh, sc))
    acc = None
    for c in range(N_FF_CHUNK):
        gate = _dot(h, wg_ref[c])
        up = _dot(h, wu_ref[c])
        part = _dot(_bf(_silu(gate) * up), wo_ref[c])
        acc = part if acc is None else acc + part
    y = x + 0.5 * g * acc
    if final:
        ms = jnp.mean(y * y, axis=-1, keepdims=True)
        y = y * lax.rsqrt(ms + EPS) * fn_ref[...]
    o_ref[...] = y


def ffn_half_step(x, mods, ln_w, w_in, w_out, final_w, *, mod_base, final, seq_len):
    t = x.shape[0]
    tm = ROW_TILE
    wg = _bf(w_in[:, :D_FF]).reshape(D_MODEL, N_FF_CHUNK, FF_CHUNK).transpose(1, 0, 2)
    wu = _bf(w_in[:, D_FF:]).reshape(D_MODEL, N_FF_CHUNK, FF_CHUNK).transpose(1, 0, 2)
    wo = _bf(w_out).reshape(N_FF_CHUNK, FF_CHUNK, D_MODEL)
    midx = _seq_mod_index(seq_len, mods.shape[0], tm)
    return pl.pallas_call(
        functools.partial(_ffn_kernel, mod_base=mod_base, final=final),
        grid=(t // tm,),
        in_specs=[
            pl.BlockSpec((tm, D_MODEL), lambda i: (i, 0)),
            pl.BlockSpec((1, N_MOD, D_MODEL), lambda i: (midx(i), 0, 0)),
            _const_spec((1, D_MODEL)),
            _const_spec((N_FF_CHUNK, D_MODEL, FF_CHUNK)),
            _const_spec((N_FF_CHUNK, D_MODEL, FF_CHUNK)),
            _const_spec((N_FF_CHUNK, FF_CHUNK, D_MODEL)),
            _const_spec((1, D_MODEL)),
        ],
        out_specs=pl.BlockSpec((tm, D_MODEL), lambda i: (i, 0)),
        out_shape=jax.ShapeDtypeStruct((t, D_MODEL), F32),
        compiler_params=_cparams(("parallel",)),
        name="ffn_half_step",
    )(x, mods, ln_w.reshape(1, D_MODEL), wg, wu, wo, final_w.reshape(1, D_MODEL))


PROJ_CHUNK = 256


def _proj_kernel(x_ref, mod_ref, ln_ref, w_ref, o_ref):
    x = x_ref[...]
    m = mod_ref[0]
    u = _bf(_norm_mod(x, ln_ref[...], m[3:4], m[4:5]))
    for c in range(N_IN // PROJ_CHUNK):
        sl = slice(c * PROJ_CHUNK, (c + 1) * PROJ_CHUNK)
        o_ref[:, sl] = _dot(u, w_ref[:, sl])


def mixer_in_proj(x, mods, ln_w, w_in, *, seq_len):
    t = x.shape[0]
    tm = ROW_TILE
    midx = _seq_mod_index(seq_len, mods.shape[0], tm)
    return pl.pallas_call(
        _proj_kernel,
        grid=(t // tm,),
        in_specs=[
            pl.BlockSpec((tm, D_MODEL), lambda i: (i, 0)),
            pl.BlockSpec((1, N_MOD, D_MODEL), lambda i: (midx(i), 0, 0)),
            _const_spec((1, D_MODEL)),
            _const_spec((D_MODEL, N_IN)),
        ],
        out_specs=pl.BlockSpec((tm, N_IN), lambda i: (i, 0)),
        out_shape=jax.ShapeDtypeStruct((t, N_IN), F32),
        compiler_params=_cparams(("parallel",)),
        name="mixer_in_proj",
    )(x, mods, ln_w.reshape(1, D_MODEL), _bf(w_in))


MERGE_CHUNK = 256


def _merge_kernel(x_ref, mod_ref, ln_ref, att_ref, lru_ref, hgf_ref, hgb_ref, hgg_ref, hgn_ref, headb_ref, hy_ref,
                  wg_ref, wa_ref, wl_ref, wh_ref, wy_ref, wo_ref, o_ref, mix_ref):
    x = x_ref[...]
    m = mod_ref[0]
    u = _bf(_norm_mod(x, ln_ref[...], m[3:4], m[4:5]))
    o = hgf_ref[0] + hgb_ref[0]
    sq = o * o
    sq_hi = _bf(sq)
    sq_lo = _bf(sq - sq_hi.astype(F32))
    ms = (_dot(sq_hi, headb_ref[...]) + _dot(sq_lo, headb_ref[...])) * (1.0 / HG_DV)
    hg = o * lax.rsqrt(ms + EPS) * hgn_ref[...] * _silu(hgg_ref[...])
    branches = (_bf(att_ref[...]), _bf(lru_ref[...]), _bf(hg), _bf(hy_ref[...]))
    w_bo = (wa_ref, wl_ref, wh_ref, wy_ref)
    for c in range(D_MODEL // MERGE_CHUNK):
        sl = slice(c * MERGE_CHUNK, (c + 1) * MERGE_CHUNK)
        mixed = None
        for n in range(N_BRANCH):
            term = _sigmoid(_dot(u, wg_ref[n, :, sl])) * _dot(branches[n], w_bo[n][:, sl])
            mixed = term if mixed is None else mixed + term
        mix_ref[:, sl] = _bf(mixed)
    o_ref[...] = x + m[5:6] * _dot(mix_ref[...], wo_ref[...])


def gated_merge(x, mods, ln_w, att, lru, hg_dirs, p2d, hg_norm, hy, w_gate, w_bo_attn, w_bo_lru, w_bo_hg, w_bo_hy,
                w_out, *, seq_len):
    t = x.shape[0]
    tm = ROW_TILE
    midx = _seq_mod_index(seq_len, mods.shape[0], tm)
    row = lambda w: pl.BlockSpec((tm, w), lambda i: (i, 0))
    _, _, headb = _hg_constants()
    return pl.pallas_call(
        _merge_kernel,
        grid=(t // tm,),
        in_specs=[
            row(D_MODEL),
            pl.BlockSpec((1, N_MOD, D_MODEL), lambda i: (midx(i), 0, 0)),
            _const_spec((1, D_MODEL)),
            row(D_ATTN), row(D_LRU),
            pl.BlockSpec((1, tm, D_HG), lambda i: (0, i, 0)),
            pl.BlockSpec((1, tm, D_HG), lambda i: (1, i, 0)),
            pl.BlockSpec((tm, D_HG), lambda i: (i, COL_HGATE // D_HG)),
            _const_spec((1, D_HG)),
            _const_spec((D_HG, D_HG)),
            row(D_HY),
            _const_spec((N_BRANCH, D_MODEL, D_MODEL)),
            _const_spec((D_ATTN, D_MODEL)),
            _const_spec((D_LRU, D_MODEL)),
            _const_spec((D_HG, D_MODEL)),
            _const_spec((D_HY, D_MODEL)),
            _const_spec((D_MODEL, D_MODEL)),
        ],
        out_specs=row(D_MODEL),
        out_shape=jax.ShapeDtypeStruct((t, D_MODEL), F32),
        scratch_shapes=[pltpu.VMEM((tm, D_MODEL), BF16)],
        compiler_params=_cparams(("parallel",)),
        name="gated_merge",
    )(x, mods, ln_w.reshape(1, D_MODEL), att, lru, hg_dirs, hg_dirs, p2d, hg_norm.reshape(1, D_HG), headb, hy,
      _bf(w_gate), _bf(w_bo_attn), _bf(w_bo_lru), _bf(w_bo_hg), _bf(w_bo_hy), _bf(w_out))


def _dot_nt(a, b):
    return lax.dot_general(a, b, (((1,), (1,)), ((), ())), preferred_element_type=F32)


def _ctx_attn_kernel(q_ref, k_ref, v_ref, sink_ref, o_ref):
    rows = q_ref.shape[1]
    blk = ATTN_BLOCK
    keys = _dup_heads(k_ref[0])
    vals_t = _values_t(v_ref[0])
    low = lax.broadcasted_iota(jnp.int32, (blk, 2 * HEAD_DIM), 1) < HEAD_DIM
    units = [(pair, qb) for qb in range(rows // blk) for pair in range(N_HEADS // 2)]

    def where(pair, qb):
        return slice(qb * blk, (qb + 1) * blk), slice(pair * 2 * HEAD_DIM, (pair + 1) * 2 * HEAD_DIM)

    scores = []
    for pair, qb in units:
        q_rows, lanes = where(pair, qb)
        q = q_ref[0, q_rows, lanes] * ATTN_SCALE
        q_heads = _bf(jnp.concatenate([jnp.where(low, q, 0.0), jnp.where(low, 0.0, q)], axis=0))
        scores.append(_dot_nt(keys[(2 * pair) // GQA_GROUP], q_heads))
    probs = []
    for (pair, qb), s in zip(units, scores):
        sink = jnp.concatenate([sink_ref[2 * pair:2 * pair + 1, :], sink_ref[2 * pair + 1:2 * pair + 2, :]], axis=1)
        m = jnp.maximum(jnp.max(s, axis=0, keepdims=True), sink)
        probs.append((_bf(jnp.exp(s - m)), jnp.exp(sink - m)))
    for (pair, qb), (p, sink_term) in zip(units, probs):
        q_rows, lanes = where(pair, qb)
        acc = _dot(vals_t[(2 * pair) // GQA_GROUP], p)
        out_t = acc[:HEAD_DIM, :] / (acc[HEAD_DIM:HEAD_DIM + 1, :] + sink_term)
        o_ref[0, q_rows, lanes] = jnp.transpose(jnp.concatenate([out_t[:, :blk], out_t[:, blk:]], axis=0))


def _sink_table(sink):
    return jnp.broadcast_to(sink.astype(F32)[:, None], (N_HEADS, 128))


def context_attention(p_ctx, sink):
    b, seq, _ = p_ctx.shape
    return pl.pallas_call(
        _ctx_attn_kernel,
        grid=(b,),
        in_specs=[
            pl.BlockSpec((1, seq, D_ATTN), lambda i: (i, 0, 0)),
            pl.BlockSpec((1, seq, D_KV), lambda i: (i, 0, COL_K // D_KV)),
            pl.BlockSpec((1, seq, D_KV), lambda i: (i, 0, COL_V // D_KV)),
            _const_spec((N_HEADS, 128)),
        ],
        out_specs=pl.BlockSpec((1, seq, D_ATTN), lambda i: (i, 0, 0)),
        out_shape=jax.ShapeDtypeStruct((b, seq, D_ATTN), F32),
        compiler_params=_cparams(("parallel",)),
        name="context_attention",
    )(p_ctx, p_ctx, p_ctx, _sink_table(sink))


def _rope(x, cos, sin_signed):
    n = x.shape[-1]
    lane = lax.broadcasted_iota(jnp.int32, x.shape, x.ndim - 1)
    first = (lane % (HEAD_DIM // 2)) < (HEAD_DIM // 4)
    partner = jnp.where(first, pltpu.roll(x, n - HEAD_DIM // 4, axis=x.ndim - 1),
                        pltpu.roll(x, HEAD_DIM // 4, axis=x.ndim - 1))
    return x * cos + partner * sin_signed


def _rope_tables(seq):
    rows = seq // GRID_W
    row = np.repeat(np.arange(rows, dtype=np.float32), GRID_W)
    col = np.tile(np.arange(GRID_W, dtype=np.float32), rows)
    n = HEAD_DIM // 4
    inv = (ROPE_BASE ** (-jnp.arange(n, dtype=F32) / n))
    ang_r = jnp.asarray(row)[:, None] * inv
    ang_c = jnp.asarray(col)[:, None] * inv
    cos = jnp.concatenate([jnp.cos(ang_r), jnp.cos(ang_r), jnp.cos(ang_c), jnp.cos(ang_c)], axis=-1)
    sin = jnp.concatenate([-jnp.sin(ang_r), jnp.sin(ang_r), -jnp.sin(ang_c), jnp.sin(ang_c)], axis=-1)
    return jnp.tile(cos, (1, 2)), jnp.tile(sin, (1, 2))


ATTN_PREP_ROWS = 512
ATTN_QBLOCKS = 8


def _dup_heads(x):
    lane = lax.broadcasted_iota(jnp.int32, x.shape, 1)
    swapped = pltpu.roll(x, HEAD_DIM, axis=1)
    low = lane < HEAD_DIM
    return _bf(jnp.where(low, x, swapped)), _bf(jnp.where(low, swapped, x))


def _values_t(x):
    xt = jnp.transpose(x)
    ones = jnp.ones((HEAD_DIM, x.shape[0]), F32)
    return (_bf(jnp.concatenate([xt[:HEAD_DIM], ones], axis=0)),
            _bf(jnp.concatenate([xt[HEAD_DIM:], ones], axis=0)))


def _lat_attn_kernel(q_ref, k_ref, v_ref, kc_ref, vc_ref, cos_ref, sin_ref, sink_ref, o_ref,
                     kd_ref, vt_ref, kcd_ref, vct_ref):
    seq = k_ref.shape[1]
    blk = ATTN_BLOCK
    band = 3 * blk
    past = kc_ref.shape[1]
    i = pl.program_id(1)

    @pl.when(i == 0)
    def _():
        for ti in range(seq // ATTN_PREP_ROWS):
            rows = slice(ti * ATTN_PREP_ROWS, (ti + 1) * ATTN_PREP_ROWS)
            k0, k1 = _dup_heads(_rope(k_ref[0, rows, :], cos_ref[rows, :], sin_ref[rows, :]))
            kd_ref[0, rows, :] = k0
            kd_ref[1, rows, :] = k1
            v0, v1 = _values_t(v_ref[0, rows, :])
            vt_ref[0, :, rows] = v0
            vt_ref[1, :, rows] = v1
        kc0, kc1 = _dup_heads(kc_ref[0])
        kcd_ref[0] = kc0
        kcd_ref[1] = kc1
        vc0, vc1 = _values_t(vc_ref[0])
        vct_ref[0] = vc0
        vct_ref[1] = vc1

    key = lax.broadcasted_iota(jnp.int32, (band + past, 2 * blk), 0)
    qry = lax.broadcasted_iota(jnp.int32, (band + past, 2 * blk), 1) % blk
    low = lax.broadcasted_iota(jnp.int32, (blk, 2 * HEAD_DIM), 1) < HEAD_DIM
    subs = []
    for sub in range(ATTN_QBLOCKS):
        ib = ATTN_QBLOCKS * i + sub
        start = pl.multiple_of(jnp.clip((ib - 1) * blk, 0, seq - band), blk)
        q0 = pl.multiple_of(ib * blk, blk)
        subs.append(dict(
            rows=slice(sub * blk, (sub + 1) * blk),
            cos=cos_ref[pl.ds(q0, blk), :], sin=sin_ref[pl.ds(q0, blk), :],
            mask=(jnp.abs((q0 + qry) - (start + key)) <= WINDOW) | (key >= band),
            keys=[jnp.concatenate([kd_ref[kvh, pl.ds(start, band), :], kcd_ref[kvh]], axis=0)
                  for kvh in range(N_KV_HEADS)],
            vals_t=[jnp.concatenate([vt_ref[kvh, :, pl.ds(start, band)], vct_ref[kvh]], axis=1)
                    for kvh in range(N_KV_HEADS)]))
    units = [(sub, pair) for sub in range(ATTN_QBLOCKS) for pair in range(N_HEADS // 2)]
    scores = []
    for sub, pair in units:
        u = subs[sub]
        lanes = slice(pair * 2 * HEAD_DIM, (pair + 1) * 2 * HEAD_DIM)
        q = _rope(q_ref[0, u['rows'], lanes], u['cos'], u['sin']) * ATTN_SCALE
        q_heads = _bf(jnp.concatenate([jnp.where(low, q, 0.0), jnp.where(low, 0.0, q)], axis=0))
        scores.append(_dot_nt(u['keys'][(2 * pair) // GQA_GROUP], q_heads))
    probs = []
    for (sub, pair), raw in zip(units, scores):
        s = jnp.where(subs[sub]['mask'], raw, -jnp.inf)
        sink = jnp.concatenate([sink_ref[2 * pair:2 * pair + 1, :], sink_ref[2 * pair + 1:2 * pair + 2, :]], axis=1)
        m = jnp.maximum(jnp.max(s, axis=0, keepdims=True), sink)
        probs.append((_bf(jnp.exp(s - m)), jnp.exp(sink - m)))
    for (sub, pair), (p, sink_term) in zip(units, probs):
        u = subs[sub]
        lanes = slice(pair * 2 * HEAD_DIM, (pair + 1) * 2 * HEAD_DIM)
        acc = _dot(u['vals_t'][(2 * pair) // GQA_GROUP], p)
        out_t = acc[:HEAD_DIM, :] / (acc[HEAD_DIM:HEAD_DIM + 1, :] + sink_term)
        o_ref[0, u['rows'], lanes] = jnp.transpose(jnp.concatenate([out_t[:, :blk], out_t[:, blk:]], axis=0))


def latent_attention(p_lat, k_ctx, v_ctx, sink):
    b, seq, _ = p_lat.shape
    past = k_ctx.shape[1]
    cos, sin = _rope_tables(seq)
    return pl.pallas_call(
        _lat_attn_kernel,
        grid=(b, seq // (ATTN_QBLOCKS * ATTN_BLOCK)),
        in_specs=[
            pl.BlockSpec((1, ATTN_QBLOCKS * ATTN_BLOCK, D_ATTN), lambda bi, i: (bi, i, 0)),
            pl.BlockSpec((1, seq, D_KV), lambda bi, i: (bi, 0, COL_K // D_KV)),
            pl.BlockSpec((1, seq, D_KV), lambda bi, i: (bi, 0, COL_V // D_KV)),
            pl.BlockSpec((1, past, D_KV), lambda bi, i: (bi, 0, 0)),
            pl.BlockSpec((1, past, D_KV), lambda bi, i: (bi, 0, 0)),
            _const_spec((seq, 128)),
            _const_spec((seq, 128)),
            _const_spec((N_HEADS, 128)),
        ],
        out_specs=pl.BlockSpec((1, ATTN_QBLOCKS * ATTN_BLOCK, D_ATTN), lambda bi, i: (bi, i, 0)),
        out_shape=jax.ShapeDtypeStruct((b, seq, D_ATTN), F32),
        scratch_shapes=[
            pltpu.VMEM((N_KV_HEADS, seq, D_KV), BF16),
            pltpu.VMEM((N_KV_HEADS, D_KV, seq), BF16),
            pltpu.VMEM((N_KV_HEADS, past, D_KV), BF16),
            pltpu.VMEM((N_KV_HEADS, D_KV, past), BF16),
        ],
        compiler_params=_cparams(("parallel", "arbitrary")),
        name="latent_attention",
    )(p_lat, p_lat, p_lat, k_ctx, v_ctx, cos, sin, _sink_table(sink))


LRU_HALF = D_LRU // 2
LRU_ROWS = 256
LRU_PAD = 8


def _neg_expm1(y, exp_y):
    return jnp.tanh(-0.5 * y) * (exp_y + 1.0)


def _softplus(x):
    return jnp.maximum(x, 0.0) + jnp.log1p(jnp.exp(-jnp.abs(x)))


def _gelu_tanh(x):
    return 0.5 * x * (1.0 + jnp.tanh(math.sqrt(2.0 / math.pi) * (x + 0.044715 * (x * x * x))))


LRU_SCAN = 8


def _affine_scan8(a, b, *, reverse):
    row = lax.broadcasted_iota(jnp.int32, a.shape, 0)
    k = 1
    while k < LRU_SCAN:
        if reverse:
            valid = row < LRU_SCAN - k
            shift = LRU_SCAN - k
        else:
            valid = row >= k
            shift = k
        a_prev = jnp.where(valid, pltpu.roll(a, shift, axis=0), 1.0)
        b_prev = jnp.where(valid, pltpu.roll(b, shift, axis=0), 0.0)
        b = b + a * b_prev
        a = a * a_prev
        k *= 2
    return a, b


def _lru_kernel(x_ref, g_ref, cw_ref, cb_ref, wr_ref, br_ref, wi_ref, bi_ref, lam_ref, h0_ref,
                o_ref, st_ref, pad_ref, a_ref, bx_ref, hb_ref):
    seq = x_ref.shape[1]
    zeros = jnp.zeros((LRU_PAD, LRU_HALF), F32)
    pad_ref[0:LRU_PAD, :] = zeros
    pad_ref[LRU_PAD + seq:2 * LRU_PAD + seq, :] = zeros
    pad_ref[LRU_PAD:LRU_PAD + seq, :] = x_ref[0]
    left = LRU_CONV // 2
    for ti in range(seq // LRU_ROWS):
        r0 = ti * LRU_ROWS
        xc = cb_ref[...]
        for k in range(LRU_CONV):
            xc = xc + cw_ref[k:k + 1, :] * pad_ref[r0 + LRU_PAD - left + k:r0 + LRU_PAD - left + k + LRU_ROWS, :]
        xcb = _bf(xc)
        for d in range(2):
            r = _sigmoid(_dot(xcb, wr_ref[d]) + br_ref[d:d + 1, :])
            i = _sigmoid(_dot(xcb, wi_ref[d]) + bi_ref[d:d + 1, :])
            log_a = (-LRU_C) * r * _softplus(-lam_ref[d:d + 1, :])
            a = jnp.exp(log_a)
            a_ref[d, r0:r0 + LRU_ROWS, :] = a
            bx_ref[d, r0:r0 + LRU_ROWS, :] = jnp.sqrt(_neg_expm1(2.0 * log_a, a * a)) * (i * xc)

    nt = seq // LRU_SCAN

    def step(blk, carry):
        hf, hb = carry
        rf = pl.ds(pl.multiple_of(blk * LRU_SCAN, LRU_SCAN), LRU_SCAN)
        rb = pl.ds(pl.multiple_of((nt - 1 - blk) * LRU_SCAN, LRU_SCAN), LRU_SCAN)
        af, bf_ = _affine_scan8(a_ref[0, rf, :], bx_ref[0, rf, :], reverse=False)
        ab, bb = _affine_scan8(a_ref[1, rb, :], bx_ref[1, rb, :], reverse=True)
        tile_f = af * hf + bf_
        tile_b = ab * hb + bb
        o_ref[0, rf, :] = tile_f
        hb_ref[rb, :] = tile_b
        return tile_f[LRU_SCAN - 1:LRU_SCAN, :], tile_b[0:1, :]

    hf, hb = lax.fori_loop(0, nt, step, (h0_ref[0, 0:1, :], h0_ref[0, 1:2, :]), unroll=2)
    st_ref[0, 0:1, :] = hf
    st_ref[0, 1:2, :] = hb
    for ti in range(seq // LRU_ROWS):
        rows = slice(ti * LRU_ROWS, (ti + 1) * LRU_ROWS)
        o_ref[0, rows, :] = (o_ref[0, rows, :] + hb_ref[rows, :]) * _gelu_tanh(g_ref[0, rows, :])


def _block_diag(w):
    n, blk, _ = w.shape
    eye = jnp.eye(n, dtype=w.dtype)
    return (eye[:, None, :, None] * w[:, :, None, :]).reshape(n * blk, n * blk)


def rglru_mixer(p, conv_w, conv_b, w_r, b_r, w_i, b_i, lam, h0):
    b, seq, _ = p.shape
    wr = _bf(jnp.stack([_block_diag(w_r[d]) for d in range(2)]))
    wi = _bf(jnp.stack([_block_diag(w_i[d]) for d in range(2)]))
    hw = LRU_HALF
    vec = lambda rows: pl.BlockSpec((rows, hw), lambda bi, h: (0, h))
    return pl.pallas_call(
        _lru_kernel,
        grid=(b, 2),
        in_specs=[
            pl.BlockSpec((1, seq, hw), lambda bi, h: (bi, 0, COL_RX // hw + h)),
            pl.BlockSpec((1, seq, hw), lambda bi, h: (bi, 0, COL_RG // hw + h)),
            vec(LRU_CONV), vec(1),
            pl.BlockSpec((2, hw, hw), lambda bi, h: (0, h, h)), vec(2),
            pl.BlockSpec((2, hw, hw), lambda bi, h: (0, h, h)), vec(2),
            vec(2),
            pl.BlockSpec((1, 2, hw), lambda bi, h: (bi, 0, h)),
        ],
        out_specs=[
            pl.BlockSpec((1, seq, hw), lambda bi, h: (bi, 0, h)),
            pl.BlockSpec((1, 2, hw), lambda bi, h: (bi, 0, h)),
        ],
        out_shape=[jax.ShapeDtypeStruct((b, seq, D_LRU), F32), jax.ShapeDtypeStruct((b, 2, D_LRU), F32)],
        scratch_shapes=[
            pltpu.VMEM((seq + 2 * LRU_PAD, hw), F32),
            pltpu.VMEM((2, seq, hw), F32),
            pltpu.VMEM((2, seq, hw), F32),
            pltpu.VMEM((seq, hw), F32),
        ],
        compiler_params=_cparams(("parallel", "parallel")),
        name="rglru_mixer",
    )(p, p, conv_w, conv_b.reshape(1, D_LRU), wr, b_r, wi, b_i, lam, h0)


HG_TILE = 256
HG_GROUP = 32
HG_AHEAD = 1
HG_NBUF = 2
LOG2_E = math.log2(math.e)


def _hg_constants():
    idx = np.arange(HG_TILE)
    same_blk = (idx[:, None] // HG_BLK) == (idx[None, :] // HG_BLK)
    tri_f = same_blk & (idx[None, :] <= idx[:, None])
    tri_b = same_blk & (idx[None, :] >= idx[:, None])
    head = (idx[:, None] // HG_DK) == (idx[None, :] // HG_DK)
    return jnp.asarray(np.stack([tri_f, tri_b]), BF16), jnp.asarray(same_blk, BF16), jnp.asarray(head, BF16)


def _hgrn_kernel(q_ref, z_ref, v_ref, lb_ref, tri_ref, ones_ref, headb_ref, s0_ref,
                 o_ref, st_ref, state, *scratch):
    att_bufs = scratch[0:HG_NBUF]
    upd_bufs = scratch[HG_NBUF:2 * HG_NBUF]
    qst_bufs = scratch[2 * HG_NBUF:3 * HG_NBUF]
    qs, srcs, vs, cums, qds, kns, tots = scratch[3 * HG_NBUF:]
    seg_rows = q_ref.shape[1]
    d = pl.program_id(1)
    s = pl.program_id(2)
    nseg = pl.num_programs(2)

    @pl.when(s == 0)
    def _():
        state[...] = jnp.concatenate([jnp.transpose(s0_ref[0, 0, h]) for h in range(HG_HEADS)], axis=1)

    lb = lb_ref[0]
    for ti in range(seg_rows // HG_TILE):
        rows = slice(ti * HG_TILE, (ti + 1) * HG_TILE)
        f = lb + (1.0 - lb) * _sigmoid(z_ref[0, rows, :])
        g = jnp.log(f)
        kk = 1.0 - f
        q = _silu(q_ref[0, rows, :]) * HG_SCALE
        cum = _dot_exact_rhs(tri_ref[0], g)
        tot = _dot_exact_rhs(ones_ref[...], g)
        qs[rows, :] = q
        srcs[rows, :] = (cum - jnp.log(kk)) * LOG2_E
        vs[rows, :] = v_ref[0, rows, :]
        cums[rows, :] = cum * LOG2_E
        tots[rows, :] = tot
        qds[rows, :] = q * jnp.exp(cum)
        kns[rows, :] = kk * jnp.exp(tot - cum)

    nblk = seg_rows // HG_BLK
    sgn = jnp.where(d == 0, 1, -1)
    t_signed = [(lax.broadcasted_iota(jnp.int32, (8, D_HG), 0) + 8 * h) * sgn for h in range(HG_BLK // 8)]
    lane_head_blk = lax.broadcasted_iota(jnp.int32, (HG_BLK, D_HG), 1) // HG_DK
    lane_head_dv = lax.broadcasted_iota(jnp.int32, (HG_DV, D_HG), 1) // HG_DK

    n_half = HG_BLK // 8

    def block_rows(j):
        je = jnp.where(d == 0, j, nblk - 1 - j)
        return pl.multiple_of(je * HG_BLK, HG_BLK)

    def prepare(j, slot):
        r0 = block_rows(j)
        rows = pl.ds(r0, HG_BLK)
        cum_h = [cums[pl.ds(r0 + 8 * h, 8), :] for h in range(n_half)]
        q_h = [qs[pl.ds(r0 + 8 * h, 8), :] for h in range(n_half)]
        tiles = []
        for src in range(HG_BLK):
            src_row = srcs[pl.ds(r0 + src, 1), :]
            for h in range(n_half):
                pair = jnp.exp2(cum_h[h] - src_row) * q_h[h]
                tiles.append(jnp.where(t_signed[h] >= src * sgn, pair, 0.0))
        att_bufs[slot][...] = _dot(_bf(jnp.concatenate(tiles, axis=0)), headb_ref[...])
        upd = lax.dot_general(_bf(vs[rows, :]), _bf(kns[rows, :]), (((0,), (0,)), ((), ())),
                              preferred_element_type=F32)
        upd_fold = jnp.where(lane_head_dv == 0, upd[0:HG_DV], 0.0)
        for h in range(1, HG_HEADS):
            upd_fold = upd_fold + jnp.where(lane_head_dv == h, upd[h * HG_DV:(h + 1) * HG_DV], 0.0)
        upd_bufs[slot][...] = upd_fold
        qd = qds[rows, :]
        qst_bufs[slot][...] = _bf(jnp.concatenate(
            [jnp.where(lane_head_blk == h, qd, 0.0) for h in range(HG_HEADS)], axis=0))

    def finish(j, slot, st):
        r0 = block_rows(j)
        rows = pl.ds(r0, HG_BLK)
        o_halves = [None] * n_half
        for src in range(HG_BLK):
            v_row = vs[pl.ds(r0 + src, 1), :]
            for h in range(n_half):
                term = att_bufs[slot][src * HG_BLK + h * 8:src * HG_BLK + (h + 1) * 8, :] * v_row
                o_halves[h] = term if o_halves[h] is None else o_halves[h] + term
        by_head = _dot_nt(qst_bufs[slot][...], _bf(st))
        o_state = jnp.concatenate([by_head[h * HG_BLK:(h + 1) * HG_BLK] for h in range(HG_HEADS)], axis=1)
        o_ref[0, 0, rows, :] = jnp.concatenate(o_halves, axis=0) + o_state
        return jnp.exp(tots[pl.ds(r0, 1), :]) * st + upd_bufs[slot][...]

    for k in range(HG_AHEAD):
        prepare(k, k)

    group = min(HG_GROUP, nblk)

    def block_group(i, st):
        j = group * i
        for k in range(group):
            prepare(jnp.minimum(j + k + HG_AHEAD, nblk - 1), (k + HG_AHEAD) % HG_NBUF)
            st = finish(j + k, k % HG_NBUF, st)
        return st

    if group == nblk:
        st = state[...]
        for k in range(nblk):
            if k + HG_AHEAD < nblk:
                prepare(k + HG_AHEAD, (k + HG_AHEAD) % HG_NBUF)
            st = finish(k, k % HG_NBUF, st)
        state[...] = st
    else:
        state[...] = lax.fori_loop(0, nblk // group, block_group, state[...])

    @pl.when(s == nseg - 1)
    def _():
        for h in range(HG_HEADS):
            st_ref[0, 0, h] = jnp.transpose(state[:, h * HG_DK:(h + 1) * HG_DK])


def hgrn2_scan(p, lb, s0, seg_rows):
    b, seq, _ = p.shape
    nseg = seq // seg_rows
    tri, ones, headb = _hg_constants()
    seg = lambda d, s: jnp.where(d == 0, s, nseg - 1 - s)
    col = lambda c: pl.BlockSpec((1, seg_rows, D_HG), lambda bi, d, s: (bi, seg(d, s), c // D_HG))
    scr = lambda: pltpu.VMEM((seg_rows, D_HG), F32)
    return pl.pallas_call(
        _hgrn_kernel,
        grid=(b, 2, nseg),
        in_specs=[
            col(COL_HQ),
            pl.BlockSpec((1, seg_rows, D_HG), lambda bi, d, s: (bi, seg(d, s), COL_HFF // D_HG + d)),
            col(COL_HI),
            pl.BlockSpec((1, 1, D_HG), lambda bi, d, s: (d, 0, 0)),
            pl.BlockSpec((1, HG_TILE, HG_TILE), lambda bi, d, s: (d, 0, 0)),
            _const_spec((HG_TILE, HG_TILE)),
            _const_spec((D_HG, D_HG)),
            pl.BlockSpec((1, 1, HG_HEADS, HG_DK, HG_DV), lambda bi, d, s: (bi, d, 0, 0, 0)),
        ],
        out_specs=[
            pl.BlockSpec((1, 1, seg_rows, D_HG), lambda bi, d, s: (d, bi, seg(d, s), 0)),
            pl.BlockSpec((1, 1, HG_HEADS, HG_DK, HG_DV), lambda bi, d, s: (bi, d, 0, 0, 0)),
        ],
        out_shape=[jax.ShapeDtypeStruct((2, b, seq, D_HG), F32),
                   jax.ShapeDtypeStruct((b, 2, HG_HEADS, HG_DK, HG_DV), F32)],
        scratch_shapes=[pltpu.VMEM((HG_DV, D_HG), F32)]
        + [pltpu.VMEM((HG_BLK * HG_BLK, D_HG), F32) for _ in range(HG_NBUF)]
        + [pltpu.VMEM((HG_DV, D_HG), F32) for _ in range(HG_NBUF)]
        + [pltpu.VMEM((HG_HEADS * HG_BLK, D_HG), BF16) for _ in range(HG_NBUF)] + [scr() for _ in range(7)],
        compiler_params=_cparams(("parallel", "parallel", "arbitrary")),
        name="hgrn2_scan",
    )(p, p, p, lb.reshape(2, 1, D_HG), tri, ones, headb, s0)


HY_FEAT_PAD = 128
HY_GEN_ROWS = 256


def _dot_hi(a, b):
    ah = _bf(a)
    al = _bf(a - ah.astype(F32))
    bh = _bf(b)
    bl = _bf(b - bh.astype(F32))
    return _dot(ah, bh) + _dot(ah, bl) + _dot(al, bh)


HY_HALF_COLS = HY_ORDER * D_HY


def _filter_gen_kernel(feat_ref, w1_ref, b1_ref, w2_ref, b2_ref, w3_ref, fr_ref, dl_ref, h_ref, ssq_ref, *, seq):
    i = pl.program_id(0)
    tm = feat_ref.shape[0]
    feat = feat_ref[...]
    fr = fr_ref[...]
    h = jnp.sin(fr * (_dot_hi(feat, w1_ref[...]) + b1_ref[...]))
    h = jnp.sin(fr * (_dot_hi(h, w2_ref[...]) + b2_ref[...]))
    h = _dot_hi(h, w3_ref[...])
    decay = jnp.exp(-feat[:, 0:1] * dl_ref[...])
    decay = jnp.concatenate([decay] * HY_ORDER, axis=1)
    row = i * tm + lax.broadcasted_iota(jnp.int32, (tm, HY_HALF_COLS), 0)
    taps = h * decay
    h_ref[...] = jnp.where(row == seq, 0.0, taps)

    @pl.when(i == 0)
    def _():
        ssq_ref[...] = jnp.zeros_like(ssq_ref)

    ssq_ref[...] += jnp.sum(taps * taps, axis=0, keepdims=True)


def _hyena_features(seq):
    order = np.concatenate([np.arange(seq), [0], np.arange(seq - 1, 0, -1)])
    pos = order.astype(np.float32)
    t = pos / np.float32(max(seq - 1, 1))
    bands = np.linspace(1e-4, HY_BANDS - 1, HY_BANDS, dtype=np.float32)
    ang = np.float32(2.0 * math.pi / seq) * pos[:, None] * bands[None, :]
    feat = np.concatenate([t[:, None], np.cos(ang), np.sin(ang)], axis=-1).astype(np.float32)
    out = np.zeros((2 * seq, HY_FEAT_PAD), np.float32)
    out[:, :HY_EMB] = feat
    return jnp.asarray(out)


def hyena_filter_gen(seq, w1, b1, w2, b2, w3, freq):
    feat = _hyena_features(seq)
    w1p = jnp.zeros((HY_FEAT_PAD, HY_FFN), F32).at[:HY_EMB].set(w1.astype(F32))
    deltas = jnp.asarray(np.linspace(HY_DECAY_SLOW, HY_DECAY_FAST, D_HY, dtype=np.float32)).reshape(1, D_HY)
    tm = HY_GEN_ROWS
    return pl.pallas_call(
        functools.partial(_filter_gen_kernel, seq=seq),
        grid=(2 * seq // tm,),
        in_specs=[
            pl.BlockSpec((tm, HY_FEAT_PAD), lambda i: (i, 0)),
            _const_spec((HY_FEAT_PAD, HY_FFN)), _const_spec((1, HY_FFN)),
            _const_spec((HY_FFN, HY_FFN)), _const_spec((1, HY_FFN)),
            pl.BlockSpec((HY_FFN, HY_HALF_COLS), lambda i: (0, (i * tm) // seq)),
            _const_spec((1, HY_FFN)), _const_spec((1, D_HY)),
        ],
        out_specs=[pl.BlockSpec((tm, HY_HALF_COLS), lambda i: (i, 0)), _const_spec((1, HY_HALF_COLS))],
        out_shape=[jax.ShapeDtypeStruct((2 * seq, HY_HALF_COLS), F32),
                   jax.ShapeDtypeStruct((1, HY_HALF_COLS), F32)],
        compiler_params=_cparams(("arbitrary",)),
        name="hyena_filter_gen",
    )(feat, w1p, b1.reshape(1, HY_FFN), w2, b2.reshape(1, HY_FFN), w3, freq.reshape(1, HY_FFN), deltas)


def _filter_norm(ssq_ref, n):
    return lax.rsqrt(ssq_ref[:, n * D_HY:(n + 1) * D_HY])


def _conv3(pad_ref, w_ref, b_ref, r0, rows, pad):
    acc = b_ref[...]
    for k in range(HY_SHORT):
        lo = r0 + pad - HY_SHORT // 2 + k
        acc = acc + w_ref[k:k + 1, :] * pad_ref[lo:lo + rows, :]
    return acc


def _dft_constants(seq):
    n = 2 * seq
    k = np.arange(n)
    ang = 2.0 * np.pi * np.outer(k, k) / n
    fwd = np.concatenate([np.cos(ang), -np.sin(ang)], axis=0)
    inv = np.concatenate([np.cos(ang[:seq]), -np.sin(ang[:seq])], axis=1) / n
    return jnp.asarray(fwd, BF16), jnp.asarray(inv, BF16)


def _ctx_spectrum_kernel(filt_ref, ssq_ref, fwd_ref, h_ref):
    n = pl.program_id(0)
    norm = jnp.where(n == 0, _filter_norm(ssq_ref, 0), _filter_norm(ssq_ref, 1))
    x = filt_ref[...] * norm
    xh = _bf(x)
    xl = _bf(x - xh.astype(F32))
    h_ref[0] = _dot(fwd_ref[...], xh) + _dot(fwd_ref[...], xl)


def hyena_ctx_spectrum(filt, ssq):
    n = filt.shape[0]
    fwd, _ = _dft_constants(n // 2)
    return pl.pallas_call(
        _ctx_spectrum_kernel,
        grid=(HY_ORDER,),
        in_specs=[pl.BlockSpec((n, D_HY), lambda o: (0, o)), _const_spec((1, HY_HALF_COLS)),
                  _const_spec((2 * n, n))],
        out_specs=pl.BlockSpec((1, 2 * n, D_HY), lambda o: (o, 0, 0)),
        out_shape=jax.ShapeDtypeStruct((HY_ORDER, 2 * n, D_HY), F32),
        compiler_params=_cparams(("parallel",)),
        name="hyena_ctx_spectrum",
    )(filt, ssq, fwd)


HY_PAD = 8


def _hyena_ctx_kernel(v_ref, x1_ref, x2_ref, cw_ref, cb_ref, bias_ref, spec_ref, fwd_ref, inv_ref, o_ref, pad_ref):
    seq = v_ref.shape[1]
    n = 2 * seq
    zeros = jnp.zeros((HY_PAD, D_HY), F32)
    pad_ref[0:HY_PAD, :] = zeros
    pad_ref[HY_PAD + seq:2 * HY_PAD + seq, :] = zeros

    def short_conv(src_ref, part):
        pad_ref[HY_PAD:HY_PAD + seq, :] = src_ref[0]
        return _conv3(pad_ref, cw_ref.at[part], cb_ref.at[part], 0, seq, HY_PAD)

    z = short_conv(v_ref, 0)
    for order, gate_ref in enumerate((x1_ref, x2_ref)):
        spec = _dot(fwd_ref[:, 0:seq], _bf(z))
        xr, xi = spec[:n], spec[n:]
        hr, hi = spec_ref[order, 0:n, :], spec_ref[order, n:2 * n, :]
        prod = jnp.concatenate([xr * hr - xi * hi, xr * hi + xi * hr], axis=0)
        conv = _dot(inv_ref[...], _bf(prod))
        z = short_conv(gate_ref, order + 1) * (conv + bias_ref[order:order + 1, :] * z)
    o_ref[0] = z


def hyena_ctx(p, conv_w, conv_b, bias, spec):
    b, seq, _ = p.shape
    fwd, inv = _dft_constants(seq)
    col = lambda c: pl.BlockSpec((1, seq, D_HY), lambda i: (i, 0, c))
    c0 = COL_HY // D_HY
    return pl.pallas_call(
        _hyena_ctx_kernel,
        grid=(b,),
        in_specs=[
            col(c0), col(c0 + 1), col(c0 + 2),
            _const_spec((3, HY_SHORT, D_HY)), _const_spec((3, 1, D_HY)), _const_spec((HY_ORDER, D_HY)),
            _const_spec((HY_ORDER, 4 * seq, D_HY)), _const_spec((4 * seq, 2 * seq)), _const_spec((seq, 4 * seq)),
        ],
        out_specs=pl.BlockSpec((1, seq, D_HY), lambda i: (i, 0, 0)),
        out_shape=jax.ShapeDtypeStruct((b, seq, D_HY), F32),
        scratch_shapes=[pltpu.VMEM((seq + 2 * HY_PAD, D_HY), F32)],
        compiler_params=_cparams(("parallel",)),
        name="hyena_ctx",
    )(p, p, p, conv_w.reshape(HY_SHORT, 3, D_HY).transpose(1, 0, 2), conv_b.reshape(3, 1, D_HY), bias, spec,
      fwd, inv)


FFT_N1 = 64
FFT_N2 = 128
FFT_N = FFT_N1 * FFT_N2
FFT_K1 = FFT_N1 // 2 + 1
FFT_K1_PAD = 40
FFT_ROWS = FFT_K1_PAD * FFT_N2
HY_LANES = 128
FFT_UNROLL1 = 16
FFT_UNROLL2 = 11


def _fft_constants():
    n1, n2, n, kp = FFT_N1, FFT_N2, FFT_N, FFT_K1_PAD
    k1 = np.arange(kp)
    w1 = np.exp(-2j * np.pi * np.outer(k1, np.arange(n1)) / n1)
    tw = np.exp(-2j * np.pi * np.outer(np.arange(n2), k1) / n)
    f1c = tw[:, :, None] * w1[None, :, :]
    f1t = np.concatenate([f1c.real, f1c.imag], axis=1).transpose(0, 2, 1)
    w2 = np.exp(-2j * np.pi * np.outer(np.arange(n2), np.arange(n2)) / n2)
    f2 = np.block([[w2.real, -w2.imag], [w2.imag, w2.real]])
    g2c = np.conj(w2)
    g2 = np.block([[g2c.real, -g2c.imag], [g2c.imag, g2c.real]])
    weight = np.where((k1 == 0) | (k1 == n1 // 2), 1.0, 2.0) * (k1 < FFT_K1)
    mc = np.conj(w1.T)[None, :n1 // 2, :] * np.conj(tw)[:, None, :] * weight / n
    g1 = np.concatenate([mc.real, -mc.imag], axis=2)
    return (jnp.asarray(f1t, BF16), jnp.asarray(f2, BF16), jnp.asarray(g2, BF16), jnp.asarray(g1, BF16))


def _dot_tn(a, b):
    return lax.dot_general(a, b, (((0,), (0,)), ((), ())), preferred_element_type=F32)


def _fft_stage1(x_ref, f1t_ref, a_ref, n1_in):
    def body(n2, carry):
        rows = x_ref[pl.ds(n2, n1_in, stride=FFT_N2), :]
        out = _dot_tn(f1t_ref[n2, 0:n1_in, :], _bf(rows))
        a_ref[0, pl.ds(n2, FFT_K1, stride=FFT_N2), :] = out[:FFT_K1]
        a_ref[1, pl.ds(n2, FFT_K1, stride=FFT_N2), :] = out[FFT_K1_PAD:FFT_K1_PAD + FFT_K1]
        return carry
    lax.fori_loop(0, FFT_N2, body, 0, unroll=FFT_UNROLL1)


def _lat_spectrum_kernel(filt_ref, ssq_ref, f1t_ref, f2_ref, h_ref, a_ref):
    n = pl.program_id(0)
    half = pl.program_id(1)
    norm_full = jnp.where(n == 0, _filter_norm(ssq_ref, 0), _filter_norm(ssq_ref, 1))
    norm = jnp.where(half == 0, norm_full[:, :HY_LANES], norm_full[:, HY_LANES:])
    _fft_stage1(filt_ref, f1t_ref, a_ref, FFT_N1)

    def stage2(k1, carry):
        r0 = pl.multiple_of(k1 * FFT_N2, FFT_N2)
        rows = pl.ds(r0, FFT_N2)
        a = jnp.concatenate([a_ref[0, rows, :], a_ref[1, rows, :]], axis=0)
        b = _dot(f2_ref[...], _bf(a)) * norm
        h_ref[0, 0, rows, :] = _bf(b[:FFT_N2])
        h_ref[0, 1, rows, :] = _bf(b[FFT_N2:])
        return carry
    lax.fori_loop(0, FFT_K1, stage2, 0)


def hyena_lat_spectrum(filt, ssq):
    f1t, f2, _, _ = _fft_constants()
    rows = FFT_K1 * FFT_N2
    halves = D_HY // HY_LANES
    return pl.pallas_call(
        _lat_spectrum_kernel,
        grid=(HY_ORDER, halves),
        in_specs=[
            pl.BlockSpec((FFT_N, HY_LANES), lambda o, h: (0, o * halves + h)),
            _const_spec((1, HY_HALF_COLS)),
            _const_spec((FFT_N2, FFT_N1, 2 * FFT_K1_PAD)),
            _const_spec((2 * FFT_N2, 2 * FFT_N2)),
        ],
        out_specs=pl.BlockSpec((1, 2, rows, HY_LANES), lambda o, h: (o, 0, 0, h)),
        out_shape=jax.ShapeDtypeStruct((HY_ORDER, 2, rows, D_HY), BF16),
        scratch_shapes=[pltpu.VMEM((2, FFT_ROWS, HY_LANES), F32)],
        compiler_params=_cparams(("parallel", "parallel")),
        name="hyena_lat_spectrum",
    )(filt, ssq, f1t, f2)


HY_ROWS = 512


def _hyena_lat_kernel(v_ref, x1_ref, x2_ref, cw_ref, cb_ref, bias_ref, spec_ref, f1t_ref, f2_ref, g2_ref, g1_ref,
                      o_ref, pad_ref, z_ref, a_ref):
    seq = v_ref.shape[1]
    zeros = jnp.zeros((HY_PAD, HY_LANES), F32)
    pad_ref[0:HY_PAD, :] = zeros
    pad_ref[HY_PAD + seq:2 * HY_PAD + seq, :] = zeros
    pad_ref[HY_PAD:HY_PAD + seq, :] = v_ref[0]
    for ti in range(seq // HY_ROWS):
        r0 = ti * HY_ROWS
        z_ref[r0:r0 + HY_ROWS, :] = _conv3(pad_ref, cw_ref.at[0], cb_ref.at[0], r0, HY_ROWS, HY_PAD)
    tail = jnp.zeros((FFT_ROWS - FFT_K1 * FFT_N2, HY_LANES), F32)
    a_ref[0, FFT_K1 * FFT_N2:FFT_ROWS, :] = tail
    a_ref[1, FFT_K1 * FFT_N2:FFT_ROWS, :] = tail

    for order, gate_ref in enumerate((x1_ref, x2_ref)):
        _fft_stage1(z_ref, f1t_ref, a_ref, FFT_N1 // 2)

        def stage2(k1, carry):
            r0 = pl.multiple_of(k1 * FFT_N2, FFT_N2)
            rows = pl.ds(r0, FFT_N2)
            a = jnp.concatenate([a_ref[0, rows, :], a_ref[1, rows, :]], axis=0)
            b = _dot(f2_ref[...], _bf(a))
            br, bi = b[:FFT_N2], b[FFT_N2:]
            hr = spec_ref[order, 0, rows, :].astype(F32)
            hi = spec_ref[order, 1, rows, :].astype(F32)
            y = jnp.concatenate([br * hr - bi * hi, br * hi + bi * hr], axis=0)
            c = _dot(g2_ref[...], _bf(y))
            a_ref[0, rows, :] = c[:FFT_N2]
            a_ref[1, rows, :] = c[FFT_N2:]
            return carry
        lax.fori_loop(0, FFT_K1, stage2, 0, unroll=FFT_UNROLL2)

        def stage1_inv(n2, carry):
            c = jnp.concatenate([a_ref[0, pl.ds(n2, FFT_K1_PAD, stride=FFT_N2), :],
                                 a_ref[1, pl.ds(n2, FFT_K1_PAD, stride=FFT_N2), :]], axis=0)
            o_ref[0, pl.ds(n2, FFT_N1 // 2, stride=FFT_N2), :] = _dot(g1_ref[n2], _bf(c))
            return carry
        lax.fori_loop(0, FFT_N2, stage1_inv, 0, unroll=FFT_UNROLL1)

        pad_ref[HY_PAD:HY_PAD + seq, :] = gate_ref[0]
        for ti in range(seq // HY_ROWS):
            r0 = ti * HY_ROWS
            rows = slice(r0, r0 + HY_ROWS)
            gate = _conv3(pad_ref, cw_ref.at[order + 1], cb_ref.at[order + 1], r0, HY_ROWS, HY_PAD)
            z = gate * (o_ref[0, rows, :] + bias_ref[order:order + 1, :] * z_ref[rows, :])
            if order + 1 < HY_ORDER:
                z_ref[rows, :] = z
            else:
                o_ref[0, rows, :] = z


def hyena_lat(p, conv_w, conv_b, bias, spec):
    b, seq, _ = p.shape
    assert 2 * seq == FFT_N
    f1t, f2, g2, g1 = _fft_constants()
    hw = HY_LANES
    col = lambda c: pl.BlockSpec((1, seq, hw), lambda i, h: (i, 0, (COL_HY + c * D_HY) // hw + h))
    return pl.pallas_call(
        _hyena_lat_kernel,
        grid=(b, D_HY // hw),
        in_specs=[
            col(0), col(1), col(2),
            pl.BlockSpec((3, HY_SHORT, hw), lambda i, h: (0, 0, h)),
            pl.BlockSpec((3, 1, hw), lambda i, h: (0, 0, h)),
            pl.BlockSpec((HY_ORDER, hw), lambda i, h: (0, h)),
            pl.BlockSpec((HY_ORDER, 2, FFT_K1 * FFT_N2, hw), lambda i, h: (0, 0, 0, h)),
            _const_spec((FFT_N2, FFT_N1, 2 * FFT_K1_PAD)),
            _const_spec((2 * FFT_N2, 2 * FFT_N2)),
            _const_spec((2 * FFT_N2, 2 * FFT_N2)),
            _const_spec((FFT_N2, FFT_N1 // 2, 2 * FFT_K1_PAD)),
        ],
        out_specs=pl.BlockSpec((1, seq, hw), lambda i, h: (i, 0, h)),
        out_shape=jax.ShapeDtypeStruct((b, seq, D_HY), F32),
        scratch_shapes=[
            pltpu.VMEM((seq + 2 * HY_PAD, hw), F32),
            pltpu.VMEM((seq, hw), F32),
            pltpu.VMEM((2, FFT_ROWS, hw), F32),
        ],
        compiler_params=_cparams(("parallel", "parallel")),
        name="hyena_lat",
    )(p, p, p, conv_w.reshape(HY_SHORT, 3, D_HY).transpose(1, 0, 2), conv_b.reshape(3, 1, D_HY), bias, spec,
      f1t, f2, g2, g1)


HG_SEG_ROWS = 512
COND_ROWS = 16


def _trunk_layer(x, mods, lp, lb, batch, seq, ctx, final):
    x = ffn_half_step(x, mods, lp['ln_ffn1'], lp['w_ffn1_in'], lp['w_ffn1_out'], lp['final_norm'],
                      mod_base=0, final=False, seq_len=seq)
    p2d = mixer_in_proj(x, mods, lp['ln_mix'], lp['w_in'], seq_len=seq)
    p = p2d.reshape(batch, seq, N_IN)

    filt, ssq = hyena_filter_gen(seq, lp['hy_w1'], lp['hy_b1'], lp['hy_w2'], lp['hy_b2'], lp['hy_w3'],
                                 lp['hy_freq'])
    if ctx is None:
        att = context_attention(p, lp['attn_sink'])
        lru_h0 = jnp.zeros((batch, 2, D_LRU), F32)
        hg_s0 = jnp.zeros((batch, 2, HG_HEADS, HG_DK, HG_DV), F32)
        hy = hyena_ctx(p, lp['hy_conv_w'], lp['hy_conv_b'], lp['hy_bias'], hyena_ctx_spectrum(filt, ssq))
    else:
        k_ctx, v_ctx, lru_h0, hg_state = ctx
        att = latent_attention(p, k_ctx, v_ctx, lp['attn_sink'])
        hg_s0 = hg_state.astype(F32)
        hy = hyena_lat(p, lp['hy_conv_w'], lp['hy_conv_b'], lp['hy_bias'], hyena_lat_spectrum(filt, ssq))
    lru, lru_state = rglru_mixer(p, lp['lru_conv_w'], lp['lru_conv_b'], lp['lru_w_r'], lp['lru_b_r'],
                                 lp['lru_w_i'], lp['lru_b_i'], lp['lru_lambda'], lru_h0.astype(F32))
    o_dirs, hg_state_out = hgrn2_scan(p, lb, hg_s0, min(seq, HG_SEG_ROWS))
    t = batch * seq

    x = gated_merge(x, mods, lp['ln_mix'], att.reshape(t, D_ATTN), lru.reshape(t, D_LRU),
                    o_dirs.reshape(2, t, D_HG), p2d, lp['hg_norm'], hy.reshape(t, D_HY), lp['w_gate'],
                    lp['w_bo_attn'], lp['w_bo_lru'], lp['w_bo_hg'], lp['w_bo_hy'], lp['w_out'], seq_len=seq)
    x = ffn_half_step(x, mods, lp['ln_ffn2'], lp['w_ffn2_in'], lp['w_ffn2_out'], lp['final_norm'],
                      mod_base=6, final=final, seq_len=seq)
    state = None
    if ctx is None:
        k = p[:, :, COL_K:COL_K + D_KV].reshape(batch, seq, N_KV_HEADS, HEAD_DIM)
        v = p[:, :, COL_V:COL_V + D_KV].reshape(batch, seq, N_KV_HEADS, HEAD_DIM)
        state = (k, v, lru_state, hg_state_out)
    return x, state


def kernel(x_prompt, x_sample, cache_k, cache_v, state_lru, state_hgrn, c, c_ctx, ln_ffn1, ln_mix, ln_ffn2, w_ada, b_ada, w_ffn1_in, w_ffn1_out, w_ffn2_in, w_ffn2_out, w_in, attn_sink, lru_conv_w, lru_conv_b, lru_w_r, lru_b_r, lru_w_i, lru_b_i, lru_lambda, hg_lb_logits, hg_norm, hy_conv_w, hy_conv_b, hy_w1, hy_b1, hy_w2, hy_b2, hy_w3, hy_freq, hy_bias, w_bo_attn, w_bo_lru, w_bo_hg, w_bo_hy, w_gate, w_out, final_norm):
    batch, seq, _ = x_prompt.shape
    dec_batch, dec_seq, _ = x_sample.shape
    lb_soft = jax.nn.softmax(hg_lb_logits.astype(F32), axis=0)
    lb_all = jnp.cumsum(lb_soft, axis=0) - lb_soft[0]

    stacked = dict(ln_ffn1=ln_ffn1, ln_mix=ln_mix, ln_ffn2=ln_ffn2, w_ffn1_in=w_ffn1_in, w_ffn1_out=w_ffn1_out,
                   w_ffn2_in=w_ffn2_in, w_ffn2_out=w_ffn2_out, w_in=w_in, attn_sink=attn_sink,
                   lru_conv_w=lru_conv_w, lru_conv_b=lru_conv_b, lru_w_r=lru_w_r, lru_b_r=lru_b_r,
                   lru_w_i=lru_w_i, lru_b_i=lru_b_i, lru_lambda=lru_lambda, hg_norm=hg_norm,
                   hy_conv_w=hy_conv_w, hy_conv_b=hy_conv_b, hy_w1=hy_w1, hy_b1=hy_b1, hy_w2=hy_w2, hy_b2=hy_b2,
                   hy_w3=hy_w3, hy_freq=hy_freq, hy_bias=hy_bias, w_bo_attn=w_bo_attn, w_bo_lru=w_bo_lru,
                   w_bo_hg=w_bo_hg, w_bo_hy=w_bo_hy, w_gate=w_gate, w_out=w_out)

    cond = jnp.zeros((COND_ROWS, D_MODEL), F32).at[0].set(c_ctx).at[1:1 + dec_batch].set(c)
    mods = ada_mods(cond, w_ada, b_ada).reshape(DEPTH, COND_ROWS, N_MOD, D_MODEL)

    h = x_prompt.reshape(batch * seq, D_MODEL)
    z = x_sample.reshape(dec_batch * dec_seq, D_MODEL)
    ks, vs, lrus, hgs = [], [], [], []
    for l in range(DEPTH):
        lp = {name: w[l] for name, w in stacked.items()}
        lp['final_norm'] = final_norm
        final = l == DEPTH - 1
        h, (k_l, v_l, lru_l, hg_l) = _trunk_layer(h, mods[l, 0:1], lp, lb_all[l], batch, seq, None, final)
        ks.append(k_l)
        vs.append(v_l)
        lrus.append(lru_l)
        hgs.append(hg_l)
        ctx = (cache_k[:, l].reshape(dec_batch, -1, D_KV), cache_v[:, l].reshape(dec_batch, -1, D_KV),
               state_lru[:, l], state_hgrn[:, l])
        z, _ = _trunk_layer(z, mods[l, 1:1 + dec_batch], lp, lb_all[l], dec_batch, dec_seq, ctx, final)

    y_prompt = h.reshape(batch, seq, D_MODEL)
    y_sample = z.reshape(dec_batch, dec_seq, D_MODEL)
    return (y_prompt, y_sample, jnp.stack(ks, axis=1), jnp.stack(vs, axis=1), jnp.stack(lrus, axis=1),
            jnp.stack(hgs, axis=1))
```
